```python
import jax, jax.numpy as jnp
from jax import lax
import numpy as np

D_MODEL = 1024
BATCH = 32
SEQ = 256
DEPTH = 2
DEC_BATCH = 2
DEC_SEQ = 2048
PAST_LEN = 512

GRID_W = 64
N_GROUPS = 4
GROUP_W = D_MODEL // N_GROUPS
POOL_GROUPS = 4
POOL_GW = GROUP_W // POOL_GROUPS
POOL_WINDOWS = (2, 4, 8, 16)
MLA_HEADS = 4
Q_LORA = 192
KV_LORA = 128
QK_NOPE = 64
QK_ROPE = 32
V_DIM = GROUP_W // MLA_HEADS
ROPE_BASE = 10000.0
Q_BLOCK = 128
FOURIER_GROUPS = 4
FOURIER_GW = GROUP_W // FOURIER_GROUPS
LRU_W = GROUP_W
LRU_HEADS = 4
LRU_BW = LRU_W // LRU_HEADS
CONV_W = 4
LRU_C = 8.0
N_EXPERTS = 32
TOP_K = 4
D_FF = D_MODEL
SWIGLU_LIMIT = 7.0
SWIGLU_ALPHA = 1.702
MOE_BLOCK = 256
RMS_EPS = 1e-6
IN_SPLITS = (GROUP_W, Q_LORA, KV_LORA, QK_ROPE, GROUP_W, LRU_W, LRU_W)
D_IN = GROUP_W + Q_LORA + KV_LORA + QK_ROPE + GROUP_W + 2 * LRU_W

kernel_name = "hybrid_pool_mla_fnet_rglru_moe_dit_step"

F32 = jnp.float32


def rmsnorm(x, g):
    xf = x.astype(F32)
    y = xf * lax.rsqrt(jnp.mean(xf * xf, axis=-1, keepdims=True) + RMS_EPS)
    return y.astype(x.dtype) * g


def split_in(z):
    out, start = [], 0
    for w in IN_SPLITS:
        out.append(z[..., start:start + w])
        start += w
    return out


def modulation(cvec, w_mod, b_mod):
    m = jax.nn.silu(cvec) @ w_mod + b_mod
    return tuple(t[:, None, :] for t in jnp.split(m, 6, axis=-1))


def pool_mixer(u, w_pool, pool_scale):
    B, S, _ = u.shape
    ug = u.astype(F32).reshape(B, S, POOL_GROUPS, POOL_GW)
    cs = jnp.concatenate([jnp.zeros((B, 1, POOL_GROUPS, POOL_GW), F32), jnp.cumsum(ug, axis=1)], axis=1)
    half = jnp.array(POOL_WINDOWS, jnp.int32) // 2
    pos = jnp.arange(S, dtype=jnp.int32)[:, None]
    lo = jnp.clip(pos - half, 0, S)
    hi = jnp.clip(pos + half, 0, S)
    gidx = jnp.arange(POOL_GROUPS)[None, :]
    win_sum = cs[:, hi, gidx] - cs[:, lo, gidx]
    pooled = win_sum / (hi - lo).astype(F32)[None, :, :, None]
    d = (pooled - ug).astype(u.dtype)
    y = jnp.einsum('bsgc,gcd->bsgd', d, w_pool).reshape(B, S, GROUP_W)
    return y * pool_scale


def axial_rope_tables(n_tokens):
    rows = n_tokens // GRID_W
    row = jnp.repeat(jnp.arange(rows, dtype=F32), GRID_W)
    col = jnp.tile(jnp.arange(GRID_W, dtype=F32), rows)
    n_freq = QK_ROPE // 4
    inv = ROPE_BASE ** (-jnp.arange(n_freq, dtype=F32) / n_freq)
    ang = jnp.concatenate([row[:, None] * inv, col[:, None] * inv], axis=-1)
    return jnp.cos(ang), jnp.sin(ang)


def apply_rope(x, cos, sin):
    xf = x.astype(F32)
    x1, x2 = jnp.split(xf, 2, axis=-1)
    c = cos[None, :, None, :]
    s = sin[None, :, None, :]
    return jnp.concatenate([x1 * c - x2 * s, x1 * s + x2 * c], axis=-1).astype(x.dtype)


def mla_expand(ckv_n, krope, w_ukv):
    B, L, _ = ckv_n.shape
    kv = (ckv_n @ w_ukv).reshape(B, L, MLA_HEADS, QK_NOPE + V_DIM)
    k_nope, v = kv[..., :QK_NOPE], kv[..., QK_NOPE:]
    k_rope = jnp.broadcast_to(krope[:, :, None, :], (B, L, MLA_HEADS, QK_ROPE)).astype(k_nope.dtype)
    return jnp.concatenate([k_nope, k_rope], axis=-1), v


def block_attention(q, k, v):
    B, Sq, H, Dk = q.shape
    nb = Sq // Q_BLOCK
    qb = jnp.moveaxis(q.reshape(B, nb, Q_BLOCK, H, Dk), 1, 0)
    scale = (QK_NOPE + QK_ROPE) ** -0.5

    def one_block(qblk):
        s = jnp.einsum('bqhd,bkhd->bhqk', qblk, k).astype(F32) * scale
        p = jax.nn.softmax(s, axis=-1)
        return jnp.einsum('bhqk,bkhd->bqhd', p.astype(v.dtype), v)

    out = lax.map(one_block, qb)
    return jnp.moveaxis(out, 0, 1).reshape(B, Sq, H, v.shape[-1])


def fourier_mixer(u, w_fourier):
    B, S, _ = u.shape
    ug = u.astype(F32).reshape(B, S, FOURIER_GROUPS, FOURIER_GW)
    f = jnp.fft.fft2(ug, axes=(1, 3), norm='ortho').real
    return f.reshape(B, S, GROUP_W).astype(u.dtype) @ w_fourier


def centred_dwconv(x, w, b):
    S = x.shape[1]
    left = CONV_W // 2
    xp = jnp.pad(x, ((0, 0), (left, CONV_W - 1 - left), (0, 0)))
    y = xp[:, 0:S] * w[0]
    for k in range(1, CONV_W):
        y = y + xp[:, k:k + S] * w[k]
    return y + b


def rglru_scan(x, w_a, b_a, w_x, b_x, lam, h0, reverse):
    B, S, _ = x.shape
    xb = x.reshape(B, S, LRU_HEADS, LRU_BW)
    r = jax.nn.sigmoid(jnp.einsum('bshi,hij->bshj', xb, w_a).reshape(B, S, LRU_W) + b_a)
    i = jax.nn.sigmoid(jnp.einsum('bshi,hij->bshj', xb, w_x).reshape(B, S, LRU_W) + b_x)
    log_a = -LRU_C * r * jax.nn.softplus(-lam.astype(F32))
    a = jnp.exp(log_a)
    bterm = jnp.sqrt(-jnp.expm1(2.0 * log_a)) * (i * x)
    first = S - 1 if reverse else 0
    bterm = bterm.at[:, first].add(a[:, first] * h0)

    def combine(e1, e2):
        a1, b1 = e1
        a2, b2 = e2
        return a1 * a2, a2 * b1 + b2

    _, h = lax.associative_scan(combine, (a, bterm), axis=1, reverse=reverse)
    return h


def lru_mixer(zr, zg, h0, lp):
    xc = centred_dwconv(zr, lp['conv_w'], lp['conv_b']).astype(F32)
    h0 = h0.astype(F32)
    hf = rglru_scan(xc, lp['lru_wa'][0], lp['lru_ba'][0], lp['lru_wx'][0], lp['lru_bx'][0],
                    lp['lru_lambda'][0], h0[:, 0], reverse=False)
    hb = rglru_scan(xc, lp['lru_wa'][1], lp['lru_ba'][1], lp['lru_wx'][1], lp['lru_bx'][1],
                    lp['lru_lambda'][1], h0[:, 1], reverse=True)
    y = ((hf + hb) * jax.nn.gelu(zg.astype(F32))).astype(zr.dtype)
    return y, hf, hb


def routed_ffn(h, lp):
    T = h.shape[0]
    logits = h.astype(F32) @ lp['router_w'].astype(F32) + lp['router_b'].astype(F32)
    top_logit, top_e = lax.top_k(logits, TOP_K)
    top_p = jax.nn.softmax(top_logit, axis=-1)
    TK = T * TOP_K
    flat_e = top_e.reshape(TK)
    flat_p = top_p.reshape(TK)
    flat_tok = jnp.arange(TK, dtype=jnp.int32) // TOP_K
    order = jnp.argsort(flat_e)
    sorted_e = flat_e[order]
    counts = jnp.bincount(flat_e, length=N_EXPERTS)
    padded = (counts + MOE_BLOCK - 1) // MOE_BLOCK * MOE_BLOCK
    starts = jnp.cumsum(counts) - counts
    pad_ends = jnp.cumsum(padded)
    pad_starts = pad_ends - padded
    dest = pad_starts[sorted_e] + jnp.arange(TK, dtype=jnp.int32) - starts[sorted_e]
    n_blocks = -(-TK // MOE_BLOCK) + N_EXPERTS
    n_slots = n_blocks * MOE_BLOCK
    slot_tok = jnp.full((n_slots,), T, jnp.int32).at[dest].set(flat_tok[order])
    slot_p = jnp.zeros((n_slots,), F32).at[dest].set(flat_p[order])
    block_e = jnp.minimum(
        jnp.searchsorted(pad_ends, jnp.arange(n_blocks, dtype=pad_ends.dtype) * MOE_BLOCK, side='right'),
        N_EXPERTS - 1)
    h_pad = jnp.concatenate([h, jnp.zeros((1, D_MODEL), h.dtype)], axis=0)
    xs = h_pad[slot_tok].reshape(n_blocks, MOE_BLOCK, D_MODEL)

    def expert_block(args):
        xb, e = args
        g = xb @ lp['w_gate'][e] + lp['b_gate'][e]
        u = xb @ lp['w_up'][e] + lp['b_up'][e]
        g = jnp.minimum(g, SWIGLU_LIMIT)
        u = jnp.clip(u, -SWIGLU_LIMIT, SWIGLU_LIMIT)
        act = (u + 1.0) * (g * jax.nn.sigmoid(SWIGLU_ALPHA * g))
        return act @ lp['w_down'][e] + lp['b_down'][e]

    ys = lax.map(expert_block, (xs, block_e)).reshape(n_slots, D_MODEL)
    out = jnp.zeros((T + 1, D_MODEL), ys.dtype).at[slot_tok].add(ys * slot_p[:, None].astype(ys.dtype))
    return out[:T].astype(h.dtype)


def trunk_layer(x, mod, lp, ctx):
    B, S, _ = x.shape
    shift1, scale1, gate1, shift2, scale2, gate2 = mod
    h = rmsnorm(x, lp['attn_norm_g']) * (1.0 + scale1) + shift1
    za, zq, zkv, zkr, zf, zr, zg = split_in(h @ lp['w_in'])

    ya = pool_mixer(za, lp['pool_w'], lp['pool_scale']).astype(x.dtype)

    q = (rmsnorm(zq, lp['q_norm_g']) @ lp['w_uq']).reshape(B, S, MLA_HEADS, QK_NOPE + QK_ROPE)
    ckv_n = rmsnorm(zkv, lp['kv_norm_g'])
    if ctx is None:
        k, v = mla_expand(ckv_n, zkr, lp['w_ukv'])
        h0 = jnp.zeros((B, 2, LRU_W), F32)
    else:
        ctx_ckv, ctx_krope, h0 = ctx
        cos, sin = axial_rope_tables(S)
        q = jnp.concatenate([q[..., :QK_NOPE], apply_rope(q[..., QK_NOPE:], cos, sin)], axis=-1)
        kr_lat = apply_rope(zkr[:, :, None, :], cos, sin)[:, :, 0, :]
        k_lat, v_lat = mla_expand(ckv_n, kr_lat, lp['w_ukv'])
        k_ctx, v_ctx = mla_expand(ctx_ckv.astype(ckv_n.dtype), ctx_krope, lp['w_ukv'])
        k = jnp.concatenate([k_ctx, k_lat], axis=1)
        v = jnp.concatenate([v_ctx, v_lat], axis=1)
    yb = block_attention(q, k, v).reshape(B, S, GROUP_W).astype(x.dtype)

    yc = fourier_mixer(zf, lp['fourier_w']).astype(x.dtype)
    yd, hf, hb = lru_mixer(zr, zg, h0, lp)
    yd = yd.astype(x.dtype)

    ycat = rmsnorm(jnp.stack([ya, yb, yc, yd], axis=2), lp['out_norm_g']).reshape(B, S, D_MODEL)
    x = x + gate1 * (ycat @ lp['w_out'])

    h2 = rmsnorm(x, lp['ffn_norm_g']) * (1.0 + scale2) + shift2
    x = x + gate2 * routed_ffn(h2.reshape(B * S, D_MODEL), lp).reshape(B, S, D_MODEL)
    if ctx is None:
        return x, (ckv_n, zkr, jnp.stack([hf[:, -1], hb[:, 0]], axis=1))
    return x, None


def setup_inputs(seed: int = 0) -> dict:
    key = jax.random.key(seed)
    ks = iter(jax.random.split(key, 48))

    def nrm(shape, s):
        return jax.random.normal(next(ks), shape, F32) * s

    L, D, E, F = DEPTH, D_MODEL, N_EXPERTS, D_FF
    u = jax.random.uniform(next(ks), (L, 2, LRU_W), F32, minval=0.9, maxval=0.999)
    sg = u ** (1.0 / LRU_C)
    lam = jnp.log(sg) - jnp.log1p(-sg)
    return {
        'x_prompt': nrm((BATCH, SEQ, D), 1.0),
        'x_sample': nrm((DEC_BATCH, DEC_SEQ, D), 1.0),
        'cache_ckv': nrm((DEC_BATCH, L, PAST_LEN, KV_LORA), 1.0),
        'cache_krope': nrm((DEC_BATCH, L, PAST_LEN, QK_ROPE), 1.0),
        'state_lru': nrm((DEC_BATCH, L, 2, LRU_W), 0.5),
        'c': nrm((DEC_BATCH, D), 1.0),
        'c_ctx': nrm((D,), 1.0),
        'w_mod': nrm((L, D, 6 * D), 0.5 * D ** -0.5),
        'b_mod': nrm((L, 6 * D), 0.02),
        'attn_norm_g': 1.0 + nrm((L, D), 0.1),
        'w_in': nrm((L, D, D_IN), D ** -0.5),
        'pool_w': nrm((L, POOL_GROUPS, POOL_GW, POOL_GW), POOL_GW ** -0.5),
        'pool_scale': 1.0 + nrm((L, GROUP_W), 0.1),
        'q_norm_g': 1.0 + nrm((L, Q_LORA), 0.1),
        'w_uq': nrm((L, Q_LORA, MLA_HEADS * (QK_NOPE + QK_ROPE)), Q_LORA ** -0.5),
        'kv_norm_g': 1.0 + nrm((L, KV_LORA), 0.1),
        'w_ukv': nrm((L, KV_LORA, MLA_HEADS * (QK_NOPE + V_DIM)), KV_LORA ** -0.5),
        'fourier_w': nrm((L, GROUP_W, GROUP_W), GROUP_W ** -0.5),
        'conv_w': nrm((L, CONV_W, LRU_W), 0.5),
        'conv_b': nrm((L, LRU_W), 0.02),
        'lru_wa': nrm((L, 2, LRU_HEADS, LRU_BW, LRU_BW), LRU_BW ** -0.5),
        'lru_ba': nrm((L, 2, LRU_W), 0.02),
        'lru_wx': nrm((L, 2, LRU_HEADS, LRU_BW, LRU_BW), LRU_BW ** -0.5),
        'lru_bx': nrm((L, 2, LRU_W), 0.02),
        'lru_lambda': lam,
        'out_norm_g': 1.0 + nrm((L, N_GROUPS, GROUP_W), 0.1),
        'w_out': nrm((L, D, D), D ** -0.5),
        'ffn_norm_g': 1.0 + nrm((L, D), 0.1),
        'router_w': nrm((L, D, E), D ** -0.5),
        'router_b': nrm((L, E), 0.01),
        'w_gate': nrm((L, E, D, F), D ** -0.5),
        'b_gate': nrm((L, E, F), 0.01),
        'w_up': nrm((L, E, D, F), D ** -0.5),
        'b_up': nrm((L, E, F), 0.01),
        'w_down': nrm((L, E, F, D), F ** -0.5),
        'b_down': nrm((L, E, D), 0.01),
        'final_norm_g': 1.0 + nrm((D,), 0.1),
    }


def reference(x_prompt, x_sample, cache_ckv, cache_krope, state_lru, c, c_ctx,
              w_mod, b_mod, attn_norm_g, w_in, pool_w, pool_scale, q_norm_g, w_uq, kv_norm_g, w_ukv,
              fourier_w, conv_w, conv_b, lru_wa, lru_ba, lru_wx, lru_bx, lru_lambda,
              out_norm_g, w_out, ffn_norm_g, router_w, router_b,
              w_gate, b_gate, w_up, b_up, w_down, b_down, final_norm_g):
    yp = x_prompt
    ys = x_sample
    new_ckv, new_krope, new_lru = [], [], []
    for l in range(DEPTH):
        lp = {
            'attn_norm_g': attn_norm_g[l], 'w_in': w_in[l],
            'pool_w': pool_w[l], 'pool_scale': pool_scale[l],
            'q_norm_g': q_norm_g[l], 'w_uq': w_uq[l], 'kv_norm_g': kv_norm_g[l], 'w_ukv': w_ukv[l],
            'fourier_w': fourier_w[l],
            'conv_w': conv_w[l], 'conv_b': conv_b[l],
            'lru_wa': lru_wa[l], 'lru_ba': lru_ba[l], 'lru_wx': lru_wx[l], 'lru_bx': lru_bx[l],
            'lru_lambda': lru_lambda[l],
            'out_norm_g': out_norm_g[l], 'w_out': w_out[l], 'ffn_norm_g': ffn_norm_g[l],
            'router_w': router_w[l], 'router_b': router_b[l],
            'w_gate': w_gate[l], 'b_gate': b_gate[l], 'w_up': w_up[l], 'b_up': b_up[l],
            'w_down': w_down[l], 'b_down': b_down[l],
        }
        mod_ctx = modulation(c_ctx[None, :], w_mod[l], b_mod[l])
        mod_lat = modulation(c, w_mod[l], b_mod[l])
        yp, (ckv_l, krope_l, lru_l) = trunk_layer(yp, mod_ctx, lp, None)
        ys, _ = trunk_layer(ys, mod_lat, lp, (cache_ckv[:, l], cache_krope[:, l], state_lru[:, l]))
        new_ckv.append(ckv_l)
        new_krope.append(krope_l)
        new_lru.append(lru_l)
    y_prompt = rmsnorm(yp, final_norm_g)
    y_sample = rmsnorm(ys, final_norm_g)
    return (y_prompt, y_sample, jnp.stack(new_ckv, axis=1), jnp.stack(new_krope, axis=1), jnp.stack(new_lru, axis=1))
```

```python
import functools

import numpy as np
import jax
import jax.numpy as jnp
from jax import lax
from jax.experimental import pallas as pl
from jax.experimental.pallas import tpu as pltpu

F32 = jnp.float32
BF16 = jnp.bfloat16
I32 = jnp.int32
U32 = jnp.uint32

D = 1024
BATCH, SEQ = 32, 256
DEC_BATCH, DEC_SEQ, PAST = 2, 2048, 512
T_P = BATCH * SEQ
T_S = DEC_BATCH * DEC_SEQ
T = T_P + T_S
DEPTH = 2
GRID_W = 64
GW = 256
Q_LORA, KV_LORA, QK_NOPE, QK_ROPE, V_DIM, HEADS = 192, 128, 64, 32, 64, 4
HEAD_PAD = 128
ROPE_BASE = 10000.0
POOL_WINDOWS = (2, 4, 8, 16)
LRU_C = 8.0
N_EXP, TOP_K = 32, 4
LIMIT, ALPHA = 7.0, 1.702
EPS = 1e-6

LANES = 128
SUBLANES = 8
VMEM_LIMIT = 56 * 1024 * 1024

TM = 512
N_PT = T_P // TM
TILES_PER_DEC = DEC_SEQ // TM
TM_E = 512
N_SLOTS = T * TOP_K + N_EXP * TM_E
NB_E = N_SLOTS // TM_E
TB_C = 256
SEQ_PAD = 32
FRONT = 8

W_COLS = 1536
COL_Q, COL_KV, COL_KR = 1024, 1280, 1408


def _cparams(sem, vmem=VMEM_LIMIT):
    return pltpu.CompilerParams(dimension_semantics=sem, vmem_limit_bytes=vmem)


def _bdot(a, b):
    return jnp.dot(a.astype(BF16), b.astype(BF16), preferred_element_type=F32)


def _split(a):
    hi = a.astype(BF16)
    lo = (a - hi.astype(F32)).astype(BF16)
    return hi, lo


def _dot3(a, b):
    ah, al = _split(a)
    bh, bl = _split(b)
    d = functools.partial(jnp.dot, preferred_element_type=F32)
    return d(ah, bh) + (d(al, bh) + d(ah, bl))


def _rms(x, n=None):
    n = x.shape[-1] if n is None else n
    return x * lax.rsqrt(jnp.sum(x * x, axis=-1, keepdims=True) * (1.0 / n) + EPS)


def _expm1(y):
    u = jnp.exp(y)
    near = (u - 1.0) * y / jnp.log(u)
    return jnp.where(u == 1.0, y, jnp.where(jnp.abs(y) < 0.5, near, u - 1.0))


def _tile_row(i):
    return jnp.where(i >= N_PT, 1 + (i - N_PT) // TILES_PER_DEC, 0)


def _mod_kernel(c_ref, w_ref, b_ref, o_ref):
    s = jax.nn.silu(c_ref[...])
    o_ref[0] = _dot3(s, w_ref[0]) + b_ref[0]


def _modulation(cvec, w_mod, b_mod):
    tn = 768
    n = 6 * D
    return pl.pallas_call(
        _mod_kernel,
        out_shape=jax.ShapeDtypeStruct((DEPTH, SUBLANES, n), F32),
        grid=(DEPTH, n // tn),
        in_specs=[pl.BlockSpec((SUBLANES, D), lambda l, j: (0, 0)),
                  pl.BlockSpec((1, D, tn), lambda l, j: (l, 0, j)),
                  pl.BlockSpec((1, 1, tn), lambda l, j: (l, 0, j))],
        out_specs=pl.BlockSpec((1, SUBLANES, tn), lambda l, j: (l, 0, j)),
        compiler_params=_cparams(("parallel", "parallel")),
        name="modulation",
    )(cvec, w_mod, b_mod.reshape(DEPTH, 1, n))


def _front_kernel(x_ref, mod_ref, g_ref, w_ref, qg_ref, wuq_ref, kvg_ref, wkn_ref, wkr_ref, wv_ref,
                  cq_ref, saq_ref, sbq_ref, ck_ref, sak_ref, sbk_ref,
                  zmix_ref, q_ref, k_ref, v_ref, ckv_ref, kr_ref):
    i = pl.program_id(0)
    row = _tile_row(i)
    shift1 = mod_ref[pl.ds(row, 1), 0:D]
    scale1 = mod_ref[pl.ds(row, 1), D:2 * D]
    h = _rms(x_ref[...]) * g_ref[...]
    h = h * (1.0 + scale1) + shift1
    z = jnp.dot(h.astype(BF16), w_ref[...], preferred_element_type=F32)
    zmix_ref[...] = z[:, 0:4 * GW]
    qn = _rms(z[:, COL_Q:COL_Q + 2 * LANES], Q_LORA) * qg_ref[...]
    q = _bdot(qn, wuq_ref[...])
    ckv = _rms(z[:, COL_KV:COL_KV + KV_LORA]) * kvg_ref[...]
    ckv_ref[...] = ckv
    ckv_b = ckv.astype(BF16)
    v_ref[...] = jnp.dot(ckv_b, wv_ref[...], preferred_element_type=F32).astype(BF16)
    kn = jnp.dot(ckv_b, wkn_ref[...], preferred_element_type=F32)
    kr = z[:, COL_KR:COL_KR + LANES]

    @pl.when(i < N_PT)
    def _():
        q_ref[...] = q.astype(BF16)
        kr_ref[...] = kr[:, 0:QK_ROPE]
        k_ref[...] = (kn + _bdot(kr, wkr_ref[...])).astype(BF16)

    @pl.when(i >= N_PT)
    def _():
        wq = HEADS * HEAD_PAD
        half = QK_ROPE // 2
        qr = (q * cq_ref[...] + pltpu.roll(q, wq - half, 1) * saq_ref[...]
              + pltpu.roll(q, half, 1) * sbq_ref[...])
        krr = (kr * ck_ref[...] + pltpu.roll(kr, LANES - half, 1) * sak_ref[...]
               + pltpu.roll(kr, half, 1) * sbk_ref[...])
        q_ref[...] = qr.astype(BF16)
        kr_ref[...] = krr[:, 0:QK_ROPE]
        k_ref[...] = (kn + _bdot(krr, wkr_ref[...])).astype(BF16)


def _front(x, mod_l, lw, rope):
    full = lambda shape: pl.BlockSpec(shape, lambda i: (0,) * len(shape))
    rows = lambda w: pl.BlockSpec((TM, w), lambda i: (i, 0))
    rope_rows = lambda w: pl.BlockSpec(
        (TM, w), lambda i: (jnp.maximum(i - N_PT, 0) % TILES_PER_DEC, 0))
    wq = HEADS * HEAD_PAD
    return pl.pallas_call(
        _front_kernel,
        out_shape=(jax.ShapeDtypeStruct((T, 4 * GW), F32),
                   jax.ShapeDtypeStruct((T, wq), BF16),
                   jax.ShapeDtypeStruct((T, wq), BF16),
                   jax.ShapeDtypeStruct((T, HEADS * V_DIM), BF16),
                   jax.ShapeDtypeStruct((T, KV_LORA), F32),
                   jax.ShapeDtypeStruct((T, QK_ROPE), F32)),
        grid=(T // TM,),
        in_specs=[rows(D), full((SUBLANES, 6 * D)), full((1, D)), full((D, W_COLS)),
                  full((1, 2 * LANES)), full((2 * LANES, wq)), full((1, KV_LORA)),
                  full((KV_LORA, wq)), full((LANES, wq)), full((KV_LORA, HEADS * V_DIM)),
                  rope_rows(wq), rope_rows(wq), rope_rows(wq),
                  rope_rows(LANES), rope_rows(LANES), rope_rows(LANES)],
        out_specs=(rows(4 * GW), rows(wq), rows(wq), rows(HEADS * V_DIM), rows(KV_LORA),
                   rows(QK_ROPE)),
        compiler_params=_cparams(("parallel",)),
        name="front",
    )(x, mod_l, lw["attn_g"], lw["w_in"], lw["q_g"], lw["w_uq"], lw["kv_g"], lw["w_kn"],
      lw["w_kr"], lw["w_v"], *rope)


def _kvexp_kernel(ckv_ref, kr_ref, wkn_ref, wkr_ref, wv_ref, k_ref, v_ref):
    ckv_b = ckv_ref[...].astype(BF16)
    v_ref[...] = jnp.dot(ckv_b, wv_ref[...], preferred_element_type=F32).astype(BF16)
    kn = jnp.dot(ckv_b, wkn_ref[...], preferred_element_type=F32)
    k_ref[...] = (kn + _bdot(kr_ref[...], wkr_ref[...])).astype(BF16)


def _kv_expand(ckv, kr_pad, lw):
    n = ckv.shape[0]
    wq = HEADS * HEAD_PAD
    full = lambda shape: pl.BlockSpec(shape, lambda i: (0,) * len(shape))
    return pl.pallas_call(
        _kvexp_kernel,
        out_shape=(jax.ShapeDtypeStruct((n, wq), BF16),
                   jax.ShapeDtypeStruct((n, HEADS * V_DIM), BF16)),
        grid=(1,),
        in_specs=[full((n, KV_LORA)), full((n, LANES)), full((KV_LORA, wq)), full((LANES, wq)),
                  full((KV_LORA, HEADS * V_DIM))],
        out_specs=(full((n, wq)), full((n, HEADS * V_DIM))),
        compiler_params=_cparams(("arbitrary",)),
        name="kv_expand",
    )(ckv, kr_pad, lw["w_kn"], lw["w_kr"], lw["w_v"])


def _attn_kernel(q_ref, k_ref, v_ref, o_ref):
    scale = (QK_NOPE + QK_ROPE) ** -0.5
    outs = []
    for h in range(HEADS):
        qh = q_ref[:, h * HEAD_PAD:(h + 1) * HEAD_PAD]
        kh = k_ref[:, h * HEAD_PAD:(h + 1) * HEAD_PAD]
        s = lax.dot_general(qh, kh, (((1,), (1,)), ((), ())), preferred_element_type=F32) * scale
        m = jnp.max(s, axis=-1, keepdims=True)
        p = jnp.exp(s - m)
        l = jnp.sum(p, axis=-1, keepdims=True)
        vh = v_ref[:, h * V_DIM:(h + 1) * V_DIM]
        outs.append(jnp.dot(p.astype(BF16), vh, preferred_element_type=F32) / l)
    o_ref[...] = jnp.concatenate(outs, axis=-1)


def _attention(q, k, v, n_batch, seq, n_keys, tq, row0):
    nq = seq // tq
    wq = HEADS * HEAD_PAD
    return pl.pallas_call(
        _attn_kernel,
        out_shape=jax.ShapeDtypeStruct((n_batch * seq, GW), F32),
        grid=(n_batch, nq),
        in_specs=[pl.BlockSpec((tq, wq), lambda b, i: (row0 // tq + b * nq + i, 0)),
                  pl.BlockSpec((n_keys, wq), lambda b, i: (b, 0)),
                  pl.BlockSpec((n_keys, HEADS * V_DIM), lambda b, i: (b, 0))],
        out_specs=pl.BlockSpec((tq, GW), lambda b, i: (b * nq + i, 0)),
        compiler_params=_cparams(("parallel", "parallel")),
        name="attention",
    )(q, k, v)


def _fourier_kernel(z_ref, c_ref, s_ref, bdc_ref, bds_ref, w_ref, o_ref, xc_ref, xs_ref, *, norm):
    @pl.when(pl.program_id(1) == 0)
    def _():
        xb = z_ref[...].astype(BF16)
        xc_ref[...] = jnp.dot(xb, bdc_ref[...], preferred_element_type=F32).astype(BF16)
        xs_ref[...] = jnp.dot(xb, bds_ref[...], preferred_element_type=F32).astype(BF16)

    f = (jnp.dot(c_ref[...], xc_ref[...], preferred_element_type=F32)
         - jnp.dot(s_ref[...], xs_ref[...], preferred_element_type=F32)) * norm
    o_ref[...] = _bdot(f, w_ref[...])


def _fourier(zmix, consts, lw, n_batch, seq, row0):
    ts = min(seq, 512)
    nj = seq // ts
    cmat, smat = consts
    full = lambda shape: pl.BlockSpec(shape, lambda b, j: (0,) * len(shape))
    return pl.pallas_call(
        functools.partial(_fourier_kernel, norm=float((seq * (GW // 4)) ** -0.5)),
        out_shape=jax.ShapeDtypeStruct((n_batch * seq, GW), F32),
        grid=(n_batch, nj),
        in_specs=[pl.BlockSpec((seq, GW), lambda b, j: (row0 // seq + b, 1)),
                  pl.BlockSpec((ts, seq), lambda b, j: (j, 0)),
                  pl.BlockSpec((ts, seq), lambda b, j: (j, 0)),
                  full((GW, GW)), full((GW, GW)), full((GW, GW))],
        out_specs=pl.BlockSpec((ts, GW), lambda b, j: (b * nj + j, 0)),
        scratch_shapes=[pltpu.VMEM((seq, GW), BF16), pltpu.VMEM((seq, GW), BF16)],
        compiler_params=_cparams(("parallel", "arbitrary")),
        name="fourier",
    )(zmix, cmat, smat, lw["bd_c"], lw["bd_s"], lw["w_f"])


def _seq_kernel(za_ref, zr_ref, zg_ref, h0_ref, wp_ref, ps_ref, cw_ref, cb_ref, wa_ref, ba_ref,
                wx_ref, bx_ref, lam_ref, ya_ref, yd_ref, st_ref,
                pa_ref, pb_ref, xp_ref, a_ref, b_ref, *, seq):
    n = seq + SEQ_PAD
    span = seq + 2 * FRONT
    zeros_pad = jnp.zeros((n, GW), F32)

    za = za_ref[...]
    pa_ref[...] = zeros_pad
    pb_ref[...] = zeros_pad
    pa_ref[FRONT:FRONT + seq, :] = za
    pb_ref[0:span, :] = pa_ref[0:span, :] + pa_ref[1:span + 1, :]
    win2 = pb_ref[FRONT - 1:FRONT - 1 + seq, :]
    pa_ref[0:span, :] = pb_ref[0:span, :] + pb_ref[2:span + 2, :]
    win4 = pa_ref[FRONT - 2:FRONT - 2 + seq, :]
    pb_ref[0:span, :] = pa_ref[0:span, :] + pa_ref[4:span + 4, :]
    win8 = pb_ref[FRONT - 4:FRONT - 4 + seq, :]
    pa_ref[0:span, :] = pb_ref[0:span, :] + pb_ref[8:span + 8, :]
    win16 = pa_ref[FRONT - 8:FRONT - 8 + seq, :]
    pos = lax.broadcasted_iota(I32, (seq, GW), 0)
    grp = lax.broadcasted_iota(I32, (seq, GW), 1) // (GW // 4)
    half = jnp.where(grp == 0, 1, jnp.where(grp == 1, 2, jnp.where(grp == 2, 4, 8)))
    cnt = (jnp.minimum(pos + half, seq) - jnp.maximum(pos - half, 0)).astype(F32)
    win = jnp.where(grp == 0, win2, jnp.where(grp == 1, win4, jnp.where(grp == 2, win8, win16)))
    dlt = win / cnt - za
    ya_ref[...] = _bdot(dlt, wp_ref[...]) * ps_ref[...]

    xp_ref[...] = zeros_pad
    xp_ref[FRONT:FRONT + seq, :] = zr_ref[...]
    xc = cb_ref[...] + cw_ref[0:1, :] * xp_ref[FRONT - 2:FRONT - 2 + seq, :]
    for kk in range(1, 4):
        xc = xc + cw_ref[kk:kk + 1, :] * xp_ref[FRONT - 2 + kk:FRONT - 2 + kk + seq, :]
    xcb = xc.astype(BF16)
    sub = lax.broadcasted_iota(I32, (seq, GW), 0) % SUBLANES
    n_grp = seq // SUBLANES

    total = None
    for d in range(2):
        r = jax.nn.sigmoid(jnp.dot(xcb, wa_ref[d], preferred_element_type=F32) + ba_ref[d])
        ig = jax.nn.sigmoid(jnp.dot(xcb, wx_ref[d], preferred_element_type=F32) + bx_ref[d])
        log_a = (-LRU_C) * r * jax.nn.softplus(-lam_ref[d])
        a = jnp.exp(log_a)
        b = jnp.sqrt(-_expm1(2.0 * log_a)) * (ig * xc)
        for k in (1, 2, 4):
            if d == 0:
                ap, bp, m = pltpu.roll(a, k, 0), pltpu.roll(b, k, 0), sub >= k
            else:
                ap, bp, m = pltpu.roll(a, seq - k, 0), pltpu.roll(b, seq - k, 0), sub < SUBLANES - k
            b = jnp.where(m, a * bp + b, b)
            a = jnp.where(m, a * ap, a)
        a_ref[...] = a
        b_ref[...] = b

        def step(g, carry, d=d):
            gi = g if d == 0 else n_grp - 1 - g
            off = pl.multiple_of(gi * SUBLANES, SUBLANES)
            hh = a_ref[pl.ds(off, SUBLANES), :] * carry + b_ref[pl.ds(off, SUBLANES), :]
            b_ref[pl.ds(off, SUBLANES), :] = hh
            edge = hh[SUBLANES - 1:SUBLANES, :] if d == 0 else hh[0:1, :]
            return jnp.broadcast_to(edge, (SUBLANES, GW))

        last = lax.fori_loop(0, n_grp, step, jnp.broadcast_to(h0_ref[0, d:d + 1, :], (SUBLANES, GW)))
        st_ref[0, d:d + 1, :] = last[0:1, :]
        total = b_ref[...] if total is None else total + b_ref[...]

    yd_ref[...] = total * jax.nn.gelu(zg_ref[...])


def _seq_mixers(zmix, h0, lw, n_batch, seq, row0):
    full = lambda shape: pl.BlockSpec(shape, lambda b: (0,) * len(shape))
    col = lambda c: pl.BlockSpec((seq, GW), lambda b: (row0 // seq + b, c))
    out_rows = pl.BlockSpec((seq, GW), lambda b: (b, 0))
    pad = pltpu.VMEM((seq + SEQ_PAD, GW), F32)
    return pl.pallas_call(
        functools.partial(_seq_kernel, seq=seq),
        out_shape=(jax.ShapeDtypeStruct((n_batch * seq, GW), F32),
                   jax.ShapeDtypeStruct((n_batch * seq, GW), F32),
                   jax.ShapeDtypeStruct((n_batch, 2, GW), F32)),
        grid=(n_batch,),
        in_specs=[col(0), col(2), col(3), pl.BlockSpec((1, 2, GW), lambda b: (b, 0, 0)),
                  full((GW, GW)), full((1, GW)), full((4, GW)), full((1, GW)),
                  full((2, GW, GW)), full((2, 1, GW)), full((2, GW, GW)), full((2, 1, GW)),
                  full((2, 1, GW))],
        out_specs=(out_rows, out_rows, pl.BlockSpec((1, 2, GW), lambda b: (b, 0, 0))),
        scratch_shapes=[pad, pad, pad, pltpu.VMEM((seq, GW), F32), pltpu.VMEM((seq, GW), F32)],
        compiler_params=_cparams(("parallel",)),
        name="seq_mixers",
    )(zmix, zmix, zmix, h0, lw["bd_pool"], lw["pool_scale"], lw["conv_w"], lw["conv_b"],
      lw["bd_wa"], lw["lru_ba"], lw["bd_wx"], lw["lru_bx"], lw["lru_lam"])


def _post_kernel(x_ref, mod_ref, yap_ref, ybp_ref, ycp_ref, ydp_ref, yas_ref, ybs_ref, ycs_ref,
                 yds_ref, og_ref, wo_ref, fg_ref, rw_ref, rb_ref, tri_ref,
                 x1_ref, h2p_ref, te_ref, tp_ref, rk_ref, cnt_ref, carry_ref):
    i = pl.program_id(0)
    is_s = i >= N_PT
    row = _tile_row(i)
    gate1 = mod_ref[pl.ds(row, 1), 2 * D:3 * D]
    shift2 = mod_ref[pl.ds(row, 1), 3 * D:4 * D]
    scale2 = mod_ref[pl.ds(row, 1), 4 * D:5 * D]

    @pl.when(i == 0)
    def _():
        carry_ref[...] = jnp.zeros_like(carry_ref)

    groups = []
    for gi, (p_ref, s_ref) in enumerate(((yap_ref, yas_ref), (ybp_ref, ybs_ref),
                                         (ycp_ref, ycs_ref), (ydp_ref, yds_ref))):
        y = jnp.where(is_s, s_ref[...], p_ref[...])
        groups.append((_rms(y) * og_ref[gi:gi + 1, :]).astype(BF16))
    ycat = jnp.concatenate(groups, axis=-1)
    x1 = x_ref[...] + gate1 * jnp.dot(ycat, wo_ref[...], preferred_element_type=F32)
    x1_ref[...] = x1
    h2 = _rms(x1) * fg_ref[...]
    h2 = h2 * (1.0 + scale2) + shift2

    lo = pltpu.bitcast(h2[:, 0:D // 2].astype(BF16).astype(F32), U32) >> 16
    hi = pltpu.bitcast(h2[:, D // 2:D].astype(BF16).astype(F32), U32) & jnp.uint32(0xFFFF0000)
    h2p_ref[...] = hi | lo

    logits = _dot3(h2, rw_ref[...]) + rb_ref[...]
    lane = lax.broadcasted_iota(I32, (TM, LANES), 1)
    lane_f = lane.astype(F32)
    neg = jnp.float32(-jnp.inf)
    cur = jnp.where(lane < N_EXP, logits, neg)
    sel, vals, idxs = [], [], []
    for _ in range(TOP_K):
        m = jnp.max(cur, axis=-1, keepdims=True)
        idx = jnp.min(jnp.where(cur == m, lane_f, float(LANES)), axis=-1, keepdims=True)
        hit = lane_f == idx
        sel.append(hit)
        vals.append(m)
        idxs.append(idx)
        cur = jnp.where(hit, neg, cur)
    exps = [jnp.exp(v - vals[0]) for v in vals]
    denom = exps[0] + exps[1] + exps[2] + exps[3]
    onehot = jnp.where(sel[0] | sel[1] | sel[2] | sel[3], 1.0, 0.0)
    cum = jnp.dot(tri_ref[...], onehot.astype(BF16), preferred_element_type=F32) + carry_ref[0:1, :]
    te = jnp.zeros((TM, LANES), I32)
    tp = jnp.zeros((TM, LANES), F32)
    rk = jnp.zeros((TM, LANES), I32)
    for k in range(TOP_K):
        rank = jnp.sum(jnp.where(sel[k], cum - 1.0, 0.0), axis=-1, keepdims=True).astype(I32)
        te = jnp.where(lane == k, idxs[k].astype(I32), te)
        tp = jnp.where(lane == k, exps[k] / denom, tp)
        rk = jnp.where(lane == k, rank, rk)
    te_ref[...] = te
    tp_ref[...] = tp
    rk_ref[...] = rk
    new_carry = jnp.broadcast_to(cum[TM - 1:TM, :], (SUBLANES, LANES))
    carry_ref[...] = new_carry
    cnt_ref[...] = new_carry


def _post(x, mod_l, ys_prompt, ys_sample, lw, tri):
    full = lambda shape: pl.BlockSpec(shape, lambda i: (0,) * len(shape))
    rows = lambda w: pl.BlockSpec((TM, w), lambda i: (i, 0))
    prow = pl.BlockSpec((TM, GW), lambda i: (jnp.minimum(i, N_PT - 1), 0))
    srow = pl.BlockSpec((TM, GW), lambda i: (jnp.maximum(i - N_PT, 0), 0))
    return pl.pallas_call(
        _post_kernel,
        out_shape=(jax.ShapeDtypeStruct((T, D), F32),
                   jax.ShapeDtypeStruct((T, D // 2), U32),
                   jax.ShapeDtypeStruct((T, LANES), I32),
                   jax.ShapeDtypeStruct((T, LANES), F32),
                   jax.ShapeDtypeStruct((T, LANES), I32),
                   jax.ShapeDtypeStruct((SUBLANES, LANES), F32)),
        grid=(T // TM,),
        in_specs=[rows(D), full((SUBLANES, 6 * D)), prow, prow, prow, prow, srow, srow, srow, srow,
                  full((4, GW)), full((D, D)), full((1, D)), full((D, LANES)), full((1, LANES)),
                  full((TM, TM))],
        out_specs=(rows(D), rows(D // 2), rows(LANES), rows(LANES), rows(LANES),
                   full((SUBLANES, LANES))),
        scratch_shapes=[pltpu.VMEM((SUBLANES, LANES), F32)],
        compiler_params=_cparams(("arbitrary",)),
        name="post",
    )(x, mod_l, *ys_prompt, *ys_sample, lw["out_g"], lw["w_out"], lw["ffn_g"], lw["router_w"],
      lw["router_b"], tri)


def _gather_kernel(tok_ref, nused_ref, h_hbm, o_ref, sem):
    i = pl.program_id(0)

    def row_copy(r):
        tok = tok_ref[i * TM_E + r]
        return pltpu.make_async_copy(h_hbm.at[pl.ds(tok, 1)], o_ref.at[pl.ds(r, 1)], sem)

    @pl.when(i < nused_ref[0])
    def _():
        def issue(r, c):
            row_copy(r).start()
            return c
        lax.fori_loop(0, TM_E, issue, 0)

        def drain(r, c):
            row_copy(r).wait()
            return c
        lax.fori_loop(0, TM_E, drain, 0)

    @pl.when(i >= nused_ref[0])
    def _():
        o_ref[...] = jnp.zeros_like(o_ref)


def _gather_rows(slot_tok, n_used, h2p):
    return pl.pallas_call(
        _gather_kernel,
        out_shape=jax.ShapeDtypeStruct((N_SLOTS, D // 2), U32),
        grid_spec=pltpu.PrefetchScalarGridSpec(
            num_scalar_prefetch=2,
            grid=(NB_E,),
            in_specs=[pl.BlockSpec(memory_space=pl.ANY)],
            out_specs=pl.BlockSpec((TM_E, D // 2), lambda i, tok, nu: (i, 0)),
            scratch_shapes=[pltpu.SemaphoreType.DMA(())]),
        compiler_params=_cparams(("arbitrary",)),
        name="moe_gather",
    )(slot_tok, n_used, h2p)


def _expert_kernel(be_ref, nused_ref, x_ref, wg_ref, bg_ref, wu_ref, bu_ref, wd_ref, bd_ref, o_ref):
    i = pl.program_id(0)

    @pl.when(i < nused_ref[0])
    def _():
        w = x_ref[...]
        lo = pltpu.bitcast(w << 16, F32).astype(BF16)
        hi = pltpu.bitcast(w & jnp.uint32(0xFFFF0000), F32).astype(BF16)
        x = jnp.concatenate([lo, hi], axis=-1)
        g = jnp.dot(x, wg_ref[0].astype(BF16), preferred_element_type=F32) + bg_ref[0]
        u = jnp.dot(x, wu_ref[0].astype(BF16), preferred_element_type=F32) + bu_ref[0]
        g = jnp.minimum(g, LIMIT)
        u = jnp.clip(u, -LIMIT, LIMIT)
        act = (u + 1.0) * (g * jax.nn.sigmoid(ALPHA * g))
        o_ref[...] = jnp.dot(act.astype(BF16), wd_ref[0].astype(BF16),
                             preferred_element_type=F32) + bd_ref[0]

    @pl.when(i >= nused_ref[0])
    def _():
        o_ref[...] = jnp.zeros_like(o_ref)


def _experts(block_e, n_used, xs, lw):
    wspec = pl.BlockSpec((1, D, D), lambda i, be, nu: (be[i], 0, 0))
    bspec = pl.BlockSpec((1, 1, D), lambda i, be, nu: (be[i], 0, 0))
    return pl.pallas_call(
        _expert_kernel,
        out_shape=jax.ShapeDtypeStruct((N_SLOTS, D), F32),
        grid_spec=pltpu.PrefetchScalarGridSpec(
            num_scalar_prefetch=2,
            grid=(NB_E,),
            in_specs=[pl.BlockSpec((TM_E, D // 2), lambda i, be, nu: (i, 0)),
                      wspec, bspec, wspec, bspec, wspec, bspec],
            out_specs=pl.BlockSpec((TM_E, D), lambda i, be, nu: (i, 0))),
        compiler_params=_cparams(("arbitrary",)),
        name="moe_experts",
    )(block_e, n_used, xs, lw["w_gate"], lw["b_gate"], lw["w_up"], lw["b_up"], lw["w_down"],
      lw["b_down"])


def _combine_kernel(dest_ref, ys_hbm, x_ref, mod_ref, p_ref, fg_ref, o_ref, buf_ref, sem, *, final):
    i = pl.program_id(0)
    row = jnp.where(i * TB_C >= T_P, 1 + (i * TB_C - T_P) // DEC_SEQ, 0)
    gate2 = mod_ref[pl.ds(row, 1), 5 * D:6 * D]

    def row_copy(j):
        r = j // TOP_K
        k = j % TOP_K
        slot = dest_ref[i * (TB_C * TOP_K) + j]
        return pltpu.make_async_copy(ys_hbm.at[pl.ds(slot, 1)], buf_ref.at[k, pl.ds(r, 1)], sem)

    def issue(j, c):
        row_copy(j).start()
        return c
    lax.fori_loop(0, TB_C * TOP_K, issue, 0)

    def drain(j, c):
        row_copy(j).wait()
        return c
    lax.fori_loop(0, TB_C * TOP_K, drain, 0)

    p = p_ref[...]
    acc = buf_ref[0] * p[:, 0:1]
    for k in range(1, TOP_K):
        acc = acc + buf_ref[k] * p[:, k:k + 1]
    x2 = x_ref[...] + gate2 * acc
    if final:
        x2 = _rms(x2) * fg_ref[...]
    o_ref[...] = x2


def _combine(dest, ys, x1, mod_l, top_p, final_g, final):
    full = lambda shape: pl.BlockSpec(shape, lambda i, d: (0,) * len(shape))
    rows = lambda w: pl.BlockSpec((TB_C, w), lambda i, d: (i, 0))
    return pl.pallas_call(
        functools.partial(_combine_kernel, final=final),
        out_shape=jax.ShapeDtypeStruct((T, D), F32),
        grid_spec=pltpu.PrefetchScalarGridSpec(
            num_scalar_prefetch=1,
            grid=(T // TB_C,),
            in_specs=[pl.BlockSpec(memory_space=pl.ANY), rows(D), full((SUBLANES, 6 * D)),
                      rows(LANES), full((1, D))],
            out_specs=rows(D),
            scratch_shapes=[pltpu.VMEM((TOP_K, TB_C, D), F32), pltpu.SemaphoreType.DMA(())]),
        compiler_params=_cparams(("arbitrary",)),
        name="moe_combine",
    )(dest, ys, x1, mod_l, top_p, final_g)


def _rope_tables():
    rows = DEC_SEQ // GRID_W
    r = np.repeat(np.arange(rows, dtype=np.float64), GRID_W)
    c = np.tile(np.arange(GRID_W, dtype=np.float64), rows)
    n_freq = QK_ROPE // 4
    inv = (np.float32(ROPE_BASE) ** (-np.arange(n_freq, dtype=np.float32) / n_freq)).astype(np.float64)
    ang = np.concatenate([r[:, None] * inv, c[:, None] * inv], axis=-1).astype(np.float32)
    cos, sin = np.cos(ang.astype(np.float64)), np.sin(ang.astype(np.float64))
    half = QK_ROPE // 2

    def place(width, start):
        cf = np.ones((DEC_SEQ, width), np.float32)
        sa = np.zeros((DEC_SEQ, width), np.float32)
        sb = np.zeros((DEC_SEQ, width), np.float32)
        for s0 in start:
            cf[:, s0:s0 + half] = cos
            cf[:, s0 + half:s0 + 2 * half] = cos
            sa[:, s0:s0 + half] = -sin
            sb[:, s0 + half:s0 + 2 * half] = sin
        return jnp.asarray(cf), jnp.asarray(sa), jnp.asarray(sb)

    return (*place(HEADS * HEAD_PAD, [h * HEAD_PAD + QK_NOPE for h in range(HEADS)]),
            *place(LANES, [0]))


def _dft_tables(seq):
    kn = (np.arange(seq, dtype=np.int64)[:, None] * np.arange(seq, dtype=np.int64)[None, :]) % seq
    ang = 2.0 * np.pi * kn.astype(np.float64) / seq
    return (jnp.asarray(np.cos(ang), F32).astype(BF16), jnp.asarray(np.sin(ang), F32).astype(BF16))


def _block_diag(blocks):
    g, n, _ = blocks.shape
    eye = jnp.eye(g, dtype=blocks.dtype)
    return jnp.einsum("gij,gh->gihj", blocks, eye).reshape(g * n, g * n)


def _layer_weights(l, p):
    w_in = p["w_in"][l]
    za, zq, zkv, zkr, zf, zr, zg = jnp.split(
        w_in, np.cumsum([GW, Q_LORA, KV_LORA, QK_ROPE, GW, GW])[:], axis=1)
    zpad = lambda n: jnp.zeros((D, n), F32)
    w_cols = jnp.concatenate([za, zf, zr, zg, zq, zpad(2 * LANES - Q_LORA), zkv, zkr,
                              zpad(LANES - QK_ROPE)], axis=1).astype(BF16)
    wq = HEADS * HEAD_PAD
    w_uq = p["w_uq"][l].reshape(Q_LORA, HEADS, QK_NOPE + QK_ROPE)
    w_uq = jnp.pad(w_uq, ((0, 2 * LANES - Q_LORA), (0, 0), (0, HEAD_PAD - QK_NOPE - QK_ROPE)))
    w_ukv = p["w_ukv"][l].reshape(KV_LORA, HEADS, QK_NOPE + V_DIM)
    w_kn = jnp.pad(w_ukv[:, :, :QK_NOPE], ((0, 0), (0, 0), (0, HEAD_PAD - QK_NOPE)))
    w_v = w_ukv[:, :, QK_NOPE:]
    place = np.zeros((LANES, HEADS, HEAD_PAD), np.float32)
    for h in range(HEADS):
        place[np.arange(QK_ROPE), h, QK_NOPE + np.arange(QK_ROPE)] = 1.0
    c64 = np.arange(GW // 4, dtype=np.int64)
    ang = 2.0 * np.pi * ((c64[:, None] * c64[None, :]) % (GW // 4)).astype(np.float64) / (GW // 4)
    four = lambda m: jnp.asarray(np.broadcast_to(m, (4,) + m.shape), F32)
    router_w = jnp.pad(p["router_w"][l], ((0, 0), (0, LANES - N_EXP)))
    router_b = jnp.pad(p["router_b"][l], (0, LANES - N_EXP)).reshape(1, LANES)
    return {
        "attn_g": p["attn_norm_g"][l].reshape(1, D),
        "w_in": w_cols,
        "q_g": jnp.pad(p["q_norm_g"][l], (0, 2 * LANES - Q_LORA)).reshape(1, 2 * LANES),
        "w_uq": w_uq.reshape(2 * LANES, wq).astype(BF16),
        "kv_g": p["kv_norm_g"][l].reshape(1, KV_LORA),
        "w_kn": w_kn.reshape(KV_LORA, wq).astype(BF16),
        "w_kr": jnp.asarray(place.reshape(LANES, wq), BF16),
        "w_v": w_v.reshape(KV_LORA, HEADS * V_DIM).astype(BF16),
        "bd_c": _block_diag(four(np.cos(ang))).astype(BF16),
        "bd_s": _block_diag(four(np.sin(ang))).astype(BF16),
        "w_f": p["fourier_w"][l].astype(BF16),
        "bd_pool": _block_diag(p["pool_w"][l]).astype(BF16),
        "pool_scale": p["pool_scale"][l].reshape(1, GW),
        "conv_w": p["conv_w"][l],
        "conv_b": p["conv_b"][l].reshape(1, GW),
        "bd_wa": jnp.stack([_block_diag(p["lru_wa"][l, d]) for d in range(2)]).astype(BF16),
        "bd_wx": jnp.stack([_block_diag(p["lru_wx"][l, d]) for d in range(2)]).astype(BF16),
        "lru_ba": p["lru_ba"][l].reshape(2, 1, GW),
        "lru_bx": p["lru_bx"][l].reshape(2, 1, GW),
        "lru_lam": p["lru_lambda"][l].reshape(2, 1, GW),
        "out_g": p["out_norm_g"][l],
        "w_out": p["w_out"][l].astype(BF16),
        "ffn_g": p["ffn_norm_g"][l].reshape(1, D),
        "router_w": router_w,
        "router_b": router_b,
        "w_gate": p["w_gate"][l], "b_gate": p["b_gate"][l].reshape(N_EXP, 1, D),
        "w_up": p["w_up"][l], "b_up": p["b_up"][l].reshape(N_EXP, 1, D),
        "w_down": p["w_down"][l], "b_down": p["b_down"][l].reshape(N_EXP, 1, D),
    }


def _routing_tables(top_e, rank, counts):
    counts = counts.astype(I32)
    padded = (counts + TM_E - 1) // TM_E * TM_E
    pad_ends = jnp.cumsum(padded)
    pad_starts = pad_ends - padded
    dest = (pad_starts[top_e] + rank).reshape(T * TOP_K)
    tok = jnp.arange(T * TOP_K, dtype=I32) // TOP_K
    slot_tok = jnp.zeros((N_SLOTS,), I32).at[dest].set(tok, unique_indices=True)
    n_used = (pad_ends[-1] // TM_E).astype(I32)
    blk = jnp.minimum(jnp.arange(NB_E, dtype=I32), n_used - 1) * TM_E
    block_e = jnp.minimum(jnp.searchsorted(pad_ends, blk, side="right"), N_EXP - 1).astype(I32)
    return dest, slot_tok, block_e, n_used.reshape(1)


def kernel(x_prompt, x_sample, cache_ckv, cache_krope, state_lru, c, c_ctx, w_mod, b_mod, attn_norm_g, w_in, pool_w, pool_scale, q_norm_g, w_uq, kv_norm_g, w_ukv, fourier_w, conv_w, conv_b, lru_wa, lru_ba, lru_wx, lru_bx, lru_lambda, out_norm_g, w_out, ffn_norm_g, router_w, router_b, w_gate, b_gate, w_up, b_up, w_down, b_down, final_norm_g):
    params = dict(attn_norm_g=attn_norm_g, w_in=w_in, pool_w=pool_w, pool_scale=pool_scale,
                  q_norm_g=q_norm_g, w_uq=w_uq, kv_norm_g=kv_norm_g, w_ukv=w_ukv,
                  fourier_w=fourier_w, conv_w=conv_w, conv_b=conv_b, lru_wa=lru_wa, lru_ba=lru_ba,
                  lru_wx=lru_wx, lru_bx=lru_bx, lru_lambda=lru_lambda, out_norm_g=out_norm_g,
                  w_out=w_out, ffn_norm_g=ffn_norm_g, router_w=router_w, router_b=router_b,
                  w_gate=w_gate, b_gate=b_gate, w_up=w_up, b_up=b_up, w_down=w_down, b_down=b_down)
    x = jnp.concatenate([x_prompt.reshape(T_P, D), x_sample.reshape(T_S, D)], axis=0)
    cvec = jnp.concatenate([c_ctx[None, :], c, jnp.zeros((SUBLANES - 1 - DEC_BATCH, D), F32)], axis=0)
    mod = _modulation(cvec, w_mod, b_mod)
    rope = _rope_tables()
    dft_p, dft_s = _dft_tables(SEQ), _dft_tables(DEC_SEQ)
    tri = jnp.asarray(np.tril(np.ones((TM, TM), np.float32)), BF16)
    final_g = final_norm_g.reshape(1, D)
    h0_prompt = jnp.zeros((BATCH, 2, GW), F32)
    n_keys = PAST + DEC_SEQ

    new_ckv, new_krope, new_lru = [], [], []
    for l in range(DEPTH):
        lw = _layer_weights(l, params)
        zmix, q, k, v, ckv, kr = _front(x, mod[l], lw, rope)
        new_ckv.append(ckv[:T_P].reshape(BATCH, SEQ, KV_LORA))
        new_krope.append(kr[:T_P].reshape(BATCH, SEQ, QK_ROPE))

        yb_p = _attention(q, k, v, BATCH, SEQ, SEQ, SEQ, 0)
        yc_p = _fourier(zmix, dft_p, lw, BATCH, SEQ, 0)
        ya_p, yd_p, st_p = _seq_mixers(zmix, h0_prompt, lw, BATCH, SEQ, 0)
        new_lru.append(st_p)

        kr_ctx = jnp.pad(cache_krope[:, l].reshape(DEC_BATCH * PAST, QK_ROPE),
                         ((0, 0), (0, LANES - QK_ROPE)))
        k_ctx, v_ctx = _kv_expand(cache_ckv[:, l].reshape(DEC_BATCH * PAST, KV_LORA), kr_ctx, lw)
        k_s = jnp.concatenate([k_ctx.reshape(DEC_BATCH, PAST, -1),
                               k[T_P:].reshape(DEC_BATCH, DEC_SEQ, -1)], axis=1)
        v_s = jnp.concatenate([v_ctx.reshape(DEC_BATCH, PAST, -1),
                               v[T_P:].reshape(DEC_BATCH, DEC_SEQ, -1)], axis=1)
        yb_s = _attention(q, k_s.reshape(DEC_BATCH * n_keys, -1), v_s.reshape(DEC_BATCH * n_keys, -1),
                          DEC_BATCH, DEC_SEQ, n_keys, 512, T_P)
        yc_s = _fourier(zmix, dft_s, lw, DEC_BATCH, DEC_SEQ, T_P)
        ya_s, yd_s, _ = _seq_mixers(zmix, state_lru[:, l], lw, DEC_BATCH, DEC_SEQ, T_P)

        x1, h2p, top_e, top_p, rank, counts = _post(
            x, mod[l], (ya_p, yb_p, yc_p, yd_p), (ya_s, yb_s, yc_s, yd_s), lw, tri)
        dest, slot_tok, block_e, n_used = _routing_tables(
            top_e[:, :TOP_K], rank[:, :TOP_K], counts[0, :N_EXP])
        xs = _gather_rows(slot_tok, n_used, h2p)
        ys = _experts(block_e, n_used, xs, lw)
        x = _combine(dest, ys, x1, mod[l], top_p, final_g, final=(l == DEPTH - 1))

    y_prompt = x[:T_P].reshape(BATCH, SEQ, D)
    y_sample = x[T_P:].reshape(DEC_BATCH, DEC_SEQ, D)
    return (y_prompt, y_sample, jnp.stack(new_ckv, axis=1), jnp.stack(new_krope, axis=1),
            jnp.stack(new_lru, axis=1))
```

```python
import functools

import numpy as np
import jax
import jax.numpy as jnp
from jax import lax
from jax.experimental import pallas as pl
from jax.experimental.pallas import tpu as pltpu

F32 = jnp.float32
BF16 = jnp.bfloat16
I32 = jnp.int32
U32 = jnp.uint32

D = 1024
BATCH, SEQ = 32, 256
DEC_BATCH, DEC_SEQ, PAST = 2, 2048, 512
T_P = BATCH * SEQ
T_S = DEC_BATCH * DEC_SEQ
T = T_P + T_S
DEPTH = 2
GRID_W = 64
GW = 256
Q_LORA, KV_LORA, QK_NOPE, QK_ROPE, V_DIM, HEADS = 192, 128, 64, 32, 64, 4
HEAD_PAD = 128
ROPE_BASE = 10000.0
POOL_WINDOWS = (2, 4, 8, 16)
LRU_C = 8.0
N_EXP, TOP_K = 32, 4
LIMIT, ALPHA = 7.0, 1.702
EPS = 1e-6

LANES = 128
SUBLANES = 8
VMEM_LIMIT = 56 * 1024 * 1024

TM = 512
N_PT = T_P // TM
TILES_PER_DEC = DEC_SEQ // TM
TM_E = 512
N_SLOTS = T * TOP_K + N_EXP * TM_E
NB_E = N_SLOTS // TM_E
ROW_F = D // LANES
GATHER_VMEM_LIMIT = 60 * 1024 * 1024
CH_C = 16
MAX_CH = TM * TOP_K // CH_C + N_EXP
GATHER_UNROLL = 16
SEQ_PAD = 32
FRONT = 8

W_COLS = 1536
COL_Q, COL_KV, COL_KR = 1024, 1280, 1408


def _cparams(sem, vmem=VMEM_LIMIT):
    return pltpu.CompilerParams(dimension_semantics=sem, vmem_limit_bytes=vmem)


def _bdot(a, b):
    return jnp.dot(a.astype(BF16), b.astype(BF16), preferred_element_type=F32)


def _split(a):
    hi = a.astype(BF16)
    lo = (a - hi.astype(F32)).astype(BF16)
    return hi, lo


def _dot3(a, b):
    ah, al = _split(a)
    bh, bl = _split(b)
    d = functools.partial(jnp.dot, preferred_element_type=F32)
    return d(ah, bh) + (d(al, bh) + d(ah, bl))


def _rms(x, n=None):
    n = x.shape[-1] if n is None else n
    return x * lax.rsqrt(jnp.sum(x * x, axis=-1, keepdims=True) * (1.0 / n) + EPS)


def _expm1(y):
    t = jnp.tanh(0.5 * y)
    return 2.0 * t / (1.0 - t)


def _tile_row(i):
    return jnp.where(i >= N_PT, 1 + (i - N_PT) // TILES_PER_DEC, 0)


def _mod_kernel(c_ref, w_ref, b_ref, o_ref):
    s = jax.nn.silu(c_ref[...])
    o_ref[0] = _dot3(s, w_ref[0]) + b_ref[0]


def _modulation(cvec, w_mod, b_mod):
    tn = 768
    n = 6 * D
    return pl.pallas_call(
        _mod_kernel,
        out_shape=jax.ShapeDtypeStruct((DEPTH, SUBLANES, n), F32),
        grid=(DEPTH, n // tn),
        in_specs=[pl.BlockSpec((SUBLANES, D), lambda l, j: (0, 0)),
                  pl.BlockSpec((1, D, tn), lambda l, j: (l, 0, j)),
                  pl.BlockSpec((1, 1, tn), lambda l, j: (l, 0, j))],
        out_specs=pl.BlockSpec((1, SUBLANES, tn), lambda l, j: (l, 0, j)),
        compiler_params=_cparams(("parallel", "parallel")),
        name="modulation",
    )(cvec, w_mod, b_mod.reshape(DEPTH, 1, n))


def _front_kernel(x_ref, mod_ref, g_ref, w_ref, qg_ref, wuq_ref, kvg_ref, wkn_ref, wkr_ref, wv_ref,
                  cq_ref, saq_ref, sbq_ref, ck_ref, sak_ref, sbk_ref,
                  zmix_ref, q_ref, k_ref, v_ref, ckv_ref, kr_ref):
    i = pl.program_id(0)
    row = _tile_row(i)
    shift1 = mod_ref[pl.ds(row, 1), 0:D]
    scale1 = mod_ref[pl.ds(row, 1), D:2 * D]
    h = _rms(x_ref[...]) * g_ref[...]
    h = h * (1.0 + scale1) + shift1
    z = jnp.dot(h.astype(BF16), w_ref[...], preferred_element_type=F32)
    zmix_ref[...] = z[:, 0:4 * GW]
    qn = _rms(z[:, COL_Q:COL_Q + 2 * LANES], Q_LORA) * qg_ref[...]
    q = _bdot(qn, wuq_ref[...])
    ckv = _rms(z[:, COL_KV:COL_KV + KV_LORA]) * kvg_ref[...]
    ckv_ref[...] = ckv
    ckv_b = ckv.astype(BF16)
    v_ref[...] = jnp.dot(ckv_b, wv_ref[...], preferred_element_type=F32).astype(BF16)
    kn = jnp.dot(ckv_b, wkn_ref[...], preferred_element_type=F32)
    kr = z[:, COL_KR:COL_KR + LANES]

    @pl.when(i < N_PT)
    def _():
        q_ref[...] = q.astype(BF16)
        kr_ref[...] = kr[:, 0:QK_ROPE]
        k_ref[...] = (kn + _bdot(kr, wkr_ref[...])).astype(BF16)

    @pl.when(i >= N_PT)
    def _():
        wq = HEADS * HEAD_PAD
        half = QK_ROPE // 2
        qr = (q * cq_ref[...] + pltpu.roll(q, wq - half, 1) * saq_ref[...]
              + pltpu.roll(q, half, 1) * sbq_ref[...])
        krr = (kr * ck_ref[...] + pltpu.roll(kr, LANES - half, 1) * sak_ref[...]
               + pltpu.roll(kr, half, 1) * sbk_ref[...])
        q_ref[...] = qr.astype(BF16)
        kr_ref[...] = krr[:, 0:QK_ROPE]
        k_ref[...] = (kn + _bdot(krr, wkr_ref[...])).astype(BF16)


def _front(x, mod_l, lw, rope):
    full = lambda shape: pl.BlockSpec(shape, lambda i: (0,) * len(shape))
    rows = lambda w: pl.BlockSpec((TM, w), lambda i: (i, 0))
    rope_rows = lambda w: pl.BlockSpec(
        (TM, w), lambda i: (jnp.maximum(i - N_PT, 0) % TILES_PER_DEC, 0))
    wq = HEADS * HEAD_PAD
    return pl.pallas_call(
        _front_kernel,
        out_shape=(jax.ShapeDtypeStruct((T, 4 * GW), F32),
                   jax.ShapeDtypeStruct((T, wq), BF16),
                   jax.ShapeDtypeStruct((T, wq), BF16),
                   jax.ShapeDtypeStruct((T, HEADS * V_DIM), BF16),
                   jax.ShapeDtypeStruct((T, KV_LORA), F32),
                   jax.ShapeDtypeStruct((T, QK_ROPE), F32)),
        grid=(T // TM,),
        in_specs=[rows(D), full((SUBLANES, 6 * D)), full((1, D)), full((D, W_COLS)),
                  full((1, 2 * LANES)), full((2 * LANES, wq)), full((1, KV_LORA)),
                  full((KV_LORA, wq)), full((LANES, wq)), full((KV_LORA, HEADS * V_DIM)),
                  rope_rows(wq), rope_rows(wq), rope_rows(wq),
                  rope_rows(LANES), rope_rows(LANES), rope_rows(LANES)],
        out_specs=(rows(4 * GW), rows(wq), rows(wq), rows(HEADS * V_DIM), rows(KV_LORA),
                   rows(QK_ROPE)),
        compiler_params=_cparams(("parallel",)),
        name="front",
    )(x, mod_l, lw["attn_g"], lw["w_in"], lw["q_g"], lw["w_uq"], lw["kv_g"], lw["w_kn"],
      lw["w_kr"], lw["w_v"], *rope)


def _kvexp_kernel(ckv_ref, kr_ref, wkn_ref, wkr_ref, wv_ref, k_ref, v_ref):
    ckv_b = ckv_ref[...].astype(BF16)
    v_ref[...] = jnp.dot(ckv_b, wv_ref[...], preferred_element_type=F32).astype(BF16)
    kn = jnp.dot(ckv_b, wkn_ref[...], preferred_element_type=F32)
    k_ref[...] = (kn + _bdot(kr_ref[...], wkr_ref[...])).astype(BF16)


def _kv_expand(ckv, kr_pad, lw):
    n = ckv.shape[0]
    wq = HEADS * HEAD_PAD
    full = lambda shape: pl.BlockSpec(shape, lambda i: (0,) * len(shape))
    return pl.pallas_call(
        _kvexp_kernel,
        out_shape=(jax.ShapeDtypeStruct((n, wq), BF16),
                   jax.ShapeDtypeStruct((n, HEADS * V_DIM), BF16)),
        grid=(1,),
        in_specs=[full((n, KV_LORA)), full((n, LANES)), full((KV_LORA, wq)), full((LANES, wq)),
                  full((KV_LORA, HEADS * V_DIM))],
        out_specs=(full((n, wq)), full((n, HEADS * V_DIM))),
        compiler_params=_cparams(("arbitrary",)),
        name="kv_expand",
    )(ckv, kr_pad, lw["w_kn"], lw["w_kr"], lw["w_v"])


def _attn_kernel(q_ref, k_ref, v_ref, o_ref):
    scale = (QK_NOPE + QK_ROPE) ** -0.5
    outs = []
    for h in range(HEADS):
        qh = q_ref[:, h * HEAD_PAD:(h + 1) * HEAD_PAD]
        kh = k_ref[:, h * HEAD_PAD:(h + 1) * HEAD_PAD]
        s = lax.dot_general(qh, kh, (((1,), (1,)), ((), ())), preferred_element_type=F32) * scale
        m = jnp.max(s, axis=-1, keepdims=True)
        p = jnp.exp(s - m)
        l = jnp.sum(p, axis=-1, keepdims=True)
        vh = v_ref[:, h * V_DIM:(h + 1) * V_DIM]
        outs.append(jnp.dot(p.astype(BF16), vh, preferred_element_type=F32) / l)
    o_ref[...] = jnp.concatenate(outs, axis=-1)


def _attention(q, k, v, n_batch, seq, n_keys, tq, row0):
    nq = seq // tq
    wq = HEADS * HEAD_PAD
    return pl.pallas_call(
        _attn_kernel,
        out_shape=jax.ShapeDtypeStruct((n_batch * seq, GW), F32),
        grid=(n_batch, nq),
        in_specs=[pl.BlockSpec((tq, wq), lambda b, i: (row0 // tq + b * nq + i, 0)),
                  pl.BlockSpec((n_keys, wq), lambda b, i: (b, 0)),
                  pl.BlockSpec((n_keys, HEADS * V_DIM), lambda b, i: (b, 0))],
        out_specs=pl.BlockSpec((tq, GW), lambda b, i: (b * nq + i, 0)),
        compiler_params=_cparams(("parallel", "parallel")),
        name="attention",
    )(q, k, v)


def _fourier_kernel(z_ref, c_ref, s_ref, bdc_ref, bds_ref, w_ref, o_ref, xc_ref, xs_ref, *, norm):
    b = pl.program_id(1)

    @pl.when(pl.program_id(0) == 0)
    def _():
        xb = z_ref[...].astype(BF16)
        xc_ref[b] = _bdot(xb, bdc_ref[...]).astype(BF16)
        xs_ref[b] = _bdot(xb, bds_ref[...]).astype(BF16)

    f = (jnp.dot(c_ref[...].astype(BF16), xc_ref[b], preferred_element_type=F32)
         - jnp.dot(s_ref[...].astype(BF16), xs_ref[b], preferred_element_type=F32)) * norm
    o_ref[...] = _bdot(f, w_ref[...])


def _fourier(zmix, consts, lw, n_batch, seq, row0):
    ts = min(seq, 512)
    nj = seq // ts
    cmat, smat = consts
    full = lambda shape: pl.BlockSpec(shape, lambda j, b: (0,) * len(shape))
    z_spec = pl.BlockSpec((seq, GW), lambda j, b: (row0 // seq + jnp.where(j == 0, b, n_batch - 1), 1))
    return pl.pallas_call(
        functools.partial(_fourier_kernel, norm=float((seq * (GW // 4)) ** -0.5)),
        out_shape=jax.ShapeDtypeStruct((n_batch * seq, GW), F32),
        grid=(nj, n_batch),
        in_specs=[z_spec,
                  pl.BlockSpec((ts, seq), lambda j, b: (j, 0)),
                  pl.BlockSpec((ts, seq), lambda j, b: (j, 0)),
                  full((GW, GW)), full((GW, GW)), full((GW, GW))],
        out_specs=pl.BlockSpec((ts, GW), lambda j, b: (b * nj + j, 0)),
        scratch_shapes=[pltpu.VMEM((n_batch, seq, GW), BF16), pltpu.VMEM((n_batch, seq, GW), BF16)],
        compiler_params=_cparams(("arbitrary", "arbitrary")),
        name="fourier",
    )(zmix, cmat, smat, lw["bd_c"], lw["bd_s"], lw["w_f"])


def _seq_kernel(za_ref, zr_ref, zg_ref, h0_ref, wp_ref, ps_ref, cw_ref, cb_ref, wa_ref, ba_ref,
                wx_ref, bx_ref, lam_ref, ya_ref, yd_ref, st_ref,
                pa_ref, pb_ref, xp_ref, a_ref, b_ref, *, seq):
    n = seq + SEQ_PAD
    span = seq + 2 * FRONT
    zeros_pad = jnp.zeros((n, GW), F32)

    za = za_ref[...]
    pa_ref[...] = zeros_pad
    pb_ref[...] = zeros_pad
    pa_ref[FRONT:FRONT + seq, :] = za
    pb_ref[0:span, :] = pa_ref[0:span, :] + pa_ref[1:span + 1, :]
    win2 = pb_ref[FRONT - 1:FRONT - 1 + seq, :]
    pa_ref[0:span, :] = pb_ref[0:span, :] + pb_ref[2:span + 2, :]
    win4 = pa_ref[FRONT - 2:FRONT - 2 + seq, :]
    pb_ref[0:span, :] = pa_ref[0:span, :] + pa_ref[4:span + 4, :]
    win8 = pb_ref[FRONT - 4:FRONT - 4 + seq, :]
    pa_ref[0:span, :] = pb_ref[0:span, :] + pb_ref[8:span + 8, :]
    win16 = pa_ref[FRONT - 8:FRONT - 8 + seq, :]
    pos = lax.broadcasted_iota(I32, (seq, GW), 0)
    grp = lax.broadcasted_iota(I32, (seq, GW), 1) // (GW // 4)
    half = jnp.where(grp == 0, 1, jnp.where(grp == 1, 2, jnp.where(grp == 2, 4, 8)))
    cnt = (jnp.minimum(pos + half, seq) - jnp.maximum(pos - half, 0)).astype(F32)
    win = jnp.where(grp == 0, win2, jnp.where(grp == 1, win4, jnp.where(grp == 2, win8, win16)))
    dlt = win / cnt - za
    ya_ref[...] = _bdot(dlt, wp_ref[...]) * ps_ref[...]

    xp_ref[...] = zeros_pad
    xp_ref[FRONT:FRONT + seq, :] = zr_ref[...]
    xc = cb_ref[...] + cw_ref[0:1, :] * xp_ref[FRONT - 2:FRONT - 2 + seq, :]
    for kk in range(1, 4):
        xc = xc + cw_ref[kk:kk + 1, :] * xp_ref[FRONT - 2 + kk:FRONT - 2 + kk + seq, :]
    xcb = xc.astype(BF16)
    n_grp = seq // SUBLANES
    sub = lax.broadcasted_iota(I32, (n_grp, SUBLANES, GW), 1)

    total = None
    for d in range(2):
        r = jax.nn.sigmoid(jnp.dot(xcb, wa_ref[d], preferred_element_type=F32) + ba_ref[d])
        ig = jax.nn.sigmoid(jnp.dot(xcb, wx_ref[d], preferred_element_type=F32) + bx_ref[d])
        log_a = (-LRU_C) * r * jax.nn.softplus(-lam_ref[d])
        a = jnp.exp(log_a).reshape(n_grp, SUBLANES, GW)
        b = (jnp.sqrt(-_expm1(2.0 * log_a)) * (ig * xc)).reshape(n_grp, SUBLANES, GW)
        for k in (1, 2, 4):
            shift = k if d == 0 else SUBLANES - k
            m = sub >= k if d == 0 else sub < SUBLANES - k
            ap, bp = pltpu.roll(a, shift, 1), pltpu.roll(b, shift, 1)
            b = jnp.where(m, a * bp + b, b)
            a = jnp.where(m, a * ap, a)
        a_ref[...] = a.reshape(seq, GW)
        b_ref[...] = b.reshape(seq, GW)

        def step(g, carry, d=d):
            gi = g if d == 0 else n_grp - 1 - g
            off = pl.multiple_of(gi * SUBLANES, SUBLANES)
            hh = a_ref[pl.ds(off, SUBLANES), :] * carry + b_ref[pl.ds(off, SUBLANES), :]
            b_ref[pl.ds(off, SUBLANES), :] = hh
            edge = hh[SUBLANES - 1:SUBLANES, :] if d == 0 else hh[0:1, :]
            return jnp.broadcast_to(edge, (SUBLANES, GW))

        last = lax.fori_loop(0, n_grp, step, jnp.broadcast_to(h0_ref[0, d:d + 1, :], (SUBLANES, GW)))
        st_ref[0, d:d + 1, :] = last[0:1, :]
        total = b_ref[...] if total is None else total + b_ref[...]

    yd_ref[...] = total * jax.nn.gelu(zg_ref[...])


def _seq_mixers(zmix, h0, lw, n_batch, seq, row0):
    full = lambda shape: pl.BlockSpec(shape, lambda b: (0,) * len(shape))
    col = lambda c: pl.BlockSpec((seq, GW), lambda b: (row0 // seq + b, c))
    out_rows = pl.BlockSpec((seq, GW), lambda b: (b, 0))
    pad = pltpu.VMEM((seq + SEQ_PAD, GW), F32)
    return pl.pallas_call(
        functools.partial(_seq_kernel, seq=seq),
        out_shape=(jax.ShapeDtypeStruct((n_batch * seq, GW), F32),
                   jax.ShapeDtypeStruct((n_batch * seq, GW), F32),
                   jax.ShapeDtypeStruct((n_batch, 2, GW), F32)),
        grid=(n_batch,),
        in_specs=[col(0), col(2), col(3), pl.BlockSpec((1, 2, GW), lambda b: (b, 0, 0)),
                  full((GW, GW)), full((1, GW)), full((4, GW)), full((1, GW)),
                  full((2, GW, GW)), full((2, 1, GW)), full((2, GW, GW)), full((2, 1, GW)),
                  full((2, 1, GW))],
        out_specs=(out_rows, out_rows, pl.BlockSpec((1, 2, GW), lambda b: (b, 0, 0))),
        scratch_shapes=[pad, pad, pad, pltpu.VMEM((seq, GW), F32), pltpu.VMEM((seq, GW), F32)],
        compiler_params=_cparams(("parallel",)),
        name="seq_mixers",
    )(zmix, zmix, zmix, h0, lw["bd_pool"], lw["pool_scale"], lw["conv_w"], lw["conv_b"],
      lw["bd_wa"], lw["lru_ba"], lw["bd_wx"], lw["lru_bx"], lw["lru_lam"])


def _post_kernel(x_ref, mod_ref, yap_ref, ybp_ref, ycp_ref, ydp_ref, yas_ref, ybs_ref, ycs_ref,
                 yds_ref, og_ref, wo_ref, fg_ref, rw_ref, rb_ref, tri_ref, upper_ref,
                 x1_ref, h2p_ref, route_ref, prob_ref, tab_ref, cnt_ref, carry_ref):
    i = pl.program_id(0)
    is_s = i >= N_PT
    row = _tile_row(i)
    gate1 = mod_ref[pl.ds(row, 1), 2 * D:3 * D]
    shift2 = mod_ref[pl.ds(row, 1), 3 * D:4 * D]
    scale2 = mod_ref[pl.ds(row, 1), 4 * D:5 * D]

    @pl.when(i == 0)
    def _():
        carry_ref[...] = jnp.zeros_like(carry_ref)

    groups = []
    for gi, (p_ref, s_ref) in enumerate(((yap_ref, yas_ref), (ybp_ref, ybs_ref),
                                         (ycp_ref, ycs_ref), (ydp_ref, yds_ref))):
        y = jnp.where(is_s, s_ref[...], p_ref[...])
        groups.append((_rms(y) * og_ref[gi:gi + 1, :]).astype(BF16))
    ycat = jnp.concatenate(groups, axis=-1)
    x1 = x_ref[...] + gate1 * jnp.dot(ycat, wo_ref[...], preferred_element_type=F32)
    x1_ref[...] = x1
    h2 = _rms(x1) * fg_ref[...]
    h2 = h2 * (1.0 + scale2) + shift2

    for j in range(ROW_F):
        h2p_ref[pl.ds(j, TM, stride=ROW_F), :] = h2[:, j * LANES:(j + 1) * LANES]

    logits = _dot3(h2, rw_ref[...]) + rb_ref[...]
    lane = lax.broadcasted_iota(I32, (TM, LANES), 1)
    lane_f = lane.astype(F32)
    neg = jnp.float32(-jnp.inf)
    cur = jnp.where(lane < N_EXP, logits, neg)
    sel, vals, idxs = [], [], []
    for _ in range(TOP_K):
        m = jnp.max(cur, axis=-1, keepdims=True)
        idx = jnp.min(jnp.where(cur == m, lane_f, float(LANES)), axis=-1, keepdims=True)
        hit = lane_f == idx
        sel.append(hit)
        vals.append(m)
        idxs.append(idx)
        cur = jnp.where(hit, neg, cur)
    exps = [jnp.exp(v - vals[0]) for v in vals]
    denom = exps[0] + exps[1] + exps[2] + exps[3]
    onehot = jnp.where(sel[0] | sel[1] | sel[2] | sel[3], 1.0, 0.0)
    cum_l = jnp.dot(tri_ref[...], onehot.astype(BF16), preferred_element_type=F32)
    before = carry_ref[0:1, :]
    cum = cum_l + before
    n_tile = cum_l[TM - 1:TM, :]
    nch = jnp.floor((n_tile + (CH_C - 1.0)) * (1.0 / CH_C))
    nch8 = jnp.broadcast_to(nch, (SUBLANES, LANES))
    base = jnp.dot(nch8.astype(BF16), upper_ref[...], preferred_element_type=F32)[0:1, :]
    pk = jnp.zeros((TM, LANES), I32)
    pf = jnp.zeros((TM, LANES), F32)
    for k in range(TOP_K):
        rank = jnp.sum(jnp.where(sel[k], cum - 1.0, 0.0), axis=-1, keepdims=True).astype(I32)
        local = jnp.sum(jnp.where(sel[k], base * CH_C + cum_l - 1.0, 0.0), axis=-1,
                        keepdims=True).astype(I32)
        pk = jnp.where(lane == k, idxs[k].astype(I32), pk)
        pk = jnp.where(lane == TOP_K + k, rank, pk)
        pk = jnp.where(lane == 2 * TOP_K + k, local, pk)
        pf = jnp.where(lane == k, exps[k] / denom, pf)
    route_ref[...] = pk.T[0:4 * TOP_K, :]
    prob_ref[...] = pf.T[0:SUBLANES, :]
    row8 = lax.broadcasted_iota(I32, (SUBLANES, LANES), 0)
    tab_ref[...] = jnp.where(row8 == 0, before, jnp.where(row8 == 1, nch, base))
    new_carry = jnp.broadcast_to(cum[TM - 1:TM, :], (SUBLANES, LANES))
    carry_ref[...] = new_carry
    cnt_ref[...] = new_carry


def _post(x, mod_l, ys_prompt, ys_sample, lw, tri, upper):
    full = lambda shape: pl.BlockSpec(shape, lambda i: (0,) * len(shape))
    rows = lambda w: pl.BlockSpec((TM, w), lambda i: (i, 0))
    prow = pl.BlockSpec((TM, GW), lambda i: (jnp.minimum(i, N_PT - 1), 0))
    srow = pl.BlockSpec((TM, GW), lambda i: (jnp.maximum(i - N_PT, 0), 0))
    return pl.pallas_call(
        _post_kernel,
        out_shape=(jax.ShapeDtypeStruct((T, D), F32),
                   jax.ShapeDtypeStruct((T * ROW_F, LANES), F32),
                   jax.ShapeDtypeStruct((4 * TOP_K, T), I32),
                   jax.ShapeDtypeStruct((SUBLANES, T), F32),
                   jax.ShapeDtypeStruct((T // TM * SUBLANES, LANES), F32),
                   jax.ShapeDtypeStruct((SUBLANES, LANES), F32)),
        grid=(T // TM,),
        in_specs=[rows(D), full((SUBLANES, 6 * D)), prow, prow, prow, prow, srow, srow, srow, srow,
                  full((4, GW)), full((D, D)), full((1, D)), full((D, LANES)), full((1, LANES)),
                  full((TM, TM)), full((LANES, LANES))],
        out_specs=(rows(D), pl.BlockSpec((TM * ROW_F, LANES), lambda i: (i, 0)),
                   pl.BlockSpec((4 * TOP_K, TM), lambda i: (0, i)),
                   pl.BlockSpec((SUBLANES, TM), lambda i: (0, i)),
                   pl.BlockSpec((SUBLANES, LANES), lambda i: (i, 0)), full((SUBLANES, LANES))),
        scratch_shapes=[pltpu.VMEM((SUBLANES, LANES), F32)],
        compiler_params=_cparams(("arbitrary",)),
        name="post",
    )(x, mod_l, *ys_prompt, *ys_sample, lw["out_g"], lw["w_out"], lw["ffn_g"], lw["router_w"],
      lw["router_b"], tri, upper)


def _gather_kernel(dest_ref, nused_ref, zeros_hbm, src_ref, o_ref, stok_ref, sem):
    i = pl.program_id(0)

    @pl.when(i == 0)
    def _():
        init = pltpu.make_async_copy(zeros_hbm, stok_ref, sem)
        init.start()
        init.wait()

        def scatter(t8, c):
            for k in range(TOP_K):
                for u in range(SUBLANES):
                    tok = t8 * SUBLANES + u
                    stok_ref[dest_ref[k * T + tok]] = tok
            return c
        lax.fori_loop(0, T // SUBLANES, scatter, 0)

    @pl.when(i < nused_ref[0])
    def _():
        def rows(r16, c):
            r0 = pl.multiple_of(r16 * GATHER_UNROLL, GATHER_UNROLL)
            for u in range(GATHER_UNROLL):
                tok = stok_ref[i * TM_E + r0 + u]
                src = pl.multiple_of(tok * ROW_F, ROW_F)
                dst = pl.multiple_of(r0 * ROW_F, GATHER_UNROLL * ROW_F) + u * ROW_F
                o_ref[pl.ds(dst, ROW_F), :] = src_ref[pl.ds(src, ROW_F), :]
            return c
        lax.fori_loop(0, TM_E // GATHER_UNROLL, rows, 0)

    @pl.when(i >= nused_ref[0])
    def _():
        o_ref[...] = jnp.zeros_like(o_ref)


def _gather_rows(dest, n_used, h2lin):
    return pl.pallas_call(
        _gather_kernel,
        out_shape=jax.ShapeDtypeStruct((N_SLOTS * ROW_F, LANES), F32),
        grid_spec=pltpu.PrefetchScalarGridSpec(
            num_scalar_prefetch=2,
            grid=(NB_E,),
            in_specs=[pl.BlockSpec(memory_space=pl.ANY),
                      pl.BlockSpec(memory_space=pltpu.VMEM)],
            out_specs=pl.BlockSpec((TM_E * ROW_F, LANES), lambda i, d, nu: (i, 0)),
            scratch_shapes=[pltpu.SMEM((N_SLOTS,), I32), pltpu.SemaphoreType.DMA(())]),
        compiler_params=_cparams(("arbitrary",), GATHER_VMEM_LIMIT),
        name="moe_gather",
    )(dest, n_used, jnp.zeros((N_SLOTS,), I32), h2lin)


def _expert_rows(n, x_ref, w_ref, slot, bg_ref, bu_ref, bd_ref, o_ref):
    x = jnp.concatenate([x_ref[pl.ds(j, n, stride=ROW_F), :].astype(BF16) for j in range(ROW_F)],
                        axis=-1)
    g = jnp.dot(x, w_ref[slot, 0].astype(BF16), preferred_element_type=F32) + bg_ref[0, 0]
    u = jnp.dot(x, w_ref[slot, 1].astype(BF16), preferred_element_type=F32) + bu_ref[0, 0]
    g = jnp.minimum(g, LIMIT)
    u = jnp.clip(u, -LIMIT, LIMIT)
    act = (u + 1.0) * (g * jax.nn.sigmoid(ALPHA * g))
    y = jnp.dot(act.astype(BF16), w_ref[slot, 2].astype(BF16),
                preferred_element_type=F32) + bd_ref[0, 0]
    for j in range(ROW_F):
        o_ref[pl.ds(j, n, stride=ROW_F), :] = y[:, j * LANES:(j + 1) * LANES]
    if n < TM_E:
        o_ref[n * ROW_F:TM_E * ROW_F, :] = jnp.zeros(((TM_E - n) * ROW_F, LANES), F32)


def _expert_kernel(be_ref, nused_ref, valid_ref, first_ref, next_ref, par_ref,
                   x_ref, wg_hbm, bg_ref, wu_hbm, bu_ref, wd_hbm, bd_ref, o_ref, w_ref, sems, *, layer):
    i = pl.program_id(0)
    ib = jnp.minimum(i, NB_E - 1)
    live = i < nused_ref[0]
    valid = valid_ref[ib]
    expert = be_ref[ib]
    slot = par_ref[ib]

    def fetch(e, s):
        return [pltpu.make_async_copy(src.at[layer, e], w_ref.at[s, m], sems.at[s, m])
                for m, src in enumerate((wg_hbm, wu_hbm, wd_hbm))]

    @pl.when(i == 0)
    def _():
        for cp in fetch(expert, slot):
            cp.start()

    @pl.when(live & (first_ref[ib] == 1))
    def _():
        @pl.when(next_ref[ib] >= 0)
        def _():
            for cp in fetch(next_ref[ib], 1 - slot):
                cp.start()

        for cp in fetch(expert, slot):
            cp.wait()

    @pl.when(live & (valid > TM_E // 2))
    def _():
        _expert_rows(TM_E, x_ref, w_ref, slot, bg_ref, bu_ref, bd_ref, o_ref)

    @pl.when(live & (valid <= TM_E // 2))
    def _():
        _expert_rows(TM_E // 2, x_ref, w_ref, slot, bg_ref, bu_ref, bd_ref, o_ref)

    @pl.when(jnp.logical_not(live))
    def _():
        o_ref[...] = jnp.zeros_like(o_ref)


def _experts(l, sched, xs, p):
    last = NB_E - 1
    bspec = pl.BlockSpec((1, 1, 1, D), lambda i, be, *_: (l, be[jnp.minimum(i, last)], 0, 0))
    hbm = pl.BlockSpec(memory_space=pl.ANY)
    bias = lambda b: b.reshape(DEPTH, N_EXP, 1, D)
    return pl.pallas_call(
        functools.partial(_expert_kernel, layer=l),
        out_shape=jax.ShapeDtypeStruct(((NB_E + 1) * TM_E * ROW_F, LANES), F32),
        grid_spec=pltpu.PrefetchScalarGridSpec(
            num_scalar_prefetch=len(sched),
            grid=(NB_E + 1,),
            in_specs=[pl.BlockSpec((TM_E * ROW_F, LANES), lambda i, *_: (jnp.minimum(i, last), 0)),
                      hbm, bspec, hbm, bspec, hbm, bspec],
            out_specs=pl.BlockSpec((TM_E * ROW_F, LANES), lambda i, *_: (i, 0)),
            scratch_shapes=[pltpu.VMEM((2, 3, D, D), F32), pltpu.SemaphoreType.DMA((2, 3))]),
        compiler_params=_cparams(("arbitrary",)),
        name="moe_experts",
    )(*sched, xs, p["w_gate"], bias(p["b_gate"]), p["w_up"], bias(p["b_up"]),
      p["w_down"], bias(p["b_down"]))


def _combine_kernel(loc_ref, p_ref, csrc_ref, nch_ref, ys_hbm, x_ref, mod_ref, fg_ref, *rest, final):
    *outs, buf_ref, acc_ref, sems = rest
    tb = pl.program_id(0)
    n_tb = pl.num_programs(0)
    slot = tb % 2
    row = _tile_row(tb)
    gate2 = mod_ref[pl.ds(row, 1), 5 * D:6 * D]
    chunk_rows = CH_C * ROW_F

    def chunk_copy(t, c, s):
        src = pl.multiple_of(csrc_ref[t * MAX_CH + c] * ROW_F, ROW_F)
        dst = pl.multiple_of(c * chunk_rows, chunk_rows)
        return pltpu.make_async_copy(ys_hbm.at[pl.ds(src, chunk_rows)],
                                     buf_ref.at[s, pl.ds(dst, chunk_rows)], sems.at[s])

    def issue(t, s):
        def body(c, carry):
            chunk_copy(t, c, s).start()
            return carry
        lax.fori_loop(0, nch_ref[t], body, 0)

    @pl.when(tb == 0)
    def _():
        issue(0, 0)

    @pl.when(tb + 1 < n_tb)
    def _():
        issue(tb + 1, 1 - slot)

    def drain(c, carry):
        chunk_copy(tb, c, slot).wait()
        return carry
    lax.fori_loop(0, nch_ref[tb], drain, 0)

    def tokens(r8, carry):
        r0 = pl.multiple_of(r8 * SUBLANES, SUBLANES)
        for u in range(SUBLANES):
            tok = tb * TM + r0 + u
            acc = None
            for k in range(TOP_K):
                off = pl.multiple_of(loc_ref[k * T + tok] * ROW_F, ROW_F)
                term = buf_ref[slot, pl.ds(off, ROW_F), :] * p_ref[k * T + tok]
                acc = term if acc is None else acc + term
            dst = pl.multiple_of(r0 * ROW_F, SUBLANES * ROW_F) + u * ROW_F
            acc_ref[pl.ds(dst, ROW_F), :] = acc
        return carry
    lax.fori_loop(0, TM // SUBLANES, tokens, 0)

    moe = jnp.concatenate([acc_ref[pl.ds(j, TM, stride=ROW_F), :] for j in range(ROW_F)], axis=-1)
    x2 = x_ref[...] + gate2 * moe
    if final:
        x2 = _rms(x2) * fg_ref[...]
        prompt_ref, sample_ref = outs

        @pl.when(tb < N_PT)
        def _():
            prompt_ref[...] = x2

        @pl.when(tb >= N_PT)
        def _():
            sample_ref[...] = x2
    else:
        outs[0][...] = x2


def _combine(loc, top_p, chunk_src, n_chunks, ys, x1, mod_l, final_g, final):
    full = lambda shape: pl.BlockSpec(shape, lambda i, *_: (0,) * len(shape))
    rows = lambda w: pl.BlockSpec((TM, w), lambda i, *_: (i, 0))
    if final:
        out_shape = (jax.ShapeDtypeStruct((T_P, D), F32), jax.ShapeDtypeStruct((T_S, D), F32))
        out_specs = (pl.BlockSpec((TM, D), lambda i, *_: (jnp.minimum(i, N_PT - 1), 0)),
                     pl.BlockSpec((TM, D), lambda i, *_: (jnp.maximum(i - N_PT, 0), 0)))
    else:
        out_shape, out_specs = jax.ShapeDtypeStruct((T, D), F32), rows(D)
    return pl.pallas_call(
        functools.partial(_combine_kernel, final=final),
        out_shape=out_shape,
        grid_spec=pltpu.PrefetchScalarGridSpec(
            num_scalar_prefetch=4,
            grid=(T // TM,),
            in_specs=[pl.BlockSpec(memory_space=pl.ANY), rows(D), full((SUBLANES, 6 * D)),
                      full((1, D))],
            out_specs=out_specs,
            scratch_shapes=[pltpu.VMEM((2, MAX_CH * CH_C * ROW_F, LANES), F32),
                            pltpu.VMEM((TM * ROW_F, LANES), F32),
                            pltpu.SemaphoreType.DMA((2,))]),
        compiler_params=_cparams(("arbitrary",)),
        name="moe_combine",
    )(loc, top_p, chunk_src, n_chunks, ys, x1, mod_l, final_g)


def _rope_tables():
    rows = DEC_SEQ // GRID_W
    r = np.repeat(np.arange(rows, dtype=np.float64), GRID_W)
    c = np.tile(np.arange(GRID_W, dtype=np.float64), rows)
    n_freq = QK_ROPE // 4
    inv = (np.float32(ROPE_BASE) ** (-np.arange(n_freq, dtype=np.float32) / n_freq)).astype(np.float64)
    ang = np.concatenate([r[:, None] * inv, c[:, None] * inv], axis=-1).astype(np.float32)
    cos, sin = np.cos(ang.astype(np.float64)), np.sin(ang.astype(np.float64))
    half = QK_ROPE // 2

    def place(width, start):
        cf = np.ones((DEC_SEQ, width), np.float32)
        sa = np.zeros((DEC_SEQ, width), np.float32)
        sb = np.zeros((DEC_SEQ, width), np.float32)
        for s0 in start:
            cf[:, s0:s0 + half] = cos
            cf[:, s0 + half:s0 + 2 * half] = cos
            sa[:, s0:s0 + half] = -sin
            sb[:, s0 + half:s0 + 2 * half] = sin
        return jnp.asarray(cf), jnp.asarray(sa), jnp.asarray(sb)

    return (*place(HEADS * HEAD_PAD, [h * HEAD_PAD + QK_NOPE for h in range(HEADS)]),
            *place(LANES, [0]))


def _dft_tables(seq):
    kn = (np.arange(seq, dtype=np.int64)[:, None] * np.arange(seq, dtype=np.int64)[None, :]) % seq
    ang = 2.0 * np.pi * kn.astype(np.float64) / seq
    return jnp.asarray(np.cos(ang), F32), jnp.asarray(np.sin(ang), F32)


def _block_diag(blocks):
    g, n, _ = blocks.shape
    eye = jnp.eye(g, dtype=blocks.dtype)
    return jnp.einsum("gij,gh->gihj", blocks, eye).reshape(g * n, g * n)


def _layer_weights(l, p):
    w_in = p["w_in"][l]
    za, zq, zkv, zkr, zf, zr, zg = jnp.split(
        w_in, np.cumsum([GW, Q_LORA, KV_LORA, QK_ROPE, GW, GW])[:], axis=1)
    zpad = lambda n: jnp.zeros((D, n), F32)
    w_cols = jnp.concatenate([za, zf, zr, zg, zq, zpad(2 * LANES - Q_LORA), zkv, zkr,
                              zpad(LANES - QK_ROPE)], axis=1).astype(BF16)
    wq = HEADS * HEAD_PAD
    w_uq = p["w_uq"][l].reshape(Q_LORA, HEADS, QK_NOPE + QK_ROPE)
    w_uq = jnp.pad(w_uq, ((0, 2 * LANES - Q_LORA), (0, 0), (0, HEAD_PAD - QK_NOPE - QK_ROPE)))
    w_ukv = p["w_ukv"][l].reshape(KV_LORA, HEADS, QK_NOPE + V_DIM)
    w_kn = jnp.pad(w_ukv[:, :, :QK_NOPE], ((0, 0), (0, 0), (0, HEAD_PAD - QK_NOPE)))
    w_v = w_ukv[:, :, QK_NOPE:]
    place = np.zeros((LANES, HEADS, HEAD_PAD), np.float32)
    for h in range(HEADS):
        place[np.arange(QK_ROPE), h, QK_NOPE + np.arange(QK_ROPE)] = 1.0
    c64 = np.arange(GW // 4, dtype=np.int64)
    ang = 2.0 * np.pi * ((c64[:, None] * c64[None, :]) % (GW // 4)).astype(np.float64) / (GW // 4)
    four = lambda m: jnp.asarray(np.broadcast_to(m, (4,) + m.shape), F32)
    router_w = jnp.pad(p["router_w"][l], ((0, 0), (0, LANES - N_EXP)))
    router_b = jnp.pad(p["router_b"][l], (0, LANES - N_EXP)).reshape(1, LANES)
    return {
        "attn_g": p["attn_norm_g"][l].reshape(1, D),
        "w_in": w_cols,
        "q_g": jnp.pad(p["q_norm_g"][l], (0, 2 * LANES - Q_LORA)).reshape(1, 2 * LANES),
        "w_uq": w_uq.reshape(2 * LANES, wq).astype(BF16),
        "kv_g": p["kv_norm_g"][l].reshape(1, KV_LORA),
        "w_kn": w_kn.reshape(KV_LORA, wq).astype(BF16),
        "w_kr": jnp.asarray(place.reshape(LANES, wq), BF16),
        "w_v": w_v.reshape(KV_LORA, HEADS * V_DIM).astype(BF16),
        "bd_c": _block_diag(four(np.cos(ang))),
        "bd_s": _block_diag(four(np.sin(ang))),
        "w_f": p["fourier_w"][l].astype(BF16),
        "bd_pool": _block_diag(p["pool_w"][l]).astype(BF16),
        "pool_scale": p["pool_scale"][l].reshape(1, GW),
        "conv_w": p["conv_w"][l],
        "conv_b": p["conv_b"][l].reshape(1, GW),
        "bd_wa": jnp.stack([_block_diag(p["lru_wa"][l, d]) for d in range(2)]).astype(BF16),
        "bd_wx": jnp.stack([_block_diag(p["lru_wx"][l, d]) for d in range(2)]).astype(BF16),
        "lru_ba": p["lru_ba"][l].reshape(2, 1, GW),
        "lru_bx": p["lru_bx"][l].reshape(2, 1, GW),
        "lru_lam": p["lru_lambda"][l].reshape(2, 1, GW),
        "out_g": p["out_norm_g"][l],
        "w_out": p["w_out"][l].astype(BF16),
        "ffn_g": p["ffn_norm_g"][l].reshape(1, D),
        "router_w": router_w,
        "router_b": router_b,
    }


def _routing_tables(route, prob, counts, tab):
    top_e, rank = route[0:TOP_K], route[TOP_K:2 * TOP_K]
    top_p = prob[0:TOP_K].reshape(-1)
    loc = route[2 * TOP_K:3 * TOP_K].reshape(-1)
    counts = counts.astype(I32)
    padded = (counts + TM_E - 1) // TM_E * TM_E
    pad_ends = jnp.cumsum(padded)
    pad_starts = pad_ends - padded
    experts = jnp.arange(N_EXP, dtype=I32)
    onehot = top_e[:, :, None] == experts
    dest = (jnp.sum(jnp.where(onehot, pad_starts, 0), axis=-1) + rank).reshape(T * TOP_K)
    n_used = (pad_ends[-1] // TM_E).astype(I32)
    blk = jnp.minimum(jnp.arange(NB_E, dtype=I32), n_used - 1) * TM_E
    block_e = jnp.minimum(jnp.sum(pad_ends[None, :] <= blk[:, None], axis=-1), N_EXP - 1).astype(I32)
    of_block = lambda a: jnp.sum(jnp.where(block_e[:, None] == experts, a, 0), axis=-1)
    valid = jnp.clip(of_block(pad_starts + counts) - blk, 0, TM_E).astype(I32)
    first = (blk == of_block(pad_starts)).astype(I32)
    later = (experts[None, :] > experts[:, None]) & (counts[None, :] > 0)
    next_of = jnp.min(jnp.where(later, experts[None, :], N_EXP), axis=-1)
    next_e = of_block(jnp.where(next_of < N_EXP, next_of, -1)).astype(I32)
    parity = (of_block(jnp.cumsum((counts > 0).astype(I32))) % 2).astype(I32)
    sched = (block_e, n_used.reshape(1), valid, first, next_e, parity)
    tab = tab.reshape(T // TM, SUBLANES, LANES)[:, :, :N_EXP].astype(I32)
    before, nch, base = tab[:, 0], tab[:, 1], tab[:, 2]
    run_start = pad_starts[None, :] + before
    ends = base + nch
    ci = jnp.arange(MAX_CH, dtype=I32)
    e_of = jnp.minimum(jnp.sum(ends[:, None, :] <= ci[None, :, None], axis=-1), N_EXP - 1)
    pick = lambda a: jnp.sum(jnp.where(e_of[:, :, None] == experts, a[:, None, :], 0), axis=-1)
    chunk_src = jnp.clip(pick(run_start) + (ci[None, :] - pick(base)) * CH_C, 0, N_SLOTS)
    return dest, sched, loc, top_p, chunk_src.reshape(-1), ends[:, N_EXP - 1]


def kernel(x_prompt, x_sample, cache_ckv, cache_krope, state_lru, c, c_ctx, w_mod, b_mod, attn_norm_g, w_in, pool_w, pool_scale, q_norm_g, w_uq, kv_norm_g, w_ukv, fourier_w, conv_w, conv_b, lru_wa, lru_ba, lru_wx, lru_bx, lru_lambda, out_norm_g, w_out, ffn_norm_g, router_w, router_b, w_gate, b_gate, w_up, b_up, w_down, b_down, final_norm_g):
    params = dict(attn_norm_g=attn_norm_g, w_in=w_in, pool_w=pool_w, pool_scale=pool_scale,
                  q_norm_g=q_norm_g, w_uq=w_uq, kv_norm_g=kv_norm_g, w_ukv=w_ukv,
                  fourier_w=fourier_w, conv_w=conv_w, conv_b=conv_b, lru_wa=lru_wa, lru_ba=lru_ba,
                  lru_wx=lru_wx, lru_bx=lru_bx, lru_lambda=lru_lambda, out_norm_g=out_norm_g,
                  w_out=w_out, ffn_norm_g=ffn_norm_g, router_w=router_w, router_b=router_b,
                  w_gate=w_gate, b_gate=b_gate, w_up=w_up, b_up=b_up, w_down=w_down, b_down=b_down)
    x = jnp.concatenate([x_prompt.reshape(T_P, D), x_sample.reshape(T_S, D)], axis=0)
    cvec = jnp.concatenate([c_ctx[None, :], c, jnp.zeros((SUBLANES - 1 - DEC_BATCH, D), F32)], axis=0)
    mod = _modulation(cvec, w_mod, b_mod)
    rope = _rope_tables()
    dft_p, dft_s = _dft_tables(SEQ), _dft_tables(DEC_SEQ)
    tri = jnp.asarray(np.tril(np.ones((TM, TM), np.float32)), BF16)
    upper = jnp.asarray(np.triu(np.ones((LANES, LANES), np.float32), 1), BF16)
    final_g = final_norm_g.reshape(1, D)
    h0_prompt = jnp.zeros((BATCH, 2, GW), F32)
    n_keys = PAST + DEC_SEQ

    new_ckv, new_krope, new_lru = [], [], []
    for l in range(DEPTH):
        lw = _layer_weights(l, params)
        zmix, q, k, v, ckv, kr = _front(x, mod[l], lw, rope)
        new_ckv.append(ckv[:T_P].reshape(BATCH, SEQ, KV_LORA))
        new_krope.append(kr[:T_P].reshape(BATCH, SEQ, QK_ROPE))

        yb_p = _attention(q, k, v, BATCH, SEQ, SEQ, SEQ, 0)
        yc_p = _fourier(zmix, dft_p, lw, BATCH, SEQ, 0)
        ya_p, yd_p, st_p = _seq_mixers(zmix, h0_prompt, lw, BATCH, SEQ, 0)
        new_lru.append(st_p)

        kr_ctx = jnp.pad(cache_krope[:, l].reshape(DEC_BATCH * PAST, QK_ROPE),
                         ((0, 0), (0, LANES - QK_ROPE)))
        k_ctx, v_ctx = _kv_expand(cache_ckv[:, l].reshape(DEC_BATCH * PAST, KV_LORA), kr_ctx, lw)
        k_s = jnp.concatenate([k_ctx.reshape(DEC_BATCH, PAST, -1),
                               k[T_P:].reshape(DEC_BATCH, DEC_SEQ, -1)], axis=1)
        v_s = jnp.concatenate([v_ctx.reshape(DEC_BATCH, PAST, -1),
                               v[T_P:].reshape(DEC_BATCH, DEC_SEQ, -1)], axis=1)
        yb_s = _attention(q, k_s.reshape(DEC_BATCH * n_keys, -1), v_s.reshape(DEC_BATCH * n_keys, -1),
                          DEC_BATCH, DEC_SEQ, n_keys, 512, T_P)
        yc_s = _fourier(zmix, dft_s, lw, DEC_BATCH, DEC_SEQ, T_P)
        ya_s, yd_s, _ = _seq_mixers(zmix, state_lru[:, l], lw, DEC_BATCH, DEC_SEQ, T_P)

        x1, h2lin, route, prob, tab, counts = _post(
            x, mod[l], (ya_p, yb_p, yc_p, yd_p), (ya_s, yb_s, yc_s, yd_s), lw, tri, upper)
        dest, sched, loc, top_p, chunk_src, n_chunks = _routing_tables(
            route, prob, counts[0, :N_EXP], tab)
        xs = _gather_rows(dest, sched[1], h2lin)
        ys = _experts(l, sched, xs, params)
        x = _combine(loc, top_p, chunk_src, n_chunks, ys, x1, mod[l], final_g,
                     final=(l == DEPTH - 1))

    y_prompt, y_sample = x
    return (y_prompt.reshape(BATCH, SEQ, D), y_sample.reshape(DEC_BATCH, DEC_SEQ, D),
            jnp.stack(new_ckv, axis=1), jnp.stack(new_krope, axis=1), jnp.stack(new_lru, axis=1))
```

```python
import functools

import numpy as np
import jax
import jax.numpy as jnp
from jax import lax
from jax.experimental import pallas as pl
from jax.experimental.pallas import tpu as pltpu

F32 = jnp.float32
BF16 = jnp.bfloat16
I32 = jnp.int32
U32 = jnp.uint32

D = 1024
BATCH, SEQ = 32, 256
DEC_BATCH, DEC_SEQ, PAST = 2, 2048, 512
T_P = BATCH * SEQ
T_S = DEC_BATCH * DEC_SEQ
T = T_P + T_S
DEPTH = 2
GRID_W = 64
GW = 256
Q_LORA, KV_LORA, QK_NOPE, QK_ROPE, V_DIM, HEADS = 192, 128, 64, 32, 64, 4
HEAD_PAD = 128
ROPE_BASE = 10000.0
Q_SCALE = float((QK_NOPE + QK_ROPE) ** -0.5 * np.log2(np.e))
POOL_WINDOWS = (2, 4, 8, 16)
LRU_C = 8.0
N_EXP, TOP_K = 32, 4
LIMIT, ALPHA = 7.0, 1.702
EPS = 1e-6

LANES = 128
SUBLANES = 8
VMEM_LIMIT = 56 * 1024 * 1024

TM = 512
N_PT = T_P // TM
TILES_PER_DEC = DEC_SEQ // TM
TM_E = 512
N_SLOTS = T * TOP_K + N_EXP * TM_E
NB_E = N_SLOTS // TM_E
ROW_F = D // LANES
GATHER_VMEM_LIMIT = 60 * 1024 * 1024
CH_C = 16
MAX_CH = TM * TOP_K // CH_C + N_EXP
GATHER_UNROLL = 16
SEQ_PAD = 32
FRONT = 8

W_COLS = 1536
COL_Q, COL_KV, COL_KR = 1024, 1280, 1408


_PROMPT_ROWS = pl.BlockSpec((TM, D), lambda i: (jnp.minimum(i, N_PT - 1), 0))
_SAMPLE_ROWS = pl.BlockSpec((TM, D), lambda i: (jnp.maximum(i - N_PT, 0), 0))


def _cparams(sem, vmem=VMEM_LIMIT):
    return pltpu.CompilerParams(dimension_semantics=sem, vmem_limit_bytes=vmem)


def _bdot(a, b):
    return jnp.dot(a.astype(BF16), b.astype(BF16), preferred_element_type=F32)


def _split(a):
    hi = a.astype(BF16)
    lo = (a - hi.astype(F32)).astype(BF16)
    return hi, lo


def _dot3(a, b):
    ah, al = _split(a)
    bh, bl = _split(b)
    d = functools.partial(jnp.dot, preferred_element_type=F32)
    return d(ah, bh) + (d(al, bh) + d(ah, bl))


def _rms(x, n=None):
    n = x.shape[-1] if n is None else n
    return x * lax.rsqrt(jnp.sum(x * x, axis=-1, keepdims=True) * (1.0 / n) + EPS)


def _expm1(y):
    t = jnp.tanh(0.5 * y)
    return 2.0 * t / (1.0 - t)


def _sigmoid(x):
    return 0.5 * jnp.tanh(0.5 * x) + 0.5


def _tile_row(i):
    return jnp.where(i >= N_PT, 1 + (i - N_PT) // TILES_PER_DEC, 0)


def _mod_kernel(c_ref, w_ref, b_ref, o_ref):
    s = jax.nn.silu(c_ref[...])
    o_ref[0] = _dot3(s, w_ref[0]) + b_ref[0]


def _modulation(cvec, w_mod, b_mod):
    tn = 768
    n = 6 * D
    return pl.pallas_call(
        _mod_kernel,
        out_shape=jax.ShapeDtypeStruct((DEPTH, SUBLANES, n), F32),
        grid=(DEPTH, n // tn),
        in_specs=[pl.BlockSpec((SUBLANES, D), lambda l, j: (0, 0)),
                  pl.BlockSpec((1, D, tn), lambda l, j: (l, 0, j)),
                  pl.BlockSpec((1, 1, tn), lambda l, j: (l, 0, j))],
        out_specs=pl.BlockSpec((1, SUBLANES, tn), lambda l, j: (l, 0, j)),
        compiler_params=_cparams(("parallel", "parallel")),
        name="modulation",
    )(cvec, w_mod, b_mod.reshape(DEPTH, 1, n))


def _front_kernel(xp_ref, xs_ref, mod_ref, g_ref, w_ref, qg_ref, wuq_ref, kvg_ref, wkn_ref, wkr_ref,
                  wv_ref, cq_ref, saq_ref, sbq_ref, ck_ref, sak_ref, sbk_ref,
                  zmix_ref, q_ref, k_ref, v_ref, ckv_ref, kr_ref):
    i = pl.program_id(0)
    row = _tile_row(i)
    shift1 = mod_ref[pl.ds(row, 1), 0:D]
    scale1 = mod_ref[pl.ds(row, 1), D:2 * D]
    h = _rms(jnp.where(i >= N_PT, xs_ref[...], xp_ref[...])) * g_ref[...]
    h = h * (1.0 + scale1) + shift1
    z = jnp.dot(h.astype(BF16), w_ref[...], preferred_element_type=F32)
    zmix_ref[...] = z[:, 0:4 * GW]
    qn = _rms(z[:, COL_Q:COL_Q + 2 * LANES], Q_LORA) * qg_ref[...]
    q = _bdot(qn, wuq_ref[...])
    ckv = _rms(z[:, COL_KV:COL_KV + KV_LORA]) * kvg_ref[...]
    ckv_ref[...] = ckv
    ckv_b = ckv.astype(BF16)
    v_ref[...] = jnp.dot(ckv_b, wv_ref[...], preferred_element_type=F32).astype(BF16)
    kn = jnp.dot(ckv_b, wkn_ref[...], preferred_element_type=F32)
    kr = z[:, COL_KR:COL_KR + LANES]

    @pl.when(i < N_PT)
    def _():
        q_ref[...] = (q * Q_SCALE).astype(BF16)
        kr_ref[...] = kr[:, 0:QK_ROPE]
        k_ref[...] = (kn + _bdot(kr, wkr_ref[...])).astype(BF16)

    @pl.when(i >= N_PT)
    def _():
        wq = HEADS * HEAD_PAD
        half = QK_ROPE // 2
        qr = (q * cq_ref[...] + pltpu.roll(q, wq - half, 1) * saq_ref[...]
              + pltpu.roll(q, half, 1) * sbq_ref[...])
        krr = (kr * ck_ref[...] + pltpu.roll(kr, LANES - half, 1) * sak_ref[...]
               + pltpu.roll(kr, half, 1) * sbk_ref[...])
        q_ref[...] = (qr * Q_SCALE).astype(BF16)
        kr_ref[...] = krr[:, 0:QK_ROPE]
        k_ref[...] = (kn + _bdot(krr, wkr_ref[...])).astype(BF16)


def _front(x, mod_l, lw, rope):
    full = lambda shape: pl.BlockSpec(shape, lambda i: (0,) * len(shape))
    rows = lambda w: pl.BlockSpec((TM, w), lambda i: (i, 0))
    rope_rows = lambda w: pl.BlockSpec(
        (TM, w), lambda i: (jnp.maximum(i - N_PT, 0) % TILES_PER_DEC, 0))
    wq = HEADS * HEAD_PAD
    return pl.pallas_call(
        _front_kernel,
        out_shape=(jax.ShapeDtypeStruct((T, 4 * GW), F32),
                   jax.ShapeDtypeStruct((T, wq), BF16),
                   jax.ShapeDtypeStruct((T, wq), BF16),
                   jax.ShapeDtypeStruct((T, HEADS * V_DIM), BF16),
                   jax.ShapeDtypeStruct((T, KV_LORA), F32),
                   jax.ShapeDtypeStruct((T, QK_ROPE), F32)),
        grid=(T // TM,),
        in_specs=[_PROMPT_ROWS, _SAMPLE_ROWS, full((SUBLANES, 6 * D)), full((1, D)), full((D, W_COLS)),
                  full((1, 2 * LANES)), full((2 * LANES, wq)), full((1, KV_LORA)),
                  full((KV_LORA, wq)), full((LANES, wq)), full((KV_LORA, HEADS * V_DIM)),
                  rope_rows(wq), rope_rows(wq), rope_rows(wq),
                  rope_rows(LANES), rope_rows(LANES), rope_rows(LANES)],
        out_specs=(rows(4 * GW), rows(wq), rows(wq), rows(HEADS * V_DIM), rows(KV_LORA),
                   rows(QK_ROPE)),
        compiler_params=_cparams(("parallel",)),
        name="front",
    )(*x, mod_l, lw["attn_g"], lw["w_in"], lw["q_g"], lw["w_uq"], lw["kv_g"], lw["w_kn"],
      lw["w_kr"], lw["w_v"], *rope)


def _kvexp_kernel(ckv_ref, kr_ref, wkn_ref, wkr_ref, wv_ref, k_ref, v_ref):
    ckv_b = ckv_ref[...].astype(BF16)
    v_ref[...] = jnp.dot(ckv_b, wv_ref[...], preferred_element_type=F32).astype(BF16)
    kn = jnp.dot(ckv_b, wkn_ref[...], preferred_element_type=F32)
    k_ref[...] = (kn + _bdot(kr_ref[...], wkr_ref[...])).astype(BF16)


def _kv_expand(ckv, kr_pad, lw):
    n = ckv.shape[0]
    wq = HEADS * HEAD_PAD
    full = lambda shape: pl.BlockSpec(shape, lambda i: (0,) * len(shape))
    return pl.pallas_call(
        _kvexp_kernel,
        out_shape=(jax.ShapeDtypeStruct((n, wq), BF16),
                   jax.ShapeDtypeStruct((n, HEADS * V_DIM), BF16)),
        grid=(1,),
        in_specs=[full((n, KV_LORA)), full((n, LANES)), full((KV_LORA, wq)), full((LANES, wq)),
                  full((KV_LORA, HEADS * V_DIM))],
        out_specs=(full((n, wq)), full((n, HEADS * V_DIM))),
        compiler_params=_cparams(("arbitrary",)),
        name="kv_expand",
    )(ckv, kr_pad, lw["w_kn"], lw["w_kr"], lw["w_v"])


def _attn_kernel(q_ref, k_ref, v_ref, o_ref):
    outs = []
    for h in range(HEADS):
        qh = q_ref[:, h * HEAD_PAD:(h + 1) * HEAD_PAD]
        kh = k_ref[:, h * HEAD_PAD:(h + 1) * HEAD_PAD]
        s = lax.dot_general(qh, kh, (((1,), (1,)), ((), ())), preferred_element_type=F32)
        m = jnp.max(s, axis=-1, keepdims=True)
        p = jnp.exp2(s - m)
        l = jnp.sum(p, axis=-1, keepdims=True)
        vh = v_ref[:, h * V_DIM:(h + 1) * V_DIM]
        outs.append(jnp.dot(p.astype(BF16), vh, preferred_element_type=F32) / l)
    o_ref[...] = jnp.concatenate(outs, axis=-1)


def _attention(q, k, v, n_batch, seq, n_keys, tq, row0):
    nq = seq // tq
    wq = HEADS * HEAD_PAD
    return pl.pallas_call(
        _attn_kernel,
        out_shape=jax.ShapeDtypeStruct((n_batch * seq, GW), F32),
        grid=(n_batch, nq),
        in_specs=[pl.BlockSpec((tq, wq), lambda b, i: (row0 // tq + b * nq + i, 0)),
                  pl.BlockSpec((n_keys, wq), lambda b, i: (b, 0)),
                  pl.BlockSpec((n_keys, HEADS * V_DIM), lambda b, i: (b, 0))],
        out_specs=pl.BlockSpec((tq, GW), lambda b, i: (b * nq + i, 0)),
        compiler_params=_cparams(("parallel", "parallel")),
        name="attention",
    )(q, k, v)


def _fourier_kernel(z_ref, c_ref, s_ref, bdc_ref, bds_ref, w_ref, o_ref, xc_ref, xs_ref, *, norm):
    b = pl.program_id(1)

    @pl.when(pl.program_id(0) == 0)
    def _():
        xb = z_ref[...].astype(BF16)
        xc_ref[b] = _bdot(xb, bdc_ref[...]).astype(BF16)
        xs_ref[b] = _bdot(xb, bds_ref[...]).astype(BF16)

    f = (jnp.dot(c_ref[...].astype(BF16), xc_ref[b], preferred_element_type=F32)
         - jnp.dot(s_ref[...].astype(BF16), xs_ref[b], preferred_element_type=F32)) * norm
    o_ref[...] = _bdot(f, w_ref[...])


def _fourier(zmix, consts, lw, n_batch, seq, row0):
    ts = min(seq, 512)
    nj = seq // ts
    cmat, smat = consts
    full = lambda shape: pl.BlockSpec(shape, lambda j, b: (0,) * len(shape))
    z_spec = pl.BlockSpec((seq, GW), lambda j, b: (row0 // seq + jnp.where(j == 0, b, n_batch - 1), 1))
    return pl.pallas_call(
        functools.partial(_fourier_kernel, norm=float((seq * (GW // 4)) ** -0.5)),
        out_shape=jax.ShapeDtypeStruct((n_batch * seq, GW), F32),
        grid=(nj, n_batch),
        in_specs=[z_spec,
                  pl.BlockSpec((ts, seq), lambda j, b: (j, 0)),
                  pl.BlockSpec((ts, seq), lambda j, b: (j, 0)),
                  full((GW, GW)), full((GW, GW)), full((GW, GW))],
        out_specs=pl.BlockSpec((ts, GW), lambda j, b: (b * nj + j, 0)),
        scratch_shapes=[pltpu.VMEM((n_batch, seq, GW), BF16), pltpu.VMEM((n_batch, seq, GW), BF16)],
        compiler_params=_cparams(("arbitrary", "arbitrary")),
        name="fourier",
    )(zmix, cmat, smat, lw["bd_c"], lw["bd_s"], lw["w_f"])


def _seq_kernel(za_ref, zr_ref, zg_ref, h0_ref, icnt_ref, wp_ref, ps_ref, cw_ref, cb_ref, wa_ref,
                ba_ref, wx_ref, bx_ref, lam_ref, ya_ref, yd_ref, st_ref,
                pa_ref, pb_ref, xp_ref, a_ref, b_ref, *, seq):
    n = seq + SEQ_PAD
    span = seq + 2 * FRONT
    zeros_pad = jnp.zeros((n, GW), F32)

    za = za_ref[...]
    pa_ref[...] = zeros_pad
    pb_ref[...] = zeros_pad
    pa_ref[FRONT:FRONT + seq, :] = za
    pb_ref[0:span, :] = pa_ref[0:span, :] + pa_ref[1:span + 1, :]
    win2 = pb_ref[FRONT - 1:FRONT - 1 + seq, :]
    pa_ref[0:span, :] = pb_ref[0:span, :] + pb_ref[2:span + 2, :]
    win4 = pa_ref[FRONT - 2:FRONT - 2 + seq, :]
    pb_ref[0:span, :] = pa_ref[0:span, :] + pa_ref[4:span + 4, :]
    win8 = pb_ref[FRONT - 4:FRONT - 4 + seq, :]
    pa_ref[0:span, :] = pb_ref[0:span, :] + pb_ref[8:span + 8, :]
    win16 = pa_ref[FRONT - 8:FRONT - 8 + seq, :]
    grp = lax.broadcasted_iota(I32, (1, GW), 1) // (GW // 4)
    win = jnp.where(grp == 0, win2, jnp.where(grp == 1, win4, jnp.where(grp == 2, win8, win16)))
    dlt = win * icnt_ref[...] - za
    ya_ref[...] = _bdot(dlt, wp_ref[...]) * ps_ref[...]

    xp_ref[...] = zeros_pad
    xp_ref[FRONT:FRONT + seq, :] = zr_ref[...]
    xc = cb_ref[...] + cw_ref[0:1, :] * xp_ref[FRONT - 2:FRONT - 2 + seq, :]
    for kk in range(1, 4):
        xc = xc + cw_ref[kk:kk + 1, :] * xp_ref[FRONT - 2 + kk:FRONT - 2 + kk + seq, :]
    xcb = xc.astype(BF16)
    n_grp = seq // SUBLANES
    sub = lax.broadcasted_iota(I32, (n_grp, SUBLANES, GW), 1)

    total = None
    for d in range(2):
        r = _sigmoid(jnp.dot(xcb, wa_ref[d], preferred_element_type=F32) + ba_ref[d])
        ig = _sigmoid(jnp.dot(xcb, wx_ref[d], preferred_element_type=F32) + bx_ref[d])
        log_a = (-LRU_C) * r * jax.nn.softplus(-lam_ref[d])
        a = jnp.exp(log_a).reshape(n_grp, SUBLANES, GW)
        b = (jnp.sqrt(-_expm1(2.0 * log_a)) * (ig * xc)).reshape(n_grp, SUBLANES, GW)
        for k in (1, 2, 4):
            shift = k if d == 0 else SUBLANES - k
            m = sub >= k if d == 0 else sub < SUBLANES - k
            ap, bp = pltpu.roll(a, shift, 1), pltpu.roll(b, shift, 1)
            b = jnp.where(m, a * bp + b, b)
            a = jnp.where(m, a * ap, a)
        a_ref[...] = a.reshape(seq, GW)
        b_ref[...] = b.reshape(seq, GW)

        def step(g, carry, d=d):
            gi = g if d == 0 else n_grp - 1 - g
            off = pl.multiple_of(gi * SUBLANES, SUBLANES)
            hh = a_ref[pl.ds(off, SUBLANES), :] * carry + b_ref[pl.ds(off, SUBLANES), :]
            b_ref[pl.ds(off, SUBLANES), :] = hh
            edge = hh[SUBLANES - 1:SUBLANES, :] if d == 0 else hh[0:1, :]
            return jnp.broadcast_to(edge, (SUBLANES, GW))

        last = lax.fori_loop(0, n_grp, step, jnp.broadcast_to(h0_ref[0, d:d + 1, :], (SUBLANES, GW)))
        st_ref[0, d:d + 1, :] = last[0:1, :]
        total = b_ref[...] if total is None else total + b_ref[...]

    yd_ref[...] = total * jax.nn.gelu(zg_ref[...])


def _pool_inverse_counts(seq):
    pos = np.arange(seq)[:, None]
    half = np.repeat(np.array(POOL_WINDOWS) // 2, GW // len(POOL_WINDOWS))[None, :]
    cnt = np.minimum(pos + half, seq) - np.maximum(pos - half, 0)
    return jnp.asarray(1.0 / cnt, F32)


def _seq_mixers(zmix, h0, lw, n_batch, seq, row0):
    full = lambda shape: pl.BlockSpec(shape, lambda b: (0,) * len(shape))
    col = lambda c: pl.BlockSpec((seq, GW), lambda b: (row0 // seq + b, c))
    out_rows = pl.BlockSpec((seq, GW), lambda b: (b, 0))
    pad = pltpu.VMEM((seq + SEQ_PAD, GW), F32)
    return pl.pallas_call(
        functools.partial(_seq_kernel, seq=seq),
        out_shape=(jax.ShapeDtypeStruct((n_batch * seq, GW), F32),
                   jax.ShapeDtypeStruct((n_batch * seq, GW), F32),
                   jax.ShapeDtypeStruct((n_batch, 2, GW), F32)),
        grid=(n_batch,),
        in_specs=[col(0), col(2), col(3), pl.BlockSpec((1, 2, GW), lambda b: (b, 0, 0)),
                  full((seq, GW)), full((GW, GW)), full((1, GW)), full((4, GW)), full((1, GW)),
                  full((2, GW, GW)), full((2, 1, GW)), full((2, GW, GW)), full((2, 1, GW)),
                  full((2, 1, GW))],
        out_specs=(out_rows, out_rows, pl.BlockSpec((1, 2, GW), lambda b: (b, 0, 0))),
        scratch_shapes=[pad, pad, pad, pltpu.VMEM((seq, GW), F32), pltpu.VMEM((seq, GW), F32)],
        compiler_params=_cparams(("parallel",)),
        name="seq_mixers",
    )(zmix, zmix, zmix, h0, _pool_inverse_counts(seq), lw["bd_pool"], lw["pool_scale"], lw["conv_w"], lw["conv_b"],
      lw["bd_wa"], lw["lru_ba"], lw["bd_wx"], lw["lru_bx"], lw["lru_lam"])


def _post_kernel(xp_ref, xs_ref, mod_ref, yap_ref, ybp_ref, ycp_ref, ydp_ref, yas_ref, ybs_ref, ycs_ref,
                 yds_ref, og_ref, wo_ref, fg_ref, rw_ref, rb_ref, tri_ref, upper_ref,
                 x1_ref, h2p_ref, route_ref, prob_ref, tab_ref, cnt_ref, carry_ref):
    i = pl.program_id(0)
    is_s = i >= N_PT
    row = _tile_row(i)
    gate1 = mod_ref[pl.ds(row, 1), 2 * D:3 * D]
    shift2 = mod_ref[pl.ds(row, 1), 3 * D:4 * D]
    scale2 = mod_ref[pl.ds(row, 1), 4 * D:5 * D]

    @pl.when(i == 0)
    def _():
        carry_ref[...] = jnp.zeros_like(carry_ref)

    groups = []
    for gi, (p_ref, s_ref) in enumerate(((yap_ref, yas_ref), (ybp_ref, ybs_ref),
                                         (ycp_ref, ycs_ref), (ydp_ref, yds_ref))):
        y = jnp.where(is_s, s_ref[...], p_ref[...])
        groups.append((_rms(y) * og_ref[gi:gi + 1, :]).astype(BF16))
    ycat = jnp.concatenate(groups, axis=-1)
    x = jnp.where(is_s, xs_ref[...], xp_ref[...])
    x1 = x + gate1 * jnp.dot(ycat, wo_ref[...], preferred_element_type=F32)
    x1_ref[...] = x1
    h2 = _rms(x1) * fg_ref[...]
    h2 = h2 * (1.0 + scale2) + shift2

    for j in range(ROW_F):
        h2p_ref[pl.ds(j, TM, stride=ROW_F), :] = h2[:, j * LANES:(j + 1) * LANES]

    h_hi, h_lo = _split(h2)
    both = jnp.dot(h_hi, rw_ref[...], preferred_element_type=F32)
    cross = both[:, LANES:2 * LANES] + jnp.dot(h_lo, rw_ref[:, 0:LANES], preferred_element_type=F32)
    logits = both[:, 0:LANES] + cross + rb_ref[...]
    lane = lax.broadcasted_iota(I32, (TM, LANES), 1)
    lane_f = lane.astype(F32)
    neg = jnp.float32(-jnp.inf)
    cur = jnp.where(lane < N_EXP, logits, neg)
    sel, vals, idxs = [], [], []
    for _ in range(TOP_K):
        m = jnp.max(cur, axis=-1, keepdims=True)
        idx = jnp.min(jnp.where(cur == m, lane_f, float(LANES)), axis=-1, keepdims=True)
        hit = lane_f == idx
        sel.append(hit)
        vals.append(m)
        idxs.append(idx)
        cur = jnp.where(hit, neg, cur)
    exps = [jnp.exp(v - vals[0]) for v in vals]
    denom = exps[0] + exps[1] + exps[2] + exps[3]
    onehot = jnp.where(sel[0] | sel[1] | sel[2] | sel[3], 1.0, 0.0)
    cum_l = jnp.dot(tri_ref[...], onehot.astype(BF16), preferred_element_type=F32)
    before = carry_ref[0:1, :]
    cum = cum_l + before
    n_tile = cum_l[TM - 1:TM, :]
    nch = jnp.floor((n_tile + (CH_C - 1.0)) * (1.0 / CH_C))
    nch8 = jnp.broadcast_to(nch, (SUBLANES, LANES))
    base = jnp.dot(nch8.astype(BF16), upper_ref[...], preferred_element_type=F32)[0:1, :]
    pk = jnp.zeros((TM, LANES), I32)
    pf = jnp.zeros((TM, LANES), F32)
    for k in range(TOP_K):
        rank = jnp.sum(jnp.where(sel[k], cum - 1.0, 0.0), axis=-1, keepdims=True).astype(I32)
        local = jnp.sum(jnp.where(sel[k], (base * CH_C + cum_l - 1.0) * ROW_F, 0.0), axis=-1,
                        keepdims=True).astype(I32)
        pk = jnp.where(lane == k, idxs[k].astype(I32), pk)
        pk = jnp.where(lane == TOP_K + k, rank, pk)
        pk = jnp.where(lane == 2 * TOP_K + k, local, pk)
        pf = jnp.where(lane == k, exps[k] / denom, pf)
    route_ref[...] = pk.T[0:4 * TOP_K, :]
    prob_ref[...] = pf.T[0:SUBLANES, :]
    row8 = lax.broadcasted_iota(I32, (SUBLANES, LANES), 0)
    tab_ref[...] = jnp.where(row8 == 0, before, jnp.where(row8 == 1, nch, base))
    new_carry = jnp.broadcast_to(cum[TM - 1:TM, :], (SUBLANES, LANES))
    carry_ref[...] = new_carry
    cnt_ref[...] = new_carry


def _post(x, mod_l, ys_prompt, ys_sample, lw, tri, upper):
    full = lambda shape: pl.BlockSpec(shape, lambda i: (0,) * len(shape))
    rows = lambda w: pl.BlockSpec((TM, w), lambda i: (i, 0))
    prow = pl.BlockSpec((TM, GW), lambda i: (jnp.minimum(i, N_PT - 1), 0))
    srow = pl.BlockSpec((TM, GW), lambda i: (jnp.maximum(i - N_PT, 0), 0))
    return pl.pallas_call(
        _post_kernel,
        out_shape=(jax.ShapeDtypeStruct((T, D), F32),
                   jax.ShapeDtypeStruct((T * ROW_F, LANES), F32),
                   jax.ShapeDtypeStruct((4 * TOP_K, T), I32),
                   jax.ShapeDtypeStruct((SUBLANES, T), F32),
                   jax.ShapeDtypeStruct((T // TM * SUBLANES, LANES), F32),
                   jax.ShapeDtypeStruct((SUBLANES, LANES), F32)),
        grid=(T // TM,),
        in_specs=[_PROMPT_ROWS, _SAMPLE_ROWS, full((SUBLANES, 6 * D)),
                  prow, prow, prow, prow, srow, srow, srow, srow,
                  full((4, GW)), full((D, D)), full((1, D)), full((D, 2 * LANES)), full((1, LANES)),
                  full((TM, TM)), full((LANES, LANES))],
        out_specs=(rows(D), pl.BlockSpec((TM * ROW_F, LANES), lambda i: (i, 0)),
                   pl.BlockSpec((4 * TOP_K, TM), lambda i: (0, i)),
                   pl.BlockSpec((SUBLANES, TM), lambda i: (0, i)),
                   pl.BlockSpec((SUBLANES, LANES), lambda i: (i, 0)), full((SUBLANES, LANES))),
        scratch_shapes=[pltpu.VMEM((SUBLANES, LANES), F32)],
        compiler_params=_cparams(("arbitrary",)),
        name="post",
    )(*x, mod_l, *ys_prompt, *ys_sample, lw["out_g"], lw["w_out"], lw["ffn_g"], lw["router_w"],
      lw["router_b"], tri, upper)


def _gather_kernel(dest_ref, nused_ref, zeros_hbm, src_ref, o_ref, stok_ref, sem):
    i = pl.program_id(0)

    @pl.when(i == 0)
    def _():
        init = pltpu.make_async_copy(zeros_hbm, stok_ref, sem)
        init.start()
        init.wait()

        def scatter(t8, c):
            for k in range(TOP_K):
                for u in range(SUBLANES):
                    tok = t8 * SUBLANES + u
                    stok_ref[dest_ref[k * T + tok]] = tok * ROW_F
            return c
        lax.fori_loop(0, T // SUBLANES, scatter, 0)

    @pl.when(i < nused_ref[0])
    def _():
        def rows(r16, c):
            r0 = pl.multiple_of(r16 * GATHER_UNROLL, GATHER_UNROLL)
            for u in range(GATHER_UNROLL):
                src = pl.multiple_of(stok_ref[i * TM_E + r0 + u], ROW_F)
                dst = pl.multiple_of(r0 * ROW_F, GATHER_UNROLL * ROW_F) + u * ROW_F
                o_ref[pl.ds(dst, ROW_F), :] = src_ref[pl.ds(src, ROW_F), :]
            return c
        lax.fori_loop(0, TM_E // GATHER_UNROLL, rows, 0)

    @pl.when(i >= nused_ref[0])
    def _():
        o_ref[...] = jnp.zeros_like(o_ref)


def _gather_rows(dest, n_used, h2lin):
    return pl.pallas_call(
        _gather_kernel,
        out_shape=jax.ShapeDtypeStruct((N_SLOTS * ROW_F, LANES), F32),
        grid_spec=pltpu.PrefetchScalarGridSpec(
            num_scalar_prefetch=2,
            grid=(NB_E,),
            in_specs=[pl.BlockSpec(memory_space=pl.ANY),
                      pl.BlockSpec(memory_space=pltpu.VMEM)],
            out_specs=pl.BlockSpec((TM_E * ROW_F, LANES), lambda i, d, nu: (i, 0)),
            scratch_shapes=[pltpu.SMEM((N_SLOTS,), I32), pltpu.SemaphoreType.DMA(())]),
        compiler_params=_cparams(("arbitrary",), GATHER_VMEM_LIMIT),
        name="moe_gather",
    )(dest, n_used, jnp.zeros((N_SLOTS,), I32), h2lin)


def _expert_rows(n, x_ref, w_ref, slot, bg_ref, bu_ref, bd_ref, o_ref):
    x = jnp.concatenate([x_ref[pl.ds(j, n, stride=ROW_F), :].astype(BF16) for j in range(ROW_F)],
                        axis=-1)
    g = jnp.dot(x, w_ref[slot, 0].astype(BF16), preferred_element_type=F32) + bg_ref[0, 0]
    u = jnp.dot(x, w_ref[slot, 1].astype(BF16), preferred_element_type=F32) + bu_ref[0, 0]
    g = jnp.minimum(g, LIMIT)
    u = jnp.clip(u, -LIMIT, LIMIT)
    act = (u + 1.0) * (g * jax.nn.sigmoid(ALPHA * g))
    y = jnp.dot(act.astype(BF16), w_ref[slot, 2].astype(BF16),
                preferred_element_type=F32) + bd_ref[0, 0]
    for j in range(ROW_F):
        o_ref[pl.ds(j, n, stride=ROW_F), :] = y[:, j * LANES:(j + 1) * LANES]
    if n < TM_E:
        o_ref[n * ROW_F:TM_E * ROW_F, :] = jnp.zeros(((TM_E - n) * ROW_F, LANES), F32)


def _expert_kernel(be_ref, nused_ref, valid_ref, first_ref, next_ref, par_ref,
                   x_ref, wg_hbm, bg_ref, wu_hbm, bu_ref, wd_hbm, bd_ref, o_ref, w_ref, sems, *, layer):
    i = pl.program_id(0)
    ib = jnp.minimum(i, NB_E - 1)
    live = i < nused_ref[0]
    valid = valid_ref[ib]
    expert = be_ref[ib]
    slot = par_ref[ib]

    def fetch(e, s):
        return [pltpu.make_async_copy(src.at[layer, e], w_ref.at[s, m], sems.at[s, m])
                for m, src in enumerate((wg_hbm, wu_hbm, wd_hbm))]

    @pl.when(i == 0)
    def _():
        for cp in fetch(expert, slot):
            cp.start()

    @pl.when(live & (first_ref[ib] == 1))
    def _():
        @pl.when(next_ref[ib] >= 0)
        def _():
            for cp in fetch(next_ref[ib], 1 - slot):
                cp.start(priority=1)

        for cp in fetch(expert, slot):
            cp.wait()

    @pl.when(live & (valid > TM_E // 2))
    def _():
        _expert_rows(TM_E, x_ref, w_ref, slot, bg_ref, bu_ref, bd_ref, o_ref)

    @pl.when(live & (valid <= TM_E // 2))
    def _():
        _expert_rows(TM_E // 2, x_ref, w_ref, slot, bg_ref, bu_ref, bd_ref, o_ref)

    @pl.when(jnp.logical_not(live))
    def _():
        o_ref[...] = jnp.zeros_like(o_ref)


def _experts(l, sched, xs, p):
    last = NB_E - 1
    bspec = pl.BlockSpec((1, 1, 1, D), lambda i, be, *_: (l, be[jnp.minimum(i, last)], 0, 0))
    hbm = pl.BlockSpec(memory_space=pl.ANY)
    bias = lambda b: b.reshape(DEPTH, N_EXP, 1, D)
    return pl.pallas_call(
        functools.partial(_expert_kernel, layer=l),
        out_shape=jax.ShapeDtypeStruct(((NB_E + 1) * TM_E * ROW_F, LANES), F32),
        grid_spec=pltpu.PrefetchScalarGridSpec(
            num_scalar_prefetch=len(sched),
            grid=(NB_E + 1,),
            in_specs=[pl.BlockSpec((TM_E * ROW_F, LANES), lambda i, *_: (jnp.minimum(i, last), 0)),
                      hbm, bspec, hbm, bspec, hbm, bspec],
            out_specs=pl.BlockSpec((TM_E * ROW_F, LANES), lambda i, *_: (i, 0)),
            scratch_shapes=[pltpu.VMEM((2, 3, D, D), F32), pltpu.SemaphoreType.DMA((2, 3))]),
        compiler_params=_cparams(("arbitrary",)),
        name="moe_experts",
    )(*sched, xs, p["w_gate"], bias(p["b_gate"]), p["w_up"], bias(p["b_up"]),
      p["w_down"], bias(p["b_down"]))


def _combine_kernel(loc_ref, p_ref, csrc_ref, nch_ref, ys_hbm, x_ref, mod_ref, fg_ref, *rest, final):
    *outs, buf_ref, acc_ref, sems = rest
    tb = pl.program_id(0)
    n_tb = pl.num_programs(0)
    slot = tb % 2
    row = _tile_row(tb)
    gate2 = mod_ref[pl.ds(row, 1), 5 * D:6 * D]
    chunk_rows = CH_C * ROW_F

    def chunk_copy(t, c, s):
        src = pl.multiple_of(csrc_ref[t * MAX_CH + c] * ROW_F, ROW_F)
        dst = pl.multiple_of(c * chunk_rows, chunk_rows)
        return pltpu.make_async_copy(ys_hbm.at[pl.ds(src, chunk_rows)],
                                     buf_ref.at[s, pl.ds(dst, chunk_rows)], sems.at[s])

    def issue(t, s):
        def body(c, carry):
            chunk_copy(t, c, s).start()
            return carry
        lax.fori_loop(0, nch_ref[t], body, 0)

    @pl.when(tb == 0)
    def _():
        issue(0, 0)

    @pl.when(tb + 1 < n_tb)
    def _():
        issue(tb + 1, 1 - slot)

    def drain(c, carry):
        chunk_copy(tb, c, slot).wait()
        return carry
    lax.fori_loop(0, nch_ref[tb], drain, 0)

    def tokens(r8, carry):
        r0 = pl.multiple_of(r8 * SUBLANES, SUBLANES)
        for u in range(SUBLANES):
            tok = tb * TM + r0 + u
            acc = None
            for k in range(TOP_K):
                off = pl.multiple_of(loc_ref[k * T + tok], ROW_F)
                term = buf_ref[slot, pl.ds(off, ROW_F), :] * p_ref[k * T + tok]
                acc = term if acc is None else acc + term
            dst = pl.multiple_of(r0 * ROW_F, SUBLANES * ROW_F) + u * ROW_F
            acc_ref[pl.ds(dst, ROW_F), :] = acc
        return carry
    lax.fori_loop(0, TM // SUBLANES, tokens, 0)

    moe = jnp.concatenate([acc_ref[pl.ds(j, TM, stride=ROW_F), :] for j in range(ROW_F)], axis=-1)
    x2 = x_ref[...] + gate2 * moe
    if final:
        x2 = _rms(x2) * fg_ref[...]
    prompt_ref, sample_ref = outs

    @pl.when(tb < N_PT)
    def _():
        prompt_ref[...] = x2

    @pl.when(tb >= N_PT)
    def _():
        sample_ref[...] = x2


def _combine(loc, top_p, chunk_src, n_chunks, ys, x1, mod_l, final_g, final):
    full = lambda shape: pl.BlockSpec(shape, lambda i, *_: (0,) * len(shape))
    rows = lambda w: pl.BlockSpec((TM, w), lambda i, *_: (i, 0))
    out_shape = (jax.ShapeDtypeStruct((T_P, D), F32), jax.ShapeDtypeStruct((T_S, D), F32))
    out_specs = (pl.BlockSpec((TM, D), lambda i, *_: (jnp.minimum(i, N_PT - 1), 0)),
                 pl.BlockSpec((TM, D), lambda i, *_: (jnp.maximum(i - N_PT, 0), 0)))
    return pl.pallas_call(
        functools.partial(_combine_kernel, final=final),
        out_shape=out_shape,
        grid_spec=pltpu.PrefetchScalarGridSpec(
            num_scalar_prefetch=4,
            grid=(T // TM,),
            in_specs=[pl.BlockSpec(memory_space=pl.ANY), rows(D), full((SUBLANES, 6 * D)),
                      full((1, D))],
            out_specs=out_specs,
            scratch_shapes=[pltpu.VMEM((2, MAX_CH * CH_C * ROW_F, LANES), F32),
                            pltpu.VMEM((TM * ROW_F, LANES), F32),
                            pltpu.SemaphoreType.DMA((2,))]),
        compiler_params=_cparams(("arbitrary",)),
        name="moe_combine",
    )(loc, top_p, chunk_src, n_chunks, ys, x1, mod_l, final_g)


def _rope_tables():
    rows = DEC_SEQ // GRID_W
    r = np.repeat(np.arange(rows, dtype=np.float64), GRID_W)
    c = np.tile(np.arange(GRID_W, dtype=np.float64), rows)
    n_freq = QK_ROPE // 4
    inv = (np.float32(ROPE_BASE) ** (-np.arange(n_freq, dtype=np.float32) / n_freq)).astype(np.float64)
    ang = np.concatenate([r[:, None] * inv, c[:, None] * inv], axis=-1).astype(np.float32)
    cos, sin = np.cos(ang.astype(np.float64)), np.sin(ang.astype(np.float64))
    half = QK_ROPE // 2

    def place(width, start):
        cf = np.ones((DEC_SEQ, width), np.float32)
        sa = np.zeros((DEC_SEQ, width), np.float32)
        sb = np.zeros((DEC_SEQ, width), np.float32)
        for s0 in start:
            cf[:, s0:s0 + half] = cos
            cf[:, s0 + half:s0 + 2 * half] = cos
            sa[:, s0:s0 + half] = -sin
            sb[:, s0 + half:s0 + 2 * half] = sin
        return jnp.asarray(cf), jnp.asarray(sa), jnp.asarray(sb)

    return (*place(HEADS * HEAD_PAD, [h * HEAD_PAD + QK_NOPE for h in range(HEADS)]),
            *place(LANES, [0]))


def _dft_tables(seq):
    kn = (np.arange(seq, dtype=np.int64)[:, None] * np.arange(seq, dtype=np.int64)[None, :]) % seq
    ang = 2.0 * np.pi * kn.astype(np.float64) / seq
    return jnp.asarray(np.cos(ang), F32), jnp.asarray(np.sin(ang), F32)


def _block_diag(blocks):
    g, n, _ = blocks.shape
    eye = jnp.eye(g, dtype=blocks.dtype)
    return jnp.einsum("gij,gh->gihj", blocks, eye).reshape(g * n, g * n)


def _layer_weights(l, p):
    w_in = p["w_in"][l]
    za, zq, zkv, zkr, zf, zr, zg = jnp.split(
        w_in, np.cumsum([GW, Q_LORA, KV_LORA, QK_ROPE, GW, GW])[:], axis=1)
    zpad = lambda n: jnp.zeros((D, n), F32)
    w_cols = jnp.concatenate([za, zf, zr, zg, zq, zpad(2 * LANES - Q_LORA), zkv, zkr,
                              zpad(LANES - QK_ROPE)], axis=1).astype(BF16)
    wq = HEADS * HEAD_PAD
    w_uq = p["w_uq"][l].reshape(Q_LORA, HEADS, QK_NOPE + QK_ROPE)
    w_uq = jnp.pad(w_uq, ((0, 2 * LANES - Q_LORA), (0, 0), (0, HEAD_PAD - QK_NOPE - QK_ROPE)))
    w_ukv = p["w_ukv"][l].reshape(KV_LORA, HEADS, QK_NOPE + V_DIM)
    w_kn = jnp.pad(w_ukv[:, :, :QK_NOPE], ((0, 0), (0, 0), (0, HEAD_PAD - QK_NOPE)))
    w_v = w_ukv[:, :, QK_NOPE:]
    place = np.zeros((LANES, HEADS, HEAD_PAD), np.float32)
    for h in range(HEADS):
        place[np.arange(QK_ROPE), h, QK_NOPE + np.arange(QK_ROPE)] = 1.0
    c64 = np.arange(GW // 4, dtype=np.int64)
    ang = 2.0 * np.pi * ((c64[:, None] * c64[None, :]) % (GW // 4)).astype(np.float64) / (GW // 4)
    four = lambda m: jnp.asarray(np.broadcast_to(m, (4,) + m.shape), F32)
    router_hi, router_lo = _split(jnp.pad(p["router_w"][l], ((0, 0), (0, LANES - N_EXP))))
    router_w = jnp.concatenate([router_hi, router_lo], axis=1)
    router_b = jnp.pad(p["router_b"][l], (0, LANES - N_EXP)).reshape(1, LANES)
    return {
        "attn_g": p["attn_norm_g"][l].reshape(1, D),
        "w_in": w_cols,
        "q_g": jnp.pad(p["q_norm_g"][l], (0, 2 * LANES - Q_LORA)).reshape(1, 2 * LANES),
        "w_uq": w_uq.reshape(2 * LANES, wq).astype(BF16),
        "kv_g": p["kv_norm_g"][l].reshape(1, KV_LORA),
        "w_kn": w_kn.reshape(KV_LORA, wq).astype(BF16),
        "w_kr": jnp.asarray(place.reshape(LANES, wq), BF16),
        "w_v": w_v.reshape(KV_LORA, HEADS * V_DIM).astype(BF16),
        "bd_c": _block_diag(four(np.cos(ang))),
        "bd_s": _block_diag(four(np.sin(ang))),
        "w_f": p["fourier_w"][l].astype(BF16),
        "bd_pool": _block_diag(p["pool_w"][l]).astype(BF16),
        "pool_scale": p["pool_scale"][l].reshape(1, GW),
        "conv_w": p["conv_w"][l],
        "conv_b": p["conv_b"][l].reshape(1, GW),
        "bd_wa": jnp.stack([_block_diag(p["lru_wa"][l, d]) for d in range(2)]).astype(BF16),
        "bd_wx": jnp.stack([_block_diag(p["lru_wx"][l, d]) for d in range(2)]).astype(BF16),
        "lru_ba": p["lru_ba"][l].reshape(2, 1, GW),
        "lru_bx": p["lru_bx"][l].reshape(2, 1, GW),
        "lru_lam": p["lru_lambda"][l].reshape(2, 1, GW),
        "out_g": p["out_norm_g"][l],
        "w_out": p["w_out"][l].astype(BF16),
        "ffn_g": p["ffn_norm_g"][l].reshape(1, D),
        "router_w": router_w,
        "router_b": router_b,
    }


def _routing_tables(route, prob, counts, tab):
    top_e, rank = route[0:TOP_K], route[TOP_K:2 * TOP_K]
    top_p = prob[0:TOP_K].reshape(-1)
    loc = route[2 * TOP_K:3 * TOP_K].reshape(-1)
    counts = counts.astype(I32)
    padded = (counts + TM_E - 1) // TM_E * TM_E
    pad_ends = jnp.cumsum(padded)
    pad_starts = pad_ends - padded
    experts = jnp.arange(N_EXP, dtype=I32)
    onehot = top_e[:, :, None] == experts
    dest = (jnp.sum(jnp.where(onehot, pad_starts, 0), axis=-1) + rank).reshape(T * TOP_K)
    n_used = (pad_ends[-1] // TM_E).astype(I32)
    blk = jnp.minimum(jnp.arange(NB_E, dtype=I32), n_used - 1) * TM_E
    block_e = jnp.minimum(jnp.sum(pad_ends[None, :] <= blk[:, None], axis=-1), N_EXP - 1).astype(I32)
    of_block = lambda a: jnp.sum(jnp.where(block_e[:, None] == experts, a, 0), axis=-1)
    valid = jnp.clip(of_block(pad_starts + counts) - blk, 0, TM_E).astype(I32)
    first = (blk == of_block(pad_starts)).astype(I32)
    later = (experts[None, :] > experts[:, None]) & (counts[None, :] > 0)
    next_of = jnp.min(jnp.where(later, experts[None, :], N_EXP), axis=-1)
    next_e = of_block(jnp.where(next_of < N_EXP, next_of, -1)).astype(I32)
    parity = (of_block(jnp.cumsum((counts > 0).astype(I32))) % 2).astype(I32)
    sched = (block_e, n_used.reshape(1), valid, first, next_e, parity)
    tab = tab.reshape(T // TM, SUBLANES, LANES)[:, :, :N_EXP].astype(I32)
    before, nch, base = tab[:, 0], tab[:, 1], tab[:, 2]
    run_start = pad_starts[None, :] + before
    ends = base + nch
    ci = jnp.arange(MAX_CH, dtype=I32)
    e_of = jnp.minimum(jnp.sum(ends[:, None, :] <= ci[None, :, None], axis=-1), N_EXP - 1)
    pick = lambda a: jnp.sum(jnp.where(e_of[:, :, None] == experts, a[:, None, :], 0), axis=-1)
    chunk_src = jnp.clip(pick(run_start) + (ci[None, :] - pick(base)) * CH_C, 0, N_SLOTS)
    return dest, sched, loc, top_p, chunk_src.reshape(-1), ends[:, N_EXP - 1]


def kernel(x_prompt, x_sample, cache_ckv, cache_krope, state_lru, c, c_ctx, w_mod, b_mod, attn_norm_g, w_in, pool_w, pool_scale, q_norm_g, w_uq, kv_norm_g, w_ukv, fourier_w, conv_w, conv_b, lru_wa, lru_ba, lru_wx, lru_bx, lru_lambda, out_norm_g, w_out, ffn_norm_g, router_w, router_b, w_gate, b_gate, w_up, b_up, w_down, b_down, final_norm_g):
    params = dict(attn_norm_g=attn_norm_g, w_in=w_in, pool_w=pool_w, pool_scale=pool_scale,
                  q_norm_g=q_norm_g, w_uq=w_uq, kv_norm_g=kv_norm_g, w_ukv=w_ukv,
                  fourier_w=fourier_w, conv_w=conv_w, conv_b=conv_b, lru_wa=lru_wa, lru_ba=lru_ba,
                  lru_wx=lru_wx, lru_bx=lru_bx, lru_lambda=lru_lambda, out_norm_g=out_norm_g,
                  w_out=w_out, ffn_norm_g=ffn_norm_g, router_w=router_w, router_b=router_b,
                  w_gate=w_gate, b_gate=b_gate, w_up=w_up, b_up=b_up, w_down=w_down, b_down=b_down)
    x = (x_prompt.reshape(T_P, D), x_sample.reshape(T_S, D))
    cvec = jnp.concatenate([c_ctx[None, :], c, jnp.zeros((SUBLANES - 1 - DEC_BATCH, D), F32)], axis=0)
    mod = _modulation(cvec, w_mod, b_mod)
    rope = _rope_tables()
    dft_p, dft_s = _dft_tables(SEQ), _dft_tables(DEC_SEQ)
    tri = jnp.asarray(np.tril(np.ones((TM, TM), np.float32)), BF16)
    upper = jnp.asarray(np.triu(np.ones((LANES, LANES), np.float32), 1), BF16)
    final_g = final_norm_g.reshape(1, D)
    h0_prompt = jnp.zeros((BATCH, 2, GW), F32)
    n_keys = PAST + DEC_SEQ

    new_ckv, new_krope, new_lru = [], [], []
    for l in range(DEPTH):
        lw = _layer_weights(l, params)
        zmix, q, k, v, ckv, kr = _front(x, mod[l], lw, rope)
        new_ckv.append(ckv[:T_P].reshape(BATCH, SEQ, KV_LORA))
        new_krope.append(kr[:T_P].reshape(BATCH, SEQ, QK_ROPE))

        yb_p = _attention(q, k, v, BATCH, SEQ, SEQ, SEQ, 0)
        yc_p = _fourier(zmix, dft_p, lw, BATCH, SEQ, 0)
        ya_p, yd_p, st_p = _seq_mixers(zmix, h0_prompt, lw, BATCH, SEQ, 0)
        new_lru.append(st_p)

        kr_ctx = jnp.pad(cache_krope[:, l].reshape(DEC_BATCH * PAST, QK_ROPE),
                         ((0, 0), (0, LANES - QK_ROPE)))
        k_ctx, v_ctx = _kv_expand(cache_ckv[:, l].reshape(DEC_BATCH * PAST, KV_LORA), kr_ctx, lw)
        k_s = jnp.concatenate([k_ctx.reshape(DEC_BATCH, PAST, -1),
                               k[T_P:].reshape(DEC_BATCH, DEC_SEQ, -1)], axis=1)
        v_s = jnp.concatenate([v_ctx.reshape(DEC_BATCH, PAST, -1),
                               v[T_P:].reshape(DEC_BATCH, DEC_SEQ, -1)], axis=1)
        yb_s = _attention(q, k_s.reshape(DEC_BATCH * n_keys, -1), v_s.reshape(DEC_BATCH * n_keys, -1),
                          DEC_BATCH, DEC_SEQ, n_keys, 512, T_P)
        yc_s = _fourier(zmix, dft_s, lw, DEC_BATCH, DEC_SEQ, T_P)
        ya_s, yd_s, _ = _seq_mixers(zmix, state_lru[:, l], lw, DEC_BATCH, DEC_SEQ, T_P)

        x1, h2lin, route, prob, tab, counts = _post(
            x, mod[l], (ya_p, yb_p, yc_p, yd_p), (ya_s, yb_s, yc_s, yd_s), lw, tri, upper)
        dest, sched, loc, top_p, chunk_src, n_chunks = _routing_tables(
            route, prob, counts[0, :N_EXP], tab)
        xs = _gather_rows(dest, sched[1], h2lin)
        ys = _experts(l, sched, xs, params)
        x = _combine(loc, top_p, chunk_src, n_chunks, ys, x1, mod[l], final_g,
                     final=(l == DEPTH - 1))

    y_prompt, y_sample = x
    return (y_prompt.reshape(BATCH, SEQ, D), y_sample.reshape(DEC_BATCH, DEC_SEQ, D),
            jnp.stack(new_ckv, axis=1), jnp.stack(new_krope, axis=1), jnp.stack(new_lru, axis=1))
```

```python
import functools

import numpy as np
import jax
import jax.numpy as jnp
from jax import lax
from jax.experimental import pallas as pl
from jax.experimental.pallas import tpu as pltpu

F32 = jnp.float32
BF16 = jnp.bfloat16
I32 = jnp.int32
U32 = jnp.uint32

D = 1024
BATCH, SEQ = 32, 256
DEC_BATCH, DEC_SEQ, PAST = 2, 2048, 512
T_P = BATCH * SEQ
T_S = DEC_BATCH * DEC_SEQ
T = T_P + T_S
DEPTH = 2
GRID_W = 64
GW = 256
Q_LORA, KV_LORA, QK_NOPE, QK_ROPE, V_DIM, HEADS = 192, 128, 64, 32, 64, 4
HEAD_PAD = 128
ROPE_BASE = 10000.0
Q_SCALE = float((QK_NOPE + QK_ROPE) ** -0.5 * np.log2(np.e))
POOL_WINDOWS = (2, 4, 8, 16)
LRU_C = 8.0
N_EXP, TOP_K = 32, 4
LIMIT, ALPHA = 7.0, 1.702
EPS = 1e-6

LANES = 128
SUBLANES = 8
VMEM_LIMIT = 56 * 1024 * 1024

TM = 512
N_PT = T_P // TM
TILES_PER_DEC = DEC_SEQ // TM
TM_E = 512
N_SLOTS = T * TOP_K + N_EXP * TM_E
NB_E = N_SLOTS // TM_E
ROW_F = D // LANES
CH_C = 16
_REMAINDER_SIZES = (8, 4, 2, 1)
_COPY_SIZES = (CH_C,) + _REMAINDER_SIZES
_ZERO_SIZES = (8 * CH_C,) + _COPY_SIZES
MAX_CH = TM * TOP_K // CH_C + N_EXP
SEQ_PAD = 32
FRONT = 8

W_COLS = 1536
COL_Q, COL_KV, COL_KR = 1024, 1280, 1408


_PROMPT_ROWS = pl.BlockSpec((TM, D), lambda i: (jnp.minimum(i, N_PT - 1), 0))
_SAMPLE_ROWS = pl.BlockSpec((TM, D), lambda i: (jnp.maximum(i - N_PT, 0), 0))


def _cparams(sem, vmem=VMEM_LIMIT):
    return pltpu.CompilerParams(dimension_semantics=sem, vmem_limit_bytes=vmem)


def _bdot(a, b):
    return jnp.dot(a.astype(BF16), b.astype(BF16), preferred_element_type=F32)


def _split(a):
    hi = a.astype(BF16)
    lo = (a - hi.astype(F32)).astype(BF16)
    return hi, lo


def _dot3(a, b):
    ah, al = _split(a)
    bh, bl = _split(b)
    d = functools.partial(jnp.dot, preferred_element_type=F32)
    return d(ah, bh) + (d(al, bh) + d(ah, bl))


def _rms(x, n=None):
    n = x.shape[-1] if n is None else n
    return x * lax.rsqrt(jnp.sum(x * x, axis=-1, keepdims=True) * (1.0 / n) + EPS)


def _expm1(y):
    t = jnp.tanh(0.5 * y)
    return 2.0 * t / (1.0 - t)


def _sigmoid(x):
    return 0.5 * jnp.tanh(0.5 * x) + 0.5


def _tile_row(i):
    return jnp.where(i >= N_PT, 1 + (i - N_PT) // TILES_PER_DEC, 0)


def _mod_kernel(c_ref, w_ref, b_ref, o_ref):
    s = jax.nn.silu(c_ref[...])
    o_ref[0] = _dot3(s, w_ref[0]) + b_ref[0]


def _modulation(cvec, w_mod, b_mod):
    tn = 768
    n = 6 * D
    return pl.pallas_call(
        _mod_kernel,
        out_shape=jax.ShapeDtypeStruct((DEPTH, SUBLANES, n), F32),
        grid=(DEPTH, n // tn),
        in_specs=[pl.BlockSpec((SUBLANES, D), lambda l, j: (0, 0)),
                  pl.BlockSpec((1, D, tn), lambda l, j: (l, 0, j)),
                  pl.BlockSpec((1, 1, tn), lambda l, j: (l, 0, j))],
        out_specs=pl.BlockSpec((1, SUBLANES, tn), lambda l, j: (l, 0, j)),
        compiler_params=_cparams(("parallel", "parallel")),
        name="modulation",
    )(cvec, w_mod, b_mod.reshape(DEPTH, 1, n))


def _front_kernel(xp_ref, xs_ref, mod_ref, g_ref, w_ref, qg_ref, wuq_ref, kvg_ref, wkn_ref, wkr_ref,
                  wv_ref, cq_ref, saq_ref, sbq_ref, ck_ref, sak_ref, sbk_ref,
                  zmix_ref, q_ref, k_ref, v_ref, ckv_ref, kr_ref):
    i = pl.program_id(0)
    row = _tile_row(i)
    shift1 = mod_ref[pl.ds(row, 1), 0:D]
    scale1 = mod_ref[pl.ds(row, 1), D:2 * D]
    h = _rms(jnp.where(i >= N_PT, xs_ref[...], xp_ref[...])) * g_ref[...]
    h = h * (1.0 + scale1) + shift1
    z = jnp.dot(h.astype(BF16), w_ref[...], preferred_element_type=F32)
    zmix_ref[...] = z[:, 0:4 * GW]
    qn = _rms(z[:, COL_Q:COL_Q + 2 * LANES], Q_LORA) * qg_ref[...]
    q = _bdot(qn, wuq_ref[...])
    ckv = _rms(z[:, COL_KV:COL_KV + KV_LORA]) * kvg_ref[...]
    ckv_ref[...] = ckv
    ckv_b = ckv.astype(BF16)
    v_ref[...] = jnp.dot(ckv_b, wv_ref[...], preferred_element_type=F32).astype(BF16)
    kn = jnp.dot(ckv_b, wkn_ref[...], preferred_element_type=F32)
    kr = z[:, COL_KR:COL_KR + LANES]

    @pl.when(i < N_PT)
    def _():
        q_ref[...] = (q * Q_SCALE).astype(BF16)
        kr_ref[...] = kr[:, 0:QK_ROPE]
        k_ref[...] = (kn + _bdot(kr, wkr_ref[...])).astype(BF16)

    @pl.when(i >= N_PT)
    def _():
        wq = HEADS * HEAD_PAD
        half = QK_ROPE // 2
        qr = (q * cq_ref[...] + pltpu.roll(q, wq - half, 1) * saq_ref[...]
              + pltpu.roll(q, half, 1) * sbq_ref[...])
        krr = (kr * ck_ref[...] + pltpu.roll(kr, LANES - half, 1) * sak_ref[...]
               + pltpu.roll(kr, half, 1) * sbk_ref[...])
        q_ref[...] = (qr * Q_SCALE).astype(BF16)
        kr_ref[...] = krr[:, 0:QK_ROPE]
        k_ref[...] = (kn + _bdot(krr, wkr_ref[...])).astype(BF16)


def _front(x, mod_l, lw, rope):
    full = lambda shape: pl.BlockSpec(shape, lambda i: (0,) * len(shape))
    rows = lambda w: pl.BlockSpec((TM, w), lambda i: (i, 0))
    rope_rows = lambda w: pl.BlockSpec(
        (TM, w), lambda i: (jnp.maximum(i - N_PT, 0) % TILES_PER_DEC, 0))
    wq = HEADS * HEAD_PAD
    return pl.pallas_call(
        _front_kernel,
        out_shape=(jax.ShapeDtypeStruct((T, 4 * GW), F32),
                   jax.ShapeDtypeStruct((T, wq), BF16),
                   jax.ShapeDtypeStruct((T, wq), BF16),
                   jax.ShapeDtypeStruct((T, HEADS * V_DIM), BF16),
                   jax.ShapeDtypeStruct((T, KV_LORA), F32),
                   jax.ShapeDtypeStruct((T, QK_ROPE), F32)),
        grid=(T // TM,),
        in_specs=[_PROMPT_ROWS, _SAMPLE_ROWS, full((SUBLANES, 6 * D)), full((1, D)), full((D, W_COLS)),
                  full((1, 2 * LANES)), full((2 * LANES, wq)), full((1, KV_LORA)),
                  full((KV_LORA, wq)), full((LANES, wq)), full((KV_LORA, HEADS * V_DIM)),
                  rope_rows(wq), rope_rows(wq), rope_rows(wq),
                  rope_rows(LANES), rope_rows(LANES), rope_rows(LANES)],
        out_specs=(rows(4 * GW), rows(wq), rows(wq), rows(HEADS * V_DIM), rows(KV_LORA),
                   rows(QK_ROPE)),
        compiler_params=_cparams(("parallel",)),
        name="front",
    )(*x, mod_l, lw["attn_g"], lw["w_in"], lw["q_g"], lw["w_uq"], lw["kv_g"], lw["w_kn"],
      lw["w_kr"], lw["w_v"], *rope)


def _kvexp_kernel(ckv_ref, kr_ref, wkn_ref, wkr_ref, wv_ref, k_ref, v_ref):
    ckv_b = ckv_ref[...].astype(BF16)
    v_ref[...] = jnp.dot(ckv_b, wv_ref[...], preferred_element_type=F32).astype(BF16)
    kn = jnp.dot(ckv_b, wkn_ref[...], preferred_element_type=F32)
    k_ref[...] = (kn + _bdot(kr_ref[...], wkr_ref[...])).astype(BF16)


def _kv_expand(ckv, kr_pad, lw):
    n = ckv.shape[0]
    wq = HEADS * HEAD_PAD
    full = lambda shape: pl.BlockSpec(shape, lambda i: (0,) * len(shape))
    return pl.pallas_call(
        _kvexp_kernel,
        out_shape=(jax.ShapeDtypeStruct((n, wq), BF16),
                   jax.ShapeDtypeStruct((n, HEADS * V_DIM), BF16)),
        grid=(1,),
        in_specs=[full((n, KV_LORA)), full((n, LANES)), full((KV_LORA, wq)), full((LANES, wq)),
                  full((KV_LORA, HEADS * V_DIM))],
        out_specs=(full((n, wq)), full((n, HEADS * V_DIM))),
        compiler_params=_cparams(("arbitrary",)),
        name="kv_expand",
    )(ckv, kr_pad, lw["w_kn"], lw["w_kr"], lw["w_v"])


def _attn_kernel(q_ref, k_ref, v_ref, o_ref):
    outs = []
    for h in range(HEADS):
        qh = q_ref[:, h * HEAD_PAD:(h + 1) * HEAD_PAD]
        kh = k_ref[:, h * HEAD_PAD:(h + 1) * HEAD_PAD]
        s = lax.dot_general(qh, kh, (((1,), (1,)), ((), ())), preferred_element_type=F32)
        m = jnp.max(s, axis=-1, keepdims=True)
        p = jnp.exp2(s - m)
        l = jnp.sum(p, axis=-1, keepdims=True)
        vh = v_ref[:, h * V_DIM:(h + 1) * V_DIM]
        outs.append(jnp.dot(p.astype(BF16), vh, preferred_element_type=F32) / l)
    o_ref[...] = jnp.concatenate(outs, axis=-1)


def _attention(q, k, v, n_batch, seq, n_keys, tq, row0):
    nq = seq // tq
    wq = HEADS * HEAD_PAD
    return pl.pallas_call(
        _attn_kernel,
        out_shape=jax.ShapeDtypeStruct((n_batch * seq, GW), F32),
        grid=(n_batch, nq),
        in_specs=[pl.BlockSpec((tq, wq), lambda b, i: (row0 // tq + b * nq + i, 0)),
                  pl.BlockSpec((n_keys, wq), lambda b, i: (b, 0)),
                  pl.BlockSpec((n_keys, HEADS * V_DIM), lambda b, i: (b, 0))],
        out_specs=pl.BlockSpec((tq, GW), lambda b, i: (b * nq + i, 0)),
        compiler_params=_cparams(("parallel", "parallel")),
        name="attention",
    )(q, k, v)


def _fourier_kernel(z_ref, c_ref, s_ref, bdc_ref, bds_ref, w_ref, o_ref, xc_ref, xs_ref, *, norm):
    b = pl.program_id(1)

    @pl.when(pl.program_id(0) == 0)
    def _():
        xb = z_ref[...].astype(BF16)
        xc_ref[b] = _bdot(xb, bdc_ref[...]).astype(BF16)
        xs_ref[b] = _bdot(xb, bds_ref[...]).astype(BF16)

    f = (jnp.dot(c_ref[...].astype(BF16), xc_ref[b], preferred_element_type=F32)
         - jnp.dot(s_ref[...].astype(BF16), xs_ref[b], preferred_element_type=F32)) * norm
    o_ref[...] = _bdot(f, w_ref[...])


def _fourier(zmix, consts, lw, n_batch, seq, row0):
    ts = min(seq, 512)
    nj = seq // ts
    cmat, smat = consts
    full = lambda shape: pl.BlockSpec(shape, lambda j, b: (0,) * len(shape))
    z_spec = pl.BlockSpec((seq, GW), lambda j, b: (row0 // seq + jnp.where(j == 0, b, n_batch - 1), 1))
    return pl.pallas_call(
        functools.partial(_fourier_kernel, norm=float((seq * (GW // 4)) ** -0.5)),
        out_shape=jax.ShapeDtypeStruct((n_batch * seq, GW), F32),
        grid=(nj, n_batch),
        in_specs=[z_spec,
                  pl.BlockSpec((ts, seq), lambda j, b: (j, 0)),
                  pl.BlockSpec((ts, seq), lambda j, b: (j, 0)),
                  full((GW, GW)), full((GW, GW)), full((GW, GW))],
        out_specs=pl.BlockSpec((ts, GW), lambda j, b: (b * nj + j, 0)),
        scratch_shapes=[pltpu.VMEM((n_batch, seq, GW), BF16), pltpu.VMEM((n_batch, seq, GW), BF16)],
        compiler_params=_cparams(("arbitrary", "arbitrary")),
        name="fourier",
    )(zmix, cmat, smat, lw["bd_c"], lw["bd_s"], lw["w_f"])


def _seq_kernel(za_ref, zr_ref, zg_ref, h0_ref, icnt_ref, wp_ref, ps_ref, cw_ref, cb_ref, wa_ref,
                ba_ref, wx_ref, bx_ref, lam_ref, ya_ref, yd_ref, st_ref,
                pa_ref, pb_ref, xp_ref, a_ref, b_ref, *, seq):
    n = seq + SEQ_PAD
    span = seq + 2 * FRONT
    zeros_pad = jnp.zeros((n, GW), F32)

    za = za_ref[...]
    pa_ref[...] = zeros_pad
    pb_ref[...] = zeros_pad
    pa_ref[FRONT:FRONT + seq, :] = za
    pb_ref[0:span, :] = pa_ref[0:span, :] + pa_ref[1:span + 1, :]
    win2 = pb_ref[FRONT - 1:FRONT - 1 + seq, :]
    pa_ref[0:span, :] = pb_ref[0:span, :] + pb_ref[2:span + 2, :]
    win4 = pa_ref[FRONT - 2:FRONT - 2 + seq, :]
    pb_ref[0:span, :] = pa_ref[0:span, :] + pa_ref[4:span + 4, :]
    win8 = pb_ref[FRONT - 4:FRONT - 4 + seq, :]
    pa_ref[0:span, :] = pb_ref[0:span, :] + pb_ref[8:span + 8, :]
    win16 = pa_ref[FRONT - 8:FRONT - 8 + seq, :]
    grp = lax.broadcasted_iota(I32, (1, GW), 1) // (GW // 4)
    win = jnp.where(grp == 0, win2, jnp.where(grp == 1, win4, jnp.where(grp == 2, win8, win16)))
    dlt = win * icnt_ref[...] - za
    ya_ref[...] = _bdot(dlt, wp_ref[...]) * ps_ref[...]

    xp_ref[...] = zeros_pad
    xp_ref[FRONT:FRONT + seq, :] = zr_ref[...]
    xc = cb_ref[...] + cw_ref[0:1, :] * xp_ref[FRONT - 2:FRONT - 2 + seq, :]
    for kk in range(1, 4):
        xc = xc + cw_ref[kk:kk + 1, :] * xp_ref[FRONT - 2 + kk:FRONT - 2 + kk + seq, :]
    xcb = xc.astype(BF16)
    n_grp = seq // SUBLANES
    sub = lax.broadcasted_iota(I32, (n_grp, SUBLANES, GW), 1)

    total = None
    for d in range(2):
        r = _sigmoid(jnp.dot(xcb, wa_ref[d], preferred_element_type=F32) + ba_ref[d])
        ig = _sigmoid(jnp.dot(xcb, wx_ref[d], preferred_element_type=F32) + bx_ref[d])
        log_a = (-LRU_C) * r * jax.nn.softplus(-lam_ref[d])
        a = jnp.exp(log_a).reshape(n_grp, SUBLANES, GW)
        b = (jnp.sqrt(-_expm1(2.0 * log_a)) * (ig * xc)).reshape(n_grp, SUBLANES, GW)
        for k in (1, 2, 4):
            shift = k if d == 0 else SUBLANES - k
            m = sub >= k if d == 0 else sub < SUBLANES - k
            ap, bp = pltpu.roll(a, shift, 1), pltpu.roll(b, shift, 1)
            b = jnp.where(m, a * bp + b, b)
            a = jnp.where(m, a * ap, a)
        a_ref[...] = a.reshape(seq, GW)
        b_ref[...] = b.reshape(seq, GW)

        def step(g, carry, d=d):
            gi = g if d == 0 else n_grp - 1 - g
            off = pl.multiple_of(gi * SUBLANES, SUBLANES)
            hh = a_ref[pl.ds(off, SUBLANES), :] * carry + b_ref[pl.ds(off, SUBLANES), :]
            b_ref[pl.ds(off, SUBLANES), :] = hh
            edge = hh[SUBLANES - 1:SUBLANES, :] if d == 0 else hh[0:1, :]
            return jnp.broadcast_to(edge, (SUBLANES, GW))

        last = lax.fori_loop(0, n_grp, step, jnp.broadcast_to(h0_ref[0, d:d + 1, :], (SUBLANES, GW)))
        st_ref[0, d:d + 1, :] = last[0:1, :]
        total = b_ref[...] if total is None else total + b_ref[...]

    yd_ref[...] = total * jax.nn.gelu(zg_ref[...])


def _pool_inverse_counts(seq):
    pos = np.arange(seq)[:, None]
    half = np.repeat(np.array(POOL_WINDOWS) // 2, GW // len(POOL_WINDOWS))[None, :]
    cnt = np.minimum(pos + half, seq) - np.maximum(pos - half, 0)
    return jnp.asarray(1.0 / cnt, F32)


def _seq_mixers(zmix, h0, lw, n_batch, seq, row0):
    full = lambda shape: pl.BlockSpec(shape, lambda b: (0,) * len(shape))
    col = lambda c: pl.BlockSpec((seq, GW), lambda b: (row0 // seq + b, c))
    out_rows = pl.BlockSpec((seq, GW), lambda b: (b, 0))
    pad = pltpu.VMEM((seq + SEQ_PAD, GW), F32)
    return pl.pallas_call(
        functools.partial(_seq_kernel, seq=seq),
        out_shape=(jax.ShapeDtypeStruct((n_batch * seq, GW), F32),
                   jax.ShapeDtypeStruct((n_batch * seq, GW), F32),
                   jax.ShapeDtypeStruct((n_batch, 2, GW), F32)),
        grid=(n_batch,),
        in_specs=[col(0), col(2), col(3), pl.BlockSpec((1, 2, GW), lambda b: (b, 0, 0)),
                  full((seq, GW)), full((GW, GW)), full((1, GW)), full((4, GW)), full((1, GW)),
                  full((2, GW, GW)), full((2, 1, GW)), full((2, GW, GW)), full((2, 1, GW)),
                  full((2, 1, GW))],
        out_specs=(out_rows, out_rows, pl.BlockSpec((1, 2, GW), lambda b: (b, 0, 0))),
        scratch_shapes=[pad, pad, pad, pltpu.VMEM((seq, GW), F32), pltpu.VMEM((seq, GW), F32)],
        compiler_params=_cparams(("parallel",)),
        name="seq_mixers",
    )(zmix, zmix, zmix, h0, _pool_inverse_counts(seq), lw["bd_pool"], lw["pool_scale"], lw["conv_w"], lw["conv_b"],
      lw["bd_wa"], lw["lru_ba"], lw["bd_wx"], lw["lru_bx"], lw["lru_lam"])


def _post_kernel(xp_ref, xs_ref, mod_ref, yap_ref, ybp_ref, ycp_ref, ydp_ref, yas_ref, ybs_ref, ycs_ref,
                 yds_ref, og_ref, wo_ref, fg_ref, rw_ref, rb_ref, tri_ref, upper_ref,
                 x1_ref, h2p_ref, route_ref, prob_ref, tab_ref, cnt_ref, carry_ref):
    i = pl.program_id(0)
    is_s = i >= N_PT
    row = _tile_row(i)
    gate1 = mod_ref[pl.ds(row, 1), 2 * D:3 * D]
    shift2 = mod_ref[pl.ds(row, 1), 3 * D:4 * D]
    scale2 = mod_ref[pl.ds(row, 1), 4 * D:5 * D]

    @pl.when(i == 0)
    def _():
        carry_ref[...] = jnp.zeros_like(carry_ref)

    groups = []
    for gi, (p_ref, s_ref) in enumerate(((yap_ref, yas_ref), (ybp_ref, ybs_ref),
                                         (ycp_ref, ycs_ref), (ydp_ref, yds_ref))):
        y = jnp.where(is_s, s_ref[...], p_ref[...])
        groups.append((_rms(y) * og_ref[gi:gi + 1, :]).astype(BF16))
    ycat = jnp.concatenate(groups, axis=-1)
    x = jnp.where(is_s, xs_ref[...], xp_ref[...])
    x1 = x + gate1 * jnp.dot(ycat, wo_ref[...], preferred_element_type=F32)
    x1_ref[...] = x1
    h2 = _rms(x1) * fg_ref[...]
    h2 = h2 * (1.0 + scale2) + shift2

    for j in range(ROW_F):
        h2p_ref[pl.ds(j, TM, stride=ROW_F), :] = h2[:, j * LANES:(j + 1) * LANES]

    h_hi, h_lo = _split(h2)
    both = jnp.dot(h_hi, rw_ref[...], preferred_element_type=F32)
    cross = both[:, LANES:2 * LANES] + jnp.dot(h_lo, rw_ref[:, 0:LANES], preferred_element_type=F32)
    logits = both[:, 0:LANES] + cross + rb_ref[...]
    lane = lax.broadcasted_iota(I32, (TM, LANES), 1)
    lane_f = lane.astype(F32)
    neg = jnp.float32(-jnp.inf)
    cur = jnp.where(lane < N_EXP, logits, neg)
    sel, vals, idxs = [], [], []
    for _ in range(TOP_K):
        m = jnp.max(cur, axis=-1, keepdims=True)
        idx = jnp.min(jnp.where(cur == m, lane_f, float(LANES)), axis=-1, keepdims=True)
        hit = lane_f == idx
        sel.append(hit)
        vals.append(m)
        idxs.append(idx)
        cur = jnp.where(hit, neg, cur)
    exps = [jnp.exp(v - vals[0]) for v in vals]
    denom = exps[0] + exps[1] + exps[2] + exps[3]
    onehot = jnp.where(sel[0] | sel[1] | sel[2] | sel[3], 1.0, 0.0)
    cum_l = jnp.dot(tri_ref[...], onehot.astype(BF16), preferred_element_type=F32)
    before = carry_ref[0:1, :]
    cum = cum_l + before
    n_tile = cum_l[TM - 1:TM, :]
    nch = jnp.floor((n_tile + (CH_C - 1.0)) * (1.0 / CH_C))
    n_hi = jnp.floor(n_tile * (1.0 / CH_C))
    n_lo = n_tile - n_hi * CH_C
    row8 = lax.broadcasted_iota(I32, (SUBLANES, LANES), 0)
    stacked = jnp.where(row8 == 0, nch, jnp.where(row8 == 1, n_hi, jnp.where(row8 == 2, n_lo, 0.0)))
    sums = jnp.dot(stacked.astype(BF16), upper_ref[...], preferred_element_type=F32)
    base = sums[0:1, :]
    sort_base = sums[1:2, :] * CH_C + sums[2:3, :]
    pk = jnp.zeros((TM, LANES), I32)
    pf = jnp.zeros((TM, LANES), F32)
    for k in range(TOP_K):
        order = jnp.sum(jnp.where(sel[k], (sort_base + cum_l - 1.0) * ROW_F, 0.0), axis=-1,
                        keepdims=True).astype(I32)
        local = jnp.sum(jnp.where(sel[k], (base * CH_C + cum_l - 1.0) * ROW_F, 0.0), axis=-1,
                        keepdims=True).astype(I32)
        pk = jnp.where(lane == k, idxs[k].astype(I32), pk)
        pk = jnp.where(lane == TOP_K + k, order, pk)
        pk = jnp.where(lane == 2 * TOP_K + k, local, pk)
        pf = jnp.where(lane == k, exps[k] / denom, pf)
    route_ref[...] = pk.T[0:4 * TOP_K, :]
    prob_ref[...] = pf.T[0:SUBLANES, :]
    tab_ref[...] = jnp.where(row8 == 0, before, jnp.where(row8 == 1, nch,
                                                          jnp.where(row8 == 2, base, n_tile)))
    new_carry = jnp.broadcast_to(cum[TM - 1:TM, :], (SUBLANES, LANES))
    carry_ref[...] = new_carry
    cnt_ref[...] = new_carry


def _post(x, mod_l, ys_prompt, ys_sample, lw, tri, upper):
    full = lambda shape: pl.BlockSpec(shape, lambda i: (0,) * len(shape))
    rows = lambda w: pl.BlockSpec((TM, w), lambda i: (i, 0))
    prow = pl.BlockSpec((TM, GW), lambda i: (jnp.minimum(i, N_PT - 1), 0))
    srow = pl.BlockSpec((TM, GW), lambda i: (jnp.maximum(i - N_PT, 0), 0))
    return pl.pallas_call(
        _post_kernel,
        out_shape=(jax.ShapeDtypeStruct((T, D), F32),
                   jax.ShapeDtypeStruct((T * ROW_F, LANES), F32),
                   jax.ShapeDtypeStruct((4 * TOP_K, T), I32),
                   jax.ShapeDtypeStruct((SUBLANES, T), F32),
                   jax.ShapeDtypeStruct((T // TM * SUBLANES, LANES), F32),
                   jax.ShapeDtypeStruct((SUBLANES, LANES), F32)),
        grid=(T // TM,),
        in_specs=[_PROMPT_ROWS, _SAMPLE_ROWS, full((SUBLANES, 6 * D)),
                  prow, prow, prow, prow, srow, srow, srow, srow,
                  full((4, GW)), full((D, D)), full((1, D)), full((D, 2 * LANES)), full((1, LANES)),
                  full((TM, TM)), full((LANES, LANES))],
        out_specs=(rows(D), pl.BlockSpec((TM * ROW_F, LANES), lambda i: (i, 0)),
                   pl.BlockSpec((4 * TOP_K, TM), lambda i: (0, i)),
                   pl.BlockSpec((SUBLANES, TM), lambda i: (0, i)),
                   pl.BlockSpec((SUBLANES, LANES), lambda i: (i, 0)), full((SUBLANES, LANES))),
        scratch_shapes=[pltpu.VMEM((SUBLANES, LANES), F32)],
        compiler_params=_cparams(("arbitrary",)),
        name="post",
    )(*x, mod_l, *ys_prompt, *ys_sample, lw["out_g"], lw["w_out"], lw["ffn_g"], lw["router_w"],
      lw["router_b"], tri, upper)


def _slot_copies(src_ref, src_row, dst_ref, dst_row, n_slots, sems, start, sizes=_COPY_SIZES,
                 moving_src=True):
    def piece(size, off, sem):
        rows = size * ROW_F
        src_off = src_row + off * ROW_F if moving_src else src_row
        cp = pltpu.make_async_copy(
            src_ref.at[pl.ds(pl.multiple_of(src_off, ROW_F), rows)],
            dst_ref.at[pl.ds(pl.multiple_of(dst_row + off * ROW_F, ROW_F), rows)], sem)
        if start:
            cp.start()
        else:
            cp.wait()

    done = 0
    for cls, size in enumerate(sizes):
        left = n_slots - done
        if size in _REMAINDER_SIZES:
            @pl.when((left & size) != 0)
            def _(cls=cls, size=size, done=done):
                piece(size, done, sems.at[cls])
            done = done + (left & size)
        else:
            count = left // size

            def body(c, carry, cls=cls, size=size, done=done):
                piece(size, done + c * size, sems.at[cls])
                return carry
            lax.fori_loop(0, count, body, 0)
            done = done + count * size


def _dispatch_kernel(order_ref, rstart_ref, rlen_ref, pstart_ref, plen_ref, h_ref, xs_hbm,
                     sort_ref, zero_ref, sems, pad_sems):
    tb = pl.program_id(0)
    last = pl.num_programs(0) - 1
    slot = tb % 2

    def run_copies(tile, buf, start):
        def run(e, local_row):
            n = rlen_ref[tile * N_EXP + e]
            _slot_copies(sort_ref.at[buf], local_row, xs_hbm, rstart_ref[tile * N_EXP + e] * ROW_F, n,
                         sems.at[buf], start)
            return local_row + n * ROW_F
        lax.fori_loop(0, N_EXP, run, 0)

    @pl.when(tb >= 2)
    def _():
        run_copies(tb - 2, slot, start=False)

    def tokens(t8, carry):
        r0 = pl.multiple_of(t8 * SUBLANES, SUBLANES)
        for u in range(SUBLANES):
            row = h_ref[pl.ds(pl.multiple_of(r0 * ROW_F, SUBLANES * ROW_F) + u * ROW_F, ROW_F), :]
            for k in range(TOP_K):
                dst = pl.multiple_of(order_ref[k * T + tb * TM + r0 + u], ROW_F)
                sort_ref[slot, pl.ds(dst, ROW_F), :] = row
        return carry
    lax.fori_loop(0, TM // SUBLANES, tokens, 0)
    run_copies(tb, slot, start=True)

    @pl.when(tb == last)
    def _():
        zero_ref[...] = jnp.zeros_like(zero_ref)
        for start in (True, False):
            def pad(e, carry, start=start):
                _slot_copies(zero_ref, 0, xs_hbm, pstart_ref[e] * ROW_F, plen_ref[e], pad_sems, start,
                             sizes=_ZERO_SIZES, moving_src=False)
                return carry
            lax.fori_loop(0, N_EXP + 1, pad, 0)
        run_copies(tb - 1, 1 - slot, start=False)
        run_copies(tb, slot, start=False)


def _dispatch(order, run_start, run_len, pad_start, pad_len, h2lin):
    return pl.pallas_call(
        _dispatch_kernel,
        out_shape=jax.ShapeDtypeStruct((N_SLOTS * ROW_F, LANES), F32),
        grid_spec=pltpu.PrefetchScalarGridSpec(
            num_scalar_prefetch=5,
            grid=(T // TM,),
            in_specs=[pl.BlockSpec((TM * ROW_F, LANES), lambda i, *_: (i, 0))],
            out_specs=pl.BlockSpec(memory_space=pl.ANY),
            scratch_shapes=[pltpu.VMEM((2, TM * TOP_K * ROW_F, LANES), F32),
                            pltpu.VMEM((_ZERO_SIZES[0] * ROW_F, LANES), F32),
                            pltpu.SemaphoreType.DMA((2, len(_COPY_SIZES))),
                            pltpu.SemaphoreType.DMA((len(_ZERO_SIZES),))]),
        compiler_params=_cparams(("arbitrary",)),
        name="moe_dispatch",
    )(order, run_start, run_len, pad_start, pad_len, h2lin)


def _expert_rows(n, x_ref, w_ref, slot, bg_ref, bu_ref, bd_ref, o_ref):
    x = jnp.concatenate([x_ref[pl.ds(j, n, stride=ROW_F), :].astype(BF16) for j in range(ROW_F)],
                        axis=-1)
    g = jnp.dot(x, w_ref[slot, 0].astype(BF16), preferred_element_type=F32) + bg_ref[0, 0]
    u = jnp.dot(x, w_ref[slot, 1].astype(BF16), preferred_element_type=F32) + bu_ref[0, 0]
    g = jnp.minimum(g, LIMIT)
    u = jnp.clip(u, -LIMIT, LIMIT)
    act = (u + 1.0) * (g * jax.nn.sigmoid(ALPHA * g))
    y = jnp.dot(act.astype(BF16), w_ref[slot, 2].astype(BF16),
                preferred_element_type=F32) + bd_ref[0, 0]
    for j in range(ROW_F):
        o_ref[pl.ds(j, n, stride=ROW_F), :] = y[:, j * LANES:(j + 1) * LANES]
    if n < TM_E:
        o_ref[n * ROW_F:TM_E * ROW_F, :] = jnp.zeros(((TM_E - n) * ROW_F, LANES), F32)


def _expert_kernel(be_ref, nused_ref, valid_ref, first_ref, next_ref, par_ref,
                   x_ref, wg_hbm, bg_ref, wu_hbm, bu_ref, wd_hbm, bd_ref, o_ref, w_ref, sems, *, layer):
    i = pl.program_id(0)
    ib = jnp.minimum(i, NB_E - 1)
    live = i < nused_ref[0]
    valid = valid_ref[ib]
    expert = be_ref[ib]
    slot = par_ref[ib]

    def fetch(e, s):
        return [pltpu.make_async_copy(src.at[layer, e], w_ref.at[s, m], sems.at[s, m])
                for m, src in enumerate((wg_hbm, wu_hbm, wd_hbm))]

    @pl.when(i == 0)
    def _():
        for cp in fetch(expert, slot):
            cp.start()

    @pl.when(live & (first_ref[ib] == 1))
    def _():
        @pl.when(next_ref[ib] >= 0)
        def _():
            for cp in fetch(next_ref[ib], 1 - slot):
                cp.start(priority=1)

        for cp in fetch(expert, slot):
            cp.wait()

    @pl.when(live & (valid > TM_E // 2))
    def _():
        _expert_rows(TM_E, x_ref, w_ref, slot, bg_ref, bu_ref, bd_ref, o_ref)

    @pl.when(live & (valid <= TM_E // 2))
    def _():
        _expert_rows(TM_E // 2, x_ref, w_ref, slot, bg_ref, bu_ref, bd_ref, o_ref)

    @pl.when(jnp.logical_not(live))
    def _():
        o_ref[...] = jnp.zeros_like(o_ref)


def _experts(l, sched, xs, p):
    last = NB_E - 1
    bspec = pl.BlockSpec((1, 1, 1, D), lambda i, be, *_: (l, be[jnp.minimum(i, last)], 0, 0))
    hbm = pl.BlockSpec(memory_space=pl.ANY)
    bias = lambda b: b.reshape(DEPTH, N_EXP, 1, D)
    return pl.pallas_call(
        functools.partial(_expert_kernel, layer=l),
        out_shape=jax.ShapeDtypeStruct(((NB_E + 1) * TM_E * ROW_F, LANES), F32),
        grid_spec=pltpu.PrefetchScalarGridSpec(
            num_scalar_prefetch=len(sched),
            grid=(NB_E + 1,),
            in_specs=[pl.BlockSpec((TM_E * ROW_F, LANES), lambda i, *_: (jnp.minimum(i, last), 0)),
                      hbm, bspec, hbm, bspec, hbm, bspec],
            out_specs=pl.BlockSpec((TM_E * ROW_F, LANES), lambda i, *_: (i, 0)),
            scratch_shapes=[pltpu.VMEM((2, 3, D, D), F32), pltpu.SemaphoreType.DMA((2, 3))]),
        compiler_params=_cparams(("arbitrary",)),
        name="moe_experts",
    )(*sched, xs, p["w_gate"], bias(p["b_gate"]), p["w_up"], bias(p["b_up"]),
      p["w_down"], bias(p["b_down"]))


def _combine_kernel(loc_ref, p_ref, csrc_ref, nch_ref, ys_hbm, x_ref, mod_ref, fg_ref, *rest, final):
    *outs, buf_ref, acc_ref, sems = rest
    tb = pl.program_id(0)
    n_tb = pl.num_programs(0)
    slot = tb % 2
    row = _tile_row(tb)
    gate2 = mod_ref[pl.ds(row, 1), 5 * D:6 * D]
    chunk_rows = CH_C * ROW_F

    def chunk_copy(t, c, s):
        src = pl.multiple_of(csrc_ref[t * MAX_CH + c] * ROW_F, ROW_F)
        dst = pl.multiple_of(c * chunk_rows, chunk_rows)
        return pltpu.make_async_copy(ys_hbm.at[pl.ds(src, chunk_rows)],
                                     buf_ref.at[s, pl.ds(dst, chunk_rows)], sems.at[s])

    def issue(t, s):
        def body(c, carry):
            chunk_copy(t, c, s).start()
            return carry
        lax.fori_loop(0, nch_ref[t], body, 0)

    @pl.when(tb == 0)
    def _():
        issue(0, 0)

    @pl.when(tb + 1 < n_tb)
    def _():
        issue(tb + 1, 1 - slot)

    def drain(c, carry):
        chunk_copy(tb, c, slot).wait()
        return carry
    lax.fori_loop(0, nch_ref[tb], drain, 0)

    def tokens(r8, carry):
        r0 = pl.multiple_of(r8 * SUBLANES, SUBLANES)
        for u in range(SUBLANES):
            tok = tb * TM + r0 + u
            acc = None
            for k in range(TOP_K):
                off = pl.multiple_of(loc_ref[k * T + tok], ROW_F)
                term = buf_ref[slot, pl.ds(off, ROW_F), :] * p_ref[k * T + tok]
                acc = term if acc is None else acc + term
            dst = pl.multiple_of(r0 * ROW_F, SUBLANES * ROW_F) + u * ROW_F
            acc_ref[pl.ds(dst, ROW_F), :] = acc
        return carry
    lax.fori_loop(0, TM // SUBLANES, tokens, 0)

    moe = jnp.concatenate([acc_ref[pl.ds(j, TM, stride=ROW_F), :] for j in range(ROW_F)], axis=-1)
    x2 = x_ref[...] + gate2 * moe
    if final:
        x2 = _rms(x2) * fg_ref[...]
    prompt_ref, sample_ref = outs

    @pl.when(tb < N_PT)
    def _():
        prompt_ref[...] = x2

    @pl.when(tb >= N_PT)
    def _():
        sample_ref[...] = x2


def _combine(loc, top_p, chunk_src, n_chunks, ys, x1, mod_l, final_g, final):
    full = lambda shape: pl.BlockSpec(shape, lambda i, *_: (0,) * len(shape))
    rows = lambda w: pl.BlockSpec((TM, w), lambda i, *_: (i, 0))
    out_shape = (jax.ShapeDtypeStruct((T_P, D), F32), jax.ShapeDtypeStruct((T_S, D), F32))
    out_specs = (pl.BlockSpec((TM, D), lambda i, *_: (jnp.minimum(i, N_PT - 1), 0)),
                 pl.BlockSpec((TM, D), lambda i, *_: (jnp.maximum(i - N_PT, 0), 0)))
    return pl.pallas_call(
        functools.partial(_combine_kernel, final=final),
        out_shape=out_shape,
        grid_spec=pltpu.PrefetchScalarGridSpec(
            num_scalar_prefetch=4,
            grid=(T // TM,),
            in_specs=[pl.BlockSpec(memory_space=pl.ANY), rows(D), full((SUBLANES, 6 * D)),
                      full((1, D))],
            out_specs=out_specs,
            scratch_shapes=[pltpu.VMEM((2, MAX_CH * CH_C * ROW_F, LANES), F32),
                            pltpu.VMEM((TM * ROW_F, LANES), F32),
                            pltpu.SemaphoreType.DMA((2,))]),
        compiler_params=_cparams(("arbitrary",)),
        name="moe_combine",
    )(loc, top_p, chunk_src, n_chunks, ys, x1, mod_l, final_g)


def _rope_tables():
    rows = DEC_SEQ // GRID_W
    r = np.repeat(np.arange(rows, dtype=np.float64), GRID_W)
    c = np.tile(np.arange(GRID_W, dtype=np.float64), rows)
    n_freq = QK_ROPE // 4
    inv = (np.float32(ROPE_BASE) ** (-np.arange(n_freq, dtype=np.float32) / n_freq)).astype(np.float64)
    ang = np.concatenate([r[:, None] * inv, c[:, None] * inv], axis=-1).astype(np.float32)
    cos, sin = np.cos(ang.astype(np.float64)), np.sin(ang.astype(np.float64))
    half = QK_ROPE // 2

    def place(width, start):
        cf = np.ones((DEC_SEQ, width), np.float32)
        sa = np.zeros((DEC_SEQ, width), np.float32)
        sb = np.zeros((DEC_SEQ, width), np.float32)
        for s0 in start:
            cf[:, s0:s0 + half] = cos
            cf[:, s0 + half:s0 + 2 * half] = cos
            sa[:, s0:s0 + half] = -sin
            sb[:, s0 + half:s0 + 2 * half] = sin
        return jnp.asarray(cf), jnp.asarray(sa), jnp.asarray(sb)

    return (*place(HEADS * HEAD_PAD, [h * HEAD_PAD + QK_NOPE for h in range(HEADS)]),
            *place(LANES, [0]))


def _dft_tables(seq):
    kn = (np.arange(seq, dtype=np.int64)[:, None] * np.arange(seq, dtype=np.int64)[None, :]) % seq
    ang = 2.0 * np.pi * kn.astype(np.float64) / seq
    return jnp.asarray(np.cos(ang), F32), jnp.asarray(np.sin(ang), F32)


def _block_diag(blocks):
    g, n, _ = blocks.shape
    eye = jnp.eye(g, dtype=blocks.dtype)
    return jnp.einsum("gij,gh->gihj", blocks, eye).reshape(g * n, g * n)


def _layer_weights(l, p):
    w_in = p["w_in"][l]
    za, zq, zkv, zkr, zf, zr, zg = jnp.split(
        w_in, np.cumsum([GW, Q_LORA, KV_LORA, QK_ROPE, GW, GW])[:], axis=1)
    zpad = lambda n: jnp.zeros((D, n), F32)
    w_cols = jnp.concatenate([za, zf, zr, zg, zq, zpad(2 * LANES - Q_LORA), zkv, zkr,
                              zpad(LANES - QK_ROPE)], axis=1).astype(BF16)
    wq = HEADS * HEAD_PAD
    w_uq = p["w_uq"][l].reshape(Q_LORA, HEADS, QK_NOPE + QK_ROPE)
    w_uq = jnp.pad(w_uq, ((0, 2 * LANES - Q_LORA), (0, 0), (0, HEAD_PAD - QK_NOPE - QK_ROPE)))
    w_ukv = p["w_ukv"][l].reshape(KV_LORA, HEADS, QK_NOPE + V_DIM)
    w_kn = jnp.pad(w_ukv[:, :, :QK_NOPE], ((0, 0), (0, 0), (0, HEAD_PAD - QK_NOPE)))
    w_v = w_ukv[:, :, QK_NOPE:]
    place = np.zeros((LANES, HEADS, HEAD_PAD), np.float32)
    for h in range(HEADS):
        place[np.arange(QK_ROPE), h, QK_NOPE + np.arange(QK_ROPE)] = 1.0
    c64 = np.arange(GW // 4, dtype=np.int64)
    ang = 2.0 * np.pi * ((c64[:, None] * c64[None, :]) % (GW // 4)).astype(np.float64) / (GW // 4)
    four = lambda m: jnp.asarray(np.broadcast_to(m, (4,) + m.shape), F32)
    router_hi, router_lo = _split(jnp.pad(p["router_w"][l], ((0, 0), (0, LANES - N_EXP))))
    router_w = jnp.concatenate([router_hi, router_lo], axis=1)
    router_b = jnp.pad(p["router_b"][l], (0, LANES - N_EXP)).reshape(1, LANES)
    return {
        "attn_g": p["attn_norm_g"][l].reshape(1, D),
        "w_in": w_cols,
        "q_g": jnp.pad(p["q_norm_g"][l], (0, 2 * LANES - Q_LORA)).reshape(1, 2 * LANES),
        "w_uq": w_uq.reshape(2 * LANES, wq).astype(BF16),
        "kv_g": p["kv_norm_g"][l].reshape(1, KV_LORA),
        "w_kn": w_kn.reshape(KV_LORA, wq).astype(BF16),
        "w_kr": jnp.asarray(place.reshape(LANES, wq), BF16),
        "w_v": w_v.reshape(KV_LORA, HEADS * V_DIM).astype(BF16),
        "bd_c": _block_diag(four(np.cos(ang))),
        "bd_s": _block_diag(four(np.sin(ang))),
        "w_f": p["fourier_w"][l].astype(BF16),
        "bd_pool": _block_diag(p["pool_w"][l]).astype(BF16),
        "pool_scale": p["pool_scale"][l].reshape(1, GW),
        "conv_w": p["conv_w"][l],
        "conv_b": p["conv_b"][l].reshape(1, GW),
        "bd_wa": jnp.stack([_block_diag(p["lru_wa"][l, d]) for d in range(2)]).astype(BF16),
        "bd_wx": jnp.stack([_block_diag(p["lru_wx"][l, d]) for d in range(2)]).astype(BF16),
        "lru_ba": p["lru_ba"][l].reshape(2, 1, GW),
        "lru_bx": p["lru_bx"][l].reshape(2, 1, GW),
        "lru_lam": p["lru_lambda"][l].reshape(2, 1, GW),
        "out_g": p["out_norm_g"][l],
        "w_out": p["w_out"][l].astype(BF16),
        "ffn_g": p["ffn_norm_g"][l].reshape(1, D),
        "router_w": router_w,
        "router_b": router_b,
    }


def _routing_tables(route, prob, counts, tab):
    order = route[TOP_K:2 * TOP_K].reshape(-1)
    top_p = prob[0:TOP_K].reshape(-1)
    loc = route[2 * TOP_K:3 * TOP_K].reshape(-1)
    counts = counts.astype(I32)
    padded = (counts + TM_E - 1) // TM_E * TM_E
    pad_ends = jnp.cumsum(padded)
    pad_starts = pad_ends - padded
    experts = jnp.arange(N_EXP, dtype=I32)
    n_used = (pad_ends[-1] // TM_E).astype(I32)
    blk = jnp.minimum(jnp.arange(NB_E, dtype=I32), n_used - 1) * TM_E
    block_e = jnp.minimum(jnp.sum(pad_ends[None, :] <= blk[:, None], axis=-1), N_EXP - 1).astype(I32)
    of_block = lambda a: jnp.sum(jnp.where(block_e[:, None] == experts, a, 0), axis=-1)
    valid = jnp.clip(of_block(pad_starts + counts) - blk, 0, TM_E).astype(I32)
    first = (blk == of_block(pad_starts)).astype(I32)
    later = (experts[None, :] > experts[:, None]) & (counts[None, :] > 0)
    next_of = jnp.min(jnp.where(later, experts[None, :], N_EXP), axis=-1)
    next_e = of_block(jnp.where(next_of < N_EXP, next_of, -1)).astype(I32)
    parity = (of_block(jnp.cumsum((counts > 0).astype(I32))) % 2).astype(I32)
    sched = (block_e, n_used.reshape(1), valid, first, next_e, parity)
    tab = tab.reshape(T // TM, SUBLANES, LANES)[:, :, :N_EXP].astype(I32)
    before, nch, base, run_len = tab[:, 0], tab[:, 1], tab[:, 2], tab[:, 3]
    run_start = pad_starts[None, :] + before
    empty_start = jnp.append(pad_starts + counts, pad_ends[-1])
    empty_len = jnp.append(padded - counts, N_SLOTS - pad_ends[-1])
    dispatch = (order, run_start.reshape(-1), run_len.reshape(-1), empty_start, empty_len)
    ends = base + nch
    ci = jnp.arange(MAX_CH, dtype=I32)
    e_of = jnp.minimum(jnp.sum(ends[:, None, :] <= ci[None, :, None], axis=-1), N_EXP - 1)
    pick = lambda a: jnp.sum(jnp.where(e_of[:, :, None] == experts, a[:, None, :], 0), axis=-1)
    chunk_src = jnp.clip(pick(run_start) + (ci[None, :] - pick(base)) * CH_C, 0, N_SLOTS)
    return dispatch, sched, loc, top_p, chunk_src.reshape(-1), ends[:, N_EXP - 1]


def kernel(x_prompt, x_sample, cache_ckv, cache_krope, state_lru, c, c_ctx, w_mod, b_mod, attn_norm_g, w_in, pool_w, pool_scale, q_norm_g, w_uq, kv_norm_g, w_ukv, fourier_w, conv_w, conv_b, lru_wa, lru_ba, lru_wx, lru_bx, lru_lambda, out_norm_g, w_out, ffn_norm_g, router_w, router_b, w_gate, b_gate, w_up, b_up, w_down, b_down, final_norm_g):
    params = dict(attn_norm_g=attn_norm_g, w_in=w_in, pool_w=pool_w, pool_scale=pool_scale,
                  q_norm_g=q_norm_g, w_uq=w_uq, kv_norm_g=kv_norm_g, w_ukv=w_ukv,
                  fourier_w=fourier_w, conv_w=conv_w, conv_b=conv_b, lru_wa=lru_wa, lru_ba=lru_ba,
                  lru_wx=lru_wx, lru_bx=lru_bx, lru_lambda=lru_lambda, out_norm_g=out_norm_g,
                  w_out=w_out, ffn_norm_g=ffn_norm_g, router_w=router_w, router_b=router_b,
                  w_gate=w_gate, b_gate=b_gate, w_up=w_up, b_up=b_up, w_down=w_down, b_down=b_down)
    x = (x_prompt.reshape(T_P, D), x_sample.reshape(T_S, D))
    cvec = jnp.concatenate([c_ctx[None, :], c, jnp.zeros((SUBLANES - 1 - DEC_BATCH, D), F32)], axis=0)
    mod = _modulation(cvec, w_mod, b_mod)
    rope = _rope_tables()
    dft_p, dft_s = _dft_tables(SEQ), _dft_tables(DEC_SEQ)
    tri = jnp.asarray(np.tril(np.ones((TM, TM), np.float32)), BF16)
    upper = jnp.asarray(np.triu(np.ones((LANES, LANES), np.float32), 1), BF16)
    final_g = final_norm_g.reshape(1, D)
    h0_prompt = jnp.zeros((BATCH, 2, GW), F32)
    n_keys = PAST + DEC_SEQ

    new_ckv, new_krope, new_lru = [], [], []
    for l in range(DEPTH):
        lw = _layer_weights(l, params)
        zmix, q, k, v, ckv, kr = _front(x, mod[l], lw, rope)
        new_ckv.append(ckv[:T_P].reshape(BATCH, SEQ, KV_LORA))
        new_krope.append(kr[:T_P].reshape(BATCH, SEQ, QK_ROPE))

        yb_p = _attention(q, k, v, BATCH, SEQ, SEQ, SEQ, 0)
        yc_p = _fourier(zmix, dft_p, lw, BATCH, SEQ, 0)
        ya_p, yd_p, st_p = _seq_mixers(zmix, h0_prompt, lw, BATCH, SEQ, 0)
        new_lru.append(st_p)

        kr_ctx = jnp.pad(cache_krope[:, l].reshape(DEC_BATCH * PAST, QK_ROPE),
                         ((0, 0), (0, LANES - QK_ROPE)))
        k_ctx, v_ctx = _kv_expand(cache_ckv[:, l].reshape(DEC_BATCH * PAST, KV_LORA), kr_ctx, lw)
        k_s = jnp.concatenate([k_ctx.reshape(DEC_BATCH, PAST, -1),
                               k[T_P:].reshape(DEC_BATCH, DEC_SEQ, -1)], axis=1)
        v_s = jnp.concatenate([v_ctx.reshape(DEC_BATCH, PAST, -1),
                               v[T_P:].reshape(DEC_BATCH, DEC_SEQ, -1)], axis=1)
        yb_s = _attention(q, k_s.reshape(DEC_BATCH * n_keys, -1), v_s.reshape(DEC_BATCH * n_keys, -1),
                          DEC_BATCH, DEC_SEQ, n_keys, 512, T_P)
        yc_s = _fourier(zmix, dft_s, lw, DEC_BATCH, DEC_SEQ, T_P)
        ya_s, yd_s, _ = _seq_mixers(zmix, state_lru[:, l], lw, DEC_BATCH, DEC_SEQ, T_P)

        x1, h2lin, route, prob, tab, counts = _post(
            x, mod[l], (ya_p, yb_p, yc_p, yd_p), (ya_s, yb_s, yc_s, yd_s), lw, tri, upper)
        dispatch, sched, loc, top_p, chunk_src, n_chunks = _routing_tables(
            route, prob, counts[0, :N_EXP], tab)
        xs = _dispatch(*dispatch, h2lin)
        ys = _experts(l, sched, xs, params)
        x = _combine(loc, top_p, chunk_src, n_chunks, ys, x1, mod[l], final_g,
                     final=(l == DEPTH - 1))

    y_prompt, y_sample = x
    return (y_prompt.reshape(BATCH, SEQ, D), y_sample.reshape(DEC_BATCH, DEC_SEQ, D),
            jnp.stack(new_ckv, axis=1), jnp.stack(new_krope, axis=1), jnp.stack(new_lru, axis=1))
```

```python
import functools

import numpy as np
import jax
import jax.numpy as jnp
from jax import lax
from jax.experimental import pallas as pl
from jax.experimental.pallas import tpu as pltpu

F32 = jnp.float32
BF16 = jnp.bfloat16
I32 = jnp.int32

D = 1024
BATCH, SEQ = 32, 256
DEC_BATCH, DEC_SEQ, PAST = 2, 2048, 512
T_P = BATCH * SEQ
T_S = DEC_BATCH * DEC_SEQ
T = T_P + T_S
DEPTH = 2
GRID_W = 64
GW = 256
Q_LORA, KV_LORA, QK_NOPE, QK_ROPE, V_DIM, HEADS = 192, 128, 64, 32, 64, 4
HEAD_PAD = 128
ROPE_BASE = 10000.0
Q_SCALE = float((QK_NOPE + QK_ROPE) ** -0.5 * np.log2(np.e))
POOL_WINDOWS = (2, 4, 8, 16)
LRU_C = 8.0
N_EXP, TOP_K = 32, 4
LIMIT, ALPHA = 7.0, 1.702
EPS = 1e-6

LANES = 128
SUBLANES = 8
VMEM_LIMIT = 56 * 1024 * 1024

TM = 512
N_PT = T_P // TM
TILES_PER_DEC = DEC_SEQ // TM
TM_E = 512
EXPERT_ROWS = (TM_E, TM_E // 2, TM_E // 4)
N_SLOTS = T * TOP_K + N_EXP * TM_E
NB_E = N_SLOTS // TM_E
ROW_F = D // LANES
GATHER_VMEM_LIMIT = 60 * 1024 * 1024
CH_C = 16
MAX_CH = TM * TOP_K // CH_C + N_EXP
GATHER_UNROLL = 16
PROMPT_GROUP = 4
SEQ_PAD = 32
FRONT = 8

W_COLS = 1536
COL_Q, COL_KV, COL_KR = 1024, 1280, 1408


_PROMPT_ROWS = pl.BlockSpec((TM, D), lambda i: (jnp.minimum(i, N_PT - 1), 0))
_SAMPLE_ROWS = pl.BlockSpec((TM, D), lambda i: (jnp.maximum(i - N_PT, 0), 0))


def _cparams(sem, vmem=VMEM_LIMIT):
    return pltpu.CompilerParams(dimension_semantics=sem, vmem_limit_bytes=vmem)


def _bdot(a, b):
    return jnp.dot(a.astype(BF16), b.astype(BF16), preferred_element_type=F32)


def _split(a):
    hi = a.astype(BF16)
    lo = (a - hi.astype(F32)).astype(BF16)
    return hi, lo


def _dot3(a, b):
    ah, al = _split(a)
    bh, bl = _split(b)
    d = functools.partial(jnp.dot, preferred_element_type=F32)
    return d(ah, bh) + (d(al, bh) + d(ah, bl))


def _rms(x, n=None):
    n = x.shape[-1] if n is None else n
    return x * lax.rsqrt(jnp.sum(x * x, axis=-1, keepdims=True) * (1.0 / n) + EPS)


def _neg_expm1_double(x):
    t = jnp.tanh(x)
    return -2.0 * t / (1.0 - t)


def _sigmoid(x):
    return 0.5 * jnp.tanh(0.5 * x) + 0.5


def _tile_row(i):
    return jnp.where(i >= N_PT, 1 + (i - N_PT) // TILES_PER_DEC, 0)


def _mod_kernel(c_ref, w_ref, b_ref, o_ref):
    s = jax.nn.silu(c_ref[...])
    o_ref[0] = _dot3(s, w_ref[0]) + b_ref[0]


def _modulation(cvec, w_mod, b_mod):
    tn = 768
    n = 6 * D
    return pl.pallas_call(
        _mod_kernel,
        out_shape=jax.ShapeDtypeStruct((DEPTH, SUBLANES, n), F32),
        grid=(DEPTH, n // tn),
        in_specs=[pl.BlockSpec((SUBLANES, D), lambda l, j: (0, 0)),
                  pl.BlockSpec((1, D, tn), lambda l, j: (l, 0, j)),
                  pl.BlockSpec((1, 1, tn), lambda l, j: (l, 0, j))],
        out_specs=pl.BlockSpec((1, SUBLANES, tn), lambda l, j: (l, 0, j)),
        compiler_params=_cparams(("parallel", "parallel")),
        name="modulation",
    )(cvec, w_mod, b_mod.reshape(DEPTH, 1, n))


def _front_kernel(xp_ref, xs_ref, mod_ref, g_ref, w_ref, qg_ref, wuq_ref, kvg_ref, wkn_ref, wkr_ref,
                  wv_ref, cq_ref, saq_ref, sbq_ref, ck_ref, sak_ref, sbk_ref,
                  zmix_ref, q_ref, k_ref, v_ref, ckv_ref, kr_ref):
    i = pl.program_id(0)
    row = _tile_row(i)
    shift1 = mod_ref[pl.ds(row, 1), 0:D]
    scale1 = mod_ref[pl.ds(row, 1), D:2 * D]
    h = _rms(jnp.where(i >= N_PT, xs_ref[...], xp_ref[...])) * g_ref[...]
    h = h * (1.0 + scale1) + shift1
    z = jnp.dot(h.astype(BF16), w_ref[...], preferred_element_type=F32)
    zmix_ref[...] = z[:, 0:4 * GW]
    qn = _rms(z[:, COL_Q:COL_Q + 2 * LANES], Q_LORA) * qg_ref[...]
    q = _bdot(qn, wuq_ref[...])
    ckv = _rms(z[:, COL_KV:COL_KV + KV_LORA]) * kvg_ref[...]
    ckv_ref[...] = ckv
    ckv_b = ckv.astype(BF16)
    v_ref[...] = jnp.dot(ckv_b, wv_ref[...], preferred_element_type=F32).astype(BF16)
    kn = jnp.dot(ckv_b, wkn_ref[...], preferred_element_type=F32)
    kr = z[:, COL_KR:COL_KR + LANES]

    @pl.when(i < N_PT)
    def _():
        q_ref[...] = (q * Q_SCALE).astype(BF16)
        kr_ref[...] = kr[:, 0:QK_ROPE]
        k_ref[...] = (kn + _bdot(kr, wkr_ref[...])).astype(BF16)

    @pl.when(i >= N_PT)
    def _():
        wq = HEADS * HEAD_PAD
        half = QK_ROPE // 2
        qr = (q * cq_ref[...] + pltpu.roll(q, wq - half, 1) * saq_ref[...]
              + pltpu.roll(q, half, 1) * sbq_ref[...])
        krr = (kr * ck_ref[...] + pltpu.roll(kr, LANES - half, 1) * sak_ref[...]
               + pltpu.roll(kr, half, 1) * sbk_ref[...])
        q_ref[...] = (qr * Q_SCALE).astype(BF16)
        kr_ref[...] = krr[:, 0:QK_ROPE]
        k_ref[...] = (kn + _bdot(krr, wkr_ref[...])).astype(BF16)


def _front(x, mod_l, lw, rope):
    full = lambda shape: pl.BlockSpec(shape, lambda i: (0,) * len(shape))
    rows = lambda w: pl.BlockSpec((TM, w), lambda i: (i, 0))
    rope_rows = lambda w: pl.BlockSpec(
        (TM, w), lambda i: (jnp.maximum(i - N_PT, 0) % TILES_PER_DEC, 0))
    wq = HEADS * HEAD_PAD
    return pl.pallas_call(
        _front_kernel,
        out_shape=(jax.ShapeDtypeStruct((T, 4 * GW), F32),
                   jax.ShapeDtypeStruct((T, wq), BF16),
                   jax.ShapeDtypeStruct((T, wq), BF16),
                   jax.ShapeDtypeStruct((T, HEADS * V_DIM), BF16),
                   jax.ShapeDtypeStruct((T, KV_LORA), F32),
                   jax.ShapeDtypeStruct((T, QK_ROPE), F32)),
        grid=(T // TM,),
        in_specs=[_PROMPT_ROWS, _SAMPLE_ROWS, full((SUBLANES, 6 * D)), full((1, D)), full((D, W_COLS)),
                  full((1, 2 * LANES)), full((2 * LANES, wq)), full((1, KV_LORA)),
                  full((KV_LORA, wq)), full((LANES, wq)), full((KV_LORA, HEADS * V_DIM)),
                  rope_rows(wq), rope_rows(wq), rope_rows(wq),
                  rope_rows(LANES), rope_rows(LANES), rope_rows(LANES)],
        out_specs=(rows(4 * GW), rows(wq), rows(wq), rows(HEADS * V_DIM), rows(KV_LORA),
                   rows(QK_ROPE)),
        compiler_params=_cparams(("parallel",)),
        name="front",
    )(*x, mod_l, lw["attn_g"], lw["w_in"], lw["q_g"], lw["w_uq"], lw["kv_g"], lw["w_kn"],
      lw["w_kr"], lw["w_v"], *rope)


def _kvexp_kernel(ckv_ref, kr_ref, wkn_ref, wkr_ref, wv_ref, k_ref, v_ref):
    ckv_b = ckv_ref[...].astype(BF16)
    v_ref[...] = jnp.dot(ckv_b, wv_ref[...], preferred_element_type=F32).astype(BF16)
    kn = jnp.dot(ckv_b, wkn_ref[...], preferred_element_type=F32)
    k_ref[...] = (kn + _bdot(kr_ref[...], wkr_ref[...])).astype(BF16)


def _kv_expand(ckv, kr_pad, lw):
    n = ckv.shape[0]
    wq = HEADS * HEAD_PAD
    full = lambda shape: pl.BlockSpec(shape, lambda i: (0,) * len(shape))
    return pl.pallas_call(
        _kvexp_kernel,
        out_shape=(jax.ShapeDtypeStruct((n, wq), BF16),
                   jax.ShapeDtypeStruct((n, HEADS * V_DIM), BF16)),
        grid=(1,),
        in_specs=[full((n, KV_LORA)), full((n, LANES)), full((KV_LORA, wq)), full((LANES, wq)),
                  full((KV_LORA, HEADS * V_DIM))],
        out_specs=(full((n, wq)), full((n, HEADS * V_DIM))),
        compiler_params=_cparams(("arbitrary",)),
        name="kv_expand",
    )(ckv, kr_pad, lw["w_kn"], lw["w_kr"], lw["w_v"])


def _attn_kernel(q_ref, k_ref, v_ref, o_ref, *, group, tq, n_keys):
    for g in range(group):
        outs = []
        for h in range(HEADS):
            qh = q_ref[g * tq:(g + 1) * tq, h * HEAD_PAD:(h + 1) * HEAD_PAD]
            kh = k_ref[g * n_keys:(g + 1) * n_keys, h * HEAD_PAD:(h + 1) * HEAD_PAD]
            s = lax.dot_general(qh, kh, (((1,), (1,)), ((), ())), preferred_element_type=F32)
            m = jnp.max(s, axis=-1, keepdims=True)
            p = jnp.exp2(s - m)
            l = jnp.sum(p, axis=-1, keepdims=True)
            vh = v_ref[g * n_keys:(g + 1) * n_keys, h * V_DIM:(h + 1) * V_DIM]
            outs.append(jnp.dot(p.astype(BF16), vh, preferred_element_type=F32) / l)
        o_ref[g * tq:(g + 1) * tq, :] = jnp.concatenate(outs, axis=-1)


def _attention(q, k, v, n_batch, seq, n_keys, tq, row0, group=1):
    nq = seq // tq
    wq = HEADS * HEAD_PAD
    return pl.pallas_call(
        functools.partial(_attn_kernel, group=group, tq=tq, n_keys=n_keys),
        out_shape=jax.ShapeDtypeStruct((n_batch * seq, GW), F32),
        grid=(n_batch // group, nq),
        in_specs=[pl.BlockSpec((group * tq, wq), lambda b, i: (row0 // (group * tq) + b * nq + i, 0)),
                  pl.BlockSpec((group * n_keys, wq), lambda b, i: (b, 0)),
                  pl.BlockSpec((group * n_keys, HEADS * V_DIM), lambda b, i: (b, 0))],
        out_specs=pl.BlockSpec((group * tq, GW), lambda b, i: (b * nq + i, 0)),
        compiler_params=_cparams(("parallel", "parallel")),
        name="attention",
    )(q, k, v)


def _fourier_kernel(z_ref, c_ref, s_ref, bdc_ref, bds_ref, w_ref, o_ref, xc_ref, xs_ref, *,
                    norm, group, seq, ts):
    b = pl.program_id(1)

    @pl.when(pl.program_id(0) == 0)
    def _():
        xb = z_ref[...].astype(BF16)
        xc_ref[b] = _bdot(xb, bdc_ref[...]).astype(BF16)
        xs_ref[b] = _bdot(xb, bds_ref[...]).astype(BF16)

    cb, sb = c_ref[...].astype(BF16), s_ref[...].astype(BF16)
    for g in range(group):
        f = (jnp.dot(cb, xc_ref[b, g * seq:(g + 1) * seq, :], preferred_element_type=F32)
             - jnp.dot(sb, xs_ref[b, g * seq:(g + 1) * seq, :], preferred_element_type=F32)) * norm
        o_ref[g * ts:(g + 1) * ts, :] = _bdot(f, w_ref[...])


def _fourier(zmix, consts, lw, n_batch, seq, row0, group=1):
    ts = min(seq, 512)
    nj = seq // ts
    assert group == 1 or nj == 1
    nb = n_batch // group
    cmat, smat = consts
    full = lambda shape: pl.BlockSpec(shape, lambda j, b: (0,) * len(shape))
    z_spec = pl.BlockSpec((group * seq, GW),
                          lambda j, b: (row0 // (group * seq) + jnp.where(j == 0, b, nb - 1), 1))
    return pl.pallas_call(
        functools.partial(_fourier_kernel, norm=float((seq * (GW // 4)) ** -0.5), group=group,
                          seq=seq, ts=ts),
        out_shape=jax.ShapeDtypeStruct((n_batch * seq, GW), F32),
        grid=(nj, nb),
        in_specs=[z_spec,
                  pl.BlockSpec((ts, seq), lambda j, b: (j, 0)),
                  pl.BlockSpec((ts, seq), lambda j, b: (j, 0)),
                  full((GW, GW)), full((GW, GW)), full((GW, GW))],
        out_specs=pl.BlockSpec((group * ts, GW), lambda j, b: (b * nj + j, 0)),
        scratch_shapes=[pltpu.VMEM((nb, group * seq, GW), BF16),
                        pltpu.VMEM((nb, group * seq, GW), BF16)],
        compiler_params=_cparams(("arbitrary", "arbitrary")),
        name="fourier",
    )(zmix, cmat, smat, lw["bd_c"], lw["bd_s"], lw["w_f"])


def _seq_kernel(za_ref, zr_ref, zg_ref, h0_ref, icnt_ref, wp_ref, ps_ref, cw_ref, cb_ref, wa_ref,
                ba_ref, wx_ref, bx_ref, lam_ref, ya_ref, yd_ref, st_ref,
                pa_ref, pb_ref, xp_ref, a_ref, b_ref, *, seq):
    n = seq + SEQ_PAD
    span = seq + 2 * FRONT
    zeros_pad = jnp.zeros((n, GW), F32)

    za = za_ref[...]
    pa_ref[...] = zeros_pad
    pb_ref[...] = zeros_pad
    pa_ref[FRONT:FRONT + seq, :] = za
    pb_ref[0:span, :] = pa_ref[0:span, :] + pa_ref[1:span + 1, :]
    win2 = pb_ref[FRONT - 1:FRONT - 1 + seq, :]
    pa_ref[0:span, :] = pb_ref[0:span, :] + pb_ref[2:span + 2, :]
    win4 = pa_ref[FRONT - 2:FRONT - 2 + seq, :]
    pb_ref[0:span, :] = pa_ref[0:span, :] + pa_ref[4:span + 4, :]
    win8 = pb_ref[FRONT - 4:FRONT - 4 + seq, :]
    pa_ref[0:span, :] = pb_ref[0:span, :] + pb_ref[8:span + 8, :]
    win16 = pa_ref[FRONT - 8:FRONT - 8 + seq, :]
    grp = lax.broadcasted_iota(I32, (1, GW), 1) // (GW // 4)
    win = jnp.where(grp == 0, win2, jnp.where(grp == 1, win4, jnp.where(grp == 2, win8, win16)))
    dlt = win * icnt_ref[...] - za
    ya_ref[...] = _bdot(dlt, wp_ref[...]) * ps_ref[...]

    xp_ref[...] = zeros_pad
    xp_ref[FRONT:FRONT + seq, :] = zr_ref[...]
    xc = cb_ref[...] + cw_ref[0:1, :] * xp_ref[FRONT - 2:FRONT - 2 + seq, :]
    for kk in range(1, 4):
        xc = xc + cw_ref[kk:kk + 1, :] * xp_ref[FRONT - 2 + kk:FRONT - 2 + kk + seq, :]
    xcb = xc.astype(BF16)
    n_grp = seq // SUBLANES
    sub = lax.broadcasted_iota(I32, (n_grp, SUBLANES, GW), 1)

    total = None
    for d in range(2):
        r = _sigmoid(jnp.dot(xcb, wa_ref[d], preferred_element_type=F32) + ba_ref[d])
        ig = _sigmoid(jnp.dot(xcb, wx_ref[d], preferred_element_type=F32) + bx_ref[d])
        log_a = (-LRU_C) * r * jax.nn.softplus(-lam_ref[d])
        a = jnp.exp(log_a).reshape(n_grp, SUBLANES, GW)
        b = (jnp.sqrt(_neg_expm1_double(log_a)) * (ig * xc)).reshape(n_grp, SUBLANES, GW)
        for k in (1, 2, 4):
            shift = k if d == 0 else SUBLANES - k
            m = sub >= k if d == 0 else sub < SUBLANES - k
            ap, bp = pltpu.roll(a, shift, 1), pltpu.roll(b, shift, 1)
            b = jnp.where(m, a * bp + b, b)
            a = jnp.where(m, a * ap, a)
        a_ref[...] = a.reshape(seq, GW)
        b_ref[...] = b.reshape(seq, GW)

        def step(g, carry, d=d):
            gi = g if d == 0 else n_grp - 1 - g
            off = pl.multiple_of(gi * SUBLANES, SUBLANES)
            hh = a_ref[pl.ds(off, SUBLANES), :] * carry + b_ref[pl.ds(off, SUBLANES), :]
            b_ref[pl.ds(off, SUBLANES), :] = hh
            edge = hh[SUBLANES - 1:SUBLANES, :] if d == 0 else hh[0:1, :]
            return jnp.broadcast_to(edge, (SUBLANES, GW))

        last = lax.fori_loop(0, n_grp, step, jnp.broadcast_to(h0_ref[0, d:d + 1, :], (SUBLANES, GW)))
        st_ref[0, d:d + 1, :] = last[0:1, :]
        total = b_ref[...] if total is None else total + b_ref[...]

    yd_ref[...] = total * jax.nn.gelu(zg_ref[...])


def _pool_inverse_counts(seq):
    pos = np.arange(seq)[:, None]
    half = np.repeat(np.array(POOL_WINDOWS) // 2, GW // len(POOL_WINDOWS))[None, :]
    cnt = np.minimum(pos + half, seq) - np.maximum(pos - half, 0)
    return jnp.asarray(1.0 / cnt, F32)


def _seq_mixers(zmix, h0, lw, n_batch, seq, row0):
    full = lambda shape: pl.BlockSpec(shape, lambda b: (0,) * len(shape))
    col = lambda c: pl.BlockSpec((seq, GW), lambda b: (row0 // seq + b, c))
    out_rows = pl.BlockSpec((seq, GW), lambda b: (b, 0))
    pad = pltpu.VMEM((seq + SEQ_PAD, GW), F32)
    return pl.pallas_call(
        functools.partial(_seq_kernel, seq=seq),
        out_shape=(jax.ShapeDtypeStruct((n_batch * seq, GW), F32),
                   jax.ShapeDtypeStruct((n_batch * seq, GW), F32),
                   jax.ShapeDtypeStruct((n_batch, 2, GW), F32)),
        grid=(n_batch,),
        in_specs=[col(0), col(2), col(3), pl.BlockSpec((1, 2, GW), lambda b: (b, 0, 0)),
                  full((seq, GW)), full((GW, GW)), full((1, GW)), full((4, GW)), full((1, GW)),
                  full((2, GW, GW)), full((2, 1, GW)), full((2, GW, GW)), full((2, 1, GW)),
                  full((2, 1, GW))],
        out_specs=(out_rows, out_rows, pl.BlockSpec((1, 2, GW), lambda b: (b, 0, 0))),
        scratch_shapes=[pad, pad, pad, pltpu.VMEM((seq, GW), F32), pltpu.VMEM((seq, GW), F32)],
        compiler_params=_cparams(("parallel",)),
        name="seq_mixers",
    )(zmix, zmix, zmix, h0, _pool_inverse_counts(seq), lw["bd_pool"], lw["pool_scale"], lw["conv_w"],
      lw["conv_b"], lw["bd_wa"], lw["lru_ba"], lw["bd_wx"], lw["lru_bx"], lw["lru_lam"])


def _post_kernel(xp_ref, xs_ref, mod_ref, yap_ref, ybp_ref, ycp_ref, ydp_ref, yas_ref, ybs_ref, ycs_ref,
                 yds_ref, og_ref, wo_ref, fg_ref, rw_ref, rb_ref, tri_ref, upper_ref,
                 x1_ref, h2p_ref, route_ref, prob_ref, tab_ref, cnt_ref, carry_ref):
    i = pl.program_id(0)
    is_s = i >= N_PT
    row = _tile_row(i)
    gate1 = mod_ref[pl.ds(row, 1), 2 * D:3 * D]
    shift2 = mod_ref[pl.ds(row, 1), 3 * D:4 * D]
    scale2 = mod_ref[pl.ds(row, 1), 4 * D:5 * D]

    @pl.when(i == 0)
    def _():
        carry_ref[...] = jnp.zeros_like(carry_ref)

    groups = []
    for gi, (p_ref, s_ref) in enumerate(((yap_ref, yas_ref), (ybp_ref, ybs_ref),
                                         (ycp_ref, ycs_ref), (ydp_ref, yds_ref))):
        y = jnp.where(is_s, s_ref[...], p_ref[...])
        groups.append((_rms(y) * og_ref[gi:gi + 1, :]).astype(BF16))
    ycat = jnp.concatenate(groups, axis=-1)
    x = jnp.where(is_s, xs_ref[...], xp_ref[...])
    x1 = x + gate1 * jnp.dot(ycat, wo_ref[...], preferred_element_type=F32)
    x1_ref[...] = x1
    h2 = _rms(x1) * fg_ref[...]
    h2 = h2 * (1.0 + scale2) + shift2

    for j in range(ROW_F):
        h2p_ref[pl.ds(j, TM, stride=ROW_F), :] = h2[:, j * LANES:(j + 1) * LANES]

    h_hi, h_lo = _split(h2)
    both = jnp.dot(h_hi, rw_ref[...], preferred_element_type=F32)
    cross = both[:, LANES:2 * LANES] + jnp.dot(h_lo, rw_ref[:, 0:LANES], preferred_element_type=F32)
    logits = both[:, 0:LANES] + cross + rb_ref[...]
    lane = lax.broadcasted_iota(I32, (TM, LANES), 1)
    lane_f = lane.astype(F32)
    neg = jnp.float32(-jnp.inf)
    cur = jnp.where(lane < N_EXP, logits, neg)
    sel, vals, idxs = [], [], []
    for _ in range(TOP_K):
        m = jnp.max(cur, axis=-1, keepdims=True)
        idx = jnp.min(jnp.where(cur == m, lane_f, float(LANES)), axis=-1, keepdims=True)
        hit = lane_f == idx
        sel.append(hit)
        vals.append(m)
        idxs.append(idx)
        cur = jnp.where(hit, neg, cur)
    exps = [jnp.exp(v - vals[0]) for v in vals]
    denom = exps[0] + exps[1] + exps[2] + exps[3]
    onehot = jnp.where(sel[0] | sel[1] | sel[2] | sel[3], 1.0, 0.0)
    cum_l = jnp.dot(tri_ref[...], onehot.astype(BF16), preferred_element_type=F32)
    before = carry_ref[0:1, :]
    cum = cum_l + before
    n_tile = cum_l[TM - 1:TM, :]
    nch = jnp.floor((n_tile + (CH_C - 1.0)) * (1.0 / CH_C))
    nch8 = jnp.broadcast_to(nch, (SUBLANES, LANES))
    base = jnp.dot(nch8.astype(BF16), upper_ref[...], preferred_element_type=F32)[0:1, :]
    pk = jnp.zeros((TM, LANES), I32)
    pf = jnp.zeros((TM, LANES), F32)
    for k in range(TOP_K):
        rank = jnp.sum(jnp.where(sel[k], cum - 1.0, 0.0), axis=-1, keepdims=True).astype(I32)
        local = jnp.sum(jnp.where(sel[k], (base * CH_C + cum_l - 1.0) * ROW_F, 0.0), axis=-1,
                        keepdims=True).astype(I32)
        pk = jnp.where(lane == k, idxs[k].astype(I32), pk)
        pk = jnp.where(lane == TOP_K + k, rank, pk)
        pk = jnp.where(lane == 2 * TOP_K + k, local, pk)
        pf = jnp.where(lane == k, exps[k] / denom, pf)
    route_ref[...] = pk.T[0:4 * TOP_K, :]
    prob_ref[...] = pf.T[0:SUBLANES, :]
    row8 = lax.broadcasted_iota(I32, (SUBLANES, LANES), 0)
    tab_ref[...] = jnp.where(row8 == 0, before, jnp.where(row8 == 1, nch, base))
    new_carry = jnp.broadcast_to(cum[TM - 1:TM, :], (SUBLANES, LANES))
    carry_ref[...] = new_carry
    cnt_ref[...] = new_carry


def _post(x, mod_l, ys_prompt, ys_sample, lw, tri, upper):
    full = lambda shape: pl.BlockSpec(shape, lambda i: (0,) * len(shape))
    rows = lambda w: pl.BlockSpec((TM, w), lambda i: (i, 0))
    prow = pl.BlockSpec((TM, GW), lambda i: (jnp.minimum(i, N_PT - 1), 0))
    srow = pl.BlockSpec((TM, GW), lambda i: (jnp.maximum(i - N_PT, 0), 0))
    return pl.pallas_call(
        _post_kernel,
        out_shape=(jax.ShapeDtypeStruct((T, D), F32),
                   jax.ShapeDtypeStruct((T * ROW_F, LANES), F32),
                   jax.ShapeDtypeStruct((4 * TOP_K, T), I32),
                   jax.ShapeDtypeStruct((SUBLANES, T), F32),
                   jax.ShapeDtypeStruct((T // TM * SUBLANES, LANES), F32),
                   jax.ShapeDtypeStruct((SUBLANES, LANES), F32)),
        grid=(T // TM,),
        in_specs=[_PROMPT_ROWS, _SAMPLE_ROWS, full((SUBLANES, 6 * D)),
                  prow, prow, prow, prow, srow, srow, srow, srow,
                  full((4, GW)), full((D, D)), full((1, D)), full((D, 2 * LANES)), full((1, LANES)),
                  full((TM, TM)), full((LANES, LANES))],
        out_specs=(rows(D), pl.BlockSpec((TM * ROW_F, LANES), lambda i: (i, 0)),
                   pl.BlockSpec((4 * TOP_K, TM), lambda i: (0, i)),
                   pl.BlockSpec((SUBLANES, TM), lambda i: (0, i)),
                   pl.BlockSpec((SUBLANES, LANES), lambda i: (i, 0)), full((SUBLANES, LANES))),
        scratch_shapes=[pltpu.VMEM((SUBLANES, LANES), F32)],
        compiler_params=_cparams(("arbitrary",)),
        name="post",
    )(*x, mod_l, *ys_prompt, *ys_sample, lw["out_g"], lw["w_out"], lw["ffn_g"], lw["router_w"],
      lw["router_b"], tri, upper)


def _gather_kernel(dest_ref, nused_ref, zeros_hbm, src_ref, o_ref, stok_ref, sem):
    i = pl.program_id(0)

    @pl.when(i == 0)
    def _():
        init = pltpu.make_async_copy(zeros_hbm, stok_ref, sem)
        init.start()
        init.wait()

        def scatter(t8, c):
            row0 = t8 * (SUBLANES * ROW_F)
            for k in range(TOP_K):
                for u in range(SUBLANES):
                    stok_ref[dest_ref[k * T + t8 * SUBLANES + u]] = row0 + u * ROW_F
            return c
        lax.fori_loop(0, T // SUBLANES, scatter, 0)

    @pl.when(i < nused_ref[0])
    def _():
        def rows(r16, c):
            r0 = pl.multiple_of(r16 * GATHER_UNROLL, GATHER_UNROLL)
            for u in range(GATHER_UNROLL):
                src = pl.multiple_of(stok_ref[i * TM_E + r0 + u], ROW_F)
                dst = pl.multiple_of(r0 * ROW_F, GATHER_UNROLL * ROW_F) + u * ROW_F
                o_ref[pl.ds(dst, ROW_F), :] = src_ref[pl.ds(src, ROW_F), :]
            return c
        lax.fori_loop(0, TM_E // GATHER_UNROLL, rows, 0)

    @pl.when(i >= nused_ref[0])
    def _():
        o_ref[...] = jnp.zeros_like(o_ref)


def _gather_rows(dest, n_used, h2lin):
    return pl.pallas_call(
        _gather_kernel,
        out_shape=jax.ShapeDtypeStruct((N_SLOTS * ROW_F, LANES), F32),
        grid_spec=pltpu.PrefetchScalarGridSpec(
            num_scalar_prefetch=2,
            grid=(NB_E,),
            in_specs=[pl.BlockSpec(memory_space=pl.ANY),
                      pl.BlockSpec(memory_space=pltpu.VMEM)],
            out_specs=pl.BlockSpec((TM_E * ROW_F, LANES), lambda i, d, nu: (i, 0)),
            scratch_shapes=[pltpu.SMEM((N_SLOTS,), I32), pltpu.SemaphoreType.DMA(())]),
        compiler_params=_cparams(("arbitrary",), GATHER_VMEM_LIMIT),
        name="moe_gather",
    )(dest, n_used, jnp.zeros((N_SLOTS,), I32), h2lin)


def _expert_rows(n, x_ref, w_ref, slot, bg_ref, bu_ref, bd_ref, o_ref):
    x = jnp.concatenate([x_ref[pl.ds(j, n, stride=ROW_F), :].astype(BF16) for j in range(ROW_F)],
                        axis=-1)
    g = jnp.dot(x, w_ref[slot, 0].astype(BF16), preferred_element_type=F32) + bg_ref[0, 0]
    u = jnp.dot(x, w_ref[slot, 1].astype(BF16), preferred_element_type=F32) + bu_ref[0, 0]
    g = jnp.minimum(g, LIMIT)
    u = jnp.clip(u, -LIMIT, LIMIT)
    act = (u + 1.0) * (g * jax.nn.sigmoid(ALPHA * g))
    y = jnp.dot(act.astype(BF16), w_ref[slot, 2].astype(BF16),
                preferred_element_type=F32) + bd_ref[0, 0]
    for j in range(ROW_F):
        o_ref[pl.ds(j, n, stride=ROW_F), :] = y[:, j * LANES:(j + 1) * LANES]
    if n < TM_E:
        o_ref[n * ROW_F:TM_E * ROW_F, :] = jnp.zeros(((TM_E - n) * ROW_F, LANES), F32)


def _expert_kernel(be_ref, nused_ref, valid_ref, first_ref, next_ref, par_ref,
                   x_ref, wg_hbm, bg_ref, wu_hbm, bu_ref, wd_hbm, bd_ref, o_ref, w_ref, sems, *, layer):
    i = pl.program_id(0)
    ib = jnp.minimum(i, NB_E - 1)
    live = i < nused_ref[0]
    valid = valid_ref[ib]
    expert = be_ref[ib]
    slot = par_ref[ib]

    def fetch(e, s):
        return [pltpu.make_async_copy(src.at[layer, e], w_ref.at[s, m], sems.at[s, m])
                for m, src in enumerate((wg_hbm, wu_hbm, wd_hbm))]

    @pl.when(i == 0)
    def _():
        for cp in fetch(expert, slot):
            cp.start()

    @pl.when(live & (first_ref[ib] == 1))
    def _():
        @pl.when(next_ref[ib] >= 0)
        def _():
            for cp in fetch(next_ref[ib], 1 - slot):
                cp.start()

        for cp in fetch(expert, slot):
            cp.wait()

    for n_idx, n in enumerate(EXPERT_ROWS):
        fits = valid <= n
        if n_idx + 1 < len(EXPERT_ROWS):
            fits = fits & (valid > EXPERT_ROWS[n_idx + 1])

        @pl.when(live & fits)
        def _(n=n):
            _expert_rows(n, x_ref, w_ref, slot, bg_ref, bu_ref, bd_ref, o_ref)

    @pl.when(jnp.logical_not(live))
    def _():
        o_ref[...] = jnp.zeros_like(o_ref)


def _experts(l, sched, xs, p):
    last = NB_E - 1
    bspec = pl.BlockSpec((1, 1, 1, D), lambda i, be, *_: (l, be[jnp.minimum(i, last)], 0, 0))
    hbm = pl.BlockSpec(memory_space=pl.ANY)
    bias = lambda b: b.reshape(DEPTH, N_EXP, 1, D)
    return pl.pallas_call(
        functools.partial(_expert_kernel, layer=l),
        out_shape=jax.ShapeDtypeStruct(((NB_E + 1) * TM_E * ROW_F, LANES), F32),
        grid_spec=pltpu.PrefetchScalarGridSpec(
            num_scalar_prefetch=len(sched),
            grid=(NB_E + 1,),
            in_specs=[pl.BlockSpec((TM_E * ROW_F, LANES), lambda i, *_: (jnp.minimum(i, last), 0)),
                      hbm, bspec, hbm, bspec, hbm, bspec],
            out_specs=pl.BlockSpec((TM_E * ROW_F, LANES), lambda i, *_: (i, 0)),
            scratch_shapes=[pltpu.VMEM((2, 3, D, D), F32), pltpu.SemaphoreType.DMA((2, 3))]),
        compiler_params=_cparams(("arbitrary",)),
        name="moe_experts",
    )(*sched, xs, p["w_gate"], bias(p["b_gate"]), p["w_up"], bias(p["b_up"]),
      p["w_down"], bias(p["b_down"]))


def _combine_kernel(loc_ref, p_ref, csrc_ref, nch_ref, ys_hbm, x_ref, mod_ref, fg_ref, *rest, final):
    *outs, buf_ref, acc_ref, sems = rest
    tb = pl.program_id(0)
    n_tb = pl.num_programs(0)
    slot = tb % 2
    row = _tile_row(tb)
    gate2 = mod_ref[pl.ds(row, 1), 5 * D:6 * D]
    chunk_rows = CH_C * ROW_F

    def chunk_copy(t, c, s):
        src = pl.multiple_of(csrc_ref[t * MAX_CH + c] * ROW_F, ROW_F)
        dst = pl.multiple_of(c * chunk_rows, chunk_rows)
        return pltpu.make_async_copy(ys_hbm.at[pl.ds(src, chunk_rows)],
                                     buf_ref.at[s, pl.ds(dst, chunk_rows)], sems.at[s])

    def issue(t, s):
        def body(c, carry):
            chunk_copy(t, c, s).start()
            return carry
        lax.fori_loop(0, nch_ref[t], body, 0)

    @pl.when(tb == 0)
    def _():
        issue(0, 0)

    @pl.when(tb + 1 < n_tb)
    def _():
        issue(tb + 1, 1 - slot)

    def drain(c, carry):
        chunk_copy(tb, c, slot).wait()
        return carry
    lax.fori_loop(0, nch_ref[tb], drain, 0)

    def tokens(r8, carry):
        r0 = pl.multiple_of(r8 * SUBLANES, SUBLANES)
        for u in range(SUBLANES):
            tok = tb * TM + r0 + u
            acc = None
            for k in range(TOP_K):
                off = pl.multiple_of(loc_ref[k * T + tok], ROW_F)
                term = buf_ref[slot, pl.ds(off, ROW_F), :] * p_ref[k * T + tok]
                acc = term if acc is None else acc + term
            dst = pl.multiple_of(r0 * ROW_F, SUBLANES * ROW_F) + u * ROW_F
            acc_ref[pl.ds(dst, ROW_F), :] = acc
        return carry
    lax.fori_loop(0, TM // SUBLANES, tokens, 0)

    moe = jnp.concatenate([acc_ref[pl.ds(j, TM, stride=ROW_F), :] for j in range(ROW_F)], axis=-1)
    x2 = x_ref[...] + gate2 * moe
    if final:
        x2 = _rms(x2) * fg_ref[...]
    prompt_ref, sample_ref = outs

    @pl.when(tb < N_PT)
    def _():
        prompt_ref[...] = x2

    @pl.when(tb >= N_PT)
    def _():
        sample_ref[...] = x2


def _combine(loc, top_p, chunk_src, n_chunks, ys, x1, mod_l, final_g, final):
    full = lambda shape: pl.BlockSpec(shape, lambda i, *_: (0,) * len(shape))
    rows = lambda w: pl.BlockSpec((TM, w), lambda i, *_: (i, 0))
    out_shape = (jax.ShapeDtypeStruct((T_P, D), F32), jax.ShapeDtypeStruct((T_S, D), F32))
    out_specs = (pl.BlockSpec((TM, D), lambda i, *_: (jnp.minimum(i, N_PT - 1), 0)),
                 pl.BlockSpec((TM, D), lambda i, *_: (jnp.maximum(i - N_PT, 0), 0)))
    return pl.pallas_call(
        functools.partial(_combine_kernel, final=final),
        out_shape=out_shape,
        grid_spec=pltpu.PrefetchScalarGridSpec(
            num_scalar_prefetch=4,
            grid=(T // TM,),
            in_specs=[pl.BlockSpec(memory_space=pl.ANY), rows(D), full((SUBLANES, 6 * D)),
                      full((1, D))],
            out_specs=out_specs,
            scratch_shapes=[pltpu.VMEM((2, MAX_CH * CH_C * ROW_F, LANES), F32),
                            pltpu.VMEM((TM * ROW_F, LANES), F32),
                            pltpu.SemaphoreType.DMA((2,))]),
        compiler_params=_cparams(("arbitrary",)),
        name="moe_combine",
    )(loc, top_p, chunk_src, n_chunks, ys, x1, mod_l, final_g)


def _rope_tables():
    rows = DEC_SEQ // GRID_W
    r = np.repeat(np.arange(rows, dtype=np.float64), GRID_W)
    c = np.tile(np.arange(GRID_W, dtype=np.float64), rows)
    n_freq = QK_ROPE // 4
    inv = (np.float32(ROPE_BASE) ** (-np.arange(n_freq, dtype=np.float32) / n_freq)).astype(np.float64)
    ang = np.concatenate([r[:, None] * inv, c[:, None] * inv], axis=-1).astype(np.float32)
    cos, sin = np.cos(ang.astype(np.float64)), np.sin(ang.astype(np.float64))
    half = QK_ROPE // 2

    def place(width, start):
        cf = np.ones((DEC_SEQ, width), np.float32)
        sa = np.zeros((DEC_SEQ, width), np.float32)
        sb = np.zeros((DEC_SEQ, width), np.float32)
        for s0 in start:
            cf[:, s0:s0 + half] = cos
            cf[:, s0 + half:s0 + 2 * half] = cos
            sa[:, s0:s0 + half] = -sin
            sb[:, s0 + half:s0 + 2 * half] = sin
        return jnp.asarray(cf), jnp.asarray(sa), jnp.asarray(sb)

    return (*place(HEADS * HEAD_PAD, [h * HEAD_PAD + QK_NOPE for h in range(HEADS)]),
            *place(LANES, [0]))


def _dft_tables(seq):
    kn = (np.arange(seq, dtype=np.int64)[:, None] * np.arange(seq, dtype=np.int64)[None, :]) % seq
    ang = 2.0 * np.pi * kn.astype(np.float64) / seq
    return jnp.asarray(np.cos(ang), F32), jnp.asarray(np.sin(ang), F32)


def _block_diag(blocks):
    g, n, _ = blocks.shape
    eye = jnp.eye(g, dtype=blocks.dtype)
    return jnp.einsum("gij,gh->gihj", blocks, eye).reshape(g * n, g * n)


def _layer_weights(l, p):
    w_in = p["w_in"][l]
    za, zq, zkv, zkr, zf, zr, zg = jnp.split(
        w_in, np.cumsum([GW, Q_LORA, KV_LORA, QK_ROPE, GW, GW])[:], axis=1)
    zpad = lambda n: jnp.zeros((D, n), F32)
    w_cols = jnp.concatenate([za, zf, zr, zg, zq, zpad(2 * LANES - Q_LORA), zkv, zkr,
                              zpad(LANES - QK_ROPE)], axis=1).astype(BF16)
    wq = HEADS * HEAD_PAD
    w_uq = p["w_uq"][l].reshape(Q_LORA, HEADS, QK_NOPE + QK_ROPE)
    w_uq = jnp.pad(w_uq, ((0, 2 * LANES - Q_LORA), (0, 0), (0, HEAD_PAD - QK_NOPE - QK_ROPE)))
    w_ukv = p["w_ukv"][l].reshape(KV_LORA, HEADS, QK_NOPE + V_DIM)
    w_kn = jnp.pad(w_ukv[:, :, :QK_NOPE], ((0, 0), (0, 0), (0, HEAD_PAD - QK_NOPE)))
    w_v = w_ukv[:, :, QK_NOPE:]
    place = np.zeros((LANES, HEADS, HEAD_PAD), np.float32)
    for h in range(HEADS):
        place[np.arange(QK_ROPE), h, QK_NOPE + np.arange(QK_ROPE)] = 1.0
    c64 = np.arange(GW // 4, dtype=np.int64)
    ang = 2.0 * np.pi * ((c64[:, None] * c64[None, :]) % (GW // 4)).astype(np.float64) / (GW // 4)
    four = lambda m: jnp.asarray(np.broadcast_to(m, (4,) + m.shape), F32)
    router_hi, router_lo = _split(jnp.pad(p["router_w"][l], ((0, 0), (0, LANES - N_EXP))))
    router_w = jnp.concatenate([router_hi, router_lo], axis=1)
    router_b = jnp.pad(p["router_b"][l], (0, LANES - N_EXP)).reshape(1, LANES)
    return {
        "attn_g": p["attn_norm_g"][l].reshape(1, D),
        "w_in": w_cols,
        "q_g": jnp.pad(p["q_norm_g"][l], (0, 2 * LANES - Q_LORA)).reshape(1, 2 * LANES),
        "w_uq": w_uq.reshape(2 * LANES, wq).astype(BF16),
        "kv_g": p["kv_norm_g"][l].reshape(1, KV_LORA),
        "w_kn": w_kn.reshape(KV_LORA, wq).astype(BF16),
        "w_kr": jnp.asarray(place.reshape(LANES, wq), BF16),
        "w_v": w_v.reshape(KV_LORA, HEADS * V_DIM).astype(BF16),
        "bd_c": _block_diag(four(np.cos(ang))),
        "bd_s": _block_diag(four(np.sin(ang))),
        "w_f": p["fourier_w"][l].astype(BF16),
        "bd_pool": _block_diag(p["pool_w"][l]).astype(BF16),
        "pool_scale": p["pool_scale"][l].reshape(1, GW),
        "conv_w": p["conv_w"][l],
        "conv_b": p["conv_b"][l].reshape(1, GW),
        "bd_wa": jnp.stack([_block_diag(p["lru_wa"][l, d]) for d in range(2)]).astype(BF16),
        "bd_wx": jnp.stack([_block_diag(p["lru_wx"][l, d]) for d in range(2)]).astype(BF16),
        "lru_ba": p["lru_ba"][l].reshape(2, 1, GW),
        "lru_bx": p["lru_bx"][l].reshape(2, 1, GW),
        "lru_lam": p["lru_lambda"][l].reshape(2, 1, GW),
        "out_g": p["out_norm_g"][l],
        "w_out": p["w_out"][l].astype(BF16),
        "ffn_g": p["ffn_norm_g"][l].reshape(1, D),
        "router_w": router_w,
        "router_b": router_b,
    }


def _routing_tables(route, prob, counts, tab):
    top_e, rank = route[0:TOP_K], route[TOP_K:2 * TOP_K]
    top_p = prob[0:TOP_K].reshape(-1)
    loc = route[2 * TOP_K:3 * TOP_K].reshape(-1)
    counts = counts.astype(I32)
    padded = (counts + TM_E - 1) // TM_E * TM_E
    pad_ends = jnp.cumsum(padded)
    pad_starts = pad_ends - padded
    experts = jnp.arange(N_EXP, dtype=I32)
    onehot = top_e[:, :, None] == experts
    dest = (jnp.sum(jnp.where(onehot, pad_starts, 0), axis=-1) + rank).reshape(T * TOP_K)
    n_used = (pad_ends[-1] // TM_E).astype(I32)
    blk = jnp.minimum(jnp.arange(NB_E, dtype=I32), n_used - 1) * TM_E
    block_e = jnp.minimum(jnp.sum(pad_ends[None, :] <= blk[:, None], axis=-1), N_EXP - 1).astype(I32)
    of_block = lambda a: jnp.sum(jnp.where(block_e[:, None] == experts, a, 0), axis=-1)
    valid = jnp.clip(of_block(pad_starts + counts) - blk, 0, TM_E).astype(I32)
    first = (blk == of_block(pad_starts)).astype(I32)
    later = (experts[None, :] > experts[:, None]) & (counts[None, :] > 0)
    next_of = jnp.min(jnp.where(later, experts[None, :], N_EXP), axis=-1)
    next_e = of_block(jnp.where(next_of < N_EXP, next_of, -1)).astype(I32)
    parity = (of_block(jnp.cumsum((counts > 0).astype(I32))) % 2).astype(I32)
    sched = (block_e, n_used.reshape(1), valid, first, next_e, parity)
    tab = tab.reshape(T // TM, SUBLANES, LANES)[:, :, :N_EXP].astype(I32)
    before, nch, base = tab[:, 0], tab[:, 1], tab[:, 2]
    run_start = pad_starts[None, :] + before
    ends = base + nch
    ci = jnp.arange(MAX_CH, dtype=I32)
    e_of = jnp.minimum(jnp.sum(ends[:, None, :] <= ci[None, :, None], axis=-1), N_EXP - 1)
    pick = lambda a: jnp.sum(jnp.where(e_of[:, :, None] == experts, a[:, None, :], 0), axis=-1)
    chunk_src = jnp.clip(pick(run_start) + (ci[None, :] - pick(base)) * CH_C, 0, N_SLOTS)
    return dest, sched, loc, top_p, chunk_src.reshape(-1), ends[:, N_EXP - 1]


def kernel(x_prompt, x_sample, cache_ckv, cache_krope, state_lru, c, c_ctx, w_mod, b_mod, attn_norm_g, w_in, pool_w, pool_scale, q_norm_g, w_uq, kv_norm_g, w_ukv, fourier_w, conv_w, conv_b, lru_wa, lru_ba, lru_wx, lru_bx, lru_lambda, out_norm_g, w_out, ffn_norm_g, router_w, router_b, w_gate, b_gate, w_up, b_up, w_down, b_down, final_norm_g):
    params = dict(attn_norm_g=attn_norm_g, w_in=w_in, pool_w=pool_w, pool_scale=pool_scale,
                  q_norm_g=q_norm_g, w_uq=w_uq, kv_norm_g=kv_norm_g, w_ukv=w_ukv,
                  fourier_w=fourier_w, conv_w=conv_w, conv_b=conv_b, lru_wa=lru_wa, lru_ba=lru_ba,
                  lru_wx=lru_wx, lru_bx=lru_bx, lru_lambda=lru_lambda, out_norm_g=out_norm_g,
                  w_out=w_out, ffn_norm_g=ffn_norm_g, router_w=router_w, router_b=router_b,
                  w_gate=w_gate, b_gate=b_gate, w_up=w_up, b_up=b_up, w_down=w_down, b_down=b_down)
    x = (x_prompt.reshape(T_P, D), x_sample.reshape(T_S, D))
    cvec = jnp.concatenate([c_ctx[None, :], c, jnp.zeros((SUBLANES - 1 - DEC_BATCH, D), F32)], axis=0)
    mod = _modulation(cvec, w_mod, b_mod)
    rope = _rope_tables()
    dft_p, dft_s = _dft_tables(SEQ), _dft_tables(DEC_SEQ)
    tri = jnp.asarray(np.tril(np.ones((TM, TM), np.float32)), BF16)
    upper = jnp.asarray(np.triu(np.ones((LANES, LANES), np.float32), 1), BF16)
    final_g = final_norm_g.reshape(1, D)
    h0_prompt = jnp.zeros((BATCH, 2, GW), F32)
    n_keys = PAST + DEC_SEQ

    new_ckv, new_krope, new_lru = [], [], []
    for l in range(DEPTH):
        lw = _layer_weights(l, params)
        zmix, q, k, v, ckv, kr = _front(x, mod[l], lw, rope)
        new_ckv.append(ckv[:T_P].reshape(BATCH, SEQ, KV_LORA))
        new_krope.append(kr[:T_P].reshape(BATCH, SEQ, QK_ROPE))

        yb_p = _attention(q, k, v, BATCH, SEQ, SEQ, SEQ, 0, group=PROMPT_GROUP)
        yc_p = _fourier(zmix, dft_p, lw, BATCH, SEQ, 0, group=PROMPT_GROUP)
        ya_p, yd_p, st_p = _seq_mixers(zmix, h0_prompt, lw, BATCH, SEQ, 0)
        new_lru.append(st_p)

        kr_ctx = jnp.pad(cache_krope[:, l].reshape(DEC_BATCH * PAST, QK_ROPE),
                         ((0, 0), (0, LANES - QK_ROPE)))
        k_ctx, v_ctx = _kv_expand(cache_ckv[:, l].reshape(DEC_BATCH * PAST, KV_LORA), kr_ctx, lw)
        k_s = jnp.concatenate([k_ctx.reshape(DEC_BATCH, PAST, -1),
                               k[T_P:].reshape(DEC_BATCH, DEC_SEQ, -1)], axis=1)
        v_s = jnp.concatenate([v_ctx.reshape(DEC_BATCH, PAST, -1),
                               v[T_P:].reshape(DEC_BATCH, DEC_SEQ, -1)], axis=1)
        yb_s = _attention(q, k_s.reshape(DEC_BATCH * n_keys, -1), v_s.reshape(DEC_BATCH * n_keys, -1),
                          DEC_BATCH, DEC_SEQ, n_keys, 512, T_P)
        yc_s = _fourier(zmix, dft_s, lw, DEC_BATCH, DEC_SEQ, T_P)
        ya_s, yd_s, _ = _seq_mixers(zmix, state_lru[:, l], lw, DEC_BATCH, DEC_SEQ, T_P)

        x1, h2lin, route, prob, tab, counts = _post(
            x, mod[l], (ya_p, yb_p, yc_p, yd_p), (ya_s, yb_s, yc_s, yd_s), lw, tri, upper)
        dest, sched, loc, top_p, chunk_src, n_chunks = _routing_tables(
            route, prob, counts[0, :N_EXP], tab)
        xs = _gather_rows(dest, sched[1], h2lin)
        ys = _experts(l, sched, xs, params)
        x = _combine(loc, top_p, chunk_src, n_chunks, ys, x1, mod[l], final_g,
                     final=(l == DEPTH - 1))

    y_prompt, y_sample = x
    return (y_prompt.reshape(BATCH, SEQ, D), y_sample.reshape(DEC_BATCH, DEC_SEQ, D),
            jnp.stack(new_ckv, axis=1), jnp.stack(new_krope, axis=1), jnp.stack(new_lru, axis=1))
```

```python
import functools

import numpy as np
import jax
import jax.numpy as jnp
from jax import lax
from jax.experimental import pallas as pl
from jax.experimental.pallas import tpu as pltpu

F32 = jnp.float32
BF16 = jnp.bfloat16
I32 = jnp.int32

D = 1024
BATCH, SEQ = 32, 256
DEC_BATCH, DEC_SEQ, PAST = 2, 2048, 512
T_P = BATCH * SEQ
T_S = DEC_BATCH * DEC_SEQ
T = T_P + T_S
DEPTH = 2
GRID_W = 64
GW = 256
Q_LORA, KV_LORA, QK_NOPE, QK_ROPE, V_DIM, HEADS = 192, 128, 64, 32, 64, 4
HEAD_PAD = 128
ROPE_BASE = 10000.0
Q_SCALE = float((QK_NOPE + QK_ROPE) ** -0.5 * np.log2(np.e))
POOL_WINDOWS = (2, 4, 8, 16)
LRU_C = 8.0
N_EXP, TOP_K = 32, 4
LIMIT, ALPHA = 7.0, 1.702
EPS = 1e-6

LANES = 128
SUBLANES = 8
VMEM_LIMIT = 56 * 1024 * 1024

TM = 512
N_PT = T_P // TM
TILES_PER_DEC = DEC_SEQ // TM
TM_E = 512
EXPERT_ROWS = (TM_E, TM_E // 2, TM_E // 4)
N_SLOTS = T * TOP_K + N_EXP * TM_E
NB_E = N_SLOTS // TM_E
ROW_F = D // LANES
GATHER_VMEM_LIMIT = 60 * 1024 * 1024
CH_C = 16
MAX_CH = TM * TOP_K // CH_C + N_EXP
GATHER_UNROLL = 16
PROMPT_GROUP = 4
SEQ_PAD = 32
FRONT = 8

W_COLS = 1536
COL_Q, COL_KV, COL_KR = 1024, 1280, 1408


_PROMPT_ROWS = pl.BlockSpec((TM, D), lambda i: (jnp.minimum(i, N_PT - 1), 0))
_SAMPLE_ROWS = pl.BlockSpec((TM, D), lambda i: (jnp.maximum(i - N_PT, 0), 0))


def _cparams(sem, vmem=VMEM_LIMIT):
    return pltpu.CompilerParams(dimension_semantics=sem, vmem_limit_bytes=vmem)


def _bdot(a, b):
    return jnp.dot(a.astype(BF16), b.astype(BF16), preferred_element_type=F32)


def _split(a):
    hi = a.astype(BF16)
    lo = (a - hi.astype(F32)).astype(BF16)
    return hi, lo


def _dot3(a, b):
    ah, al = _split(a)
    bh, bl = _split(b)
    d = functools.partial(jnp.dot, preferred_element_type=F32)
    return d(ah, bh) + (d(al, bh) + d(ah, bl))


def _rms(x, n=None):
    n = x.shape[-1] if n is None else n
    return x * lax.rsqrt(jnp.sum(x * x, axis=-1, keepdims=True) * (1.0 / n) + EPS)


def _neg_expm1_double(x):
    t = jnp.tanh(x)
    return -2.0 * t / (1.0 - t)


def _sigmoid(x):
    return 0.5 * jnp.tanh(0.5 * x) + 0.5


def _tile_row(i):
    return jnp.where(i >= N_PT, 1 + (i - N_PT) // TILES_PER_DEC, 0)


def _mod_kernel(c_ref, w_ref, b_ref, o_ref):
    s = jax.nn.silu(c_ref[...])
    o_ref[0] = _dot3(s, w_ref[0]) + b_ref[0]


def _modulation(cvec, w_mod, b_mod):
    tn = 768
    n = 6 * D
    return pl.pallas_call(
        _mod_kernel,
        out_shape=jax.ShapeDtypeStruct((DEPTH, SUBLANES, n), F32),
        grid=(DEPTH, n // tn),
        in_specs=[pl.BlockSpec((SUBLANES, D), lambda l, j: (0, 0)),
                  pl.BlockSpec((1, D, tn), lambda l, j: (l, 0, j)),
                  pl.BlockSpec((1, 1, tn), lambda l, j: (l, 0, j))],
        out_specs=pl.BlockSpec((1, SUBLANES, tn), lambda l, j: (l, 0, j)),
        compiler_params=_cparams(("parallel", "parallel")),
        name="modulation",
    )(cvec, w_mod, b_mod.reshape(DEPTH, 1, n))


def _front_kernel(xp_ref, xs_ref, mod_ref, g_ref, w_ref, qg_ref, wuq_ref, kvg_ref, wkn_ref, wkr_ref,
                  wv_ref, cq_ref, saq_ref, sbq_ref, ck_ref, sak_ref, sbk_ref,
                  zmix_ref, q_ref, k_ref, v_ref, ckv_ref, kr_ref):
    i = pl.program_id(0)
    row = _tile_row(i)
    shift1 = mod_ref[pl.ds(row, 1), 0:D]
    scale1 = mod_ref[pl.ds(row, 1), D:2 * D]
    h = _rms(jnp.where(i >= N_PT, xs_ref[...], xp_ref[...])) * g_ref[...]
    h = h * (1.0 + scale1) + shift1
    z = jnp.dot(h.astype(BF16), w_ref[...], preferred_element_type=F32)
    zmix_ref[...] = z[:, 0:4 * GW]
    qn = _rms(z[:, COL_Q:COL_Q + 2 * LANES], Q_LORA) * qg_ref[...]
    q = _bdot(qn, wuq_ref[...])
    ckv = _rms(z[:, COL_KV:COL_KV + KV_LORA]) * kvg_ref[...]
    ckv_ref[...] = ckv
    ckv_b = ckv.astype(BF16)
    v_ref[...] = jnp.dot(ckv_b, wv_ref[...], preferred_element_type=F32).astype(BF16)
    kn = jnp.dot(ckv_b, wkn_ref[...], preferred_element_type=F32)
    kr = z[:, COL_KR:COL_KR + LANES]

    @pl.when(i < N_PT)
    def _():
        q_ref[...] = (q * Q_SCALE).astype(BF16)
        kr_ref[...] = kr[:, 0:QK_ROPE]
        k_ref[...] = (kn + _bdot(kr, wkr_ref[...])).astype(BF16)

    @pl.when(i >= N_PT)
    def _():
        wq = HEADS * HEAD_PAD
        half = QK_ROPE // 2
        qr = (q * cq_ref[...] + pltpu.roll(q, wq - half, 1) * saq_ref[...]
              + pltpu.roll(q, half, 1) * sbq_ref[...])
        krr = (kr * ck_ref[...] + pltpu.roll(kr, LANES - half, 1) * sak_ref[...]
               + pltpu.roll(kr, half, 1) * sbk_ref[...])
        q_ref[...] = (qr * Q_SCALE).astype(BF16)
        kr_ref[...] = krr[:, 0:QK_ROPE]
        k_ref[...] = (kn + _bdot(krr, wkr_ref[...])).astype(BF16)


def _front(x, mod_l, lw, rope):
    full = lambda shape: pl.BlockSpec(shape, lambda i: (0,) * len(shape))
    rows = lambda w: pl.BlockSpec((TM, w), lambda i: (i, 0))
    rope_rows = lambda w: pl.BlockSpec(
        (TM, w), lambda i: (jnp.maximum(i - N_PT, 0) % TILES_PER_DEC, 0))
    wq = HEADS * HEAD_PAD
    return pl.pallas_call(
        _front_kernel,
        out_shape=(jax.ShapeDtypeStruct((T, 4 * GW), F32),
                   jax.ShapeDtypeStruct((T, wq), BF16),
                   jax.ShapeDtypeStruct((T, wq), BF16),
                   jax.ShapeDtypeStruct((T, HEADS * V_DIM), BF16),
                   jax.ShapeDtypeStruct((T, KV_LORA), F32),
                   jax.ShapeDtypeStruct((T, QK_ROPE), F32)),
        grid=(T // TM,),
        in_specs=[_PROMPT_ROWS, _SAMPLE_ROWS, full((SUBLANES, 6 * D)), full((1, D)), full((D, W_COLS)),
                  full((1, 2 * LANES)), full((2 * LANES, wq)), full((1, KV_LORA)),
                  full((KV_LORA, wq)), full((LANES, wq)), full((KV_LORA, HEADS * V_DIM)),
                  rope_rows(wq), rope_rows(wq), rope_rows(wq),
                  rope_rows(LANES), rope_rows(LANES), rope_rows(LANES)],
        out_specs=(rows(4 * GW), rows(wq), rows(wq), rows(HEADS * V_DIM), rows(KV_LORA),
                   rows(QK_ROPE)),
        compiler_params=_cparams(("parallel",)),
        name="front",
    )(*x, mod_l, lw["attn_g"], lw["w_in"], lw["q_g"], lw["w_uq"], lw["kv_g"], lw["w_kn"],
      lw["w_kr"], lw["w_v"], *rope)


def _kvexp_kernel(ckv_ref, kr_ref, wkn_ref, wkr_ref, wv_ref, k_ref, v_ref):
    ckv_b = ckv_ref[...].astype(BF16)
    v_ref[...] = jnp.dot(ckv_b, wv_ref[...], preferred_element_type=F32).astype(BF16)
    kn = jnp.dot(ckv_b, wkn_ref[...], preferred_element_type=F32)
    k_ref[...] = (kn + _bdot(kr_ref[...], wkr_ref[...])).astype(BF16)


def _kv_expand(ckv, kr_pad, lw):
    n = ckv.shape[0]
    wq = HEADS * HEAD_PAD
    full = lambda shape: pl.BlockSpec(shape, lambda i: (0,) * len(shape))
    return pl.pallas_call(
        _kvexp_kernel,
        out_shape=(jax.ShapeDtypeStruct((n, wq), BF16),
                   jax.ShapeDtypeStruct((n, HEADS * V_DIM), BF16)),
        grid=(1,),
        in_specs=[full((n, KV_LORA)), full((n, LANES)), full((KV_LORA, wq)), full((LANES, wq)),
                  full((KV_LORA, HEADS * V_DIM))],
        out_specs=(full((n, wq)), full((n, HEADS * V_DIM))),
        compiler_params=_cparams(("arbitrary",)),
        name="kv_expand",
    )(ckv, kr_pad, lw["w_kn"], lw["w_kr"], lw["w_v"])


def _attn_kernel(q_ref, k_ref, v_ref, o_ref, *, group, tq, n_keys):
    for g in range(group):
        outs = []
        for h in range(HEADS):
            qh = q_ref[g * tq:(g + 1) * tq, h * HEAD_PAD:(h + 1) * HEAD_PAD]
            kh = k_ref[g * n_keys:(g + 1) * n_keys, h * HEAD_PAD:(h + 1) * HEAD_PAD]
            s = lax.dot_general(qh, kh, (((1,), (1,)), ((), ())), preferred_element_type=F32)
            m = jnp.max(s, axis=-1, keepdims=True)
            p = jnp.exp2(s - m)
            l = jnp.sum(p, axis=-1, keepdims=True)
            vh = v_ref[g * n_keys:(g + 1) * n_keys, h * V_DIM:(h + 1) * V_DIM]
            outs.append(jnp.dot(p.astype(BF16), vh, preferred_element_type=F32) / l)
        o_ref[g * tq:(g + 1) * tq, :] = jnp.concatenate(outs, axis=-1)


def _attention(q, k, v, n_batch, seq, n_keys, tq, row0, group=1):
    nq = seq // tq
    wq = HEADS * HEAD_PAD
    return pl.pallas_call(
        functools.partial(_attn_kernel, group=group, tq=tq, n_keys=n_keys),
        out_shape=jax.ShapeDtypeStruct((n_batch * seq, GW), F32),
        grid=(n_batch // group, nq),
        in_specs=[pl.BlockSpec((group * tq, wq), lambda b, i: (row0 // (group * tq) + b * nq + i, 0)),
                  pl.BlockSpec((group * n_keys, wq), lambda b, i: (b, 0)),
                  pl.BlockSpec((group * n_keys, HEADS * V_DIM), lambda b, i: (b, 0))],
        out_specs=pl.BlockSpec((group * tq, GW), lambda b, i: (b * nq + i, 0)),
        compiler_params=_cparams(("parallel", "parallel")),
        name="attention",
    )(q, k, v)


def _fourier_kernel(z_ref, c_ref, s_ref, bdc_ref, bds_ref, w_ref, o_ref, xc_ref, xs_ref, *,
                    norm, group, seq, ts):
    b = pl.program_id(1)

    @pl.when(pl.program_id(0) == 0)
    def _():
        xb = z_ref[...].astype(BF16)
        xc_ref[b] = _bdot(xb, bdc_ref[...]).astype(BF16)
        xs_ref[b] = _bdot(xb, bds_ref[...]).astype(BF16)

    cb, sb = c_ref[...].astype(BF16), s_ref[...].astype(BF16)
    for g in range(group):
        f = (jnp.dot(cb, xc_ref[b, g * seq:(g + 1) * seq, :], preferred_element_type=F32)
             - jnp.dot(sb, xs_ref[b, g * seq:(g + 1) * seq, :], preferred_element_type=F32)) * norm
        o_ref[g * ts:(g + 1) * ts, :] = _bdot(f, w_ref[...])


def _fourier(zmix, consts, lw, n_batch, seq, row0, group=1):
    ts = min(seq, 512)
    nj = seq // ts
    assert group == 1 or nj == 1
    nb = n_batch // group
    cmat, smat = consts
    full = lambda shape: pl.BlockSpec(shape, lambda j, b: (0,) * len(shape))
    z_spec = pl.BlockSpec((group * seq, GW),
                          lambda j, b: (row0 // (group * seq) + jnp.where(j == 0, b, nb - 1), 1))
    return pl.pallas_call(
        functools.partial(_fourier_kernel, norm=float((seq * (GW // 4)) ** -0.5), group=group,
                          seq=seq, ts=ts),
        out_shape=jax.ShapeDtypeStruct((n_batch * seq, GW), F32),
        grid=(nj, nb),
        in_specs=[z_spec,
                  pl.BlockSpec((ts, seq), lambda j, b: (j, 0)),
                  pl.BlockSpec((ts, seq), lambda j, b: (j, 0)),
                  full((GW, GW)), full((GW, GW)), full((GW, GW))],
        out_specs=pl.BlockSpec((group * ts, GW), lambda j, b: (b * nj + j, 0)),
        scratch_shapes=[pltpu.VMEM((nb, group * seq, GW), BF16),
                        pltpu.VMEM((nb, group * seq, GW), BF16)],
        compiler_params=_cparams(("arbitrary", "arbitrary")),
        name="fourier",
    )(zmix, cmat, smat, lw["bd_c"], lw["bd_s"], lw["w_f"])


def _seq_kernel(za_ref, zr_ref, zg_ref, h0_ref, icnt_ref, wp_ref, ps_ref, cw_ref, cb_ref, wa_ref,
                ba_ref, wx_ref, bx_ref, lam_ref, ya_ref, yd_ref, st_ref,
                pa_ref, pb_ref, xp_ref, a_ref, b_ref, *, seq):
    n = seq + SEQ_PAD
    span = seq + 2 * FRONT
    zeros_pad = jnp.zeros((n, GW), F32)

    za = za_ref[...]
    pa_ref[...] = zeros_pad
    pb_ref[...] = zeros_pad
    pa_ref[FRONT:FRONT + seq, :] = za
    pb_ref[0:span, :] = pa_ref[0:span, :] + pa_ref[1:span + 1, :]
    win2 = pb_ref[FRONT - 1:FRONT - 1 + seq, :]
    pa_ref[0:span, :] = pb_ref[0:span, :] + pb_ref[2:span + 2, :]
    win4 = pa_ref[FRONT - 2:FRONT - 2 + seq, :]
    pb_ref[0:span, :] = pa_ref[0:span, :] + pa_ref[4:span + 4, :]
    win8 = pb_ref[FRONT - 4:FRONT - 4 + seq, :]
    pa_ref[0:span, :] = pb_ref[0:span, :] + pb_ref[8:span + 8, :]
    win16 = pa_ref[FRONT - 8:FRONT - 8 + seq, :]
    grp = lax.broadcasted_iota(I32, (1, GW), 1) // (GW // 4)
    win = jnp.where(grp == 0, win2, jnp.where(grp == 1, win4, jnp.where(grp == 2, win8, win16)))
    dlt = win * icnt_ref[...] - za
    ya_ref[...] = _bdot(dlt, wp_ref[...]) * ps_ref[...]

    xp_ref[...] = zeros_pad
    xp_ref[FRONT:FRONT + seq, :] = zr_ref[...]
    xc = cb_ref[...] + cw_ref[0:1, :] * xp_ref[FRONT - 2:FRONT - 2 + seq, :]
    for kk in range(1, 4):
        xc = xc + cw_ref[kk:kk + 1, :] * xp_ref[FRONT - 2 + kk:FRONT - 2 + kk + seq, :]
    xcb = xc.astype(BF16)
    n_grp = seq // SUBLANES
    sub = lax.broadcasted_iota(I32, (n_grp, SUBLANES, GW), 1)

    total = None
    for d in range(2):
        r = _sigmoid(jnp.dot(xcb, wa_ref[d], preferred_element_type=F32) + ba_ref[d])
        ig = _sigmoid(jnp.dot(xcb, wx_ref[d], preferred_element_type=F32) + bx_ref[d])
        log_a = (-LRU_C) * r * jax.nn.softplus(-lam_ref[d])
        a = jnp.exp(log_a).reshape(n_grp, SUBLANES, GW)
        b = (jnp.sqrt(_neg_expm1_double(log_a)) * (ig * xc)).reshape(n_grp, SUBLANES, GW)
        for k in (1, 2, 4):
            shift = k if d == 0 else SUBLANES - k
            m = sub >= k if d == 0 else sub < SUBLANES - k
            ap, bp = pltpu.roll(a, shift, 1), pltpu.roll(b, shift, 1)
            b = jnp.where(m, a * bp + b, b)
            a = jnp.where(m, a * ap, a)
        a_ref[...] = a.reshape(seq, GW)
        b_ref[...] = b.reshape(seq, GW)

        def step(g, carry, d=d):
            gi = g if d == 0 else n_grp - 1 - g
            off = pl.multiple_of(gi * SUBLANES, SUBLANES)
            hh = a_ref[pl.ds(off, SUBLANES), :] * carry + b_ref[pl.ds(off, SUBLANES), :]
            b_ref[pl.ds(off, SUBLANES), :] = hh
            edge = hh[SUBLANES - 1:SUBLANES, :] if d == 0 else hh[0:1, :]
            return jnp.broadcast_to(edge, (SUBLANES, GW))

        last = lax.fori_loop(0, n_grp, step, jnp.broadcast_to(h0_ref[0, d:d + 1, :], (SUBLANES, GW)))
        st_ref[0, d:d + 1, :] = last[0:1, :]
        total = b_ref[...] if total is None else total + b_ref[...]

    yd_ref[...] = total * jax.nn.gelu(zg_ref[...])


def _pool_inverse_counts(seq):
    pos = np.arange(seq)[:, None]
    half = np.repeat(np.array(POOL_WINDOWS) // 2, GW // len(POOL_WINDOWS))[None, :]
    cnt = np.minimum(pos + half, seq) - np.maximum(pos - half, 0)
    return jnp.asarray(1.0 / cnt, F32)


def _seq_mixers(zmix, h0, lw, n_batch, seq, row0):
    full = lambda shape: pl.BlockSpec(shape, lambda b: (0,) * len(shape))
    col = lambda c: pl.BlockSpec((seq, GW), lambda b: (row0 // seq + b, c))
    out_rows = pl.BlockSpec((seq, GW), lambda b: (b, 0))
    pad = pltpu.VMEM((seq + SEQ_PAD, GW), F32)
    return pl.pallas_call(
        functools.partial(_seq_kernel, seq=seq),
        out_shape=(jax.ShapeDtypeStruct((n_batch * seq, GW), F32),
                   jax.ShapeDtypeStruct((n_batch * seq, GW), F32),
                   jax.ShapeDtypeStruct((n_batch, 2, GW), F32)),
        grid=(n_batch,),
        in_specs=[col(0), col(2), col(3), pl.BlockSpec((1, 2, GW), lambda b: (b, 0, 0)),
                  full((seq, GW)), full((GW, GW)), full((1, GW)), full((4, GW)), full((1, GW)),
                  full((2, GW, GW)), full((2, 1, GW)), full((2, GW, GW)), full((2, 1, GW)),
                  full((2, 1, GW))],
        out_specs=(out_rows, out_rows, pl.BlockSpec((1, 2, GW), lambda b: (b, 0, 0))),
        scratch_shapes=[pad, pad, pad, pltpu.VMEM((seq, GW), F32), pltpu.VMEM((seq, GW), F32)],
        compiler_params=_cparams(("parallel",)),
        name="seq_mixers",
    )(zmix, zmix, zmix, h0, _pool_inverse_counts(seq), lw["bd_pool"], lw["pool_scale"], lw["conv_w"],
      lw["conv_b"], lw["bd_wa"], lw["lru_ba"], lw["bd_wx"], lw["lru_bx"], lw["lru_lam"])


def _post_kernel(xp_ref, xs_ref, mod_ref, yap_ref, ybp_ref, ycp_ref, ydp_ref, yas_ref, ybs_ref, ycs_ref,
                 yds_ref, og_ref, wo_ref, fg_ref, rw_ref, rb_ref, tri_ref, upper_ref,
                 x1_ref, h2p_ref, route_ref, prob_ref, tab_ref, cnt_ref, carry_ref):
    i = pl.program_id(0)
    is_s = i >= N_PT
    row = _tile_row(i)
    gate1 = mod_ref[pl.ds(row, 1), 2 * D:3 * D]
    shift2 = mod_ref[pl.ds(row, 1), 3 * D:4 * D]
    scale2 = mod_ref[pl.ds(row, 1), 4 * D:5 * D]

    @pl.when(i == 0)
    def _():
        carry_ref[...] = jnp.zeros_like(carry_ref)

    groups = []
    for gi, (p_ref, s_ref) in enumerate(((yap_ref, yas_ref), (ybp_ref, ybs_ref),
                                         (ycp_ref, ycs_ref), (ydp_ref, yds_ref))):
        y = jnp.where(is_s, s_ref[...], p_ref[...])
        groups.append((_rms(y) * og_ref[gi:gi + 1, :]).astype(BF16))
    ycat = jnp.concatenate(groups, axis=-1)
    x = jnp.where(is_s, xs_ref[...], xp_ref[...])
    x1 = x + gate1 * jnp.dot(ycat, wo_ref[...], preferred_element_type=F32)
    x1_ref[...] = x1
    h2 = _rms(x1) * fg_ref[...]
    h2 = h2 * (1.0 + scale2) + shift2

    for j in range(ROW_F):
        h2p_ref[pl.ds(j, TM, stride=ROW_F), :] = h2[:, j * LANES:(j + 1) * LANES]

    h_hi, h_lo = _split(h2)
    both = jnp.dot(h_hi, rw_ref[...], preferred_element_type=F32)
    cross = both[:, LANES:2 * LANES] + jnp.dot(h_lo, rw_ref[:, 0:LANES], preferred_element_type=F32)
    logits = both[:, 0:LANES] + cross + rb_ref[...]
    lane = lax.broadcasted_iota(I32, (TM, LANES), 1)
    lane_f = lane.astype(F32)
    neg = jnp.float32(-jnp.inf)
    cur = jnp.where(lane < N_EXP, logits, neg)
    sel, vals, idxs = [], [], []
    for _ in range(TOP_K):
        m = jnp.max(cur, axis=-1, keepdims=True)
        idx = jnp.min(jnp.where(cur == m, lane_f, float(LANES)), axis=-1, keepdims=True)
        hit = lane_f == idx
        sel.append(hit)
        vals.append(m)
        idxs.append(idx)
        cur = jnp.where(hit, neg, cur)
    exps = [jnp.exp(v - vals[0]) for v in vals]
    denom = exps[0] + exps[1] + exps[2] + exps[3]
    onehot = jnp.where(sel[0] | sel[1] | sel[2] | sel[3], 1.0, 0.0)
    cum_l = jnp.dot(tri_ref[...], onehot.astype(BF16), preferred_element_type=F32)
    before = carry_ref[0:1, :]
    cum = cum_l + before
    n_tile = cum_l[TM - 1:TM, :]
    nch = jnp.floor((n_tile + (CH_C - 1.0)) * (1.0 / CH_C))
    nch8 = jnp.broadcast_to(nch, (SUBLANES, LANES))
    base = jnp.dot(nch8.astype(BF16), upper_ref[...], preferred_element_type=F32)[0:1, :]
    pk = jnp.zeros((TM, LANES), I32)
    pf = jnp.zeros((TM, LANES), F32)
    for k in range(TOP_K):
        rank = jnp.sum(jnp.where(sel[k], cum - 1.0, 0.0), axis=-1, keepdims=True).astype(I32)
        local = jnp.sum(jnp.where(sel[k], (base * CH_C + cum_l - 1.0) * ROW_F, 0.0), axis=-1,
                        keepdims=True).astype(I32)
        pk = jnp.where(lane == k, idxs[k].astype(I32), pk)
        pk = jnp.where(lane == TOP_K + k, rank, pk)
        pk = jnp.where(lane == 2 * TOP_K + k, local, pk)
        pf = jnp.where(lane == k, exps[k] / denom, pf)
    route_ref[...] = pk.T[0:4 * TOP_K, :]
    prob_ref[...] = pf.T[0:SUBLANES, :]
    row8 = lax.broadcasted_iota(I32, (SUBLANES, LANES), 0)
    tab_ref[...] = jnp.where(row8 == 0, before, jnp.where(row8 == 1, nch, base))
    new_carry = jnp.broadcast_to(cum[TM - 1:TM, :], (SUBLANES, LANES))
    carry_ref[...] = new_carry
    cnt_ref[...] = new_carry


def _post(x, mod_l, ys_prompt, ys_sample, lw, tri, upper):
    full = lambda shape: pl.BlockSpec(shape, lambda i: (0,) * len(shape))
    rows = lambda w: pl.BlockSpec((TM, w), lambda i: (i, 0))
    prow = pl.BlockSpec((TM, GW), lambda i: (jnp.minimum(i, N_PT - 1), 0))
    srow = pl.BlockSpec((TM, GW), lambda i: (jnp.maximum(i - N_PT, 0), 0))
    return pl.pallas_call(
        _post_kernel,
        out_shape=(jax.ShapeDtypeStruct((T, D), F32),
                   jax.ShapeDtypeStruct((T * ROW_F, LANES), F32),
                   jax.ShapeDtypeStruct((4 * TOP_K, T), I32),
                   jax.ShapeDtypeStruct((SUBLANES, T), F32),
                   jax.ShapeDtypeStruct((T // TM * SUBLANES, LANES), F32),
                   jax.ShapeDtypeStruct((SUBLANES, LANES), F32)),
        grid=(T // TM,),
        in_specs=[_PROMPT_ROWS, _SAMPLE_ROWS, full((SUBLANES, 6 * D)),
                  prow, prow, prow, prow, srow, srow, srow, srow,
                  full((4, GW)), full((D, D)), full((1, D)), full((D, 2 * LANES)), full((1, LANES)),
                  full((TM, TM)), full((LANES, LANES))],
        out_specs=(rows(D), pl.BlockSpec((TM * ROW_F, LANES), lambda i: (i, 0)),
                   pl.BlockSpec((4 * TOP_K, TM), lambda i: (0, i)),
                   pl.BlockSpec((SUBLANES, TM), lambda i: (0, i)),
                   pl.BlockSpec((SUBLANES, LANES), lambda i: (i, 0)), full((SUBLANES, LANES))),
        scratch_shapes=[pltpu.VMEM((SUBLANES, LANES), F32)],
        compiler_params=_cparams(("arbitrary",)),
        name="post",
    )(*x, mod_l, *ys_prompt, *ys_sample, lw["out_g"], lw["w_out"], lw["ffn_g"], lw["router_w"],
      lw["router_b"], tri, upper)


def _gather_kernel(dest_ref, nused_ref, zeros_hbm, src_ref, o_ref, stok_ref, rows_ref, sem):
    i = pl.program_id(0)

    @pl.when(i == 0)
    def _():
        init = pltpu.make_async_copy(zeros_hbm, stok_ref, sem)
        init.start()
        init.wait()

        def scatter(t8, c):
            row0 = t8 * (SUBLANES * ROW_F)
            for k in range(TOP_K):
                for u in range(SUBLANES):
                    stok_ref[dest_ref[k * T + t8 * SUBLANES + u]] = row0 + u * ROW_F
            return c
        lax.fori_loop(0, T // SUBLANES, scatter, 0)

    @pl.when(i < nused_ref[0])
    def _():
        def rows(r16, c):
            r0 = pl.multiple_of(r16 * GATHER_UNROLL, GATHER_UNROLL)
            for u in range(GATHER_UNROLL):
                src = pl.multiple_of(stok_ref[i * TM_E + r0 + u], ROW_F)
                dst = pl.multiple_of(r0 * ROW_F, GATHER_UNROLL * ROW_F) + u * ROW_F
                rows_ref[pl.ds(dst, ROW_F), :] = src_ref[pl.ds(src, ROW_F), :]
            return c
        lax.fori_loop(0, TM_E // GATHER_UNROLL, rows, 0)
        for j in range(ROW_F):
            o_ref[:, j * LANES:(j + 1) * LANES] = rows_ref[pl.ds(j, TM_E, stride=ROW_F), :].astype(BF16)

    @pl.when(i >= nused_ref[0])
    def _():
        o_ref[...] = jnp.zeros_like(o_ref)


def _gather_rows(dest, n_used, h2lin):
    return pl.pallas_call(
        _gather_kernel,
        out_shape=jax.ShapeDtypeStruct((N_SLOTS, D), BF16),
        grid_spec=pltpu.PrefetchScalarGridSpec(
            num_scalar_prefetch=2,
            grid=(NB_E,),
            in_specs=[pl.BlockSpec(memory_space=pl.ANY),
                      pl.BlockSpec(memory_space=pltpu.VMEM)],
            out_specs=pl.BlockSpec((TM_E, D), lambda i, d, nu: (i, 0)),
            scratch_shapes=[pltpu.SMEM((N_SLOTS,), I32), pltpu.VMEM((TM_E * ROW_F, LANES), F32),
                            pltpu.SemaphoreType.DMA(())]),
        compiler_params=_cparams(("arbitrary",), GATHER_VMEM_LIMIT),
        name="moe_gather",
    )(dest, n_used, jnp.zeros((N_SLOTS,), I32), h2lin)


def _expert_rows(n, x_ref, w_ref, slot, bg_ref, bu_ref, bd_ref, o_ref):
    x = x_ref[0:n, :]
    g = jnp.dot(x, w_ref[slot, 0].astype(BF16), preferred_element_type=F32) + bg_ref[0, 0]
    u = jnp.dot(x, w_ref[slot, 1].astype(BF16), preferred_element_type=F32) + bu_ref[0, 0]
    g = jnp.minimum(g, LIMIT)
    u = jnp.clip(u, -LIMIT, LIMIT)
    act = (u + 1.0) * (g * jax.nn.sigmoid(ALPHA * g))
    y = jnp.dot(act.astype(BF16), w_ref[slot, 2].astype(BF16),
                preferred_element_type=F32) + bd_ref[0, 0]
    for j in range(ROW_F):
        o_ref[pl.ds(j, n, stride=ROW_F), :] = y[:, j * LANES:(j + 1) * LANES]
    if n < TM_E:
        o_ref[n * ROW_F:TM_E * ROW_F, :] = jnp.zeros(((TM_E - n) * ROW_F, LANES), F32)


def _expert_kernel(be_ref, nused_ref, valid_ref, first_ref, next_ref, par_ref, pos_ref, tail_ref,
                   x_ref, wg_hbm, bg_ref, wu_hbm, bu_ref, wd_hbm, bd_ref, o_ref, w_ref, sems, *, layer):
    i = pl.program_id(0)
    ib = jnp.minimum(i, NB_E - 1)
    live = i < nused_ref[0]
    valid = valid_ref[ib]
    expert = be_ref[ib]
    slot = par_ref[ib]
    n_mats = 3

    def fetch(e, s, m):
        src = (wg_hbm, wu_hbm, wd_hbm)[m]
        return pltpu.make_async_copy(src.at[layer, e], w_ref.at[s, m], sems.at[s, m])

    @pl.when(i == 0)
    def _():
        for m in range(n_mats):
            fetch(expert, slot, m).start()

    @pl.when(live & (first_ref[ib] == 1))
    def _():
        for m in range(n_mats):
            fetch(expert, slot, m).wait()

    for m in range(n_mats):
        due = (pos_ref[ib] == m) | ((tail_ref[ib] == 1) & (pos_ref[ib] < m))

        @pl.when(live & (next_ref[ib] >= 0) & due)
        def _(m=m):
            fetch(next_ref[ib], 1 - slot, m).start()

    for n_idx, n in enumerate(EXPERT_ROWS):
        fits = valid <= n
        if n_idx + 1 < len(EXPERT_ROWS):
            fits = fits & (valid > EXPERT_ROWS[n_idx + 1])

        @pl.when(live & fits)
        def _(n=n):
            _expert_rows(n, x_ref, w_ref, slot, bg_ref, bu_ref, bd_ref, o_ref)

    @pl.when(jnp.logical_not(live))
    def _():
        o_ref[...] = jnp.zeros_like(o_ref)


def _experts(l, sched, xs, p):
    last = NB_E - 1
    bspec = pl.BlockSpec((1, 1, 1, D), lambda i, be, *_: (l, be[jnp.minimum(i, last)], 0, 0))
    hbm = pl.BlockSpec(memory_space=pl.ANY)
    bias = lambda b: b.reshape(DEPTH, N_EXP, 1, D)
    return pl.pallas_call(
        functools.partial(_expert_kernel, layer=l),
        out_shape=jax.ShapeDtypeStruct(((NB_E + 1) * TM_E * ROW_F, LANES), F32),
        grid_spec=pltpu.PrefetchScalarGridSpec(
            num_scalar_prefetch=len(sched),
            grid=(NB_E + 1,),
            in_specs=[pl.BlockSpec((TM_E, D), lambda i, *_: (jnp.minimum(i, last), 0)),
                      hbm, bspec, hbm, bspec, hbm, bspec],
            out_specs=pl.BlockSpec((TM_E * ROW_F, LANES), lambda i, *_: (i, 0)),
            scratch_shapes=[pltpu.VMEM((2, 3, D, D), F32), pltpu.SemaphoreType.DMA((2, 3))]),
        compiler_params=_cparams(("arbitrary",)),
        name="moe_experts",
    )(*sched, xs, p["w_gate"], bias(p["b_gate"]), p["w_up"], bias(p["b_up"]),
      p["w_down"], bias(p["b_down"]))


def _combine_kernel(loc_ref, p_ref, csrc_ref, nch_ref, ys_hbm, x_ref, mod_ref, fg_ref, *rest, final):
    *outs, buf_ref, acc_ref, sems = rest
    tb = pl.program_id(0)
    n_tb = pl.num_programs(0)
    slot = tb % 2
    row = _tile_row(tb)
    gate2 = mod_ref[pl.ds(row, 1), 5 * D:6 * D]
    chunk_rows = CH_C * ROW_F

    def chunk_copy(t, c, s):
        src = pl.multiple_of(csrc_ref[t * MAX_CH + c] * ROW_F, ROW_F)
        dst = pl.multiple_of(c * chunk_rows, chunk_rows)
        return pltpu.make_async_copy(ys_hbm.at[pl.ds(src, chunk_rows)],
                                     buf_ref.at[s, pl.ds(dst, chunk_rows)], sems.at[s])

    def issue(t, s):
        def body(c, carry):
            chunk_copy(t, c, s).start()
            return carry
        lax.fori_loop(0, nch_ref[t], body, 0)

    @pl.when(tb == 0)
    def _():
        issue(0, 0)

    @pl.when(tb + 1 < n_tb)
    def _():
        issue(tb + 1, 1 - slot)

    def drain(c, carry):
        chunk_copy(tb, c, slot).wait()
        return carry
    lax.fori_loop(0, nch_ref[tb], drain, 0)

    def tokens(r8, carry):
        r0 = pl.multiple_of(r8 * SUBLANES, SUBLANES)
        for u in range(SUBLANES):
            tok = tb * TM + r0 + u
            acc = None
            for k in range(TOP_K):
                off = pl.multiple_of(loc_ref[k * T + tok], ROW_F)
                term = buf_ref[slot, pl.ds(off, ROW_F), :] * p_ref[k * T + tok]
                acc = term if acc is None else acc + term
            dst = pl.multiple_of(r0 * ROW_F, SUBLANES * ROW_F) + u * ROW_F
            acc_ref[pl.ds(dst, ROW_F), :] = acc
        return carry
    lax.fori_loop(0, TM // SUBLANES, tokens, 0)

    moe = jnp.concatenate([acc_ref[pl.ds(j, TM, stride=ROW_F), :] for j in range(ROW_F)], axis=-1)
    x2 = x_ref[...] + gate2 * moe
    if final:
        x2 = _rms(x2) * fg_ref[...]
    prompt_ref, sample_ref = outs

    @pl.when(tb < N_PT)
    def _():
        prompt_ref[...] = x2

    @pl.when(tb >= N_PT)
    def _():
        sample_ref[...] = x2


def _combine(loc, top_p, chunk_src, n_chunks, ys, x1, mod_l, final_g, final):
    full = lambda shape: pl.BlockSpec(shape, lambda i, *_: (0,) * len(shape))
    rows = lambda w: pl.BlockSpec((TM, w), lambda i, *_: (i, 0))
    out_shape = (jax.ShapeDtypeStruct((T_P, D), F32), jax.ShapeDtypeStruct((T_S, D), F32))
    out_specs = (pl.BlockSpec((TM, D), lambda i, *_: (jnp.minimum(i, N_PT - 1), 0)),
                 pl.BlockSpec((TM, D), lambda i, *_: (jnp.maximum(i - N_PT, 0), 0)))
    return pl.pallas_call(
        functools.partial(_combine_kernel, final=final),
        out_shape=out_shape,
        grid_spec=pltpu.PrefetchScalarGridSpec(
            num_scalar_prefetch=4,
            grid=(T // TM,),
            in_specs=[pl.BlockSpec(memory_space=pl.ANY), rows(D), full((SUBLANES, 6 * D)),
                      full((1, D))],
            out_specs=out_specs,
            scratch_shapes=[pltpu.VMEM((2, MAX_CH * CH_C * ROW_F, LANES), F32),
                            pltpu.VMEM((TM * ROW_F, LANES), F32),
                            pltpu.SemaphoreType.DMA((2,))]),
        compiler_params=_cparams(("arbitrary",)),
        name="moe_combine",
    )(loc, top_p, chunk_src, n_chunks, ys, x1, mod_l, final_g)


def _rope_tables():
    rows = DEC_SEQ // GRID_W
    r = np.repeat(np.arange(rows, dtype=np.float64), GRID_W)
    c = np.tile(np.arange(GRID_W, dtype=np.float64), rows)
    n_freq = QK_ROPE // 4
    inv = (np.float32(ROPE_BASE) ** (-np.arange(n_freq, dtype=np.float32) / n_freq)).astype(np.float64)
    ang = np.concatenate([r[:, None] * inv, c[:, None] * inv], axis=-1).astype(np.float32)
    cos, sin = np.cos(ang.astype(np.float64)), np.sin(ang.astype(np.float64))
    half = QK_ROPE // 2

    def place(width, start):
        cf = np.ones((DEC_SEQ, width), np.float32)
        sa = np.zeros((DEC_SEQ, width), np.float32)
        sb = np.zeros((DEC_SEQ, width), np.float32)
        for s0 in start:
            cf[:, s0:s0 + half] = cos
            cf[:, s0 + half:s0 + 2 * half] = cos
            sa[:, s0:s0 + half] = -sin
            sb[:, s0 + half:s0 + 2 * half] = sin
        return jnp.asarray(cf), jnp.asarray(sa), jnp.asarray(sb)

    return (*place(HEADS * HEAD_PAD, [h * HEAD_PAD + QK_NOPE for h in range(HEADS)]),
            *place(LANES, [0]))


def _dft_tables(seq):
    kn = (np.arange(seq, dtype=np.int64)[:, None] * np.arange(seq, dtype=np.int64)[None, :]) % seq
    ang = 2.0 * np.pi * kn.astype(np.float64) / seq
    return jnp.asarray(np.cos(ang), F32), jnp.asarray(np.sin(ang), F32)


def _block_diag(blocks):
    g, n, _ = blocks.shape
    eye = jnp.eye(g, dtype=blocks.dtype)
    return jnp.einsum("gij,gh->gihj", blocks, eye).reshape(g * n, g * n)


def _layer_weights(l, p):
    w_in = p["w_in"][l]
    za, zq, zkv, zkr, zf, zr, zg = jnp.split(
        w_in, np.cumsum([GW, Q_LORA, KV_LORA, QK_ROPE, GW, GW])[:], axis=1)
    zpad = lambda n: jnp.zeros((D, n), F32)
    w_cols = jnp.concatenate([za, zf, zr, zg, zq, zpad(2 * LANES - Q_LORA), zkv, zkr,
                              zpad(LANES - QK_ROPE)], axis=1).astype(BF16)
    wq = HEADS * HEAD_PAD
    w_uq = p["w_uq"][l].reshape(Q_LORA, HEADS, QK_NOPE + QK_ROPE)
    w_uq = jnp.pad(w_uq, ((0, 2 * LANES - Q_LORA), (0, 0), (0, HEAD_PAD - QK_NOPE - QK_ROPE)))
    w_ukv = p["w_ukv"][l].reshape(KV_LORA, HEADS, QK_NOPE + V_DIM)
    w_kn = jnp.pad(w_ukv[:, :, :QK_NOPE], ((0, 0), (0, 0), (0, HEAD_PAD - QK_NOPE)))
    w_v = w_ukv[:, :, QK_NOPE:]
    place = np.zeros((LANES, HEADS, HEAD_PAD), np.float32)
    for h in range(HEADS):
        place[np.arange(QK_ROPE), h, QK_NOPE + np.arange(QK_ROPE)] = 1.0
    c64 = np.arange(GW // 4, dtype=np.int64)
    ang = 2.0 * np.pi * ((c64[:, None] * c64[None, :]) % (GW // 4)).astype(np.float64) / (GW // 4)
    four = lambda m: jnp.asarray(np.broadcast_to(m, (4,) + m.shape), F32)
    router_hi, router_lo = _split(jnp.pad(p["router_w"][l], ((0, 0), (0, LANES - N_EXP))))
    router_w = jnp.concatenate([router_hi, router_lo], axis=1)
    router_b = jnp.pad(p["router_b"][l], (0, LANES - N_EXP)).reshape(1, LANES)
    return {
        "attn_g": p["attn_norm_g"][l].reshape(1, D),
        "w_in": w_cols,
        "q_g": jnp.pad(p["q_norm_g"][l], (0, 2 * LANES - Q_LORA)).reshape(1, 2 * LANES),
        "w_uq": w_uq.reshape(2 * LANES, wq).astype(BF16),
        "kv_g": p["kv_norm_g"][l].reshape(1, KV_LORA),
        "w_kn": w_kn.reshape(KV_LORA, wq).astype(BF16),
        "w_kr": jnp.asarray(place.reshape(LANES, wq), BF16),
        "w_v": w_v.reshape(KV_LORA, HEADS * V_DIM).astype(BF16),
        "bd_c": _block_diag(four(np.cos(ang))),
        "bd_s": _block_diag(four(np.sin(ang))),
        "w_f": p["fourier_w"][l].astype(BF16),
        "bd_pool": _block_diag(p["pool_w"][l]).astype(BF16),
        "pool_scale": p["pool_scale"][l].reshape(1, GW),
        "conv_w": p["conv_w"][l],
        "conv_b": p["conv_b"][l].reshape(1, GW),
        "bd_wa": jnp.stack([_block_diag(p["lru_wa"][l, d]) for d in range(2)]).astype(BF16),
        "bd_wx": jnp.stack([_block_diag(p["lru_wx"][l, d]) for d in range(2)]).astype(BF16),
        "lru_ba": p["lru_ba"][l].reshape(2, 1, GW),
        "lru_bx": p["lru_bx"][l].reshape(2, 1, GW),
        "lru_lam": p["lru_lambda"][l].reshape(2, 1, GW),
        "out_g": p["out_norm_g"][l],
        "w_out": p["w_out"][l].astype(BF16),
        "ffn_g": p["ffn_norm_g"][l].reshape(1, D),
        "router_w": router_w,
        "router_b": router_b,
    }


def _routing_tables(route, prob, counts, tab):
    top_e, rank = route[0:TOP_K], route[TOP_K:2 * TOP_K]
    top_p = prob[0:TOP_K].reshape(-1)
    loc = route[2 * TOP_K:3 * TOP_K].reshape(-1)
    counts = counts.astype(I32)
    padded = (counts + TM_E - 1) // TM_E * TM_E
    pad_ends = jnp.cumsum(padded)
    pad_starts = pad_ends - padded
    experts = jnp.arange(N_EXP, dtype=I32)
    onehot = top_e[:, :, None] == experts
    dest = (jnp.sum(jnp.where(onehot, pad_starts, 0), axis=-1) + rank).reshape(T * TOP_K)
    n_used = (pad_ends[-1] // TM_E).astype(I32)
    blk = jnp.minimum(jnp.arange(NB_E, dtype=I32), n_used - 1) * TM_E
    block_e = jnp.minimum(jnp.sum(pad_ends[None, :] <= blk[:, None], axis=-1), N_EXP - 1).astype(I32)
    of_block = lambda a: jnp.sum(jnp.where(block_e[:, None] == experts, a, 0), axis=-1)
    valid = jnp.clip(of_block(pad_starts + counts) - blk, 0, TM_E).astype(I32)
    first = (blk == of_block(pad_starts)).astype(I32)
    later = (experts[None, :] > experts[:, None]) & (counts[None, :] > 0)
    next_of = jnp.min(jnp.where(later, experts[None, :], N_EXP), axis=-1)
    next_e = of_block(jnp.where(next_of < N_EXP, next_of, -1)).astype(I32)
    parity = (of_block(jnp.cumsum((counts > 0).astype(I32))) % 2).astype(I32)
    pos = ((blk - of_block(pad_starts)) // TM_E).astype(I32)
    tail = (blk + TM_E == of_block(pad_ends)).astype(I32)
    sched = (block_e, n_used.reshape(1), valid, first, next_e, parity, pos, tail)
    tab = tab.reshape(T // TM, SUBLANES, LANES)[:, :, :N_EXP].astype(I32)
    before, nch, base = tab[:, 0], tab[:, 1], tab[:, 2]
    run_start = pad_starts[None, :] + before
    ends = base + nch
    ci = jnp.arange(MAX_CH, dtype=I32)
    e_of = jnp.minimum(jnp.sum(ends[:, None, :] <= ci[None, :, None], axis=-1), N_EXP - 1)
    pick = lambda a: jnp.sum(jnp.where(e_of[:, :, None] == experts, a[:, None, :], 0), axis=-1)
    chunk_src = jnp.clip(pick(run_start) + (ci[None, :] - pick(base)) * CH_C, 0, N_SLOTS)
    return dest, sched, loc, top_p, chunk_src.reshape(-1), ends[:, N_EXP - 1]


def kernel(x_prompt, x_sample, cache_ckv, cache_krope, state_lru, c, c_ctx, w_mod, b_mod, attn_norm_g, w_in, pool_w, pool_scale, q_norm_g, w_uq, kv_norm_g, w_ukv, fourier_w, conv_w, conv_b, lru_wa, lru_ba, lru_wx, lru_bx, lru_lambda, out_norm_g, w_out, ffn_norm_g, router_w, router_b, w_gate, b_gate, w_up, b_up, w_down, b_down, final_norm_g):
    params = dict(attn_norm_g=attn_norm_g, w_in=w_in, pool_w=pool_w, pool_scale=pool_scale,
                  q_norm_g=q_norm_g, w_uq=w_uq, kv_norm_g=kv_norm_g, w_ukv=w_ukv,
                  fourier_w=fourier_w, conv_w=conv_w, conv_b=conv_b, lru_wa=lru_wa, lru_ba=lru_ba,
                  lru_wx=lru_wx, lru_bx=lru_bx, lru_lambda=lru_lambda, out_norm_g=out_norm_g,
                  w_out=w_out, ffn_norm_g=ffn_norm_g, router_w=router_w, router_b=router_b,
                  w_gate=w_gate, b_gate=b_gate, w_up=w_up, b_up=b_up, w_down=w_down, b_down=b_down)
    x = (x_prompt.reshape(T_P, D), x_sample.reshape(T_S, D))
    cvec = jnp.concatenate([c_ctx[None, :], c, jnp.zeros((SUBLANES - 1 - DEC_BATCH, D), F32)], axis=0)
    mod = _modulation(cvec, w_mod, b_mod)
    rope = _rope_tables()
    dft_p, dft_s = _dft_tables(SEQ), _dft_tables(DEC_SEQ)
    tri = jnp.asarray(np.tril(np.ones((TM, TM), np.float32)), BF16)
    upper = jnp.asarray(np.triu(np.ones((LANES, LANES), np.float32), 1), BF16)
    final_g = final_norm_g.reshape(1, D)
    h0_prompt = jnp.zeros((BATCH, 2, GW), F32)
    n_keys = PAST + DEC_SEQ

    new_ckv, new_krope, new_lru = [], [], []
    for l in range(DEPTH):
        lw = _layer_weights(l, params)
        zmix, q, k, v, ckv, kr = _front(x, mod[l], lw, rope)
        new_ckv.append(ckv[:T_P].reshape(BATCH, SEQ, KV_LORA))
        new_krope.append(kr[:T_P].reshape(BATCH, SEQ, QK_ROPE))

        yb_p = _attention(q, k, v, BATCH, SEQ, SEQ, SEQ, 0, group=PROMPT_GROUP)
        yc_p = _fourier(zmix, dft_p, lw, BATCH, SEQ, 0, group=PROMPT_GROUP)
        ya_p, yd_p, st_p = _seq_mixers(zmix, h0_prompt, lw, BATCH, SEQ, 0)
        new_lru.append(st_p)

        kr_ctx = jnp.pad(cache_krope[:, l].reshape(DEC_BATCH * PAST, QK_ROPE),
                         ((0, 0), (0, LANES - QK_ROPE)))
        k_ctx, v_ctx = _kv_expand(cache_ckv[:, l].reshape(DEC_BATCH * PAST, KV_LORA), kr_ctx, lw)
        k_s = jnp.concatenate([k_ctx.reshape(DEC_BATCH, PAST, -1),
                               k[T_P:].reshape(DEC_BATCH, DEC_SEQ, -1)], axis=1)
        v_s = jnp.concatenate([v_ctx.reshape(DEC_BATCH, PAST, -1),
                               v[T_P:].reshape(DEC_BATCH, DEC_SEQ, -1)], axis=1)
        yb_s = _attention(q, k_s.reshape(DEC_BATCH * n_keys, -1), v_s.reshape(DEC_BATCH * n_keys, -1),
                          DEC_BATCH, DEC_SEQ, n_keys, 512, T_P)
        yc_s = _fourier(zmix, dft_s, lw, DEC_BATCH, DEC_SEQ, T_P)
        ya_s, yd_s, _ = _seq_mixers(zmix, state_lru[:, l], lw, DEC_BATCH, DEC_SEQ, T_P)

        x1, h2lin, route, prob, tab, counts = _post(
            x, mod[l], (ya_p, yb_p, yc_p, yd_p), (ya_s, yb_s, yc_s, yd_s), lw, tri, upper)
        dest, sched, loc, top_p, chunk_src, n_chunks = _routing_tables(
            route, prob, counts[0, :N_EXP], tab)
        xs = _gather_rows(dest, sched[1], h2lin)
        ys = _experts(l, sched, xs, params)
        x = _combine(loc, top_p, chunk_src, n_chunks, ys, x1, mod[l], final_g,
                     final=(l == DEPTH - 1))

    y_prompt, y_sample = x
    return (y_prompt.reshape(BATCH, SEQ, D), y_sample.reshape(DEC_BATCH, DEC_SEQ, D),
            jnp.stack(new_ckv, axis=1), jnp.stack(new_krope, axis=1), jnp.stack(new_lru, axis=1))
```

```python
import functools

import numpy as np
import jax
import jax.numpy as jnp
from jax import lax
from jax.experimental import pallas as pl
from jax.experimental.pallas import tpu as pltpu

F32 = jnp.float32
BF16 = jnp.bfloat16
I32 = jnp.int32

D = 1024
BATCH, SEQ = 32, 256
DEC_BATCH, DEC_SEQ, PAST = 2, 2048, 512
T_P = BATCH * SEQ
T_S = DEC_BATCH * DEC_SEQ
T = T_P + T_S
DEPTH = 2
GRID_W = 64
GW = 256
Q_LORA, KV_LORA, QK_NOPE, QK_ROPE, V_DIM, HEADS = 192, 128, 64, 32, 64, 4
HEAD_PAD = 128
ROPE_BASE = 10000.0
Q_SCALE = float((QK_NOPE + QK_ROPE) ** -0.5 * np.log2(np.e))
POOL_WINDOWS = (2, 4, 8, 16)
LRU_C = 8.0
N_EXP, TOP_K = 32, 4
LIMIT, ALPHA = 7.0, 1.702
EPS = 1e-6

LANES = 128
SUBLANES = 8
VMEM_LIMIT = 56 * 1024 * 1024

TM = 512
N_PT = T_P // TM
TILES_PER_DEC = DEC_SEQ // TM
TM_E = 512
EXPERT_ROWS = (TM_E, TM_E // 2, TM_E // 4)
W_PIECES = 8
N_SLOTS = T * TOP_K + N_EXP * TM_E
NB_E = N_SLOTS // TM_E
ROW_F = D // LANES
GATHER_VMEM_LIMIT = 60 * 1024 * 1024
CH_C = 16
MAX_CH = TM * TOP_K // CH_C + N_EXP
GATHER_UNROLL = 16
PROMPT_GROUP = 4
SEQ_PAD = 32
FRONT = 8

W_COLS = 1536
COL_Q, COL_KV, COL_KR = 1024, 1280, 1408


_PROMPT_ROWS = pl.BlockSpec((TM, D), lambda i: (jnp.minimum(i, N_PT - 1), 0))
_SAMPLE_ROWS = pl.BlockSpec((TM, D), lambda i: (jnp.maximum(i - N_PT, 0), 0))


def _cparams(sem, vmem=VMEM_LIMIT):
    return pltpu.CompilerParams(dimension_semantics=sem, vmem_limit_bytes=vmem)


def _bdot(a, b):
    return jnp.dot(a.astype(BF16), b.astype(BF16), preferred_element_type=F32)


def _split(a):
    hi = a.astype(BF16)
    lo = (a - hi.astype(F32)).astype(BF16)
    return hi, lo


def _dot3(a, b):
    ah, al = _split(a)
    bh, bl = _split(b)
    d = functools.partial(jnp.dot, preferred_element_type=F32)
    return d(ah, bh) + (d(al, bh) + d(ah, bl))


def _rms(x, n=None):
    n = x.shape[-1] if n is None else n
    return x * lax.rsqrt(jnp.sum(x * x, axis=-1, keepdims=True) * (1.0 / n) + EPS)


def _neg_expm1_double(x):
    t = jnp.tanh(x)
    return -2.0 * t / (1.0 - t)


def _sigmoid(x):
    return 0.5 * jnp.tanh(0.5 * x) + 0.5


def _tile_row(i):
    return jnp.where(i >= N_PT, 1 + (i - N_PT) // TILES_PER_DEC, 0)


def _mod_kernel(c_ref, w_ref, b_ref, o_ref):
    s = jax.nn.silu(c_ref[...])
    o_ref[0] = _dot3(s, w_ref[0]) + b_ref[0]


def _modulation(cvec, w_mod, b_mod):
    tn = 768
    n = 6 * D
    return pl.pallas_call(
        _mod_kernel,
        out_shape=jax.ShapeDtypeStruct((DEPTH, SUBLANES, n), F32),
        grid=(DEPTH, n // tn),
        in_specs=[pl.BlockSpec((SUBLANES, D), lambda l, j: (0, 0)),
                  pl.BlockSpec((1, D, tn), lambda l, j: (l, 0, j)),
                  pl.BlockSpec((1, 1, tn), lambda l, j: (l, 0, j))],
        out_specs=pl.BlockSpec((1, SUBLANES, tn), lambda l, j: (l, 0, j)),
        compiler_params=_cparams(("parallel", "parallel")),
        name="modulation",
    )(cvec, w_mod, b_mod.reshape(DEPTH, 1, n))


def _front_kernel(xp_ref, xs_ref, mod_ref, g_ref, w_ref, qg_ref, wuq_ref, kvg_ref, wkn_ref, wkr_ref,
                  wv_ref, cq_ref, saq_ref, sbq_ref, ck_ref, sak_ref, sbk_ref,
                  zmix_ref, q_ref, k_ref, v_ref, ckv_ref, kr_ref):
    i = pl.program_id(0)
    row = _tile_row(i)
    shift1 = mod_ref[pl.ds(row, 1), 0:D]
    scale1 = mod_ref[pl.ds(row, 1), D:2 * D]
    h = _rms(jnp.where(i >= N_PT, xs_ref[...], xp_ref[...])) * g_ref[...]
    h = h * (1.0 + scale1) + shift1
    z = jnp.dot(h.astype(BF16), w_ref[...], preferred_element_type=F32)
    zmix_ref[...] = z[:, 0:4 * GW]
    qn = _rms(z[:, COL_Q:COL_Q + 2 * LANES], Q_LORA) * qg_ref[...]
    q = _bdot(qn, wuq_ref[...])
    ckv = _rms(z[:, COL_KV:COL_KV + KV_LORA]) * kvg_ref[...]
    ckv_ref[...] = ckv
    ckv_b = ckv.astype(BF16)
    v_ref[...] = jnp.dot(ckv_b, wv_ref[...], preferred_element_type=F32).astype(BF16)
    kn = jnp.dot(ckv_b, wkn_ref[...], preferred_element_type=F32)
    kr = z[:, COL_KR:COL_KR + LANES]

    @pl.when(i < N_PT)
    def _():
        q_ref[...] = (q * Q_SCALE).astype(BF16)
        kr_ref[...] = kr[:, 0:QK_ROPE]
        k_ref[...] = (kn + _bdot(kr, wkr_ref[...])).astype(BF16)

    @pl.when(i >= N_PT)
    def _():
        wq = HEADS * HEAD_PAD
        half = QK_ROPE // 2
        qr = (q * cq_ref[...] + pltpu.roll(q, wq - half, 1) * saq_ref[...]
              + pltpu.roll(q, half, 1) * sbq_ref[...])
        krr = (kr * ck_ref[...] + pltpu.roll(kr, LANES - half, 1) * sak_ref[...]
               + pltpu.roll(kr, half, 1) * sbk_ref[...])
        q_ref[...] = (qr * Q_SCALE).astype(BF16)
        kr_ref[...] = krr[:, 0:QK_ROPE]
        k_ref[...] = (kn + _bdot(krr, wkr_ref[...])).astype(BF16)


def _front(x, mod_l, lw, rope):
    full = lambda shape: pl.BlockSpec(shape, lambda i: (0,) * len(shape))
    rows = lambda w: pl.BlockSpec((TM, w), lambda i: (i, 0))
    rope_rows = lambda w: pl.BlockSpec(
        (TM, w), lambda i: (jnp.maximum(i - N_PT, 0) % TILES_PER_DEC, 0))
    wq = HEADS * HEAD_PAD
    return pl.pallas_call(
        _front_kernel,
        out_shape=(jax.ShapeDtypeStruct((T, 4 * GW), F32),
                   jax.ShapeDtypeStruct((T, wq), BF16),
                   jax.ShapeDtypeStruct((T, wq), BF16),
                   jax.ShapeDtypeStruct((T, HEADS * V_DIM), BF16),
                   jax.ShapeDtypeStruct((T, KV_LORA), F32),
                   jax.ShapeDtypeStruct((T, QK_ROPE), F32)),
        grid=(T // TM,),
        in_specs=[_PROMPT_ROWS, _SAMPLE_ROWS, full((SUBLANES, 6 * D)), full((1, D)), full((D, W_COLS)),
                  full((1, 2 * LANES)), full((2 * LANES, wq)), full((1, KV_LORA)),
                  full((KV_LORA, wq)), full((LANES, wq)), full((KV_LORA, HEADS * V_DIM)),
                  rope_rows(wq), rope_rows(wq), rope_rows(wq),
                  rope_rows(LANES), rope_rows(LANES), rope_rows(LANES)],
        out_specs=(rows(4 * GW), rows(wq), rows(wq), rows(HEADS * V_DIM), rows(KV_LORA),
                   rows(QK_ROPE)),
        compiler_params=_cparams(("parallel",)),
        name="front",
    )(*x, mod_l, lw["attn_g"], lw["w_in"], lw["q_g"], lw["w_uq"], lw["kv_g"], lw["w_kn"],
      lw["w_kr"], lw["w_v"], *rope)


def _kvexp_kernel(ckv_ref, kr_ref, wkn_ref, wkr_ref, wv_ref, k_ref, v_ref):
    ckv_b = ckv_ref[...].astype(BF16)
    v_ref[...] = jnp.dot(ckv_b, wv_ref[...], preferred_element_type=F32).astype(BF16)
    kn = jnp.dot(ckv_b, wkn_ref[...], preferred_element_type=F32)
    k_ref[...] = (kn + _bdot(kr_ref[...], wkr_ref[...])).astype(BF16)


def _kv_expand(ckv, kr_pad, lw):
    n = ckv.shape[0]
    wq = HEADS * HEAD_PAD
    full = lambda shape: pl.BlockSpec(shape, lambda i: (0,) * len(shape))
    return pl.pallas_call(
        _kvexp_kernel,
        out_shape=(jax.ShapeDtypeStruct((n, wq), BF16),
                   jax.ShapeDtypeStruct((n, HEADS * V_DIM), BF16)),
        grid=(1,),
        in_specs=[full((n, KV_LORA)), full((n, LANES)), full((KV_LORA, wq)), full((LANES, wq)),
                  full((KV_LORA, HEADS * V_DIM))],
        out_specs=(full((n, wq)), full((n, HEADS * V_DIM))),
        compiler_params=_cparams(("arbitrary",)),
        name="kv_expand",
    )(ckv, kr_pad, lw["w_kn"], lw["w_kr"], lw["w_v"])


def _attn_kernel(q_ref, k_ref, v_ref, o_ref, *, group, tq, n_keys):
    for g in range(group):
        outs = []
        for h in range(HEADS):
            qh = q_ref[g * tq:(g + 1) * tq, h * HEAD_PAD:(h + 1) * HEAD_PAD]
            kh = k_ref[g * n_keys:(g + 1) * n_keys, h * HEAD_PAD:(h + 1) * HEAD_PAD]
            s = lax.dot_general(qh, kh, (((1,), (1,)), ((), ())), preferred_element_type=F32)
            m = jnp.max(s, axis=-1, keepdims=True)
            p = jnp.exp2(s - m)
            l = jnp.sum(p, axis=-1, keepdims=True)
            vh = v_ref[g * n_keys:(g + 1) * n_keys, h * V_DIM:(h + 1) * V_DIM]
            outs.append(jnp.dot(p.astype(BF16), vh, preferred_element_type=F32) / l)
        o_ref[g * tq:(g + 1) * tq, :] = jnp.concatenate(outs, axis=-1)


def _attention(q, k, v, n_batch, seq, n_keys, tq, row0, group=1):
    nq = seq // tq
    wq = HEADS * HEAD_PAD
    return pl.pallas_call(
        functools.partial(_attn_kernel, group=group, tq=tq, n_keys=n_keys),
        out_shape=jax.ShapeDtypeStruct((n_batch * seq, GW), F32),
        grid=(n_batch // group, nq),
        in_specs=[pl.BlockSpec((group * tq, wq), lambda b, i: (row0 // (group * tq) + b * nq + i, 0)),
                  pl.BlockSpec((group * n_keys, wq), lambda b, i: (b, 0)),
                  pl.BlockSpec((group * n_keys, HEADS * V_DIM), lambda b, i: (b, 0))],
        out_specs=pl.BlockSpec((group * tq, GW), lambda b, i: (b * nq + i, 0)),
        compiler_params=_cparams(("parallel", "parallel")),
        name="attention",
    )(q, k, v)


def _fourier_kernel(z_ref, c_ref, s_ref, bdc_ref, bds_ref, w_ref, o_ref, xc_ref, xs_ref, *,
                    norm, group, seq, ts):
    b = pl.program_id(1)

    @pl.when(pl.program_id(0) == 0)
    def _():
        xb = z_ref[...].astype(BF16)
        xc_ref[b] = _bdot(xb, bdc_ref[...]).astype(BF16)
        xs_ref[b] = _bdot(xb, bds_ref[...]).astype(BF16)

    cb, sb = c_ref[...].astype(BF16), s_ref[...].astype(BF16)
    for g in range(group):
        f = (jnp.dot(cb, xc_ref[b, g * seq:(g + 1) * seq, :], preferred_element_type=F32)
             - jnp.dot(sb, xs_ref[b, g * seq:(g + 1) * seq, :], preferred_element_type=F32)) * norm
        o_ref[g * ts:(g + 1) * ts, :] = _bdot(f, w_ref[...])


def _fourier(zmix, consts, lw, n_batch, seq, row0, group=1):
    ts = min(seq, 512)
    nj = seq // ts
    assert group == 1 or nj == 1
    nb = n_batch // group
    cmat, smat = consts
    full = lambda shape: pl.BlockSpec(shape, lambda j, b: (0,) * len(shape))
    z_spec = pl.BlockSpec((group * seq, GW),
                          lambda j, b: (row0 // (group * seq) + jnp.where(j == 0, b, nb - 1), 1))
    return pl.pallas_call(
        functools.partial(_fourier_kernel, norm=float((seq * (GW // 4)) ** -0.5), group=group,
                          seq=seq, ts=ts),
        out_shape=jax.ShapeDtypeStruct((n_batch * seq, GW), F32),
        grid=(nj, nb),
        in_specs=[z_spec,
                  pl.BlockSpec((ts, seq), lambda j, b: (j, 0)),
                  pl.BlockSpec((ts, seq), lambda j, b: (j, 0)),
                  full((GW, GW)), full((GW, GW)), full((GW, GW))],
        out_specs=pl.BlockSpec((group * ts, GW), lambda j, b: (b * nj + j, 0)),
        scratch_shapes=[pltpu.VMEM((nb, group * seq, GW), BF16),
                        pltpu.VMEM((nb, group * seq, GW), BF16)],
        compiler_params=_cparams(("arbitrary", "arbitrary")),
        name="fourier",
    )(zmix, cmat, smat, lw["bd_c"], lw["bd_s"], lw["w_f"])


def _seq_kernel(za_ref, zr_ref, zg_ref, h0_ref, icnt_ref, wp_ref, ps_ref, cw_ref, cb_ref, wa_ref,
                ba_ref, wx_ref, bx_ref, lam_ref, ya_ref, yd_ref, st_ref,
                pa_ref, pb_ref, xp_ref, a_ref, b_ref, *, seq):
    n = seq + SEQ_PAD
    span = seq + 2 * FRONT
    zeros_pad = jnp.zeros((n, GW), F32)

    za = za_ref[...]
    pa_ref[...] = zeros_pad
    pb_ref[...] = zeros_pad
    pa_ref[FRONT:FRONT + seq, :] = za
    pb_ref[0:span, :] = pa_ref[0:span, :] + pa_ref[1:span + 1, :]
    win2 = pb_ref[FRONT - 1:FRONT - 1 + seq, :]
    pa_ref[0:span, :] = pb_ref[0:span, :] + pb_ref[2:span + 2, :]
    win4 = pa_ref[FRONT - 2:FRONT - 2 + seq, :]
    pb_ref[0:span, :] = pa_ref[0:span, :] + pa_ref[4:span + 4, :]
    win8 = pb_ref[FRONT - 4:FRONT - 4 + seq, :]
    pa_ref[0:span, :] = pb_ref[0:span, :] + pb_ref[8:span + 8, :]
    win16 = pa_ref[FRONT - 8:FRONT - 8 + seq, :]
    grp = lax.broadcasted_iota(I32, (1, GW), 1) // (GW // 4)
    win = jnp.where(grp == 0, win2, jnp.where(grp == 1, win4, jnp.where(grp == 2, win8, win16)))
    dlt = win * icnt_ref[...] - za
    ya_ref[...] = _bdot(dlt, wp_ref[...]) * ps_ref[...]

    xp_ref[...] = zeros_pad
    xp_ref[FRONT:FRONT + seq, :] = zr_ref[...]
    xc = cb_ref[...] + cw_ref[0:1, :] * xp_ref[FRONT - 2:FRONT - 2 + seq, :]
    for kk in range(1, 4):
        xc = xc + cw_ref[kk:kk + 1, :] * xp_ref[FRONT - 2 + kk:FRONT - 2 + kk + seq, :]
    xcb = xc.astype(BF16)
    n_grp = seq // SUBLANES
    sub = lax.broadcasted_iota(I32, (n_grp, SUBLANES, GW), 1)

    total = None
    for d in range(2):
        r = _sigmoid(jnp.dot(xcb, wa_ref[d], preferred_element_type=F32) + ba_ref[d])
        ig = _sigmoid(jnp.dot(xcb, wx_ref[d], preferred_element_type=F32) + bx_ref[d])
        log_a = (-LRU_C) * r * jax.nn.softplus(-lam_ref[d])
        a = jnp.exp(log_a).reshape(n_grp, SUBLANES, GW)
        b = (jnp.sqrt(_neg_expm1_double(log_a)) * (ig * xc)).reshape(n_grp, SUBLANES, GW)
        for k in (1, 2, 4):
            shift = k if d == 0 else SUBLANES - k
            m = sub >= k if d == 0 else sub < SUBLANES - k
            ap, bp = pltpu.roll(a, shift, 1), pltpu.roll(b, shift, 1)
            b = jnp.where(m, a * bp + b, b)
            a = jnp.where(m, a * ap, a)
        a_ref[...] = a.reshape(seq, GW)
        b_ref[...] = b.reshape(seq, GW)

        def step(g, carry, d=d):
            gi = g if d == 0 else n_grp - 1 - g
            off = pl.multiple_of(gi * SUBLANES, SUBLANES)
            hh = a_ref[pl.ds(off, SUBLANES), :] * carry + b_ref[pl.ds(off, SUBLANES), :]
            b_ref[pl.ds(off, SUBLANES), :] = hh
            edge = hh[SUBLANES - 1:SUBLANES, :] if d == 0 else hh[0:1, :]
            return jnp.broadcast_to(edge, (SUBLANES, GW))

        last = lax.fori_loop(0, n_grp, step, jnp.broadcast_to(h0_ref[0, d:d + 1, :], (SUBLANES, GW)))
        st_ref[0, d:d + 1, :] = last[0:1, :]
        total = b_ref[...] if total is None else total + b_ref[...]

    yd_ref[...] = total * jax.nn.gelu(zg_ref[...])


def _pool_inverse_counts(seq):
    pos = np.arange(seq)[:, None]
    half = np.repeat(np.array(POOL_WINDOWS) // 2, GW // len(POOL_WINDOWS))[None, :]
    cnt = np.minimum(pos + half, seq) - np.maximum(pos - half, 0)
    return jnp.asarray(1.0 / cnt, F32)


def _seq_mixers(zmix, h0, lw, n_batch, seq, row0):
    full = lambda shape: pl.BlockSpec(shape, lambda b: (0,) * len(shape))
    col = lambda c: pl.BlockSpec((seq, GW), lambda b: (row0 // seq + b, c))
    out_rows = pl.BlockSpec((seq, GW), lambda b: (b, 0))
    pad = pltpu.VMEM((seq + SEQ_PAD, GW), F32)
    return pl.pallas_call(
        functools.partial(_seq_kernel, seq=seq),
        out_shape=(jax.ShapeDtypeStruct((n_batch * seq, GW), F32),
                   jax.ShapeDtypeStruct((n_batch * seq, GW), F32),
                   jax.ShapeDtypeStruct((n_batch, 2, GW), F32)),
        grid=(n_batch,),
        in_specs=[col(0), col(2), col(3), pl.BlockSpec((1, 2, GW), lambda b: (b, 0, 0)),
                  full((seq, GW)), full((GW, GW)), full((1, GW)), full((4, GW)), full((1, GW)),
                  full((2, GW, GW)), full((2, 1, GW)), full((2, GW, GW)), full((2, 1, GW)),
                  full((2, 1, GW))],
        out_specs=(out_rows, out_rows, pl.BlockSpec((1, 2, GW), lambda b: (b, 0, 0))),
        scratch_shapes=[pad, pad, pad, pltpu.VMEM((seq, GW), F32), pltpu.VMEM((seq, GW), F32)],
        compiler_params=_cparams(("parallel",)),
        name="seq_mixers",
    )(zmix, zmix, zmix, h0, _pool_inverse_counts(seq), lw["bd_pool"], lw["pool_scale"], lw["conv_w"],
      lw["conv_b"], lw["bd_wa"], lw["lru_ba"], lw["bd_wx"], lw["lru_bx"], lw["lru_lam"])


def _post_kernel(xp_ref, xs_ref, mod_ref, yap_ref, ybp_ref, ycp_ref, ydp_ref, yas_ref, ybs_ref, ycs_ref,
                 yds_ref, og_ref, wo_ref, fg_ref, rw_ref, rb_ref, tri_ref, upper_ref,
                 x1_ref, h2p_ref, route_ref, prob_ref, tab_ref, cnt_ref, carry_ref):
    i = pl.program_id(0)
    is_s = i >= N_PT
    row = _tile_row(i)
    gate1 = mod_ref[pl.ds(row, 1), 2 * D:3 * D]
    shift2 = mod_ref[pl.ds(row, 1), 3 * D:4 * D]
    scale2 = mod_ref[pl.ds(row, 1), 4 * D:5 * D]

    @pl.when(i == 0)
    def _():
        carry_ref[...] = jnp.zeros_like(carry_ref)

    groups = []
    for gi, (p_ref, s_ref) in enumerate(((yap_ref, yas_ref), (ybp_ref, ybs_ref),
                                         (ycp_ref, ycs_ref), (ydp_ref, yds_ref))):
        y = jnp.where(is_s, s_ref[...], p_ref[...])
        groups.append((_rms(y) * og_ref[gi:gi + 1, :]).astype(BF16))
    ycat = jnp.concatenate(groups, axis=-1)
    x = jnp.where(is_s, xs_ref[...], xp_ref[...])
    x1 = x + gate1 * jnp.dot(ycat, wo_ref[...], preferred_element_type=F32)
    x1_ref[...] = x1
    h2 = _rms(x1) * fg_ref[...]
    h2 = h2 * (1.0 + scale2) + shift2

    for j in range(ROW_F):
        h2p_ref[pl.ds(j, TM, stride=ROW_F), :] = h2[:, j * LANES:(j + 1) * LANES]

    h_hi, h_lo = _split(h2)
    both = jnp.dot(h_hi, rw_ref[...], preferred_element_type=F32)
    cross = both[:, LANES:2 * LANES] + jnp.dot(h_lo, rw_ref[:, 0:LANES], preferred_element_type=F32)
    logits = both[:, 0:LANES] + cross + rb_ref[...]
    lane = lax.broadcasted_iota(I32, (TM, LANES), 1)
    lane_f = lane.astype(F32)
    neg = jnp.float32(-jnp.inf)
    cur = jnp.where(lane < N_EXP, logits, neg)
    sel, vals, idxs = [], [], []
    for _ in range(TOP_K):
        m = jnp.max(cur, axis=-1, keepdims=True)
        idx = jnp.min(jnp.where(cur == m, lane_f, float(LANES)), axis=-1, keepdims=True)
        hit = lane_f == idx
        sel.append(hit)
        vals.append(m)
        idxs.append(idx)
        cur = jnp.where(hit, neg, cur)
    exps = [jnp.exp(v - vals[0]) for v in vals]
    denom = exps[0] + exps[1] + exps[2] + exps[3]
    onehot = jnp.where(sel[0] | sel[1] | sel[2] | sel[3], 1.0, 0.0)
    cum_l = jnp.dot(tri_ref[...], onehot.astype(BF16), preferred_element_type=F32)
    before = carry_ref[0:1, :]
    cum = cum_l + before
    n_tile = cum_l[TM - 1:TM, :]
    nch = jnp.floor((n_tile + (CH_C - 1.0)) * (1.0 / CH_C))
    nch8 = jnp.broadcast_to(nch, (SUBLANES, LANES))
    base = jnp.dot(nch8.astype(BF16), upper_ref[...], preferred_element_type=F32)[0:1, :]
    pk = jnp.zeros((TM, LANES), I32)
    pf = jnp.zeros((TM, LANES), F32)
    for k in range(TOP_K):
        rank = jnp.sum(jnp.where(sel[k], cum - 1.0, 0.0), axis=-1, keepdims=True).astype(I32)
        local = jnp.sum(jnp.where(sel[k], (base * CH_C + cum_l - 1.0) * ROW_F, 0.0), axis=-1,
                        keepdims=True).astype(I32)
        pk = jnp.where(lane == k, idxs[k].astype(I32), pk)
        pk = jnp.where(lane == TOP_K + k, rank, pk)
        pk = jnp.where(lane == 2 * TOP_K + k, local, pk)
        pf = jnp.where(lane == k, exps[k] / denom, pf)
    route_ref[...] = pk.T[0:4 * TOP_K, :]
    prob_ref[...] = pf.T[0:SUBLANES, :]
    row8 = lax.broadcasted_iota(I32, (SUBLANES, LANES), 0)
    tab_ref[...] = jnp.where(row8 == 0, before, jnp.where(row8 == 1, nch, base))
    new_carry = jnp.broadcast_to(cum[TM - 1:TM, :], (SUBLANES, LANES))
    carry_ref[...] = new_carry
    cnt_ref[...] = new_carry


def _post(x, mod_l, ys_prompt, ys_sample, lw, tri, upper):
    full = lambda shape: pl.BlockSpec(shape, lambda i: (0,) * len(shape))
    rows = lambda w: pl.BlockSpec((TM, w), lambda i: (i, 0))
    prow = pl.BlockSpec((TM, GW), lambda i: (jnp.minimum(i, N_PT - 1), 0))
    srow = pl.BlockSpec((TM, GW), lambda i: (jnp.maximum(i - N_PT, 0), 0))
    return pl.pallas_call(
        _post_kernel,
        out_shape=(jax.ShapeDtypeStruct((T, D), F32),
                   jax.ShapeDtypeStruct((T * ROW_F, LANES), F32),
                   jax.ShapeDtypeStruct((4 * TOP_K, T), I32),
                   jax.ShapeDtypeStruct((SUBLANES, T), F32),
                   jax.ShapeDtypeStruct((T // TM * SUBLANES, LANES), F32),
                   jax.ShapeDtypeStruct((SUBLANES, LANES), F32)),
        grid=(T // TM,),
        in_specs=[_PROMPT_ROWS, _SAMPLE_ROWS, full((SUBLANES, 6 * D)),
                  prow, prow, prow, prow, srow, srow, srow, srow,
                  full((4, GW)), full((D, D)), full((1, D)), full((D, 2 * LANES)), full((1, LANES)),
                  full((TM, TM)), full((LANES, LANES))],
        out_specs=(rows(D), pl.BlockSpec((TM * ROW_F, LANES), lambda i: (i, 0)),
                   pl.BlockSpec((4 * TOP_K, TM), lambda i: (0, i)),
                   pl.BlockSpec((SUBLANES, TM), lambda i: (0, i)),
                   pl.BlockSpec((SUBLANES, LANES), lambda i: (i, 0)), full((SUBLANES, LANES))),
        scratch_shapes=[pltpu.VMEM((SUBLANES, LANES), F32)],
        compiler_params=_cparams(("arbitrary",)),
        name="post",
    )(*x, mod_l, *ys_prompt, *ys_sample, lw["out_g"], lw["w_out"], lw["ffn_g"], lw["router_w"],
      lw["router_b"], tri, upper)


def _gather_kernel(dest_ref, nused_ref, zeros_hbm, src_ref, o_ref, stok_ref, sem):
    i = pl.program_id(0)

    @pl.when(i == 0)
    def _():
        init = pltpu.make_async_copy(zeros_hbm, stok_ref, sem)
        init.start()
        init.wait()

        def scatter(t8, c):
            row0 = t8 * (SUBLANES * ROW_F)
            for k in range(TOP_K):
                for u in range(SUBLANES):
                    stok_ref[dest_ref[k * T + t8 * SUBLANES + u]] = row0 + u * ROW_F
            return c
        lax.fori_loop(0, T // SUBLANES, scatter, 0)

    @pl.when(i < nused_ref[0])
    def _():
        def rows(r16, c):
            r0 = pl.multiple_of(r16 * GATHER_UNROLL, GATHER_UNROLL)
            for u in range(GATHER_UNROLL):
                src = pl.multiple_of(stok_ref[i * TM_E + r0 + u], ROW_F)
                dst = pl.multiple_of(r0 * ROW_F, GATHER_UNROLL * ROW_F) + u * ROW_F
                o_ref[pl.ds(dst, ROW_F), :] = src_ref[pl.ds(src, ROW_F), :]
            return c
        lax.fori_loop(0, TM_E // GATHER_UNROLL, rows, 0)

    @pl.when(i >= nused_ref[0])
    def _():
        o_ref[...] = jnp.zeros_like(o_ref)


def _gather_rows(dest, n_used, h2lin):
    return pl.pallas_call(
        _gather_kernel,
        out_shape=jax.ShapeDtypeStruct((N_SLOTS * ROW_F, LANES), F32),
        grid_spec=pltpu.PrefetchScalarGridSpec(
            num_scalar_prefetch=2,
            grid=(NB_E,),
            in_specs=[pl.BlockSpec(memory_space=pl.ANY),
                      pl.BlockSpec(memory_space=pltpu.VMEM)],
            out_specs=pl.BlockSpec((TM_E * ROW_F, LANES), lambda i, d, nu: (i, 0)),
            scratch_shapes=[pltpu.SMEM((N_SLOTS,), I32), pltpu.SemaphoreType.DMA(())]),
        compiler_params=_cparams(("arbitrary",), GATHER_VMEM_LIMIT),
        name="moe_gather",
    )(dest, n_used, jnp.zeros((N_SLOTS,), I32), h2lin)


def _expert_rows(n, x_ref, w_ref, slot, bg_ref, bu_ref, bd_ref, o_ref):
    x = jnp.concatenate([x_ref[pl.ds(j, n, stride=ROW_F), :].astype(BF16) for j in range(ROW_F)],
                        axis=-1)
    g = jnp.dot(x, w_ref[slot, 0].astype(BF16), preferred_element_type=F32) + bg_ref[0, 0]
    u = jnp.dot(x, w_ref[slot, 1].astype(BF16), preferred_element_type=F32) + bu_ref[0, 0]
    g = jnp.minimum(g, LIMIT)
    u = jnp.clip(u, -LIMIT, LIMIT)
    act = (u + 1.0) * (g * jax.nn.sigmoid(ALPHA * g))
    y = jnp.dot(act.astype(BF16), w_ref[slot, 2].astype(BF16),
                preferred_element_type=F32) + bd_ref[0, 0]
    for j in range(ROW_F):
        o_ref[pl.ds(j, n, stride=ROW_F), :] = y[:, j * LANES:(j + 1) * LANES]
    if n < TM_E:
        o_ref[n * ROW_F:TM_E * ROW_F, :] = jnp.zeros(((TM_E - n) * ROW_F, LANES), F32)


def _expert_kernel(be_ref, nused_ref, valid_ref, first_ref, next_ref, par_ref,
                   x_ref, wg_hbm, bg_ref, wu_hbm, bu_ref, wd_hbm, bd_ref, o_ref, w_ref, sems, *, layer):
    i = pl.program_id(0)
    ib = jnp.minimum(i, NB_E - 1)
    live = i < nused_ref[0]
    valid = valid_ref[ib]
    expert = be_ref[ib]
    slot = par_ref[ib]
    band = D // W_PIECES

    def fetch(e, s):
        return [pltpu.make_async_copy(src.at[layer, e, pl.ds(c * band, band)],
                                      w_ref.at[s, m, pl.ds(c * band, band)], sems.at[s, m])
                for m, src in enumerate((wg_hbm, wu_hbm, wd_hbm)) for c in range(W_PIECES)]

    @pl.when(i == 0)
    def _():
        for cp in fetch(expert, slot):
            cp.start()

    @pl.when(live & (first_ref[ib] == 1))
    def _():
        @pl.when(next_ref[ib] >= 0)
        def _():
            for cp in fetch(next_ref[ib], 1 - slot):
                cp.start()

        for cp in fetch(expert, slot):
            cp.wait()

    for n_idx, n in enumerate(EXPERT_ROWS):
        fits = valid <= n
        if n_idx + 1 < len(EXPERT_ROWS):
            fits = fits & (valid > EXPERT_ROWS[n_idx + 1])

        @pl.when(live & fits)
        def _(n=n):
            _expert_rows(n, x_ref, w_ref, slot, bg_ref, bu_ref, bd_ref, o_ref)

    @pl.when(jnp.logical_not(live))
    def _():
        o_ref[...] = jnp.zeros_like(o_ref)


def _experts(l, sched, xs, p):
    last = NB_E - 1
    bspec = pl.BlockSpec((1, 1, 1, D), lambda i, be, *_: (l, be[jnp.minimum(i, last)], 0, 0))
    hbm = pl.BlockSpec(memory_space=pl.ANY)
    bias = lambda b: b.reshape(DEPTH, N_EXP, 1, D)
    return pl.pallas_call(
        functools.partial(_expert_kernel, layer=l),
        out_shape=jax.ShapeDtypeStruct(((NB_E + 1) * TM_E * ROW_F, LANES), F32),
        grid_spec=pltpu.PrefetchScalarGridSpec(
            num_scalar_prefetch=len(sched),
            grid=(NB_E + 1,),
            in_specs=[pl.BlockSpec((TM_E * ROW_F, LANES), lambda i, *_: (jnp.minimum(i, last), 0)),
                      hbm, bspec, hbm, bspec, hbm, bspec],
            out_specs=pl.BlockSpec((TM_E * ROW_F, LANES), lambda i, *_: (i, 0)),
            scratch_shapes=[pltpu.VMEM((2, 3, D, D), F32), pltpu.SemaphoreType.DMA((2, 3))]),
        compiler_params=_cparams(("arbitrary",)),
        name="moe_experts",
    )(*sched, xs, p["w_gate"], bias(p["b_gate"]), p["w_up"], bias(p["b_up"]),
      p["w_down"], bias(p["b_down"]))


def _combine_kernel(loc_ref, p_ref, csrc_ref, nch_ref, ys_hbm, x_ref, mod_ref, fg_ref, *rest, final):
    *outs, buf_ref, acc_ref, sems = rest
    tb = pl.program_id(0)
    n_tb = pl.num_programs(0)
    slot = tb % 2
    row = _tile_row(tb)
    gate2 = mod_ref[pl.ds(row, 1), 5 * D:6 * D]
    chunk_rows = CH_C * ROW_F

    def chunk_copy(t, c, s):
        src = pl.multiple_of(csrc_ref[t * MAX_CH + c] * ROW_F, ROW_F)
        dst = pl.multiple_of(c * chunk_rows, chunk_rows)
        return pltpu.make_async_copy(ys_hbm.at[pl.ds(src, chunk_rows)],
                                     buf_ref.at[s, pl.ds(dst, chunk_rows)], sems.at[s])

    def issue(t, s):
        def body(c, carry):
            chunk_copy(t, c, s).start()
            return carry
        lax.fori_loop(0, nch_ref[t], body, 0)

    @pl.when(tb == 0)
    def _():
        issue(0, 0)

    @pl.when(tb + 1 < n_tb)
    def _():
        issue(tb + 1, 1 - slot)

    def drain(c, carry):
        chunk_copy(tb, c, slot).wait()
        return carry
    lax.fori_loop(0, nch_ref[tb], drain, 0)

    def tokens(r8, carry):
        r0 = pl.multiple_of(r8 * SUBLANES, SUBLANES)
        for u in range(SUBLANES):
            tok = tb * TM + r0 + u
            acc = None
            for k in range(TOP_K):
                off = pl.multiple_of(loc_ref[k * T + tok], ROW_F)
                term = buf_ref[slot, pl.ds(off, ROW_F), :] * p_ref[k * T + tok]
                acc = term if acc is None else acc + term
            dst = pl.multiple_of(r0 * ROW_F, SUBLANES * ROW_F) + u * ROW_F
            acc_ref[pl.ds(dst, ROW_F), :] = acc
        return carry
    lax.fori_loop(0, TM // SUBLANES, tokens, 0)

    moe = jnp.concatenate([acc_ref[pl.ds(j, TM, stride=ROW_F), :] for j in range(ROW_F)], axis=-1)
    x2 = x_ref[...] + gate2 * moe
    if final:
        x2 = _rms(x2) * fg_ref[...]
    prompt_ref, sample_ref = outs

    @pl.when(tb < N_PT)
    def _():
        prompt_ref[...] = x2

    @pl.when(tb >= N_PT)
    def _():
        sample_ref[...] = x2


def _combine(loc, top_p, chunk_src, n_chunks, ys, x1, mod_l, final_g, final):
    full = lambda shape: pl.BlockSpec(shape, lambda i, *_: (0,) * len(shape))
    rows = lambda w: pl.BlockSpec((TM, w), lambda i, *_: (i, 0))
    out_shape = (jax.ShapeDtypeStruct((T_P, D), F32), jax.ShapeDtypeStruct((T_S, D), F32))
    out_specs = (pl.BlockSpec((TM, D), lambda i, *_: (jnp.minimum(i, N_PT - 1), 0)),
                 pl.BlockSpec((TM, D), lambda i, *_: (jnp.maximum(i - N_PT, 0), 0)))
    return pl.pallas_call(
        functools.partial(_combine_kernel, final=final),
        out_shape=out_shape,
        grid_spec=pltpu.PrefetchScalarGridSpec(
            num_scalar_prefetch=4,
            grid=(T // TM,),
            in_specs=[pl.BlockSpec(memory_space=pl.ANY), rows(D), full((SUBLANES, 6 * D)),
                      full((1, D))],
            out_specs=out_specs,
            scratch_shapes=[pltpu.VMEM((2, MAX_CH * CH_C * ROW_F, LANES), F32),
                            pltpu.VMEM((TM * ROW_F, LANES), F32),
                            pltpu.SemaphoreType.DMA((2,))]),
        compiler_params=_cparams(("arbitrary",)),
        name="moe_combine",
    )(loc, top_p, chunk_src, n_chunks, ys, x1, mod_l, final_g)


def _rope_tables():
    rows = DEC_SEQ // GRID_W
    r = np.repeat(np.arange(rows, dtype=np.float64), GRID_W)
    c = np.tile(np.arange(GRID_W, dtype=np.float64), rows)
    n_freq = QK_ROPE // 4
    inv = (np.float32(ROPE_BASE) ** (-np.arange(n_freq, dtype=np.float32) / n_freq)).astype(np.float64)
    ang = np.concatenate([r[:, None] * inv, c[:, None] * inv], axis=-1).astype(np.float32)
    cos, sin = np.cos(ang.astype(np.float64)), np.sin(ang.astype(np.float64))
    half = QK_ROPE // 2

    def place(width, start):
        cf = np.ones((DEC_SEQ, width), np.float32)
        sa = np.zeros((DEC_SEQ, width), np.float32)
        sb = np.zeros((DEC_SEQ, width), np.float32)
        for s0 in start:
            cf[:, s0:s0 + half] = cos
            cf[:, s0 + half:s0 + 2 * half] = cos
            sa[:, s0:s0 + half] = -sin
            sb[:, s0 + half:s0 + 2 * half] = sin
        return jnp.asarray(cf), jnp.asarray(sa), jnp.asarray(sb)

    return (*place(HEADS * HEAD_PAD, [h * HEAD_PAD + QK_NOPE for h in range(HEADS)]),
            *place(LANES, [0]))


def _dft_tables(seq):
    kn = (np.arange(seq, dtype=np.int64)[:, None] * np.arange(seq, dtype=np.int64)[None, :]) % seq
    ang = 2.0 * np.pi * kn.astype(np.float64) / seq
    return jnp.asarray(np.cos(ang), F32), jnp.asarray(np.sin(ang), F32)


def _block_diag(blocks):
    g, n, _ = blocks.shape
    eye = jnp.eye(g, dtype=blocks.dtype)
    return jnp.einsum("gij,gh->gihj", blocks, eye).reshape(g * n, g * n)


def _layer_weights(l, p):
    w_in = p["w_in"][l]
    za, zq, zkv, zkr, zf, zr, zg = jnp.split(
        w_in, np.cumsum([GW, Q_LORA, KV_LORA, QK_ROPE, GW, GW])[:], axis=1)
    zpad = lambda n: jnp.zeros((D, n), F32)
    w_cols = jnp.concatenate([za, zf, zr, zg, zq, zpad(2 * LANES - Q_LORA), zkv, zkr,
                              zpad(LANES - QK_ROPE)], axis=1).astype(BF16)
    wq = HEADS * HEAD_PAD
    w_uq = p["w_uq"][l].reshape(Q_LORA, HEADS, QK_NOPE + QK_ROPE)
    w_uq = jnp.pad(w_uq, ((0, 2 * LANES - Q_LORA), (0, 0), (0, HEAD_PAD - QK_NOPE - QK_ROPE)))
    w_ukv = p["w_ukv"][l].reshape(KV_LORA, HEADS, QK_NOPE + V_DIM)
    w_kn = jnp.pad(w_ukv[:, :, :QK_NOPE], ((0, 0), (0, 0), (0, HEAD_PAD - QK_NOPE)))
    w_v = w_ukv[:, :, QK_NOPE:]
    place = np.zeros((LANES, HEADS, HEAD_PAD), np.float32)
    for h in range(HEADS):
        place[np.arange(QK_ROPE), h, QK_NOPE + np.arange(QK_ROPE)] = 1.0
    c64 = np.arange(GW // 4, dtype=np.int64)
    ang = 2.0 * np.pi * ((c64[:, None] * c64[None, :]) % (GW // 4)).astype(np.float64) / (GW // 4)
    four = lambda m: jnp.asarray(np.broadcast_to(m, (4,) + m.shape), F32)
    router_hi, router_lo = _split(jnp.pad(p["router_w"][l], ((0, 0), (0, LANES - N_EXP))))
    router_w = jnp.concatenate([router_hi, router_lo], axis=1)
    router_b = jnp.pad(p["router_b"][l], (0, LANES - N_EXP)).reshape(1, LANES)
    return {
        "attn_g": p["attn_norm_g"][l].reshape(1, D),
        "w_in": w_cols,
        "q_g": jnp.pad(p["q_norm_g"][l], (0, 2 * LANES - Q_LORA)).reshape(1, 2 * LANES),
        "w_uq": w_uq.reshape(2 * LANES, wq).astype(BF16),
        "kv_g": p["kv_norm_g"][l].reshape(1, KV_LORA),
        "w_kn": w_kn.reshape(KV_LORA, wq).astype(BF16),
        "w_kr": jnp.asarray(place.reshape(LANES, wq), BF16),
        "w_v": w_v.reshape(KV_LORA, HEADS * V_DIM).astype(BF16),
        "bd_c": _block_diag(four(np.cos(ang))),
        "bd_s": _block_diag(four(np.sin(ang))),
        "w_f": p["fourier_w"][l].astype(BF16),
        "bd_pool": _block_diag(p["pool_w"][l]).astype(BF16),
        "pool_scale": p["pool_scale"][l].reshape(1, GW),
        "conv_w": p["conv_w"][l],
        "conv_b": p["conv_b"][l].reshape(1, GW),
        "bd_wa": jnp.stack([_block_diag(p["lru_wa"][l, d]) for d in range(2)]).astype(BF16),
        "bd_wx": jnp.stack([_block_diag(p["lru_wx"][l, d]) for d in range(2)]).astype(BF16),
        "lru_ba": p["lru_ba"][l].reshape(2, 1, GW),
        "lru_bx": p["lru_bx"][l].reshape(2, 1, GW),
        "lru_lam": p["lru_lambda"][l].reshape(2, 1, GW),
        "out_g": p["out_norm_g"][l],
        "w_out": p["w_out"][l].astype(BF16),
        "ffn_g": p["ffn_norm_g"][l].reshape(1, D),
        "router_w": router_w,
        "router_b": router_b,
    }


def _routing_tables(route, prob, counts, tab):
    top_e, rank = route[0:TOP_K], route[TOP_K:2 * TOP_K]
    top_p = prob[0:TOP_K].reshape(-1)
    loc = route[2 * TOP_K:3 * TOP_K].reshape(-1)
    counts = counts.astype(I32)
    padded = (counts + TM_E - 1) // TM_E * TM_E
    pad_ends = jnp.cumsum(padded)
    pad_starts = pad_ends - padded
    experts = jnp.arange(N_EXP, dtype=I32)
    onehot = top_e[:, :, None] == experts
    dest = (jnp.sum(jnp.where(onehot, pad_starts, 0), axis=-1) + rank).reshape(T * TOP_K)
    n_used = (pad_ends[-1] // TM_E).astype(I32)
    blk = jnp.minimum(jnp.arange(NB_E, dtype=I32), n_used - 1) * TM_E
    block_e = jnp.minimum(jnp.sum(pad_ends[None, :] <= blk[:, None], axis=-1), N_EXP - 1).astype(I32)
    of_block = lambda a: jnp.sum(jnp.where(block_e[:, None] == experts, a, 0), axis=-1)
    valid = jnp.clip(of_block(pad_starts + counts) - blk, 0, TM_E).astype(I32)
    first = (blk == of_block(pad_starts)).astype(I32)
    later = (experts[None, :] > experts[:, None]) & (counts[None, :] > 0)
    next_of = jnp.min(jnp.where(later, experts[None, :], N_EXP), axis=-1)
    next_e = of_block(jnp.where(next_of < N_EXP, next_of, -1)).astype(I32)
    parity = (of_block(jnp.cumsum((counts > 0).astype(I32))) % 2).astype(I32)
    sched = (block_e, n_used.reshape(1), valid, first, next_e, parity)
    tab = tab.reshape(T // TM, SUBLANES, LANES)[:, :, :N_EXP].astype(I32)
    before, nch, base = tab[:, 0], tab[:, 1], tab[:, 2]
    run_start = pad_starts[None, :] + before
    ends = base + nch
    ci = jnp.arange(MAX_CH, dtype=I32)
    e_of = jnp.minimum(jnp.sum(ends[:, None, :] <= ci[None, :, None], axis=-1), N_EXP - 1)
    pick = lambda a: jnp.sum(jnp.where(e_of[:, :, None] == experts, a[:, None, :], 0), axis=-1)
    chunk_src = jnp.clip(pick(run_start) + (ci[None, :] - pick(base)) * CH_C, 0, N_SLOTS)
    return dest, sched, loc, top_p, chunk_src.reshape(-1), ends[:, N_EXP - 1]


def kernel(x_prompt, x_sample, cache_ckv, cache_krope, state_lru, c, c_ctx, w_mod, b_mod, attn_norm_g, w_in, pool_w, pool_scale, q_norm_g, w_uq, kv_norm_g, w_ukv, fourier_w, conv_w, conv_b, lru_wa, lru_ba, lru_wx, lru_bx, lru_lambda, out_norm_g, w_out, ffn_norm_g, router_w, router_b, w_gate, b_gate, w_up, b_up, w_down, b_down, final_norm_g):
    params = dict(attn_norm_g=attn_norm_g, w_in=w_in, pool_w=pool_w, pool_scale=pool_scale,
                  q_norm_g=q_norm_g, w_uq=w_uq, kv_norm_g=kv_norm_g, w_ukv=w_ukv,
                  fourier_w=fourier_w, conv_w=conv_w, conv_b=conv_b, lru_wa=lru_wa, lru_ba=lru_ba,
                  lru_wx=lru_wx, lru_bx=lru_bx, lru_lambda=lru_lambda, out_norm_g=out_norm_g,
                  w_out=w_out, ffn_norm_g=ffn_norm_g, router_w=router_w, router_b=router_b,
                  w_gate=w_gate, b_gate=b_gate, w_up=w_up, b_up=b_up, w_down=w_down, b_down=b_down)
    x = (x_prompt.reshape(T_P, D), x_sample.reshape(T_S, D))
    cvec = jnp.concatenate([c_ctx[None, :], c, jnp.zeros((SUBLANES - 1 - DEC_BATCH, D), F32)], axis=0)
    mod = _modulation(cvec, w_mod, b_mod)
    rope = _rope_tables()
    dft_p, dft_s = _dft_tables(SEQ), _dft_tables(DEC_SEQ)
    tri = jnp.asarray(np.tril(np.ones((TM, TM), np.float32)), BF16)
    upper = jnp.asarray(np.triu(np.ones((LANES, LANES), np.float32), 1), BF16)
    final_g = final_norm_g.reshape(1, D)
    h0_prompt = jnp.zeros((BATCH, 2, GW), F32)
    n_keys = PAST + DEC_SEQ

    new_ckv, new_krope, new_lru = [], [], []
    for l in range(DEPTH):
        lw = _layer_weights(l, params)
        zmix, q, k, v, ckv, kr = _front(x, mod[l], lw, rope)
        new_ckv.append(ckv[:T_P].reshape(BATCH, SEQ, KV_LORA))
        new_krope.append(kr[:T_P].reshape(BATCH, SEQ, QK_ROPE))

        yb_p = _attention(q, k, v, BATCH, SEQ, SEQ, SEQ, 0, group=PROMPT_GROUP)
        yc_p = _fourier(zmix, dft_p, lw, BATCH, SEQ, 0, group=PROMPT_GROUP)
        ya_p, yd_p, st_p = _seq_mixers(zmix, h0_prompt, lw, BATCH, SEQ, 0)
        new_lru.append(st_p)

        kr_ctx = jnp.pad(cache_krope[:, l].reshape(DEC_BATCH * PAST, QK_ROPE),
                         ((0, 0), (0, LANES - QK_ROPE)))
        k_ctx, v_ctx = _kv_expand(cache_ckv[:, l].reshape(DEC_BATCH * PAST, KV_LORA), kr_ctx, lw)
        k_s = jnp.concatenate([k_ctx.reshape(DEC_BATCH, PAST, -1),
                               k[T_P:].reshape(DEC_BATCH, DEC_SEQ, -1)], axis=1)
        v_s = jnp.concatenate([v_ctx.reshape(DEC_BATCH, PAST, -1),
                               v[T_P:].reshape(DEC_BATCH, DEC_SEQ, -1)], axis=1)
        yb_s = _attention(q, k_s.reshape(DEC_BATCH * n_keys, -1), v_s.reshape(DEC_BATCH * n_keys, -1),
                          DEC_BATCH, DEC_SEQ, n_keys, 512, T_P)
        yc_s = _fourier(zmix, dft_s, lw, DEC_BATCH, DEC_SEQ, T_P)
        ya_s, yd_s, _ = _seq_mixers(zmix, state_lru[:, l], lw, DEC_BATCH, DEC_SEQ, T_P)

        x1, h2lin, route, prob, tab, counts = _post(
            x, mod[l], (ya_p, yb_p, yc_p, yd_p), (ya_s, yb_s, yc_s, yd_s), lw, tri, upper)
        dest, sched, loc, top_p, chunk_src, n_chunks = _routing_tables(
            route, prob, counts[0, :N_EXP], tab)
        xs = _gather_rows(dest, sched[1], h2lin)
        ys = _experts(l, sched, xs, params)
        x = _combine(loc, top_p, chunk_src, n_chunks, ys, x1, mod[l], final_g,
                     final=(l == DEPTH - 1))

    y_prompt, y_sample = x
    return (y_prompt.reshape(BATCH, SEQ, D), y_sample.reshape(DEC_BATCH, DEC_SEQ, D),
            jnp.stack(new_ckv, axis=1), jnp.stack(new_krope, axis=1), jnp.stack(new_lru, axis=1))
```

```python
import functools

import numpy as np
import jax
import jax.numpy as jnp
from jax import lax
from jax.experimental import pallas as pl
from jax.experimental.pallas import tpu as pltpu

F32 = jnp.float32
BF16 = jnp.bfloat16
I32 = jnp.int32

D = 1024
BATCH, SEQ = 32, 256
DEC_BATCH, DEC_SEQ, PAST = 2, 2048, 512
T_P = BATCH * SEQ
T_S = DEC_BATCH * DEC_SEQ
T = T_P + T_S
DEPTH = 2
GRID_W = 64
GW = 256
Q_LORA, KV_LORA, QK_NOPE, QK_ROPE, V_DIM, HEADS = 192, 128, 64, 32, 64, 4
HEAD_PAD = 128
ROPE_BASE = 10000.0
Q_SCALE = float((QK_NOPE + QK_ROPE) ** -0.5 * np.log2(np.e))
POOL_WINDOWS = (2, 4, 8, 16)
LRU_C = 8.0
N_EXP, TOP_K = 32, 4
LIMIT, ALPHA = 7.0, 1.702
EPS = 1e-6

LANES = 128
SUBLANES = 8
VMEM_LIMIT = 56 * 1024 * 1024

TM = 512
N_PT = T_P // TM
TILES_PER_DEC = DEC_SEQ // TM
TM_E = 512
EXPERT_ROWS = (TM_E, TM_E // 2, TM_E // 4)
N_SLOTS = T * TOP_K + N_EXP * TM_E
NB_E = N_SLOTS // TM_E
ROW_F = D // LANES
GATHER_VMEM_LIMIT = 60 * 1024 * 1024
CH_C = 16
MAX_CH = TM * TOP_K // CH_C + N_EXP
GATHER_UNROLL = 16
PROMPT_GROUP = 4
SEQ_PAD = 32
FRONT = 8

W_COLS = 1536
COL_Q, COL_KV, COL_KR = 1024, 1280, 1408


_PROMPT_ROWS = pl.BlockSpec((TM, D), lambda i: (jnp.minimum(i, N_PT - 1), 0))
_SAMPLE_ROWS = pl.BlockSpec((TM, D), lambda i: (jnp.maximum(i - N_PT, 0), 0))


def _cparams(sem, vmem=VMEM_LIMIT):
    return pltpu.CompilerParams(dimension_semantics=sem, vmem_limit_bytes=vmem)


def _bdot(a, b):
    return jnp.dot(a.astype(BF16), b.astype(BF16), preferred_element_type=F32)


def _split(a):
    hi = a.astype(BF16)
    lo = (a - hi.astype(F32)).astype(BF16)
    return hi, lo


def _dot3(a, b):
    ah, al = _split(a)
    bh, bl = _split(b)
    d = functools.partial(jnp.dot, preferred_element_type=F32)
    return d(ah, bh) + (d(al, bh) + d(ah, bl))


def _rms(x, n=None):
    n = x.shape[-1] if n is None else n
    return x * lax.rsqrt(jnp.sum(x * x, axis=-1, keepdims=True) * (1.0 / n) + EPS)


def _neg_expm1_double(x):
    t = jnp.tanh(x)
    return -2.0 * t / (1.0 - t)


def _sigmoid(x):
    return 0.5 * jnp.tanh(0.5 * x) + 0.5


def _tile_row(i):
    return jnp.where(i >= N_PT, 1 + (i - N_PT) // TILES_PER_DEC, 0)


def _mod_kernel(c_ref, w_ref, b_ref, o_ref):
    s = jax.nn.silu(c_ref[...])
    o_ref[0] = _dot3(s, w_ref[0]) + b_ref[0]


def _modulation(cvec, w_mod, b_mod):
    tn = 1536
    n = 6 * D
    return pl.pallas_call(
        _mod_kernel,
        out_shape=jax.ShapeDtypeStruct((DEPTH, SUBLANES, n), F32),
        grid=(DEPTH, n // tn),
        in_specs=[pl.BlockSpec((SUBLANES, D), lambda l, j: (0, 0)),
                  pl.BlockSpec((1, D, tn), lambda l, j: (l, 0, j)),
                  pl.BlockSpec((1, 1, tn), lambda l, j: (l, 0, j))],
        out_specs=pl.BlockSpec((1, SUBLANES, tn), lambda l, j: (l, 0, j)),
        compiler_params=_cparams(("parallel", "parallel")),
        name="modulation",
    )(cvec, w_mod, b_mod.reshape(DEPTH, 1, n))


def _front_kernel(xp_ref, xs_ref, mod_ref, g_ref, w_ref, qg_ref, wuq_ref, kvg_ref, wkn_ref, wkr_ref,
                  wv_ref, cq_ref, saq_ref, sbq_ref, ck_ref, sak_ref, sbk_ref,
                  zmix_ref, q_ref, k_ref, v_ref, ckv_ref, kr_ref):
    i = pl.program_id(0)
    row = _tile_row(i)
    shift1 = mod_ref[pl.ds(row, 1), 0:D]
    scale1 = mod_ref[pl.ds(row, 1), D:2 * D]
    h = _rms(jnp.where(i >= N_PT, xs_ref[...], xp_ref[...])) * g_ref[...]
    h = h * (1.0 + scale1) + shift1
    z = jnp.dot(h.astype(BF16), w_ref[...], preferred_element_type=F32)
    zmix_ref[...] = z[:, 0:4 * GW]
    qn = _rms(z[:, COL_Q:COL_Q + 2 * LANES], Q_LORA) * qg_ref[...]
    q = _bdot(qn, wuq_ref[...])
    ckv = _rms(z[:, COL_KV:COL_KV + KV_LORA]) * kvg_ref[...]
    ckv_b = ckv.astype(BF16)
    v_ref[...] = jnp.dot(ckv_b, wv_ref[...], preferred_element_type=F32).astype(BF16)
    kn = jnp.dot(ckv_b, wkn_ref[...], preferred_element_type=F32)
    kr = z[:, COL_KR:COL_KR + LANES]

    @pl.when(i < N_PT)
    def _():
        q_ref[...] = (q * Q_SCALE).astype(BF16)
        ckv_ref[...] = ckv
        kr_ref[...] = kr[:, 0:QK_ROPE]
        k_ref[...] = (kn + _bdot(kr, wkr_ref[...])).astype(BF16)

    @pl.when(i >= N_PT)
    def _():
        wq = HEADS * HEAD_PAD
        half = QK_ROPE // 2
        qr = (q * cq_ref[...] + pltpu.roll(q, wq - half, 1) * saq_ref[...]
              + pltpu.roll(q, half, 1) * sbq_ref[...])
        krr = (kr * ck_ref[...] + pltpu.roll(kr, LANES - half, 1) * sak_ref[...]
               + pltpu.roll(kr, half, 1) * sbk_ref[...])
        q_ref[...] = (qr * Q_SCALE).astype(BF16)
        k_ref[...] = (kn + _bdot(krr, wkr_ref[...])).astype(BF16)


def _front(x, mod_l, lw, rope):
    full = lambda shape: pl.BlockSpec(shape, lambda i: (0,) * len(shape))
    rows = lambda w: pl.BlockSpec((TM, w), lambda i: (i, 0))
    prompt_rows = lambda w: pl.BlockSpec((TM, w), lambda i: (jnp.minimum(i, N_PT - 1), 0))
    rope_rows = lambda w: pl.BlockSpec(
        (TM, w), lambda i: (jnp.maximum(i - N_PT, 0) % TILES_PER_DEC, 0))
    wq = HEADS * HEAD_PAD
    return pl.pallas_call(
        _front_kernel,
        out_shape=(jax.ShapeDtypeStruct((T, 4 * GW), F32),
                   jax.ShapeDtypeStruct((T, wq), BF16),
                   jax.ShapeDtypeStruct((T, wq), BF16),
                   jax.ShapeDtypeStruct((T, HEADS * V_DIM), BF16),
                   jax.ShapeDtypeStruct((T_P, KV_LORA), F32),
                   jax.ShapeDtypeStruct((T_P, QK_ROPE), F32)),
        grid=(T // TM,),
        in_specs=[_PROMPT_ROWS, _SAMPLE_ROWS, full((SUBLANES, 6 * D)), full((1, D)), full((D, W_COLS)),
                  full((1, 2 * LANES)), full((2 * LANES, wq)), full((1, KV_LORA)),
                  full((KV_LORA, wq)), full((LANES, wq)), full((KV_LORA, HEADS * V_DIM)),
                  rope_rows(wq), rope_rows(wq), rope_rows(wq),
                  rope_rows(LANES), rope_rows(LANES), rope_rows(LANES)],
        out_specs=(rows(4 * GW), rows(wq), rows(wq), rows(HEADS * V_DIM), prompt_rows(KV_LORA),
                   prompt_rows(QK_ROPE)),
        compiler_params=_cparams(("arbitrary",)),
        name="front",
    )(*x, mod_l, lw["attn_g"], lw["w_in"], lw["q_g"], lw["w_uq"], lw["kv_g"], lw["w_kn"],
      lw["w_kr"], lw["w_v"], *rope)


def _kvexp_kernel(ckv_ref, kr_ref, wkn_ref, wkr_ref, wv_ref, k_ref, v_ref):
    ckv_b = ckv_ref[...].astype(BF16)
    v_ref[...] = jnp.dot(ckv_b, wv_ref[...], preferred_element_type=F32).astype(BF16)
    kn = jnp.dot(ckv_b, wkn_ref[...], preferred_element_type=F32)
    k_ref[...] = (kn + _bdot(kr_ref[...], wkr_ref[...])).astype(BF16)


def _kv_expand(ckv, kr_pad, lw):
    n = ckv.shape[0]
    wq = HEADS * HEAD_PAD
    full = lambda shape: pl.BlockSpec(shape, lambda i: (0,) * len(shape))
    return pl.pallas_call(
        _kvexp_kernel,
        out_shape=(jax.ShapeDtypeStruct((n, wq), BF16),
                   jax.ShapeDtypeStruct((n, HEADS * V_DIM), BF16)),
        grid=(1,),
        in_specs=[full((n, KV_LORA)), full((n, LANES)), full((KV_LORA, wq)), full((LANES, wq)),
                  full((KV_LORA, HEADS * V_DIM))],
        out_specs=(full((n, wq)), full((n, HEADS * V_DIM))),
        compiler_params=_cparams(("arbitrary",)),
        name="kv_expand",
    )(ckv, kr_pad, lw["w_kn"], lw["w_kr"], lw["w_v"])


def _attn_kernel(q_ref, *refs, group, tq, seg_keys):
    o_ref = refs[-1]
    dims = (((1,), (1,)), ((), ()))
    for g in range(group):
        outs = []
        for h in range(HEADS):
            qh = q_ref[g * tq:(g + 1) * tq, h * HEAD_PAD:(h + 1) * HEAD_PAD]
            scores = []
            for si, nk in enumerate(seg_keys):
                kh = refs[2 * si][g * nk:(g + 1) * nk, h * HEAD_PAD:(h + 1) * HEAD_PAD]
                scores.append(lax.dot_general(qh, kh, dims, preferred_element_type=F32))
            m = functools.reduce(jnp.maximum, [jnp.max(s, axis=-1, keepdims=True) for s in scores])
            l, acc = None, None
            for si, (nk, s) in enumerate(zip(seg_keys, scores)):
                p = jnp.exp2(s - m)
                vh = refs[2 * si + 1][g * nk:(g + 1) * nk, h * V_DIM:(h + 1) * V_DIM]
                part = jnp.dot(p.astype(BF16), vh, preferred_element_type=F32)
                psum = jnp.sum(p, axis=-1, keepdims=True)
                l, acc = (psum, part) if l is None else (l + psum, acc + part)
            outs.append(acc / l)
        o_ref[g * tq:(g + 1) * tq, :] = jnp.concatenate(outs, axis=-1)


def _attention(q, segments, n_batch, seq, tq, row0, group=1):
    nq = seq // tq
    wq = HEADS * HEAD_PAD
    in_specs = [pl.BlockSpec((group * tq, wq), lambda b, i: (row0 // (group * tq) + b * nq + i, 0))]
    operands = [q]
    for k_rows, v_rows, nk, first in segments:
        index = lambda b, i, base=first // (group * nk): (base + b, 0)
        in_specs += [pl.BlockSpec((group * nk, wq), index),
                     pl.BlockSpec((group * nk, HEADS * V_DIM), index)]
        operands += [k_rows, v_rows]
    return pl.pallas_call(
        functools.partial(_attn_kernel, group=group, tq=tq,
                          seg_keys=tuple(seg[2] for seg in segments)),
        out_shape=jax.ShapeDtypeStruct((n_batch * seq, GW), F32),
        grid=(n_batch // group, nq),
        in_specs=in_specs,
        out_specs=pl.BlockSpec((group * tq, GW), lambda b, i: (b * nq + i, 0)),
        compiler_params=_cparams(("parallel", "parallel")),
        name="attention",
    )(*operands)


def _fourier_kernel(z_ref, c_ref, s_ref, bdc_ref, bds_ref, w_ref, o_ref, xc_ref, xs_ref, *,
                    norm, group, seq, ts):
    b = pl.program_id(1)

    @pl.when(pl.program_id(0) == 0)
    def _():
        xb = z_ref[...].astype(BF16)
        xc_ref[b] = _bdot(xb, bdc_ref[...]).astype(BF16)
        xs_ref[b] = _bdot(xb, bds_ref[...]).astype(BF16)

    cb, sb = c_ref[...].astype(BF16), s_ref[...].astype(BF16)
    for g in range(group):
        f = (jnp.dot(cb, xc_ref[b, g * seq:(g + 1) * seq, :], preferred_element_type=F32)
             - jnp.dot(sb, xs_ref[b, g * seq:(g + 1) * seq, :], preferred_element_type=F32)) * norm
        o_ref[g * ts:(g + 1) * ts, :] = _bdot(f, w_ref[...])


def _fourier(zmix, consts, lw, n_batch, seq, row0, group=1):
    ts = min(seq, 512)
    nj = seq // ts
    assert group == 1 or nj == 1
    nb = n_batch // group
    cmat, smat = consts
    full = lambda shape: pl.BlockSpec(shape, lambda j, b: (0,) * len(shape))
    z_spec = pl.BlockSpec((group * seq, GW),
                          lambda j, b: (row0 // (group * seq) + jnp.where(j == 0, b, nb - 1), 1))
    return pl.pallas_call(
        functools.partial(_fourier_kernel, norm=float((seq * (GW // 4)) ** -0.5), group=group,
                          seq=seq, ts=ts),
        out_shape=jax.ShapeDtypeStruct((n_batch * seq, GW), F32),
        grid=(nj, nb),
        in_specs=[z_spec,
                  pl.BlockSpec((ts, seq), lambda j, b: (j, 0)),
                  pl.BlockSpec((ts, seq), lambda j, b: (j, 0)),
                  full((GW, GW)), full((GW, GW)), full((GW, GW))],
        out_specs=pl.BlockSpec((group * ts, GW), lambda j, b: (b * nj + j, 0)),
        scratch_shapes=[pltpu.VMEM((nb, group * seq, GW), BF16),
                        pltpu.VMEM((nb, group * seq, GW), BF16)],
        compiler_params=_cparams(("arbitrary", "arbitrary")),
        name="fourier",
    )(zmix, cmat, smat, lw["bd_c"], lw["bd_s"], lw["w_f"])


def _seq_kernel(za_ref, zr_ref, zg_ref, h0_ref, icnt_ref, wp_ref, ps_ref, cw_ref, cb_ref, wa_ref,
                ba_ref, wx_ref, bx_ref, lam_ref, ya_ref, yd_ref, st_ref,
                pa_ref, pb_ref, xp_ref, a_ref, b_ref, *, seq):
    n = seq + SEQ_PAD
    span = seq + 2 * FRONT
    zeros_pad = jnp.zeros((n, GW), F32)

    za = za_ref[...]
    pa_ref[...] = zeros_pad
    pb_ref[...] = zeros_pad
    pa_ref[FRONT:FRONT + seq, :] = za
    pb_ref[0:span, :] = pa_ref[0:span, :] + pa_ref[1:span + 1, :]
    win2 = pb_ref[FRONT - 1:FRONT - 1 + seq, :]
    pa_ref[0:span, :] = pb_ref[0:span, :] + pb_ref[2:span + 2, :]
    win4 = pa_ref[FRONT - 2:FRONT - 2 + seq, :]
    pb_ref[0:span, :] = pa_ref[0:span, :] + pa_ref[4:span + 4, :]
    win8 = pb_ref[FRONT - 4:FRONT - 4 + seq, :]
    pa_ref[0:span, :] = pb_ref[0:span, :] + pb_ref[8:span + 8, :]
    win16 = pa_ref[FRONT - 8:FRONT - 8 + seq, :]
    grp = lax.broadcasted_iota(I32, (1, GW), 1) // (GW // 4)
    win = jnp.where(grp == 0, win2, jnp.where(grp == 1, win4, jnp.where(grp == 2, win8, win16)))
    dlt = win * icnt_ref[...] - za
    ya_ref[...] = _bdot(dlt, wp_ref[...]) * ps_ref[...]

    xp_ref[...] = zeros_pad
    xp_ref[FRONT:FRONT + seq, :] = zr_ref[...]
    xc = cb_ref[...] + cw_ref[0:1, :] * xp_ref[FRONT - 2:FRONT - 2 + seq, :]
    for kk in range(1, 4):
        xc = xc + cw_ref[kk:kk + 1, :] * xp_ref[FRONT - 2 + kk:FRONT - 2 + kk + seq, :]
    xcb = xc.astype(BF16)
    n_grp = seq // SUBLANES
    sub = lax.broadcasted_iota(I32, (n_grp, SUBLANES, GW), 1)

    total = None
    for d in range(2):
        r = _sigmoid(jnp.dot(xcb, wa_ref[d], preferred_element_type=F32) + ba_ref[d])
        ig = _sigmoid(jnp.dot(xcb, wx_ref[d], preferred_element_type=F32) + bx_ref[d])
        log_a = (-LRU_C) * r * jax.nn.softplus(-lam_ref[d])
        a = jnp.exp(log_a).reshape(n_grp, SUBLANES, GW)
        b = (jnp.sqrt(_neg_expm1_double(log_a)) * (ig * xc)).reshape(n_grp, SUBLANES, GW)
        for k in (1, 2, 4):
            shift = k if d == 0 else SUBLANES - k
            m = sub >= k if d == 0 else sub < SUBLANES - k
            ap, bp = pltpu.roll(a, shift, 1), pltpu.roll(b, shift, 1)
            b = jnp.where(m, a * bp + b, b)
            a = jnp.where(m, a * ap, a)
        a_ref[...] = a.reshape(seq, GW)
        b_ref[...] = b.reshape(seq, GW)

        def step(g, carry, d=d):
            gi = g if d == 0 else n_grp - 1 - g
            off = pl.multiple_of(gi * SUBLANES, SUBLANES)
            hh = a_ref[pl.ds(off, SUBLANES), :] * carry + b_ref[pl.ds(off, SUBLANES), :]
            b_ref[pl.ds(off, SUBLANES), :] = hh
            edge = hh[SUBLANES - 1:SUBLANES, :] if d == 0 else hh[0:1, :]
            return jnp.broadcast_to(edge, (SUBLANES, GW))

        last = lax.fori_loop(0, n_grp, step, jnp.broadcast_to(h0_ref[0, d:d + 1, :], (SUBLANES, GW)))
        st_ref[0, d:d + 1, :] = last[0:1, :]
        total = b_ref[...] if total is None else total + b_ref[...]

    yd_ref[...] = total * jax.nn.gelu(zg_ref[...])


def _pool_inverse_counts(seq):
    pos = np.arange(seq)[:, None]
    half = np.repeat(np.array(POOL_WINDOWS) // 2, GW // len(POOL_WINDOWS))[None, :]
    cnt = np.minimum(pos + half, seq) - np.maximum(pos - half, 0)
    return jnp.asarray(1.0 / cnt, F32)


def _seq_mixers(zmix, h0, lw, n_batch, seq, row0):
    full = lambda shape: pl.BlockSpec(shape, lambda b: (0,) * len(shape))
    col = lambda c: pl.BlockSpec((seq, GW), lambda b: (row0 // seq + b, c))
    out_rows = pl.BlockSpec((seq, GW), lambda b: (b, 0))
    pad = pltpu.VMEM((seq + SEQ_PAD, GW), F32)
    return pl.pallas_call(
        functools.partial(_seq_kernel, seq=seq),
        out_shape=(jax.ShapeDtypeStruct((n_batch * seq, GW), F32),
                   jax.ShapeDtypeStruct((n_batch * seq, GW), F32),
                   jax.ShapeDtypeStruct((n_batch, 2, GW), F32)),
        grid=(n_batch,),
        in_specs=[col(0), col(2), col(3), pl.BlockSpec((1, 2, GW), lambda b: (b, 0, 0)),
                  full((seq, GW)), full((GW, GW)), full((1, GW)), full((4, GW)), full((1, GW)),
                  full((2, GW, GW)), full((2, 1, GW)), full((2, GW, GW)), full((2, 1, GW)),
                  full((2, 1, GW))],
        out_specs=(out_rows, out_rows, pl.BlockSpec((1, 2, GW), lambda b: (b, 0, 0))),
        scratch_shapes=[pad, pad, pad, pltpu.VMEM((seq, GW), F32), pltpu.VMEM((seq, GW), F32)],
        compiler_params=_cparams(("parallel",)),
        name="seq_mixers",
    )(zmix, zmix, zmix, h0, _pool_inverse_counts(seq), lw["bd_pool"], lw["pool_scale"], lw["conv_w"],
      lw["conv_b"], lw["bd_wa"], lw["lru_ba"], lw["bd_wx"], lw["lru_bx"], lw["lru_lam"])


def _post_kernel(xp_ref, xs_ref, mod_ref, yap_ref, ybp_ref, ycp_ref, ydp_ref, yas_ref, ybs_ref, ycs_ref,
                 yds_ref, og_ref, wo_ref, fg_ref, rw_ref, rb_ref, tri_ref, upper_ref,
                 x1_ref, h2p_ref, route_ref, prob_ref, tab_ref, cnt_ref, carry_ref):
    i = pl.program_id(0)
    is_s = i >= N_PT
    row = _tile_row(i)
    gate1 = mod_ref[pl.ds(row, 1), 2 * D:3 * D]
    shift2 = mod_ref[pl.ds(row, 1), 3 * D:4 * D]
    scale2 = mod_ref[pl.ds(row, 1), 4 * D:5 * D]

    @pl.when(i == 0)
    def _():
        carry_ref[...] = jnp.zeros_like(carry_ref)

    groups = []
    for gi, (p_ref, s_ref) in enumerate(((yap_ref, yas_ref), (ybp_ref, ybs_ref),
                                         (ycp_ref, ycs_ref), (ydp_ref, yds_ref))):
        y = jnp.where(is_s, s_ref[...], p_ref[...])
        groups.append((_rms(y) * og_ref[gi:gi + 1, :]).astype(BF16))
    ycat = jnp.concatenate(groups, axis=-1)
    x = jnp.where(is_s, xs_ref[...], xp_ref[...])
    x1 = x + gate1 * jnp.dot(ycat, wo_ref[...], preferred_element_type=F32)
    x1_ref[...] = x1
    h2 = _rms(x1) * fg_ref[...]
    h2 = h2 * (1.0 + scale2) + shift2

    for j in range(ROW_F):
        h2p_ref[pl.ds(j, TM, stride=ROW_F), :] = h2[:, j * LANES:(j + 1) * LANES]

    h_hi, h_lo = _split(h2)
    both = jnp.dot(h_hi, rw_ref[...], preferred_element_type=F32)
    cross = both[:, LANES:2 * LANES] + jnp.dot(h_lo, rw_ref[:, 0:LANES], preferred_element_type=F32)
    logits = both[:, 0:LANES] + cross + rb_ref[...]
    lane = lax.broadcasted_iota(I32, (TM, LANES), 1)
    lane_f = lane.astype(F32)
    neg = jnp.float32(-jnp.inf)
    cur = jnp.where(lane < N_EXP, logits, neg)
    sel, vals, idxs = [], [], []
    for _ in range(TOP_K):
        m = jnp.max(cur, axis=-1, keepdims=True)
        idx = jnp.min(jnp.where(cur == m, lane_f, float(LANES)), axis=-1, keepdims=True)
        hit = lane_f == idx
        sel.append(hit)
        vals.append(m)
        idxs.append(idx)
        cur = jnp.where(hit, neg, cur)
    exps = [jnp.exp(v - vals[0]) for v in vals]
    denom = exps[0] + exps[1] + exps[2] + exps[3]
    onehot = jnp.where(sel[0] | sel[1] | sel[2] | sel[3], 1.0, 0.0)
    cum_l = jnp.dot(tri_ref[...], onehot.astype(BF16), preferred_element_type=F32)
    before = carry_ref[0:1, :]
    cum = cum_l + before
    n_tile = cum_l[TM - 1:TM, :]
    nch = jnp.floor((n_tile + (CH_C - 1.0)) * (1.0 / CH_C))
    nch8 = jnp.broadcast_to(nch, (SUBLANES, LANES))
    base = jnp.dot(nch8.astype(BF16), upper_ref[...], preferred_element_type=F32)[0:1, :]
    pk = jnp.zeros((TM, LANES), I32)
    pf = jnp.zeros((TM, LANES), F32)
    for k in range(TOP_K):
        rank = jnp.sum(jnp.where(sel[k], cum - 1.0, 0.0), axis=-1, keepdims=True).astype(I32)
        local = jnp.sum(jnp.where(sel[k], (base * CH_C + cum_l - 1.0) * ROW_F, 0.0), axis=-1,
                        keepdims=True).astype(I32)
        pk = jnp.where(lane == k, idxs[k].astype(I32), pk)
        pk = jnp.where(lane == TOP_K + k, rank, pk)
        pk = jnp.where(lane == 2 * TOP_K + k, local, pk)
        pf = jnp.where(lane == k, exps[k] / denom, pf)
    route_ref[...] = pk.T[0:4 * TOP_K, :]
    prob_ref[...] = pf.T[0:SUBLANES, :]
    row8 = lax.broadcasted_iota(I32, (SUBLANES, LANES), 0)
    tab_ref[...] = jnp.where(row8 == 0, before, jnp.where(row8 == 1, nch, base))
    new_carry = jnp.broadcast_to(cum[TM - 1:TM, :], (SUBLANES, LANES))
    carry_ref[...] = new_carry
    cnt_ref[...] = new_carry


def _post(x, mod_l, ys_prompt, ys_sample, lw, tri, upper):
    full = lambda shape: pl.BlockSpec(shape, lambda i: (0,) * len(shape))
    rows = lambda w: pl.BlockSpec((TM, w), lambda i: (i, 0))
    prow = pl.BlockSpec((TM, GW), lambda i: (jnp.minimum(i, N_PT - 1), 0))
    srow = pl.BlockSpec((TM, GW), lambda i: (jnp.maximum(i - N_PT, 0), 0))
    return pl.pallas_call(
        _post_kernel,
        out_shape=(jax.ShapeDtypeStruct((T, D), F32),
                   jax.ShapeDtypeStruct((T * ROW_F, LANES), F32),
                   jax.ShapeDtypeStruct((4 * TOP_K, T), I32),
                   jax.ShapeDtypeStruct((SUBLANES, T), F32),
                   jax.ShapeDtypeStruct((T // TM * SUBLANES, LANES), F32),
                   jax.ShapeDtypeStruct((SUBLANES, LANES), F32)),
        grid=(T // TM,),
        in_specs=[_PROMPT_ROWS, _SAMPLE_ROWS, full((SUBLANES, 6 * D)),
                  prow, prow, prow, prow, srow, srow, srow, srow,
                  full((4, GW)), full((D, D)), full((1, D)), full((D, 2 * LANES)), full((1, LANES)),
                  full((TM, TM)), full((LANES, LANES))],
        out_specs=(rows(D), pl.BlockSpec((TM * ROW_F, LANES), lambda i: (i, 0)),
                   pl.BlockSpec((4 * TOP_K, TM), lambda i: (0, i)),
                   pl.BlockSpec((SUBLANES, TM), lambda i: (0, i)),
                   pl.BlockSpec((SUBLANES, LANES), lambda i: (i, 0)), full((SUBLANES, LANES))),
        scratch_shapes=[pltpu.VMEM((SUBLANES, LANES), F32)],
        compiler_params=_cparams(("arbitrary",)),
        name="post",
    )(*x, mod_l, *ys_prompt, *ys_sample, lw["out_g"], lw["w_out"], lw["ffn_g"], lw["router_w"],
      lw["router_b"], tri, upper)


def _gather_kernel(dest_ref, nused_ref, zeros_hbm, src_hbm, o_ref, stok_ref, src_ref, sem, src_sem):
    i = pl.program_id(0)

    @pl.when(i == 0)
    def _():
        load = pltpu.make_async_copy(src_hbm, src_ref, src_sem)
        load.start()
        init = pltpu.make_async_copy(zeros_hbm, stok_ref, sem)
        init.start()
        init.wait()

        def scatter(t8, c):
            row0 = t8 * (SUBLANES * ROW_F)
            for k in range(TOP_K):
                for u in range(SUBLANES):
                    stok_ref[dest_ref[k * T + t8 * SUBLANES + u]] = row0 + u * ROW_F
            return c
        lax.fori_loop(0, T // SUBLANES, scatter, 0)
        load.wait()

    @pl.when(i < nused_ref[0])
    def _():
        def rows(r16, c):
            r0 = pl.multiple_of(r16 * GATHER_UNROLL, GATHER_UNROLL)
            for u in range(GATHER_UNROLL):
                src = pl.multiple_of(stok_ref[i * TM_E + r0 + u], ROW_F)
                dst = pl.multiple_of(r0 * ROW_F, GATHER_UNROLL * ROW_F) + u * ROW_F
                o_ref[pl.ds(dst, ROW_F), :] = src_ref[pl.ds(src, ROW_F), :]
            return c
        lax.fori_loop(0, TM_E // GATHER_UNROLL, rows, 0)

    @pl.when(i >= nused_ref[0])
    def _():
        o_ref[...] = jnp.zeros_like(o_ref)


def _gather_rows(dest, n_used, h2lin):
    return pl.pallas_call(
        _gather_kernel,
        out_shape=jax.ShapeDtypeStruct((N_SLOTS * ROW_F, LANES), F32),
        grid_spec=pltpu.PrefetchScalarGridSpec(
            num_scalar_prefetch=2,
            grid=(NB_E,),
            in_specs=[pl.BlockSpec(memory_space=pl.ANY), pl.BlockSpec(memory_space=pl.ANY)],
            out_specs=pl.BlockSpec((TM_E * ROW_F, LANES), lambda i, d, nu: (i, 0)),
            scratch_shapes=[pltpu.SMEM((N_SLOTS,), I32), pltpu.VMEM((T * ROW_F, LANES), F32),
                            pltpu.SemaphoreType.DMA(()), pltpu.SemaphoreType.DMA(())]),
        compiler_params=_cparams(("arbitrary",), GATHER_VMEM_LIMIT),
        name="moe_gather",
    )(dest, n_used, jnp.zeros((N_SLOTS,), I32), h2lin)


def _expert_rows(n, x_ref, w_ref, slot, bg_ref, bu_ref, bd_ref, o_ref):
    x = jnp.concatenate([x_ref[pl.ds(j, n, stride=ROW_F), :].astype(BF16) for j in range(ROW_F)],
                        axis=-1)
    g = jnp.dot(x, w_ref[slot, 0].astype(BF16), preferred_element_type=F32) + bg_ref[0, 0]
    u = jnp.dot(x, w_ref[slot, 1].astype(BF16), preferred_element_type=F32) + bu_ref[0, 0]
    g = jnp.minimum(g, LIMIT)
    u = jnp.clip(u, -LIMIT, LIMIT)
    act = (u + 1.0) * (g * jax.nn.sigmoid(ALPHA * g))
    y = jnp.dot(act.astype(BF16), w_ref[slot, 2].astype(BF16),
                preferred_element_type=F32) + bd_ref[0, 0]
    for j in range(ROW_F):
        o_ref[pl.ds(j, n, stride=ROW_F), :] = y[:, j * LANES:(j + 1) * LANES]
    if n < TM_E:
        o_ref[n * ROW_F:TM_E * ROW_F, :] = jnp.zeros(((TM_E - n) * ROW_F, LANES), F32)


def _expert_kernel(be_ref, nused_ref, valid_ref, first_ref, next_ref, par_ref,
                   x_ref, wg_hbm, bg_ref, wu_hbm, bu_ref, wd_hbm, bd_ref, o_ref, w_ref, sems, *, layer):
    i = pl.program_id(0)
    ib = jnp.minimum(i, NB_E - 1)
    live = i < nused_ref[0]
    valid = valid_ref[ib]
    expert = be_ref[ib]
    slot = par_ref[ib]

    def fetch(e, s):
        return [pltpu.make_async_copy(src.at[layer, e], w_ref.at[s, m], sems.at[s, m])
                for m, src in enumerate((wg_hbm, wu_hbm, wd_hbm))]

    @pl.when(i == 0)
    def _():
        for cp in fetch(expert, slot):
            cp.start()

    @pl.when(live & (first_ref[ib] == 1))
    def _():
        @pl.when(next_ref[ib] >= 0)
        def _():
            for cp in fetch(next_ref[ib], 1 - slot):
                cp.start()

        for cp in fetch(expert, slot):
            cp.wait()

    for n_idx, n in enumerate(EXPERT_ROWS):
        fits = valid <= n
        if n_idx + 1 < len(EXPERT_ROWS):
            fits = fits & (valid > EXPERT_ROWS[n_idx + 1])

        @pl.when(live & fits)
        def _(n=n):
            _expert_rows(n, x_ref, w_ref, slot, bg_ref, bu_ref, bd_ref, o_ref)

    @pl.when(jnp.logical_not(live))
    def _():
        o_ref[...] = jnp.zeros_like(o_ref)


def _experts(l, sched, xs, p):
    last = NB_E - 1
    bspec = pl.BlockSpec((1, 1, 1, D), lambda i, be, *_: (l, be[jnp.minimum(i, last)], 0, 0))
    hbm = pl.BlockSpec(memory_space=pl.ANY)
    bias = lambda b: b.reshape(DEPTH, N_EXP, 1, D)
    return pl.pallas_call(
        functools.partial(_expert_kernel, layer=l),
        out_shape=jax.ShapeDtypeStruct(((NB_E + 1) * TM_E * ROW_F, LANES), F32),
        grid_spec=pltpu.PrefetchScalarGridSpec(
            num_scalar_prefetch=len(sched),
            grid=(NB_E + 1,),
            in_specs=[pl.BlockSpec((TM_E * ROW_F, LANES), lambda i, *_: (jnp.minimum(i, last), 0)),
                      hbm, bspec, hbm, bspec, hbm, bspec],
            out_specs=pl.BlockSpec((TM_E * ROW_F, LANES), lambda i, *_: (i, 0)),
            scratch_shapes=[pltpu.VMEM((2, 3, D, D), F32), pltpu.SemaphoreType.DMA((2, 3))]),
        compiler_params=_cparams(("arbitrary",)),
        name="moe_experts",
    )(*sched, xs, p["w_gate"], bias(p["b_gate"]), p["w_up"], bias(p["b_up"]),
      p["w_down"], bias(p["b_down"]))


def _combine_kernel(loc_ref, p_ref, csrc_ref, nch_ref, ys_hbm, x_ref, mod_ref, fg_ref, *rest, final):
    *outs, buf_ref, acc_ref, sems = rest
    tb = pl.program_id(0)
    n_tb = pl.num_programs(0)
    slot = tb % 2
    row = _tile_row(tb)
    gate2 = mod_ref[pl.ds(row, 1), 5 * D:6 * D]
    chunk_rows = CH_C * ROW_F

    def chunk_copy(t, c, s):
        src = pl.multiple_of(csrc_ref[t * MAX_CH + c] * ROW_F, ROW_F)
        dst = pl.multiple_of(c * chunk_rows, chunk_rows)
        return pltpu.make_async_copy(ys_hbm.at[pl.ds(src, chunk_rows)],
                                     buf_ref.at[s, pl.ds(dst, chunk_rows)], sems.at[s])

    def issue(t, s):
        def body(c, carry):
            chunk_copy(t, c, s).start()
            return carry
        lax.fori_loop(0, nch_ref[t], body, 0)

    @pl.when(tb == 0)
    def _():
        issue(0, 0)

    @pl.when(tb + 1 < n_tb)
    def _():
        issue(tb + 1, 1 - slot)

    def drain(c, carry):
        chunk_copy(tb, c, slot).wait()
        return carry
    lax.fori_loop(0, nch_ref[tb], drain, 0)

    def tokens(r8, carry):
        r0 = pl.multiple_of(r8 * SUBLANES, SUBLANES)
        for u in range(SUBLANES):
            tok = tb * TM + r0 + u
            acc = None
            for k in range(TOP_K):
                off = pl.multiple_of(loc_ref[k * T + tok], ROW_F)
                term = buf_ref[slot, pl.ds(off, ROW_F), :] * p_ref[k * T + tok]
                acc = term if acc is None else acc + term
            dst = pl.multiple_of(r0 * ROW_F, SUBLANES * ROW_F) + u * ROW_F
            acc_ref[pl.ds(dst, ROW_F), :] = acc
        return carry
    lax.fori_loop(0, TM // SUBLANES, tokens, 0)

    moe = jnp.concatenate([acc_ref[pl.ds(j, TM, stride=ROW_F), :] for j in range(ROW_F)], axis=-1)
    x2 = x_ref[...] + gate2 * moe
    if final:
        x2 = _rms(x2) * fg_ref[...]
    prompt_ref, sample_ref = outs

    @pl.when(tb < N_PT)
    def _():
        prompt_ref[...] = x2

    @pl.when(tb >= N_PT)
    def _():
        sample_ref[...] = x2


def _combine(loc, top_p, chunk_src, n_chunks, ys, x1, mod_l, final_g, final):
    full = lambda shape: pl.BlockSpec(shape, lambda i, *_: (0,) * len(shape))
    rows = lambda w: pl.BlockSpec((TM, w), lambda i, *_: (i, 0))
    out_shape = (jax.ShapeDtypeStruct((T_P, D), F32), jax.ShapeDtypeStruct((T_S, D), F32))
    out_specs = (pl.BlockSpec((TM, D), lambda i, *_: (jnp.minimum(i, N_PT - 1), 0)),
                 pl.BlockSpec((TM, D), lambda i, *_: (jnp.maximum(i - N_PT, 0), 0)))
    return pl.pallas_call(
        functools.partial(_combine_kernel, final=final),
        out_shape=out_shape,
        grid_spec=pltpu.PrefetchScalarGridSpec(
            num_scalar_prefetch=4,
            grid=(T // TM,),
            in_specs=[pl.BlockSpec(memory_space=pl.ANY), rows(D), full((SUBLANES, 6 * D)),
                      full((1, D))],
            out_specs=out_specs,
            scratch_shapes=[pltpu.VMEM((2, MAX_CH * CH_C * ROW_F, LANES), F32),
                            pltpu.VMEM((TM * ROW_F, LANES), F32),
                            pltpu.SemaphoreType.DMA((2,))]),
        compiler_params=_cparams(("arbitrary",)),
        name="moe_combine",
    )(loc, top_p, chunk_src, n_chunks, ys, x1, mod_l, final_g)


def _rope_tables():
    rows = DEC_SEQ // GRID_W
    r = np.repeat(np.arange(rows, dtype=np.float64), GRID_W)
    c = np.tile(np.arange(GRID_W, dtype=np.float64), rows)
    n_freq = QK_ROPE // 4
    inv = (np.float32(ROPE_BASE) ** (-np.arange(n_freq, dtype=np.float32) / n_freq)).astype(np.float64)
    ang = np.concatenate([r[:, None] * inv, c[:, None] * inv], axis=-1).astype(np.float32)
    cos, sin = np.cos(ang.astype(np.float64)), np.sin(ang.astype(np.float64))
    half = QK_ROPE // 2

    def place(width, start):
        cf = np.ones((DEC_SEQ, width), np.float32)
        sa = np.zeros((DEC_SEQ, width), np.float32)
        sb = np.zeros((DEC_SEQ, width), np.float32)
        for s0 in start:
            cf[:, s0:s0 + half] = cos
            cf[:, s0 + half:s0 + 2 * half] = cos
            sa[:, s0:s0 + half] = -sin
            sb[:, s0 + half:s0 + 2 * half] = sin
        return jnp.asarray(cf), jnp.asarray(sa), jnp.asarray(sb)

    return (*place(HEADS * HEAD_PAD, [h * HEAD_PAD + QK_NOPE for h in range(HEADS)]),
            *place(LANES, [0]))


def _dft_tables(seq):
    kn = (np.arange(seq, dtype=np.int64)[:, None] * np.arange(seq, dtype=np.int64)[None, :]) % seq
    ang = 2.0 * np.pi * kn.astype(np.float64) / seq
    return jnp.asarray(np.cos(ang), F32), jnp.asarray(np.sin(ang), F32)


def _block_diag(blocks):
    g, n, _ = blocks.shape
    eye = jnp.eye(g, dtype=blocks.dtype)
    return jnp.einsum("gij,gh->gihj", blocks, eye).reshape(g * n, g * n)


def _layer_weights(l, p):
    w_in = p["w_in"][l]
    za, zq, zkv, zkr, zf, zr, zg = jnp.split(
        w_in, np.cumsum([GW, Q_LORA, KV_LORA, QK_ROPE, GW, GW])[:], axis=1)
    zpad = lambda n: jnp.zeros((D, n), F32)
    w_cols = jnp.concatenate([za, zf, zr, zg, zq, zpad(2 * LANES - Q_LORA), zkv, zkr,
                              zpad(LANES - QK_ROPE)], axis=1).astype(BF16)
    wq = HEADS * HEAD_PAD
    w_uq = p["w_uq"][l].reshape(Q_LORA, HEADS, QK_NOPE + QK_ROPE)
    w_uq = jnp.pad(w_uq, ((0, 2 * LANES - Q_LORA), (0, 0), (0, HEAD_PAD - QK_NOPE - QK_ROPE)))
    w_ukv = p["w_ukv"][l].reshape(KV_LORA, HEADS, QK_NOPE + V_DIM)
    w_kn = jnp.pad(w_ukv[:, :, :QK_NOPE], ((0, 0), (0, 0), (0, HEAD_PAD - QK_NOPE)))
    w_v = w_ukv[:, :, QK_NOPE:]
    place = np.zeros((LANES, HEADS, HEAD_PAD), np.float32)
    for h in range(HEADS):
        place[np.arange(QK_ROPE), h, QK_NOPE + np.arange(QK_ROPE)] = 1.0
    c64 = np.arange(GW // 4, dtype=np.int64)
    ang = 2.0 * np.pi * ((c64[:, None] * c64[None, :]) % (GW // 4)).astype(np.float64) / (GW // 4)
    four = lambda m: jnp.asarray(np.broadcast_to(m, (4,) + m.shape), F32)
    router_hi, router_lo = _split(jnp.pad(p["router_w"][l], ((0, 0), (0, LANES - N_EXP))))
    router_w = jnp.concatenate([router_hi, router_lo], axis=1)
    router_b = jnp.pad(p["router_b"][l], (0, LANES - N_EXP)).reshape(1, LANES)
    return {
        "attn_g": p["attn_norm_g"][l].reshape(1, D),
        "w_in": w_cols,
        "q_g": jnp.pad(p["q_norm_g"][l], (0, 2 * LANES - Q_LORA)).reshape(1, 2 * LANES),
        "w_uq": w_uq.reshape(2 * LANES, wq).astype(BF16),
        "kv_g": p["kv_norm_g"][l].reshape(1, KV_LORA),
        "w_kn": w_kn.reshape(KV_LORA, wq).astype(BF16),
        "w_kr": jnp.asarray(place.reshape(LANES, wq), BF16),
        "w_v": w_v.reshape(KV_LORA, HEADS * V_DIM).astype(BF16),
        "bd_c": _block_diag(four(np.cos(ang))),
        "bd_s": _block_diag(four(np.sin(ang))),
        "w_f": p["fourier_w"][l].astype(BF16),
        "bd_pool": _block_diag(p["pool_w"][l]).astype(BF16),
        "pool_scale": p["pool_scale"][l].reshape(1, GW),
        "conv_w": p["conv_w"][l],
        "conv_b": p["conv_b"][l].reshape(1, GW),
        "bd_wa": jnp.stack([_block_diag(p["lru_wa"][l, d]) for d in range(2)]).astype(BF16),
        "bd_wx": jnp.stack([_block_diag(p["lru_wx"][l, d]) for d in range(2)]).astype(BF16),
        "lru_ba": p["lru_ba"][l].reshape(2, 1, GW),
        "lru_bx": p["lru_bx"][l].reshape(2, 1, GW),
        "lru_lam": p["lru_lambda"][l].reshape(2, 1, GW),
        "out_g": p["out_norm_g"][l],
        "w_out": p["w_out"][l].astype(BF16),
        "ffn_g": p["ffn_norm_g"][l].reshape(1, D),
        "router_w": router_w,
        "router_b": router_b,
    }


def _routing_tables(route, prob, counts, tab):
    top_e, rank = route[0:TOP_K], route[TOP_K:2 * TOP_K]
    top_p = prob[0:TOP_K].reshape(-1)
    loc = route[2 * TOP_K:3 * TOP_K].reshape(-1)
    counts = counts.astype(I32)
    padded = (counts + TM_E - 1) // TM_E * TM_E
    pad_ends = jnp.cumsum(padded)
    pad_starts = pad_ends - padded
    experts = jnp.arange(N_EXP, dtype=I32)
    onehot = top_e[:, :, None] == experts
    dest = (jnp.sum(jnp.where(onehot, pad_starts, 0), axis=-1) + rank).reshape(T * TOP_K)
    n_used = (pad_ends[-1] // TM_E).astype(I32)
    blk = jnp.minimum(jnp.arange(NB_E, dtype=I32), n_used - 1) * TM_E
    block_e = jnp.minimum(jnp.sum(pad_ends[None, :] <= blk[:, None], axis=-1), N_EXP - 1).astype(I32)
    of_block = lambda a: jnp.sum(jnp.where(block_e[:, None] == experts, a, 0), axis=-1)
    valid = jnp.clip(of_block(pad_starts + counts) - blk, 0, TM_E).astype(I32)
    first = (blk == of_block(pad_starts)).astype(I32)
    later = (experts[None, :] > experts[:, None]) & (counts[None, :] > 0)
    next_of = jnp.min(jnp.where(later, experts[None, :], N_EXP), axis=-1)
    next_e = of_block(jnp.where(next_of < N_EXP, next_of, -1)).astype(I32)
    parity = (of_block(jnp.cumsum((counts > 0).astype(I32))) % 2).astype(I32)
    sched = (block_e, n_used.reshape(1), valid, first, next_e, parity)
    tab = tab.reshape(T // TM, SUBLANES, LANES)[:, :, :N_EXP].astype(I32)
    before, nch, base = tab[:, 0], tab[:, 1], tab[:, 2]
    run_start = pad_starts[None, :] + before
    ends = base + nch
    ci = jnp.arange(MAX_CH, dtype=I32)
    e_of = jnp.minimum(jnp.sum(ends[:, None, :] <= ci[None, :, None], axis=-1), N_EXP - 1)
    pick = lambda a: jnp.sum(jnp.where(e_of[:, :, None] == experts, a[:, None, :], 0), axis=-1)
    chunk_src = jnp.clip(pick(run_start) + (ci[None, :] - pick(base)) * CH_C, 0, N_SLOTS)
    return dest, sched, loc, top_p, chunk_src.reshape(-1), ends[:, N_EXP - 1]


def kernel(x_prompt, x_sample, cache_ckv, cache_krope, state_lru, c, c_ctx, w_mod, b_mod, attn_norm_g, w_in, pool_w, pool_scale, q_norm_g, w_uq, kv_norm_g, w_ukv, fourier_w, conv_w, conv_b, lru_wa, lru_ba, lru_wx, lru_bx, lru_lambda, out_norm_g, w_out, ffn_norm_g, router_w, router_b, w_gate, b_gate, w_up, b_up, w_down, b_down, final_norm_g):
    params = dict(attn_norm_g=attn_norm_g, w_in=w_in, pool_w=pool_w, pool_scale=pool_scale,
                  q_norm_g=q_norm_g, w_uq=w_uq, kv_norm_g=kv_norm_g, w_ukv=w_ukv,
                  fourier_w=fourier_w, conv_w=conv_w, conv_b=conv_b, lru_wa=lru_wa, lru_ba=lru_ba,
                  lru_wx=lru_wx, lru_bx=lru_bx, lru_lambda=lru_lambda, out_norm_g=out_norm_g,
                  w_out=w_out, ffn_norm_g=ffn_norm_g, router_w=router_w, router_b=router_b,
                  w_gate=w_gate, b_gate=b_gate, w_up=w_up, b_up=b_up, w_down=w_down, b_down=b_down)
    x = (x_prompt.reshape(T_P, D), x_sample.reshape(T_S, D))
    cvec = jnp.concatenate([c_ctx[None, :], c, jnp.zeros((SUBLANES - 1 - DEC_BATCH, D), F32)], axis=0)
    mod = _modulation(cvec, w_mod, b_mod)
    rope = _rope_tables()
    dft_p, dft_s = _dft_tables(SEQ), _dft_tables(DEC_SEQ)
    tri = jnp.asarray(np.tril(np.ones((TM, TM), np.float32)), BF16)
    upper = jnp.asarray(np.triu(np.ones((LANES, LANES), np.float32), 1), BF16)
    final_g = final_norm_g.reshape(1, D)
    h0_prompt = jnp.zeros((BATCH, 2, GW), F32)

    new_ckv, new_krope, new_lru = [], [], []
    for l in range(DEPTH):
        lw = _layer_weights(l, params)
        zmix, q, k, v, ckv, kr = _front(x, mod[l], lw, rope)
        new_ckv.append(ckv.reshape(BATCH, SEQ, KV_LORA))
        new_krope.append(kr.reshape(BATCH, SEQ, QK_ROPE))

        yb_p = _attention(q, [(k, v, SEQ, 0)], BATCH, SEQ, SEQ, 0, group=PROMPT_GROUP)
        yc_p = _fourier(zmix, dft_p, lw, BATCH, SEQ, 0, group=PROMPT_GROUP)
        ya_p, yd_p, st_p = _seq_mixers(zmix, h0_prompt, lw, BATCH, SEQ, 0)
        new_lru.append(st_p)

        kr_ctx = jnp.pad(cache_krope[:, l].reshape(DEC_BATCH * PAST, QK_ROPE),
                         ((0, 0), (0, LANES - QK_ROPE)))
        k_ctx, v_ctx = _kv_expand(cache_ckv[:, l].reshape(DEC_BATCH * PAST, KV_LORA), kr_ctx, lw)
        yb_s = _attention(q, [(k_ctx, v_ctx, PAST, 0), (k, v, DEC_SEQ, T_P)],
                          DEC_BATCH, DEC_SEQ, 512, T_P)
        yc_s = _fourier(zmix, dft_s, lw, DEC_BATCH, DEC_SEQ, T_P)
        ya_s, yd_s, _ = _seq_mixers(zmix, state_lru[:, l], lw, DEC_BATCH, DEC_SEQ, T_P)

        x1, h2lin, route, prob, tab, counts = _post(
            x, mod[l], (ya_p, yb_p, yc_p, yd_p), (ya_s, yb_s, yc_s, yd_s), lw, tri, upper)
        dest, sched, loc, top_p, chunk_src, n_chunks = _routing_tables(
            route, prob, counts[0, :N_EXP], tab)
        xs = _gather_rows(dest, sched[1], h2lin)
        ys = _experts(l, sched, xs, params)
        x = _combine(loc, top_p, chunk_src, n_chunks, ys, x1, mod[l], final_g,
                     final=(l == DEPTH - 1))

    y_prompt, y_sample = x
    return (y_prompt.reshape(BATCH, SEQ, D), y_sample.reshape(DEC_BATCH, DEC_SEQ, D),
            jnp.stack(new_ckv, axis=1), jnp.stack(new_krope, axis=1), jnp.stack(new_lru, axis=1))
```

```python
import functools

import numpy as np
import jax
import jax.numpy as jnp
from jax import lax
from jax.experimental import pallas as pl
from jax.experimental.pallas import tpu as pltpu

F32 = jnp.float32
BF16 = jnp.bfloat16
I32 = jnp.int32

D = 1024
BATCH, SEQ = 32, 256
DEC_BATCH, DEC_SEQ, PAST = 2, 2048, 512
T_P = BATCH * SEQ
T_S = DEC_BATCH * DEC_SEQ
T = T_P + T_S
DEPTH = 2
GRID_W = 64
GW = 256
Q_LORA, KV_LORA, QK_NOPE, QK_ROPE, V_DIM, HEADS = 192, 128, 64, 32, 64, 4
HEAD_PAD = 128
ROPE_BASE = 10000.0
Q_SCALE = float((QK_NOPE + QK_ROPE) ** -0.5 * np.log2(np.e))
POOL_WINDOWS = (2, 4, 8, 16)
LRU_C = 8.0
N_EXP, TOP_K = 32, 4
LIMIT, ALPHA = 7.0, 1.702
EPS = 1e-6

LANES = 128
SUBLANES = 8
MIB = 1024 * 1024
V7X_VMEM_BYTES = 64 * MIB
VMEM_LIMIT = V7X_VMEM_BYTES * 7 // 8

TM = 512
N_PT = T_P // TM
TILES_PER_DEC = DEC_SEQ // TM
TM_E = 512
EXPERT_ROWS = (TM_E, TM_E // 2, TM_E // 4)
N_SLOTS = T * TOP_K + N_EXP * TM_E
NB_E = N_SLOTS // TM_E
ROW_F = D // LANES
GATHER_VMEM_LIMIT = 4 * (T * D + 2 * TM_E * D) + 8 * MIB
CH_C = 32
MAX_CH = TM * TOP_K // CH_C + N_EXP
GATHER_UNROLL = 16
PROMPT_GROUP = 4
SEQ_PAD = 32
FRONT = 8

W_COLS = 1536
COL_Q, COL_KV, COL_KR = 1024, 1280, 1408


_PROMPT_ROWS = pl.BlockSpec((TM, D), lambda i: (jnp.minimum(i, N_PT - 1), 0))
_SAMPLE_ROWS = pl.BlockSpec((TM, D), lambda i: (jnp.maximum(i - N_PT, 0), 0))


def _cparams(sem, vmem=VMEM_LIMIT):
    return pltpu.CompilerParams(dimension_semantics=sem, vmem_limit_bytes=vmem)


def _bdot(a, b):
    return jnp.dot(a.astype(BF16), b.astype(BF16), preferred_element_type=F32)


def _split(a):
    hi = a.astype(BF16)
    lo = (a - hi.astype(F32)).astype(BF16)
    return hi, lo


def _dot3(a, b):
    ah, al = _split(a)
    bh, bl = _split(b)
    d = functools.partial(jnp.dot, preferred_element_type=F32)
    return d(ah, bh) + (d(al, bh) + d(ah, bl))


def _rms(x, n=None):
    n = x.shape[-1] if n is None else n
    return x * lax.rsqrt(jnp.sum(x * x, axis=-1, keepdims=True) * (1.0 / n) + EPS)


def _neg_expm1_double(x):
    t = jnp.tanh(x)
    return -2.0 * t / (1.0 - t)


def _sigmoid(x):
    return 0.5 * jnp.tanh(0.5 * x) + 0.5


def _tile_row(i):
    return jnp.where(i >= N_PT, 1 + (i - N_PT) // TILES_PER_DEC, 0)


def _mod_kernel(c_ref, w_ref, b_ref, o_ref):
    s = jax.nn.silu(c_ref[...])
    o_ref[0] = _dot3(s, w_ref[0]) + b_ref[0]


def _modulation(cvec, w_mod, b_mod):
    tn = 1536
    n = 6 * D
    return pl.pallas_call(
        _mod_kernel,
        out_shape=jax.ShapeDtypeStruct((DEPTH, SUBLANES, n), F32),
        grid=(DEPTH, n // tn),
        in_specs=[pl.BlockSpec((SUBLANES, D), lambda l, j: (0, 0)),
                  pl.BlockSpec((1, D, tn), lambda l, j: (l, 0, j)),
                  pl.BlockSpec((1, 1, tn), lambda l, j: (l, 0, j))],
        out_specs=pl.BlockSpec((1, SUBLANES, tn), lambda l, j: (l, 0, j)),
        compiler_params=_cparams(("parallel", "parallel")),
        name="modulation",
    )(cvec, w_mod, b_mod.reshape(DEPTH, 1, n))


def _front_kernel(xp_ref, xs_ref, mod_ref, g_ref, w_ref, qg_ref, wuq_ref, kvg_ref, wkn_ref, wkr_ref,
                  wv_ref, cq_ref, saq_ref, sbq_ref, ck_ref, sak_ref, sbk_ref,
                  zmix_ref, q_ref, k_ref, v_ref, ckv_ref, kr_ref):
    i = pl.program_id(0)
    row = _tile_row(i)
    shift1 = mod_ref[pl.ds(row, 1), 0:D]
    scale1 = mod_ref[pl.ds(row, 1), D:2 * D]
    h = _rms(jnp.where(i >= N_PT, xs_ref[...], xp_ref[...])) * g_ref[...]
    h = h * (1.0 + scale1) + shift1
    z = jnp.dot(h.astype(BF16), w_ref[...], preferred_element_type=F32)
    zmix_ref[...] = z[:, 0:4 * GW]
    qn = _rms(z[:, COL_Q:COL_Q + 2 * LANES], Q_LORA) * qg_ref[...]
    q = _bdot(qn, wuq_ref[...])
    ckv = _rms(z[:, COL_KV:COL_KV + KV_LORA]) * kvg_ref[...]
    ckv_b = ckv.astype(BF16)
    v_ref[...] = jnp.dot(ckv_b, wv_ref[...], preferred_element_type=F32).astype(BF16)
    kn = jnp.dot(ckv_b, wkn_ref[...], preferred_element_type=F32)
    kr = z[:, COL_KR:COL_KR + LANES]

    @pl.when(i < N_PT)
    def _():
        q_ref[...] = (q * Q_SCALE).astype(BF16)
        ckv_ref[...] = ckv
        kr_ref[...] = kr[:, 0:QK_ROPE]
        k_ref[...] = (kn + _bdot(kr, wkr_ref[...])).astype(BF16)

    @pl.when(i >= N_PT)
    def _():
        wq = HEADS * HEAD_PAD
        half = QK_ROPE // 2
        qr = (q * cq_ref[...] + pltpu.roll(q, wq - half, 1) * saq_ref[...]
              + pltpu.roll(q, half, 1) * sbq_ref[...])
        krr = (kr * ck_ref[...] + pltpu.roll(kr, LANES - half, 1) * sak_ref[...]
               + pltpu.roll(kr, half, 1) * sbk_ref[...])
        q_ref[...] = (qr * Q_SCALE).astype(BF16)
        k_ref[...] = (kn + _bdot(krr, wkr_ref[...])).astype(BF16)


def _front(x, mod_l, lw, rope):
    full = lambda shape: pl.BlockSpec(shape, lambda i: (0,) * len(shape))
    rows = lambda w: pl.BlockSpec((TM, w), lambda i: (i, 0))
    prompt_rows = lambda w: pl.BlockSpec((TM, w), lambda i: (jnp.minimum(i, N_PT - 1), 0))
    rope_rows = lambda w: pl.BlockSpec(
        (TM, w), lambda i: (jnp.maximum(i - N_PT, 0) % TILES_PER_DEC, 0))
    wq = HEADS * HEAD_PAD
    return pl.pallas_call(
        _front_kernel,
        out_shape=(jax.ShapeDtypeStruct((T, 4 * GW), F32),
                   jax.ShapeDtypeStruct((T, wq), BF16),
                   jax.ShapeDtypeStruct((T, wq), BF16),
                   jax.ShapeDtypeStruct((T, HEADS * V_DIM), BF16),
                   jax.ShapeDtypeStruct((T_P, KV_LORA), F32),
                   jax.ShapeDtypeStruct((T_P, QK_ROPE), F32)),
        grid=(T // TM,),
        in_specs=[_PROMPT_ROWS, _SAMPLE_ROWS, full((SUBLANES, 6 * D)), full((1, D)), full((D, W_COLS)),
                  full((1, 2 * LANES)), full((2 * LANES, wq)), full((1, KV_LORA)),
                  full((KV_LORA, wq)), full((LANES, wq)), full((KV_LORA, HEADS * V_DIM)),
                  rope_rows(wq), rope_rows(wq), rope_rows(wq),
                  rope_rows(LANES), rope_rows(LANES), rope_rows(LANES)],
        out_specs=(rows(4 * GW), rows(wq), rows(wq), rows(HEADS * V_DIM), prompt_rows(KV_LORA),
                   prompt_rows(QK_ROPE)),
        compiler_params=_cparams(("arbitrary",)),
        name="front",
    )(*x, mod_l, lw["attn_g"], lw["w_in"], lw["q_g"], lw["w_uq"], lw["kv_g"], lw["w_kn"],
      lw["w_kr"], lw["w_v"], *rope)


def _kvexp_kernel(ckv_ref, kr_ref, wkn_ref, wkr_ref, wv_ref, k_ref, v_ref):
    ckv_b = ckv_ref[...].astype(BF16)
    v_ref[...] = jnp.dot(ckv_b, wv_ref[...], preferred_element_type=F32).astype(BF16)
    kn = jnp.dot(ckv_b, wkn_ref[...], preferred_element_type=F32)
    k_ref[...] = (kn + _bdot(kr_ref[...], wkr_ref[...])).astype(BF16)


def _kv_expand(ckv, kr_pad, lw):
    n = ckv.shape[0]
    wq = HEADS * HEAD_PAD
    full = lambda shape: pl.BlockSpec(shape, lambda i: (0,) * len(shape))
    return pl.pallas_call(
        _kvexp_kernel,
        out_shape=(jax.ShapeDtypeStruct((n, wq), BF16),
                   jax.ShapeDtypeStruct((n, HEADS * V_DIM), BF16)),
        grid=(1,),
        in_specs=[full((n, KV_LORA)), full((n, LANES)), full((KV_LORA, wq)), full((LANES, wq)),
                  full((KV_LORA, HEADS * V_DIM))],
        out_specs=(full((n, wq)), full((n, HEADS * V_DIM))),
        compiler_params=_cparams(("arbitrary",)),
        name="kv_expand",
    )(ckv, kr_pad, lw["w_kn"], lw["w_kr"], lw["w_v"])


def _attn_kernel(q_ref, *refs, group, tq, seg_keys):
    o_ref = refs[-1]
    dims = (((1,), (1,)), ((), ()))
    for g in range(group):
        outs = []
        for h in range(HEADS):
            qh = q_ref[g * tq:(g + 1) * tq, h * HEAD_PAD:(h + 1) * HEAD_PAD]
            scores = []
            for si, nk in enumerate(seg_keys):
                kh = refs[2 * si][g * nk:(g + 1) * nk, h * HEAD_PAD:(h + 1) * HEAD_PAD]
                scores.append(lax.dot_general(qh, kh, dims, preferred_element_type=F32))
            m = functools.reduce(jnp.maximum, [jnp.max(s, axis=-1, keepdims=True) for s in scores])
            l, acc = None, None
            for si, (nk, s) in enumerate(zip(seg_keys, scores)):
                p = jnp.exp2(s - m)
                vh = refs[2 * si + 1][g * nk:(g + 1) * nk, h * V_DIM:(h + 1) * V_DIM]
                part = jnp.dot(p.astype(BF16), vh, preferred_element_type=F32)
                psum = jnp.sum(p, axis=-1, keepdims=True)
                l, acc = (psum, part) if l is None else (l + psum, acc + part)
            outs.append(acc / l)
        o_ref[g * tq:(g + 1) * tq, :] = jnp.concatenate(outs, axis=-1)


def _attention(q, segments, n_batch, seq, tq, row0, group=1):
    nq = seq // tq
    wq = HEADS * HEAD_PAD
    in_specs = [pl.BlockSpec((group * tq, wq), lambda b, i: (row0 // (group * tq) + b * nq + i, 0))]
    operands = [q]
    for k_rows, v_rows, nk, first in segments:
        index = lambda b, i, base=first // (group * nk): (base + b, 0)
        in_specs += [pl.BlockSpec((group * nk, wq), index),
                     pl.BlockSpec((group * nk, HEADS * V_DIM), index)]
        operands += [k_rows, v_rows]
    return pl.pallas_call(
        functools.partial(_attn_kernel, group=group, tq=tq,
                          seg_keys=tuple(seg[2] for seg in segments)),
        out_shape=jax.ShapeDtypeStruct((n_batch * seq, GW), F32),
        grid=(n_batch // group, nq),
        in_specs=in_specs,
        out_specs=pl.BlockSpec((group * tq, GW), lambda b, i: (b * nq + i, 0)),
        compiler_params=_cparams(("parallel", "parallel")),
        name="attention",
    )(*operands)


def _fourier_kernel(z_ref, c_ref, s_ref, bdc_ref, bds_ref, w_ref, o_ref, xc_ref, xs_ref, *,
                    norm, group, seq, ts):
    b = pl.program_id(1)

    @pl.when(pl.program_id(0) == 0)
    def _():
        xb = z_ref[...].astype(BF16)
        xc_ref[b] = _bdot(xb, bdc_ref[...]).astype(BF16)
        xs_ref[b] = _bdot(xb, bds_ref[...]).astype(BF16)

    cb, sb = c_ref[...].astype(BF16), s_ref[...].astype(BF16)
    for g in range(group):
        f = (jnp.dot(cb, xc_ref[b, g * seq:(g + 1) * seq, :], preferred_element_type=F32)
             - jnp.dot(sb, xs_ref[b, g * seq:(g + 1) * seq, :], preferred_element_type=F32)) * norm
        o_ref[g * ts:(g + 1) * ts, :] = _bdot(f, w_ref[...])


def _fourier(zmix, consts, lw, n_batch, seq, row0, group=1):
    ts = min(seq, 512)
    nj = seq // ts
    assert group == 1 or nj == 1
    nb = n_batch // group
    cmat, smat = consts
    full = lambda shape: pl.BlockSpec(shape, lambda j, b: (0,) * len(shape))
    z_spec = pl.BlockSpec((group * seq, GW),
                          lambda j, b: (row0 // (group * seq) + jnp.where(j == 0, b, nb - 1), 1))
    return pl.pallas_call(
        functools.partial(_fourier_kernel, norm=float((seq * (GW // 4)) ** -0.5), group=group,
                          seq=seq, ts=ts),
        out_shape=jax.ShapeDtypeStruct((n_batch * seq, GW), F32),
        grid=(nj, nb),
        in_specs=[z_spec,
                  pl.BlockSpec((ts, seq), lambda j, b: (j, 0)),
                  pl.BlockSpec((ts, seq), lambda j, b: (j, 0)),
                  full((GW, GW)), full((GW, GW)), full((GW, GW))],
        out_specs=pl.BlockSpec((group * ts, GW), lambda j, b: (b * nj + j, 0)),
        scratch_shapes=[pltpu.VMEM((nb, group * seq, GW), BF16),
                        pltpu.VMEM((nb, group * seq, GW), BF16)],
        compiler_params=_cparams(("arbitrary", "arbitrary")),
        name="fourier",
    )(zmix, cmat, smat, lw["bd_c"], lw["bd_s"], lw["w_f"])


def _seq_kernel(za_ref, zr_ref, zg_ref, h0_ref, icnt_ref, wp_ref, ps_ref, cw_ref, cb_ref, wa_ref,
                ba_ref, wx_ref, bx_ref, lam_ref, ya_ref, yd_ref, st_ref,
                pa_ref, pb_ref, xp_ref, a_ref, b_ref, *, seq):
    n = seq + SEQ_PAD
    span = seq + 2 * FRONT
    zeros_pad = jnp.zeros((n, GW), F32)

    za = za_ref[...]
    pa_ref[...] = zeros_pad
    pb_ref[...] = zeros_pad
    pa_ref[FRONT:FRONT + seq, :] = za
    pb_ref[0:span, :] = pa_ref[0:span, :] + pa_ref[1:span + 1, :]
    win2 = pb_ref[FRONT - 1:FRONT - 1 + seq, :]
    pa_ref[0:span, :] = pb_ref[0:span, :] + pb_ref[2:span + 2, :]
    win4 = pa_ref[FRONT - 2:FRONT - 2 + seq, :]
    pb_ref[0:span, :] = pa_ref[0:span, :] + pa_ref[4:span + 4, :]
    win8 = pb_ref[FRONT - 4:FRONT - 4 + seq, :]
    pa_ref[0:span, :] = pb_ref[0:span, :] + pb_ref[8:span + 8, :]
    win16 = pa_ref[FRONT - 8:FRONT - 8 + seq, :]
    grp = lax.broadcasted_iota(I32, (1, GW), 1) // (GW // 4)
    win = jnp.where(grp == 0, win2, jnp.where(grp == 1, win4, jnp.where(grp == 2, win8, win16)))
    dlt = win * icnt_ref[...] - za
    ya_ref[...] = _bdot(dlt, wp_ref[...]) * ps_ref[...]

    xp_ref[...] = zeros_pad
    xp_ref[FRONT:FRONT + seq, :] = zr_ref[...]
    xc = cb_ref[...] + cw_ref[0:1, :] * xp_ref[FRONT - 2:FRONT - 2 + seq, :]
    for kk in range(1, 4):
        xc = xc + cw_ref[kk:kk + 1, :] * xp_ref[FRONT - 2 + kk:FRONT - 2 + kk + seq, :]
    xcb = xc.astype(BF16)
    n_grp = seq // SUBLANES
    sub = lax.broadcasted_iota(I32, (n_grp, SUBLANES, GW), 1)

    for d in range(2):
        r = _sigmoid(jnp.dot(xcb, wa_ref[d], preferred_element_type=F32) + ba_ref[d])
        ig = _sigmoid(jnp.dot(xcb, wx_ref[d], preferred_element_type=F32) + bx_ref[d])
        log_a = (-LRU_C) * r * jax.nn.softplus(-lam_ref[d])
        a = jnp.exp(log_a).reshape(n_grp, SUBLANES, GW)
        b = (jnp.sqrt(_neg_expm1_double(log_a)) * (ig * xc)).reshape(n_grp, SUBLANES, GW)
        for k in (1, 2, 4):
            shift = k if d == 0 else SUBLANES - k
            m = sub >= k if d == 0 else sub < SUBLANES - k
            ap, bp = pltpu.roll(a, shift, 1), pltpu.roll(b, shift, 1)
            b = jnp.where(m, a * bp + b, b)
            a = jnp.where(m, a * ap, a)
        a_ref[d] = a.reshape(seq, GW)
        b_ref[d] = b.reshape(seq, GW)

    def step(g, carry):
        fwd, bwd = carry
        off_f = pl.multiple_of(g * SUBLANES, SUBLANES)
        off_b = pl.multiple_of((n_grp - 1 - g) * SUBLANES, SUBLANES)
        hf = a_ref[0, pl.ds(off_f, SUBLANES), :] * fwd + b_ref[0, pl.ds(off_f, SUBLANES), :]
        hb = a_ref[1, pl.ds(off_b, SUBLANES), :] * bwd + b_ref[1, pl.ds(off_b, SUBLANES), :]
        b_ref[0, pl.ds(off_f, SUBLANES), :] = hf
        b_ref[1, pl.ds(off_b, SUBLANES), :] = hb
        return (jnp.broadcast_to(hf[SUBLANES - 1:SUBLANES, :], (SUBLANES, GW)),
                jnp.broadcast_to(hb[0:1, :], (SUBLANES, GW)))

    start = tuple(jnp.broadcast_to(h0_ref[0, d:d + 1, :], (SUBLANES, GW)) for d in range(2))
    last_f, last_b = lax.fori_loop(0, n_grp, step, start)
    st_ref[0, 0:1, :] = last_f[0:1, :]
    st_ref[0, 1:2, :] = last_b[0:1, :]
    yd_ref[...] = (b_ref[0] + b_ref[1]) * jax.nn.gelu(zg_ref[...])


def _pool_inverse_counts(seq):
    pos = np.arange(seq)[:, None]
    half = np.repeat(np.array(POOL_WINDOWS) // 2, GW // len(POOL_WINDOWS))[None, :]
    cnt = np.minimum(pos + half, seq) - np.maximum(pos - half, 0)
    return jnp.asarray(1.0 / cnt, F32)


def _seq_mixers(zmix, h0, lw, n_batch, seq, row0):
    full = lambda shape: pl.BlockSpec(shape, lambda b: (0,) * len(shape))
    col = lambda c: pl.BlockSpec((seq, GW), lambda b: (row0 // seq + b, c))
    out_rows = pl.BlockSpec((seq, GW), lambda b: (b, 0))
    pad = pltpu.VMEM((seq + SEQ_PAD, GW), F32)
    return pl.pallas_call(
        functools.partial(_seq_kernel, seq=seq),
        out_shape=(jax.ShapeDtypeStruct((n_batch * seq, GW), F32),
                   jax.ShapeDtypeStruct((n_batch * seq, GW), F32),
                   jax.ShapeDtypeStruct((n_batch, 2, GW), F32)),
        grid=(n_batch,),
        in_specs=[col(0), col(2), col(3), pl.BlockSpec((1, 2, GW), lambda b: (b, 0, 0)),
                  full((seq, GW)), full((GW, GW)), full((1, GW)), full((4, GW)), full((1, GW)),
                  full((2, GW, GW)), full((2, 1, GW)), full((2, GW, GW)), full((2, 1, GW)),
                  full((2, 1, GW))],
        out_specs=(out_rows, out_rows, pl.BlockSpec((1, 2, GW), lambda b: (b, 0, 0))),
        scratch_shapes=[pad, pad, pad, pltpu.VMEM((2, seq, GW), F32), pltpu.VMEM((2, seq, GW), F32)],
        compiler_params=_cparams(("parallel",)),
        name="seq_mixers",
    )(zmix, zmix, zmix, h0, _pool_inverse_counts(seq), lw["bd_pool"], lw["pool_scale"], lw["conv_w"],
      lw["conv_b"], lw["bd_wa"], lw["lru_ba"], lw["bd_wx"], lw["lru_bx"], lw["lru_lam"])


def _post_kernel(xp_ref, xs_ref, mod_ref, yap_ref, ybp_ref, ycp_ref, ydp_ref, yas_ref, ybs_ref, ycs_ref,
                 yds_ref, og_ref, wo_ref, fg_ref, rw_ref, rb_ref, tri_ref, upper_ref,
                 x1_ref, h2p_ref, route_ref, prob_ref, tab_ref, cnt_ref, carry_ref):
    i = pl.program_id(0)
    is_s = i >= N_PT
    row = _tile_row(i)
    gate1 = mod_ref[pl.ds(row, 1), 2 * D:3 * D]
    shift2 = mod_ref[pl.ds(row, 1), 3 * D:4 * D]
    scale2 = mod_ref[pl.ds(row, 1), 4 * D:5 * D]

    @pl.when(i == 0)
    def _():
        carry_ref[...] = jnp.zeros_like(carry_ref)

    groups = []
    for gi, (p_ref, s_ref) in enumerate(((yap_ref, yas_ref), (ybp_ref, ybs_ref),
                                         (ycp_ref, ycs_ref), (ydp_ref, yds_ref))):
        y = jnp.where(is_s, s_ref[...], p_ref[...])
        groups.append((_rms(y) * og_ref[gi:gi + 1, :]).astype(BF16))
    ycat = jnp.concatenate(groups, axis=-1)
    x = jnp.where(is_s, xs_ref[...], xp_ref[...])
    x1 = x + gate1 * jnp.dot(ycat, wo_ref[...], preferred_element_type=F32)
    x1_ref[...] = x1
    h2 = _rms(x1) * fg_ref[...]
    h2 = h2 * (1.0 + scale2) + shift2

    for j in range(ROW_F):
        h2p_ref[pl.ds(j, TM, stride=ROW_F), :] = h2[:, j * LANES:(j + 1) * LANES]

    h_hi, h_lo = _split(h2)
    both = jnp.dot(h_hi, rw_ref[...], preferred_element_type=F32)
    cross = both[:, LANES:2 * LANES] + jnp.dot(h_lo, rw_ref[:, 0:LANES], preferred_element_type=F32)
    logits = both[:, 0:LANES] + cross + rb_ref[...]
    lane = lax.broadcasted_iota(I32, (TM, LANES), 1)
    lane_f = lane.astype(F32)
    neg = jnp.float32(-jnp.inf)
    cur = jnp.where(lane < N_EXP, logits, neg)
    sel, vals, idxs = [], [], []
    for _ in range(TOP_K):
        m = jnp.max(cur, axis=-1, keepdims=True)
        idx = jnp.min(jnp.where(cur == m, lane_f, float(LANES)), axis=-1, keepdims=True)
        hit = lane_f == idx
        sel.append(hit)
        vals.append(m)
        idxs.append(idx)
        cur = jnp.where(hit, neg, cur)
    exps = [jnp.exp(v - vals[0]) for v in vals]
    denom = exps[0] + exps[1] + exps[2] + exps[3]
    onehot = jnp.where(sel[0] | sel[1] | sel[2] | sel[3], 1.0, 0.0)
    cum_l = jnp.dot(tri_ref[...], onehot.astype(BF16), preferred_element_type=F32)
    before = carry_ref[0:1, :]
    cum = cum_l + before
    n_tile = cum_l[TM - 1:TM, :]
    nch = jnp.floor((n_tile + (CH_C - 1.0)) * (1.0 / CH_C))
    nch8 = jnp.broadcast_to(nch, (SUBLANES, LANES))
    base = jnp.dot(nch8.astype(BF16), upper_ref[...], preferred_element_type=F32)[0:1, :]
    pk = jnp.zeros((TM, LANES), I32)
    pf = jnp.zeros((TM, LANES), F32)
    for k in range(TOP_K):
        rank = jnp.sum(jnp.where(sel[k], cum - 1.0, 0.0), axis=-1, keepdims=True).astype(I32)
        local = jnp.sum(jnp.where(sel[k], (base * CH_C + cum_l - 1.0) * ROW_F, 0.0), axis=-1,
                        keepdims=True).astype(I32)
        pk = jnp.where(lane == k, idxs[k].astype(I32), pk)
        pk = jnp.where(lane == TOP_K + k, rank, pk)
        pk = jnp.where(lane == 2 * TOP_K + k, local, pk)
        pf = jnp.where(lane == k, exps[k] / denom, pf)
    route_ref[...] = pk.T[0:4 * TOP_K, :]
    prob_ref[...] = pf.T[0:SUBLANES, :]
    row8 = lax.broadcasted_iota(I32, (SUBLANES, LANES), 0)
    tab_ref[...] = jnp.where(row8 == 0, before, jnp.where(row8 == 1, nch, base))
    new_carry = jnp.broadcast_to(cum[TM - 1:TM, :], (SUBLANES, LANES))
    carry_ref[...] = new_carry
    cnt_ref[...] = new_carry


def _post(x, mod_l, ys_prompt, ys_sample, lw, tri, upper):
    full = lambda shape: pl.BlockSpec(shape, lambda i: (0,) * len(shape))
    rows = lambda w: pl.BlockSpec((TM, w), lambda i: (i, 0))
    prow = pl.BlockSpec((TM, GW), lambda i: (jnp.minimum(i, N_PT - 1), 0))
    srow = pl.BlockSpec((TM, GW), lambda i: (jnp.maximum(i - N_PT, 0), 0))
    return pl.pallas_call(
        _post_kernel,
        out_shape=(jax.ShapeDtypeStruct((T, D), F32),
                   jax.ShapeDtypeStruct((T * ROW_F, LANES), F32),
                   jax.ShapeDtypeStruct((4 * TOP_K, T), I32),
                   jax.ShapeDtypeStruct((SUBLANES, T), F32),
                   jax.ShapeDtypeStruct((T // TM * SUBLANES, LANES), F32),
                   jax.ShapeDtypeStruct((SUBLANES, LANES), F32)),
        grid=(T // TM,),
        in_specs=[_PROMPT_ROWS, _SAMPLE_ROWS, full((SUBLANES, 6 * D)),
                  prow, prow, prow, prow, srow, srow, srow, srow,
                  full((4, GW)), full((D, D)), full((1, D)), full((D, 2 * LANES)), full((1, LANES)),
                  full((TM, TM)), full((LANES, LANES))],
        out_specs=(rows(D), pl.BlockSpec((TM * ROW_F, LANES), lambda i: (i, 0)),
                   pl.BlockSpec((4 * TOP_K, TM), lambda i: (0, i)),
                   pl.BlockSpec((SUBLANES, TM), lambda i: (0, i)),
                   pl.BlockSpec((SUBLANES, LANES), lambda i: (i, 0)), full((SUBLANES, LANES))),
        scratch_shapes=[pltpu.VMEM((SUBLANES, LANES), F32)],
        compiler_params=_cparams(("arbitrary",)),
        name="post",
    )(*x, mod_l, *ys_prompt, *ys_sample, lw["out_g"], lw["w_out"], lw["ffn_g"], lw["router_w"],
      lw["router_b"], tri, upper)


def _gather_kernel(dest_ref, nused_ref, zeros_hbm, src_hbm, o_ref, stok_ref, src_ref, sem, src_sem):
    i = pl.program_id(0)

    @pl.when(i == 0)
    def _():
        load = pltpu.make_async_copy(src_hbm, src_ref, src_sem)
        load.start()
        init = pltpu.make_async_copy(zeros_hbm, stok_ref, sem)
        init.start()
        init.wait()

        def scatter(t8, c):
            row0 = t8 * (SUBLANES * ROW_F)
            for k in range(TOP_K):
                for u in range(SUBLANES):
                    stok_ref[dest_ref[k * T + t8 * SUBLANES + u]] = row0 + u * ROW_F
            return c
        lax.fori_loop(0, T // SUBLANES, scatter, 0)
        load.wait()

    @pl.when(i < nused_ref[0])
    def _():
        def rows(r16, c):
            r0 = pl.multiple_of(r16 * GATHER_UNROLL, GATHER_UNROLL)
            for u in range(GATHER_UNROLL):
                src = pl.multiple_of(stok_ref[i * TM_E + r0 + u], ROW_F)
                dst = pl.multiple_of(r0 * ROW_F, GATHER_UNROLL * ROW_F) + u * ROW_F
                o_ref[pl.ds(dst, ROW_F), :] = src_ref[pl.ds(src, ROW_F), :]
            return c
        lax.fori_loop(0, TM_E // GATHER_UNROLL, rows, 0)

    @pl.when(i >= nused_ref[0])
    def _():
        o_ref[...] = jnp.zeros_like(o_ref)


def _gather_rows(dest, n_used, h2lin):
    return pl.pallas_call(
        _gather_kernel,
        out_shape=jax.ShapeDtypeStruct((N_SLOTS * ROW_F, LANES), F32),
        grid_spec=pltpu.PrefetchScalarGridSpec(
            num_scalar_prefetch=2,
            grid=(NB_E,),
            in_specs=[pl.BlockSpec(memory_space=pl.ANY), pl.BlockSpec(memory_space=pl.ANY)],
            out_specs=pl.BlockSpec((TM_E * ROW_F, LANES), lambda i, d, nu: (i, 0)),
            scratch_shapes=[pltpu.SMEM((N_SLOTS,), I32), pltpu.VMEM((T * ROW_F, LANES), F32),
                            pltpu.SemaphoreType.DMA(()), pltpu.SemaphoreType.DMA(())]),
        compiler_params=_cparams(("arbitrary",), GATHER_VMEM_LIMIT),
        name="moe_gather",
    )(dest, n_used, jnp.zeros((N_SLOTS,), I32), h2lin)


def _expert_rows(n, x_ref, w_ref, slot, bg_ref, bu_ref, bd_ref, o_ref):
    x = jnp.concatenate([x_ref[pl.ds(j, n, stride=ROW_F), :].astype(BF16) for j in range(ROW_F)],
                        axis=-1)
    g = jnp.dot(x, w_ref[slot, 0].astype(BF16), preferred_element_type=F32) + bg_ref[0, 0]
    u = jnp.dot(x, w_ref[slot, 1].astype(BF16), preferred_element_type=F32) + bu_ref[0, 0]
    g = jnp.minimum(g, LIMIT)
    u = jnp.clip(u, -LIMIT, LIMIT)
    act = (u + 1.0) * (g * jax.nn.sigmoid(ALPHA * g))
    y = jnp.dot(act.astype(BF16), w_ref[slot, 2].astype(BF16),
                preferred_element_type=F32) + bd_ref[0, 0]
    for j in range(ROW_F):
        o_ref[pl.ds(j, n, stride=ROW_F), :] = y[:, j * LANES:(j + 1) * LANES]
    if n < TM_E:
        o_ref[n * ROW_F:TM_E * ROW_F, :] = jnp.zeros(((TM_E - n) * ROW_F, LANES), F32)


def _expert_kernel(be_ref, nused_ref, valid_ref, first_ref, next_ref, par_ref,
                   x_ref, wg_hbm, bg_ref, wu_hbm, bu_ref, wd_hbm, bd_ref, o_ref, w_ref, sems, *, layer):
    i = pl.program_id(0)
    ib = jnp.minimum(i, NB_E - 1)
    live = i < nused_ref[0]
    valid = valid_ref[ib]
    expert = be_ref[ib]
    slot = par_ref[ib]

    def fetch(e, s):
        return [pltpu.make_async_copy(src.at[layer, e], w_ref.at[s, m], sems.at[s, m])
                for m, src in enumerate((wg_hbm, wu_hbm, wd_hbm))]

    @pl.when(i == 0)
    def _():
        for cp in fetch(expert, slot):
            cp.start()

    @pl.when(live & (first_ref[ib] == 1))
    def _():
        @pl.when(next_ref[ib] >= 0)
        def _():
            for cp in fetch(next_ref[ib], 1 - slot):
                cp.start()

        for cp in fetch(expert, slot):
            cp.wait()

    for n_idx, n in enumerate(EXPERT_ROWS):
        fits = valid <= n
        if n_idx + 1 < len(EXPERT_ROWS):
            fits = fits & (valid > EXPERT_ROWS[n_idx + 1])

        @pl.when(live & fits)
        def _(n=n):
            _expert_rows(n, x_ref, w_ref, slot, bg_ref, bu_ref, bd_ref, o_ref)

    @pl.when(jnp.logical_not(live))
    def _():
        o_ref[...] = jnp.zeros_like(o_ref)


def _experts(l, sched, xs, p):
    last = NB_E - 1
    bspec = pl.BlockSpec((1, 1, 1, D), lambda i, be, *_: (l, be[jnp.minimum(i, last)], 0, 0))
    hbm = pl.BlockSpec(memory_space=pl.ANY)
    bias = lambda b: b.reshape(DEPTH, N_EXP, 1, D)
    return pl.pallas_call(
        functools.partial(_expert_kernel, layer=l),
        out_shape=jax.ShapeDtypeStruct(((NB_E + 1) * TM_E * ROW_F, LANES), F32),
        grid_spec=pltpu.PrefetchScalarGridSpec(
            num_scalar_prefetch=len(sched),
            grid=(NB_E + 1,),
            in_specs=[pl.BlockSpec((TM_E * ROW_F, LANES), lambda i, *_: (jnp.minimum(i, last), 0)),
                      hbm, bspec, hbm, bspec, hbm, bspec],
            out_specs=pl.BlockSpec((TM_E * ROW_F, LANES), lambda i, *_: (i, 0)),
            scratch_shapes=[pltpu.VMEM((2, 3, D, D), F32), pltpu.SemaphoreType.DMA((2, 3))]),
        compiler_params=_cparams(("arbitrary",)),
        name="moe_experts",
    )(*sched, xs, p["w_gate"], bias(p["b_gate"]), p["w_up"], bias(p["b_up"]),
      p["w_down"], bias(p["b_down"]))


def _combine_kernel(loc_ref, p_ref, csrc_ref, nch_ref, ys_hbm, x_ref, mod_ref, fg_ref, *rest, final):
    *outs, buf_ref, acc_ref, sems = rest
    tb = pl.program_id(0)
    n_tb = pl.num_programs(0)
    slot = tb % 2
    row = _tile_row(tb)
    gate2 = mod_ref[pl.ds(row, 1), 5 * D:6 * D]
    chunk_rows = CH_C * ROW_F

    def chunk_copy(t, c, s):
        src = pl.multiple_of(csrc_ref[t * MAX_CH + c] * ROW_F, ROW_F)
        dst = pl.multiple_of(c * chunk_rows, chunk_rows)
        return pltpu.make_async_copy(ys_hbm.at[pl.ds(src, chunk_rows)],
                                     buf_ref.at[s, pl.ds(dst, chunk_rows)], sems.at[s])

    def issue(t, s):
        def body(c, carry):
            chunk_copy(t, c, s).start()
            return carry
        lax.fori_loop(0, nch_ref[t], body, 0)

    @pl.when(tb == 0)
    def _():
        issue(0, 0)

    @pl.when(tb + 1 < n_tb)
    def _():
        issue(tb + 1, 1 - slot)

    def drain(c, carry):
        chunk_copy(tb, c, slot).wait()
        return carry
    lax.fori_loop(0, nch_ref[tb], drain, 0)

    def tokens(r8, carry):
        r0 = pl.multiple_of(r8 * SUBLANES, SUBLANES)
        for u in range(SUBLANES):
            tok = tb * TM + r0 + u
            acc = None
            for k in range(TOP_K):
                off = pl.multiple_of(loc_ref[k * T + tok], ROW_F)
                term = buf_ref[slot, pl.ds(off, ROW_F), :] * p_ref[k * T + tok]
                acc = term if acc is None else acc + term
            dst = pl.multiple_of(r0 * ROW_F, SUBLANES * ROW_F) + u * ROW_F
            acc_ref[pl.ds(dst, ROW_F), :] = acc
        return carry
    lax.fori_loop(0, TM // SUBLANES, tokens, 0)

    moe = jnp.concatenate([acc_ref[pl.ds(j, TM, stride=ROW_F), :] for j in range(ROW_F)], axis=-1)
    x2 = x_ref[...] + gate2 * moe
    if final:
        x2 = _rms(x2) * fg_ref[...]
    prompt_ref, sample_ref = outs

    @pl.when(tb < N_PT)
    def _():
        prompt_ref[...] = x2

    @pl.when(tb >= N_PT)
    def _():
        sample_ref[...] = x2


def _combine(loc, top_p, chunk_src, n_chunks, ys, x1, mod_l, final_g, final):
    full = lambda shape: pl.BlockSpec(shape, lambda i, *_: (0,) * len(shape))
    rows = lambda w: pl.BlockSpec((TM, w), lambda i, *_: (i, 0))
    out_shape = (jax.ShapeDtypeStruct((T_P, D), F32), jax.ShapeDtypeStruct((T_S, D), F32))
    out_specs = (pl.BlockSpec((TM, D), lambda i, *_: (jnp.minimum(i, N_PT - 1), 0)),
                 pl.BlockSpec((TM, D), lambda i, *_: (jnp.maximum(i - N_PT, 0), 0)))
    return pl.pallas_call(
        functools.partial(_combine_kernel, final=final),
        out_shape=out_shape,
        grid_spec=pltpu.PrefetchScalarGridSpec(
            num_scalar_prefetch=4,
            grid=(T // TM,),
            in_specs=[pl.BlockSpec(memory_space=pl.ANY), rows(D), full((SUBLANES, 6 * D)),
                      full((1, D))],
            out_specs=out_specs,
            scratch_shapes=[pltpu.VMEM((2, MAX_CH * CH_C * ROW_F, LANES), F32),
                            pltpu.VMEM((TM * ROW_F, LANES), F32),
                            pltpu.SemaphoreType.DMA((2,))]),
        compiler_params=_cparams(("arbitrary",)),
        name="moe_combine",
    )(loc, top_p, chunk_src, n_chunks, ys, x1, mod_l, final_g)


def _rope_tables():
    rows = DEC_SEQ // GRID_W
    r = np.repeat(np.arange(rows, dtype=np.float64), GRID_W)
    c = np.tile(np.arange(GRID_W, dtype=np.float64), rows)
    n_freq = QK_ROPE // 4
    inv = (np.float32(ROPE_BASE) ** (-np.arange(n_freq, dtype=np.float32) / n_freq)).astype(np.float64)
    ang = np.concatenate([r[:, None] * inv, c[:, None] * inv], axis=-1).astype(np.float32)
    cos, sin = np.cos(ang.astype(np.float64)), np.sin(ang.astype(np.float64))
    half = QK_ROPE // 2

    def place(width, start):
        cf = np.ones((DEC_SEQ, width), np.float32)
        sa = np.zeros((DEC_SEQ, width), np.float32)
        sb = np.zeros((DEC_SEQ, width), np.float32)
        for s0 in start:
            cf[:, s0:s0 + half] = cos
            cf[:, s0 + half:s0 + 2 * half] = cos
            sa[:, s0:s0 + half] = -sin
            sb[:, s0 + half:s0 + 2 * half] = sin
        return jnp.asarray(cf), jnp.asarray(sa), jnp.asarray(sb)

    return (*place(HEADS * HEAD_PAD, [h * HEAD_PAD + QK_NOPE for h in range(HEADS)]),
            *place(LANES, [0]))


def _dft_tables(seq):
    kn = (np.arange(seq, dtype=np.int64)[:, None] * np.arange(seq, dtype=np.int64)[None, :]) % seq
    ang = 2.0 * np.pi * kn.astype(np.float64) / seq
    return jnp.asarray(np.cos(ang), F32), jnp.asarray(np.sin(ang), F32)


def _block_diag(blocks):
    g, n, _ = blocks.shape
    eye = jnp.eye(g, dtype=blocks.dtype)
    return jnp.einsum("gij,gh->gihj", blocks, eye).reshape(g * n, g * n)


def _layer_weights(l, p):
    w_in = p["w_in"][l]
    za, zq, zkv, zkr, zf, zr, zg = jnp.split(
        w_in, np.cumsum([GW, Q_LORA, KV_LORA, QK_ROPE, GW, GW])[:], axis=1)
    zpad = lambda n: jnp.zeros((D, n), F32)
    w_cols = jnp.concatenate([za, zf, zr, zg, zq, zpad(2 * LANES - Q_LORA), zkv, zkr,
                              zpad(LANES - QK_ROPE)], axis=1).astype(BF16)
    wq = HEADS * HEAD_PAD
    w_uq = p["w_uq"][l].reshape(Q_LORA, HEADS, QK_NOPE + QK_ROPE)
    w_uq = jnp.pad(w_uq, ((0, 2 * LANES - Q_LORA), (0, 0), (0, HEAD_PAD - QK_NOPE - QK_ROPE)))
    w_ukv = p["w_ukv"][l].reshape(KV_LORA, HEADS, QK_NOPE + V_DIM)
    w_kn = jnp.pad(w_ukv[:, :, :QK_NOPE], ((0, 0), (0, 0), (0, HEAD_PAD - QK_NOPE)))
    w_v = w_ukv[:, :, QK_NOPE:]
    place = np.zeros((LANES, HEADS, HEAD_PAD), np.float32)
    for h in range(HEADS):
        place[np.arange(QK_ROPE), h, QK_NOPE + np.arange(QK_ROPE)] = 1.0
    c64 = np.arange(GW // 4, dtype=np.int64)
    ang = 2.0 * np.pi * ((c64[:, None] * c64[None, :]) % (GW // 4)).astype(np.float64) / (GW // 4)
    four = lambda m: jnp.asarray(np.broadcast_to(m, (4,) + m.shape), F32)
    router_hi, router_lo = _split(jnp.pad(p["router_w"][l], ((0, 0), (0, LANES - N_EXP))))
    router_w = jnp.concatenate([router_hi, router_lo], axis=1)
    router_b = jnp.pad(p["router_b"][l], (0, LANES - N_EXP)).reshape(1, LANES)
    return {
        "attn_g": p["attn_norm_g"][l].reshape(1, D),
        "w_in": w_cols,
        "q_g": jnp.pad(p["q_norm_g"][l], (0, 2 * LANES - Q_LORA)).reshape(1, 2 * LANES),
        "w_uq": w_uq.reshape(2 * LANES, wq).astype(BF16),
        "kv_g": p["kv_norm_g"][l].reshape(1, KV_LORA),
        "w_kn": w_kn.reshape(KV_LORA, wq).astype(BF16),
        "w_kr": jnp.asarray(place.reshape(LANES, wq), BF16),
        "w_v": w_v.reshape(KV_LORA, HEADS * V_DIM).astype(BF16),
        "bd_c": _block_diag(four(np.cos(ang))),
        "bd_s": _block_diag(four(np.sin(ang))),
        "w_f": p["fourier_w"][l].astype(BF16),
        "bd_pool": _block_diag(p["pool_w"][l]).astype(BF16),
        "pool_scale": p["pool_scale"][l].reshape(1, GW),
        "conv_w": p["conv_w"][l],
        "conv_b": p["conv_b"][l].reshape(1, GW),
        "bd_wa": jnp.stack([_block_diag(p["lru_wa"][l, d]) for d in range(2)]).astype(BF16),
        "bd_wx": jnp.stack([_block_diag(p["lru_wx"][l, d]) for d in range(2)]).astype(BF16),
        "lru_ba": p["lru_ba"][l].reshape(2, 1, GW),
        "lru_bx": p["lru_bx"][l].reshape(2, 1, GW),
        "lru_lam": p["lru_lambda"][l].reshape(2, 1, GW),
        "out_g": p["out_norm_g"][l],
        "w_out": p["w_out"][l].astype(BF16),
        "ffn_g": p["ffn_norm_g"][l].reshape(1, D),
        "router_w": router_w,
        "router_b": router_b,
    }


def _routing_tables(route, prob, counts, tab):
    top_e, rank = route[0:TOP_K], route[TOP_K:2 * TOP_K]
    top_p = prob[0:TOP_K].reshape(-1)
    loc = route[2 * TOP_K:3 * TOP_K].reshape(-1)
    counts = counts.astype(I32)
    padded = (counts + TM_E - 1) // TM_E * TM_E
    pad_ends = jnp.cumsum(padded)
    pad_starts = pad_ends - padded
    experts = jnp.arange(N_EXP, dtype=I32)
    onehot = top_e[:, :, None] == experts
    dest = (jnp.sum(jnp.where(onehot, pad_starts, 0), axis=-1) + rank).reshape(T * TOP_K)
    n_used = (pad_ends[-1] // TM_E).astype(I32)
    blk = jnp.minimum(jnp.arange(NB_E, dtype=I32), n_used - 1) * TM_E
    block_e = jnp.minimum(jnp.sum(pad_ends[None, :] <= blk[:, None], axis=-1), N_EXP - 1).astype(I32)
    of_block = lambda a: jnp.sum(jnp.where(block_e[:, None] == experts, a, 0), axis=-1)
    valid = jnp.clip(of_block(pad_starts + counts) - blk, 0, TM_E).astype(I32)
    first = (blk == of_block(pad_starts)).astype(I32)
    later = (experts[None, :] > experts[:, None]) & (counts[None, :] > 0)
    next_of = jnp.min(jnp.where(later, experts[None, :], N_EXP), axis=-1)
    next_e = of_block(jnp.where(next_of < N_EXP, next_of, -1)).astype(I32)
    parity = (of_block(jnp.cumsum((counts > 0).astype(I32))) % 2).astype(I32)
    sched = (block_e, n_used.reshape(1), valid, first, next_e, parity)
    tab = tab.reshape(T // TM, SUBLANES, LANES)[:, :, :N_EXP].astype(I32)
    before, nch, base = tab[:, 0], tab[:, 1], tab[:, 2]
    run_start = pad_starts[None, :] + before
    ends = base + nch
    ci = jnp.arange(MAX_CH, dtype=I32)
    e_of = jnp.minimum(jnp.sum(ends[:, None, :] <= ci[None, :, None], axis=-1), N_EXP - 1)
    pick = lambda a: jnp.sum(jnp.where(e_of[:, :, None] == experts, a[:, None, :], 0), axis=-1)
    chunk_src = jnp.clip(pick(run_start) + (ci[None, :] - pick(base)) * CH_C, 0, N_SLOTS)
    return dest, sched, loc, top_p, chunk_src.reshape(-1), ends[:, N_EXP - 1]


def kernel(x_prompt, x_sample, cache_ckv, cache_krope, state_lru, c, c_ctx, w_mod, b_mod, attn_norm_g, w_in, pool_w, pool_scale, q_norm_g, w_uq, kv_norm_g, w_ukv, fourier_w, conv_w, conv_b, lru_wa, lru_ba, lru_wx, lru_bx, lru_lambda, out_norm_g, w_out, ffn_norm_g, router_w, router_b, w_gate, b_gate, w_up, b_up, w_down, b_down, final_norm_g):
    params = dict(attn_norm_g=attn_norm_g, w_in=w_in, pool_w=pool_w, pool_scale=pool_scale,
                  q_norm_g=q_norm_g, w_uq=w_uq, kv_norm_g=kv_norm_g, w_ukv=w_ukv,
                  fourier_w=fourier_w, conv_w=conv_w, conv_b=conv_b, lru_wa=lru_wa, lru_ba=lru_ba,
                  lru_wx=lru_wx, lru_bx=lru_bx, lru_lambda=lru_lambda, out_norm_g=out_norm_g,
                  w_out=w_out, ffn_norm_g=ffn_norm_g, router_w=router_w, router_b=router_b,
                  w_gate=w_gate, b_gate=b_gate, w_up=w_up, b_up=b_up, w_down=w_down, b_down=b_down)
    x = (x_prompt.reshape(T_P, D), x_sample.reshape(T_S, D))
    cvec = jnp.concatenate([c_ctx[None, :], c, jnp.zeros((SUBLANES - 1 - DEC_BATCH, D), F32)], axis=0)
    mod = _modulation(cvec, w_mod, b_mod)
    rope = _rope_tables()
    dft_p, dft_s = _dft_tables(SEQ), _dft_tables(DEC_SEQ)
    tri = jnp.asarray(np.tril(np.ones((TM, TM), np.float32)), BF16)
    upper = jnp.asarray(np.triu(np.ones((LANES, LANES), np.float32), 1), BF16)
    final_g = final_norm_g.reshape(1, D)
    h0_prompt = jnp.zeros((BATCH, 2, GW), F32)

    new_ckv, new_krope, new_lru = [], [], []
    for l in range(DEPTH):
        lw = _layer_weights(l, params)
        zmix, q, k, v, ckv, kr = _front(x, mod[l], lw, rope)
        new_ckv.append(ckv.reshape(BATCH, SEQ, KV_LORA))
        new_krope.append(kr.reshape(BATCH, SEQ, QK_ROPE))

        yb_p = _attention(q, [(k, v, SEQ, 0)], BATCH, SEQ, SEQ, 0, group=PROMPT_GROUP)
        yc_p = _fourier(zmix, dft_p, lw, BATCH, SEQ, 0, group=PROMPT_GROUP)
        ya_p, yd_p, st_p = _seq_mixers(zmix, h0_prompt, lw, BATCH, SEQ, 0)
        new_lru.append(st_p)

        kr_ctx = jnp.pad(cache_krope[:, l].reshape(DEC_BATCH * PAST, QK_ROPE),
                         ((0, 0), (0, LANES - QK_ROPE)))
        k_ctx, v_ctx = _kv_expand(cache_ckv[:, l].reshape(DEC_BATCH * PAST, KV_LORA), kr_ctx, lw)
        yb_s = _attention(q, [(k_ctx, v_ctx, PAST, 0), (k, v, DEC_SEQ, T_P)],
                          DEC_BATCH, DEC_SEQ, 512, T_P)
        yc_s = _fourier(zmix, dft_s, lw, DEC_BATCH, DEC_SEQ, T_P)
        ya_s, yd_s, _ = _seq_mixers(zmix, state_lru[:, l], lw, DEC_BATCH, DEC_SEQ, T_P)

        x1, h2lin, route, prob, tab, counts = _post(
            x, mod[l], (ya_p, yb_p, yc_p, yd_p), (ya_s, yb_s, yc_s, yd_s), lw, tri, upper)
        dest, sched, loc, top_p, chunk_src, n_chunks = _routing_tables(
            route, prob, counts[0, :N_EXP], tab)
        xs = _gather_rows(dest, sched[1], h2lin)
        ys = _experts(l, sched, xs, params)
        x = _combine(loc, top_p, chunk_src, n_chunks, ys, x1, mod[l], final_g,
                     final=(l == DEPTH - 1))

    y_prompt, y_sample = x
    return (y_prompt.reshape(BATCH, SEQ, D), y_sample.reshape(DEC_BATCH, DEC_SEQ, D),
            jnp.stack(new_ckv, axis=1), jnp.stack(new_krope, axis=1), jnp.stack(new_lru, axis=1))
```

```python
import functools

import numpy as np
import jax
import jax.numpy as jnp
from jax import lax
from jax.experimental import pallas as pl
from jax.experimental.pallas import tpu as pltpu

F32 = jnp.float32
BF16 = jnp.bfloat16
I32 = jnp.int32

D = 1024
BATCH, SEQ = 32, 256
DEC_BATCH, DEC_SEQ, PAST = 2, 2048, 512
T_P = BATCH * SEQ
T_S = DEC_BATCH * DEC_SEQ
T = T_P + T_S
DEPTH = 2
GRID_W = 64
GW = 256
Q_LORA, KV_LORA, QK_NOPE, QK_ROPE, V_DIM, HEADS = 192, 128, 64, 32, 64, 4
HEAD_PAD = 128
ROPE_BASE = 10000.0
Q_SCALE = float((QK_NOPE + QK_ROPE) ** -0.5 * np.log2(np.e))
POOL_WINDOWS = (2, 4, 8, 16)
LRU_C = 8.0
N_EXP, TOP_K = 32, 4
LIMIT, ALPHA = 7.0, 1.702
EPS = 1e-6

LANES = 128
SUBLANES = 8
MIB = 1024 * 1024
V7X_VMEM_BYTES = 64 * MIB
VMEM_LIMIT = V7X_VMEM_BYTES * 7 // 8

TM = 512
N_PT = T_P // TM
TILES_PER_DEC = DEC_SEQ // TM
TM_E = 512
EXPERT_ROWS = (TM_E, TM_E // 2, TM_E // 4)
N_SLOTS = T * TOP_K + N_EXP * TM_E
NB_E = N_SLOTS // TM_E
ROW_F = D // LANES
TG = 2 * TM_E
GATHER_VMEM_LIMIT = 4 * (T * D + 2 * TG * D) + 4 * MIB
CH_C = 32
MAX_CH = TM * TOP_K // CH_C + N_EXP
GATHER_UNROLL = 16
PROMPT_GROUP = 4
SEQ_PAD = 32
FRONT = 8

W_COLS = 1536
COL_Q, COL_KV, COL_KR = 1024, 1280, 1408


_PROMPT_ROWS = pl.BlockSpec((TM, D), lambda i: (jnp.minimum(i, N_PT - 1), 0))
_SAMPLE_ROWS = pl.BlockSpec((TM, D), lambda i: (jnp.maximum(i - N_PT, 0), 0))


def _cparams(sem, vmem=VMEM_LIMIT):
    return pltpu.CompilerParams(dimension_semantics=sem, vmem_limit_bytes=vmem)


def _bdot(a, b):
    return jnp.dot(a.astype(BF16), b.astype(BF16), preferred_element_type=F32)


def _split(a):
    hi = a.astype(BF16)
    lo = (a - hi.astype(F32)).astype(BF16)
    return hi, lo


def _dot3(a, b):
    ah, al = _split(a)
    bh, bl = _split(b)
    d = functools.partial(jnp.dot, preferred_element_type=F32)
    return d(ah, bh) + (d(al, bh) + d(ah, bl))


def _rms(x, n=None):
    n = x.shape[-1] if n is None else n
    return x * lax.rsqrt(jnp.sum(x * x, axis=-1, keepdims=True) * (1.0 / n) + EPS)


def _neg_expm1_double(x):
    t = jnp.tanh(x)
    return -2.0 * t / (1.0 - t)


def _sigmoid(x):
    return 0.5 * jnp.tanh(0.5 * x) + 0.5


def _tile_row(i):
    return jnp.where(i >= N_PT, 1 + (i - N_PT) // TILES_PER_DEC, 0)


def _mod_kernel(c_ref, w_ref, b_ref, o_ref):
    s = jax.nn.silu(c_ref[...])
    o_ref[0] = _dot3(s, w_ref[0]) + b_ref[0]


def _modulation(cvec, w_mod, b_mod):
    tn = 1536
    n = 6 * D
    return pl.pallas_call(
        _mod_kernel,
        out_shape=jax.ShapeDtypeStruct((DEPTH, SUBLANES, n), F32),
        grid=(DEPTH, n // tn),
        in_specs=[pl.BlockSpec((SUBLANES, D), lambda l, j: (0, 0)),
                  pl.BlockSpec((1, D, tn), lambda l, j: (l, 0, j)),
                  pl.BlockSpec((1, 1, tn), lambda l, j: (l, 0, j))],
        out_specs=pl.BlockSpec((1, SUBLANES, tn), lambda l, j: (l, 0, j)),
        compiler_params=_cparams(("parallel", "parallel")),
        name="modulation",
    )(cvec, w_mod, b_mod.reshape(DEPTH, 1, n))


def _front_kernel(xp_ref, xs_ref, mod_ref, g_ref, w_ref, qg_ref, wuq_ref, kvg_ref, wkn_ref, wkr_ref,
                  wv_ref, cq_ref, saq_ref, sbq_ref, ck_ref, sak_ref, sbk_ref,
                  zmix_ref, q_ref, k_ref, v_ref, ckv_ref, kr_ref):
    i = pl.program_id(0)
    row = _tile_row(i)
    shift1 = mod_ref[pl.ds(row, 1), 0:D]
    scale1 = mod_ref[pl.ds(row, 1), D:2 * D]
    h = _rms(jnp.where(i >= N_PT, xs_ref[...], xp_ref[...])) * g_ref[...]
    h = h * (1.0 + scale1) + shift1
    z = jnp.dot(h.astype(BF16), w_ref[...], preferred_element_type=F32)
    zmix_ref[...] = z[:, 0:4 * GW]
    qn = _rms(z[:, COL_Q:COL_Q + 2 * LANES], Q_LORA) * qg_ref[...]
    q = _bdot(qn, wuq_ref[...])
    ckv = _rms(z[:, COL_KV:COL_KV + KV_LORA]) * kvg_ref[...]
    ckv_b = ckv.astype(BF16)
    v_ref[...] = jnp.dot(ckv_b, wv_ref[...], preferred_element_type=F32).astype(BF16)
    kn = jnp.dot(ckv_b, wkn_ref[...], preferred_element_type=F32)
    kr = z[:, COL_KR:COL_KR + LANES]

    @pl.when(i < N_PT)
    def _():
        q_ref[...] = (q * Q_SCALE).astype(BF16)
        ckv_ref[...] = ckv
        kr_ref[...] = kr[:, 0:QK_ROPE]
        k_ref[...] = (kn + _bdot(kr, wkr_ref[...])).astype(BF16)

    @pl.when(i >= N_PT)
    def _():
        wq = HEADS * HEAD_PAD
        half = QK_ROPE // 2
        qr = (q * cq_ref[...] + pltpu.roll(q, wq - half, 1) * saq_ref[...]
              + pltpu.roll(q, half, 1) * sbq_ref[...])
        krr = (kr * ck_ref[...] + pltpu.roll(kr, LANES - half, 1) * sak_ref[...]
               + pltpu.roll(kr, half, 1) * sbk_ref[...])
        q_ref[...] = (qr * Q_SCALE).astype(BF16)
        k_ref[...] = (kn + _bdot(krr, wkr_ref[...])).astype(BF16)


def _front(x, mod_l, lw, rope):
    full = lambda shape: pl.BlockSpec(shape, lambda i: (0,) * len(shape))
    rows = lambda w: pl.BlockSpec((TM, w), lambda i: (i, 0))
    prompt_rows = lambda w: pl.BlockSpec((TM, w), lambda i: (jnp.minimum(i, N_PT - 1), 0))
    rope_rows = lambda w: pl.BlockSpec(
        (TM, w), lambda i: (jnp.maximum(i - N_PT, 0) % TILES_PER_DEC, 0))
    wq = HEADS * HEAD_PAD
    return pl.pallas_call(
        _front_kernel,
        out_shape=(jax.ShapeDtypeStruct((T, 4 * GW), F32),
                   jax.ShapeDtypeStruct((T, wq), BF16),
                   jax.ShapeDtypeStruct((T, wq), BF16),
                   jax.ShapeDtypeStruct((T, HEADS * V_DIM), BF16),
                   jax.ShapeDtypeStruct((T_P, KV_LORA), F32),
                   jax.ShapeDtypeStruct((T_P, QK_ROPE), F32)),
        grid=(T // TM,),
        in_specs=[_PROMPT_ROWS, _SAMPLE_ROWS, full((SUBLANES, 6 * D)), full((1, D)), full((D, W_COLS)),
                  full((1, 2 * LANES)), full((2 * LANES, wq)), full((1, KV_LORA)),
                  full((KV_LORA, wq)), full((LANES, wq)), full((KV_LORA, HEADS * V_DIM)),
                  rope_rows(wq), rope_rows(wq), rope_rows(wq),
                  rope_rows(LANES), rope_rows(LANES), rope_rows(LANES)],
        out_specs=(rows(4 * GW), rows(wq), rows(wq), rows(HEADS * V_DIM), prompt_rows(KV_LORA),
                   prompt_rows(QK_ROPE)),
        compiler_params=_cparams(("arbitrary",)),
        name="front",
    )(*x, mod_l, lw["attn_g"], lw["w_in"], lw["q_g"], lw["w_uq"], lw["kv_g"], lw["w_kn"],
      lw["w_kr"], lw["w_v"], *rope)


def _kvexp_kernel(ckv_ref, kr_ref, wkn_ref, wkr_ref, wv_ref, k_ref, v_ref):
    ckv_b = ckv_ref[...].astype(BF16)
    v_ref[...] = jnp.dot(ckv_b, wv_ref[...], preferred_element_type=F32).astype(BF16)
    kn = jnp.dot(ckv_b, wkn_ref[...], preferred_element_type=F32)
    k_ref[...] = (kn + _bdot(kr_ref[...], wkr_ref[...])).astype(BF16)


def _kv_expand(ckv, kr_pad, lw):
    n = ckv.shape[0]
    wq = HEADS * HEAD_PAD
    full = lambda shape: pl.BlockSpec(shape, lambda i: (0,) * len(shape))
    return pl.pallas_call(
        _kvexp_kernel,
        out_shape=(jax.ShapeDtypeStruct((n, wq), BF16),
                   jax.ShapeDtypeStruct((n, HEADS * V_DIM), BF16)),
        grid=(1,),
        in_specs=[full((n, KV_LORA)), full((n, LANES)), full((KV_LORA, wq)), full((LANES, wq)),
                  full((KV_LORA, HEADS * V_DIM))],
        out_specs=(full((n, wq)), full((n, HEADS * V_DIM))),
        compiler_params=_cparams(("arbitrary",)),
        name="kv_expand",
    )(ckv, kr_pad, lw["w_kn"], lw["w_kr"], lw["w_v"])


def _attn_kernel(q_ref, *refs, group, tq, seg_keys):
    o_ref = refs[-1]
    dims = (((1,), (1,)), ((), ()))
    for g in range(group):
        outs = []
        for h in range(HEADS):
            qh = q_ref[g * tq:(g + 1) * tq, h * HEAD_PAD:(h + 1) * HEAD_PAD]
            scores = []
            for si, nk in enumerate(seg_keys):
                kh = refs[2 * si][g * nk:(g + 1) * nk, h * HEAD_PAD:(h + 1) * HEAD_PAD]
                scores.append(lax.dot_general(qh, kh, dims, preferred_element_type=F32))
            m = functools.reduce(jnp.maximum, [jnp.max(s, axis=-1, keepdims=True) for s in scores])
            l, acc = None, None
            for si, (nk, s) in enumerate(zip(seg_keys, scores)):
                p = jnp.exp2(s - m)
                vh = refs[2 * si + 1][g * nk:(g + 1) * nk, h * V_DIM:(h + 1) * V_DIM]
                part = jnp.dot(p.astype(BF16), vh, preferred_element_type=F32)
                psum = jnp.sum(p, axis=-1, keepdims=True)
                l, acc = (psum, part) if l is None else (l + psum, acc + part)
            outs.append(acc / l)
        o_ref[g * tq:(g + 1) * tq, :] = jnp.concatenate(outs, axis=-1)


def _attention(q, segments, n_batch, seq, tq, row0, group=1):
    nq = seq // tq
    wq = HEADS * HEAD_PAD
    in_specs = [pl.BlockSpec((group * tq, wq), lambda b, i: (row0 // (group * tq) + b * nq + i, 0))]
    operands = [q]
    for k_rows, v_rows, nk, first in segments:
        index = lambda b, i, base=first // (group * nk): (base + b, 0)
        in_specs += [pl.BlockSpec((group * nk, wq), index),
                     pl.BlockSpec((group * nk, HEADS * V_DIM), index)]
        operands += [k_rows, v_rows]
    return pl.pallas_call(
        functools.partial(_attn_kernel, group=group, tq=tq,
                          seg_keys=tuple(seg[2] for seg in segments)),
        out_shape=jax.ShapeDtypeStruct((n_batch * seq, GW), F32),
        grid=(n_batch // group, nq),
        in_specs=in_specs,
        out_specs=pl.BlockSpec((group * tq, GW), lambda b, i: (b * nq + i, 0)),
        compiler_params=_cparams(("parallel", "parallel")),
        name="attention",
    )(*operands)


def _fourier_kernel(z_ref, c_ref, s_ref, bdc_ref, bds_ref, w_ref, o_ref, xc_ref, xs_ref, *,
                    norm, group, seq, ts):
    b = pl.program_id(1)

    @pl.when(pl.program_id(0) == 0)
    def _():
        xb = z_ref[...].astype(BF16)
        xc_ref[b] = _bdot(xb, bdc_ref[...]).astype(BF16)
        xs_ref[b] = _bdot(xb, bds_ref[...]).astype(BF16)

    cb, sb = c_ref[...].astype(BF16), s_ref[...].astype(BF16)
    for g in range(group):
        f = (jnp.dot(cb, xc_ref[b, g * seq:(g + 1) * seq, :], preferred_element_type=F32)
             - jnp.dot(sb, xs_ref[b, g * seq:(g + 1) * seq, :], preferred_element_type=F32)) * norm
        o_ref[g * ts:(g + 1) * ts, :] = _bdot(f, w_ref[...])


def _fourier(zmix, consts, lw, n_batch, seq, row0, group=1):
    ts = min(seq, 512)
    nj = seq // ts
    assert group == 1 or nj == 1
    nb = n_batch // group
    cmat, smat = consts
    full = lambda shape: pl.BlockSpec(shape, lambda j, b: (0,) * len(shape))
    z_spec = pl.BlockSpec((group * seq, GW),
                          lambda j, b: (row0 // (group * seq) + jnp.where(j == 0, b, nb - 1), 1))
    return pl.pallas_call(
        functools.partial(_fourier_kernel, norm=float((seq * (GW // 4)) ** -0.5), group=group,
                          seq=seq, ts=ts),
        out_shape=jax.ShapeDtypeStruct((n_batch * seq, GW), F32),
        grid=(nj, nb),
        in_specs=[z_spec,
                  pl.BlockSpec((ts, seq), lambda j, b: (j, 0)),
                  pl.BlockSpec((ts, seq), lambda j, b: (j, 0)),
                  full((GW, GW)), full((GW, GW)), full((GW, GW))],
        out_specs=pl.BlockSpec((group * ts, GW), lambda j, b: (b * nj + j, 0)),
        scratch_shapes=[pltpu.VMEM((nb, group * seq, GW), BF16),
                        pltpu.VMEM((nb, group * seq, GW), BF16)],
        compiler_params=_cparams(("arbitrary", "arbitrary")),
        name="fourier",
    )(zmix, cmat, smat, lw["bd_c"], lw["bd_s"], lw["w_f"])


def _seq_kernel(za_ref, zr_ref, zg_ref, h0_ref, icnt_ref, wp_ref, ps_ref, cw_ref, cb_ref, wa_ref,
                ba_ref, wx_ref, bx_ref, lam_ref, ya_ref, yd_ref, st_ref,
                pa_ref, pb_ref, xp_ref, a_ref, b_ref, *, seq):
    n = seq + SEQ_PAD
    span = seq + 2 * FRONT
    zeros_pad = jnp.zeros((n, GW), F32)

    za = za_ref[...]
    pa_ref[...] = zeros_pad
    pb_ref[...] = zeros_pad
    pa_ref[FRONT:FRONT + seq, :] = za
    pb_ref[0:span, :] = pa_ref[0:span, :] + pa_ref[1:span + 1, :]
    win2 = pb_ref[FRONT - 1:FRONT - 1 + seq, :]
    pa_ref[0:span, :] = pb_ref[0:span, :] + pb_ref[2:span + 2, :]
    win4 = pa_ref[FRONT - 2:FRONT - 2 + seq, :]
    pb_ref[0:span, :] = pa_ref[0:span, :] + pa_ref[4:span + 4, :]
    win8 = pb_ref[FRONT - 4:FRONT - 4 + seq, :]
    pa_ref[0:span, :] = pb_ref[0:span, :] + pb_ref[8:span + 8, :]
    win16 = pa_ref[FRONT - 8:FRONT - 8 + seq, :]
    grp = lax.broadcasted_iota(I32, (1, GW), 1) // (GW // 4)
    win = jnp.where(grp == 0, win2, jnp.where(grp == 1, win4, jnp.where(grp == 2, win8, win16)))
    dlt = win * icnt_ref[...] - za
    ya_ref[...] = _bdot(dlt, wp_ref[...]) * ps_ref[...]

    xp_ref[...] = zeros_pad
    xp_ref[FRONT:FRONT + seq, :] = zr_ref[...]
    xc = cb_ref[...] + cw_ref[0:1, :] * xp_ref[FRONT - 2:FRONT - 2 + seq, :]
    for kk in range(1, 4):
        xc = xc + cw_ref[kk:kk + 1, :] * xp_ref[FRONT - 2 + kk:FRONT - 2 + kk + seq, :]
    xcb = xc.astype(BF16)
    n_grp = seq // SUBLANES
    sub = lax.broadcasted_iota(I32, (n_grp, SUBLANES, GW), 1)

    for d in range(2):
        r = _sigmoid(jnp.dot(xcb, wa_ref[d], preferred_element_type=F32) + ba_ref[d])
        ig = _sigmoid(jnp.dot(xcb, wx_ref[d], preferred_element_type=F32) + bx_ref[d])
        log_a = (-LRU_C) * r * jax.nn.softplus(-lam_ref[d])
        a = jnp.exp(log_a).reshape(n_grp, SUBLANES, GW)
        b = (jnp.sqrt(_neg_expm1_double(log_a)) * (ig * xc)).reshape(n_grp, SUBLANES, GW)
        for k in (1, 2, 4):
            shift = k if d == 0 else SUBLANES - k
            m = sub >= k if d == 0 else sub < SUBLANES - k
            ap, bp = pltpu.roll(a, shift, 1), pltpu.roll(b, shift, 1)
            b = jnp.where(m, a * bp + b, b)
            a = jnp.where(m, a * ap, a)
        a_ref[d] = a.reshape(seq, GW)
        b_ref[d] = b.reshape(seq, GW)

    def step(g, carry):
        fwd, bwd = carry
        off_f = pl.multiple_of(g * SUBLANES, SUBLANES)
        off_b = pl.multiple_of((n_grp - 1 - g) * SUBLANES, SUBLANES)
        hf = a_ref[0, pl.ds(off_f, SUBLANES), :] * fwd + b_ref[0, pl.ds(off_f, SUBLANES), :]
        hb = a_ref[1, pl.ds(off_b, SUBLANES), :] * bwd + b_ref[1, pl.ds(off_b, SUBLANES), :]
        b_ref[0, pl.ds(off_f, SUBLANES), :] = hf
        b_ref[1, pl.ds(off_b, SUBLANES), :] = hb
        return (jnp.broadcast_to(hf[SUBLANES - 1:SUBLANES, :], (SUBLANES, GW)),
                jnp.broadcast_to(hb[0:1, :], (SUBLANES, GW)))

    start = tuple(jnp.broadcast_to(h0_ref[0, d:d + 1, :], (SUBLANES, GW)) for d in range(2))
    last_f, last_b = lax.fori_loop(0, n_grp, step, start)
    st_ref[0, 0:1, :] = last_f[0:1, :]
    st_ref[0, 1:2, :] = last_b[0:1, :]
    yd_ref[...] = (b_ref[0] + b_ref[1]) * jax.nn.gelu(zg_ref[...])


def _pool_inverse_counts(seq):
    pos = np.arange(seq)[:, None]
    half = np.repeat(np.array(POOL_WINDOWS) // 2, GW // len(POOL_WINDOWS))[None, :]
    cnt = np.minimum(pos + half, seq) - np.maximum(pos - half, 0)
    return jnp.asarray(1.0 / cnt, F32)


def _seq_mixers(zmix, h0, lw, n_batch, seq, row0):
    full = lambda shape: pl.BlockSpec(shape, lambda b: (0,) * len(shape))
    col = lambda c: pl.BlockSpec((seq, GW), lambda b: (row0 // seq + b, c))
    out_rows = pl.BlockSpec((seq, GW), lambda b: (b, 0))
    pad = pltpu.VMEM((seq + SEQ_PAD, GW), F32)
    return pl.pallas_call(
        functools.partial(_seq_kernel, seq=seq),
        out_shape=(jax.ShapeDtypeStruct((n_batch * seq, GW), F32),
                   jax.ShapeDtypeStruct((n_batch * seq, GW), F32),
                   jax.ShapeDtypeStruct((n_batch, 2, GW), F32)),
        grid=(n_batch,),
        in_specs=[col(0), col(2), col(3), pl.BlockSpec((1, 2, GW), lambda b: (b, 0, 0)),
                  full((seq, GW)), full((GW, GW)), full((1, GW)), full((4, GW)), full((1, GW)),
                  full((2, GW, GW)), full((2, 1, GW)), full((2, GW, GW)), full((2, 1, GW)),
                  full((2, 1, GW))],
        out_specs=(out_rows, out_rows, pl.BlockSpec((1, 2, GW), lambda b: (b, 0, 0))),
        scratch_shapes=[pad, pad, pad, pltpu.VMEM((2, seq, GW), F32), pltpu.VMEM((2, seq, GW), F32)],
        compiler_params=_cparams(("parallel",)),
        name="seq_mixers",
    )(zmix, zmix, zmix, h0, _pool_inverse_counts(seq), lw["bd_pool"], lw["pool_scale"], lw["conv_w"],
      lw["conv_b"], lw["bd_wa"], lw["lru_ba"], lw["bd_wx"], lw["lru_bx"], lw["lru_lam"])


def _post_kernel(xp_ref, xs_ref, mod_ref, yap_ref, ybp_ref, ycp_ref, ydp_ref, yas_ref, ybs_ref, ycs_ref,
                 yds_ref, og_ref, wo_ref, fg_ref, rw_ref, rb_ref, tri_ref, upper_ref,
                 x1_ref, h2p_ref, route_ref, prob_ref, tab_ref, cnt_ref, carry_ref):
    i = pl.program_id(0)
    is_s = i >= N_PT
    row = _tile_row(i)
    gate1 = mod_ref[pl.ds(row, 1), 2 * D:3 * D]
    shift2 = mod_ref[pl.ds(row, 1), 3 * D:4 * D]
    scale2 = mod_ref[pl.ds(row, 1), 4 * D:5 * D]

    @pl.when(i == 0)
    def _():
        carry_ref[...] = jnp.zeros_like(carry_ref)

    groups = []
    for gi, (p_ref, s_ref) in enumerate(((yap_ref, yas_ref), (ybp_ref, ybs_ref),
                                         (ycp_ref, ycs_ref), (ydp_ref, yds_ref))):
        y = jnp.where(is_s, s_ref[...], p_ref[...])
        groups.append((_rms(y) * og_ref[gi:gi + 1, :]).astype(BF16))
    ycat = jnp.concatenate(groups, axis=-1)
    x = jnp.where(is_s, xs_ref[...], xp_ref[...])
    x1 = x + gate1 * jnp.dot(ycat, wo_ref[...], preferred_element_type=F32)
    x1_ref[...] = x1
    h2 = _rms(x1) * fg_ref[...]
    h2 = h2 * (1.0 + scale2) + shift2

    for j in range(ROW_F):
        h2p_ref[pl.ds(j, TM, stride=ROW_F), :] = h2[:, j * LANES:(j + 1) * LANES]

    h_hi, h_lo = _split(h2)
    both = jnp.dot(h_hi, rw_ref[...], preferred_element_type=F32)
    cross = both[:, LANES:2 * LANES] + jnp.dot(h_lo, rw_ref[:, 0:LANES], preferred_element_type=F32)
    logits = both[:, 0:LANES] + cross + rb_ref[...]
    lane = lax.broadcasted_iota(I32, (TM, LANES), 1)
    lane_f = lane.astype(F32)
    neg = jnp.float32(-jnp.inf)
    cur = jnp.where(lane < N_EXP, logits, neg)
    sel, vals, idxs = [], [], []
    for _ in range(TOP_K):
        m = jnp.max(cur, axis=-1, keepdims=True)
        idx = jnp.min(jnp.where(cur == m, lane_f, float(LANES)), axis=-1, keepdims=True)
        hit = lane_f == idx
        sel.append(hit)
        vals.append(m)
        idxs.append(idx)
        cur = jnp.where(hit, neg, cur)
    exps = [jnp.exp(v - vals[0]) for v in vals]
    denom = exps[0] + exps[1] + exps[2] + exps[3]
    onehot = jnp.where(sel[0] | sel[1] | sel[2] | sel[3], 1.0, 0.0)
    cum_l = jnp.dot(tri_ref[...], onehot.astype(BF16), preferred_element_type=F32)
    before = carry_ref[0:1, :]
    cum = cum_l + before
    n_tile = cum_l[TM - 1:TM, :]
    nch = jnp.floor((n_tile + (CH_C - 1.0)) * (1.0 / CH_C))
    nch8 = jnp.broadcast_to(nch, (SUBLANES, LANES))
    base = jnp.dot(nch8.astype(BF16), upper_ref[...], preferred_element_type=F32)[0:1, :]
    pk = jnp.zeros((TM, LANES), I32)
    pf = jnp.zeros((TM, LANES), F32)
    for k in range(TOP_K):
        rank = jnp.sum(jnp.where(sel[k], cum - 1.0, 0.0), axis=-1, keepdims=True).astype(I32)
        local = jnp.sum(jnp.where(sel[k], (base * CH_C + cum_l - 1.0) * ROW_F, 0.0), axis=-1,
                        keepdims=True).astype(I32)
        pk = jnp.where(lane == k, idxs[k].astype(I32), pk)
        pk = jnp.where(lane == TOP_K + k, rank, pk)
        pk = jnp.where(lane == 2 * TOP_K + k, local, pk)
        pf = jnp.where(lane == k, exps[k] / denom, pf)
    route_ref[...] = pk.T[0:4 * TOP_K, :]
    prob_ref[...] = pf.T[0:SUBLANES, :]
    row8 = lax.broadcasted_iota(I32, (SUBLANES, LANES), 0)
    tab_ref[...] = jnp.where(row8 == 0, before, jnp.where(row8 == 1, nch, base))
    new_carry = jnp.broadcast_to(cum[TM - 1:TM, :], (SUBLANES, LANES))
    carry_ref[...] = new_carry
    cnt_ref[...] = new_carry


def _post(x, mod_l, ys_prompt, ys_sample, lw, tri, upper):
    full = lambda shape: pl.BlockSpec(shape, lambda i: (0,) * len(shape))
    rows = lambda w: pl.BlockSpec((TM, w), lambda i: (i, 0))
    prow = pl.BlockSpec((TM, GW), lambda i: (jnp.minimum(i, N_PT - 1), 0))
    srow = pl.BlockSpec((TM, GW), lambda i: (jnp.maximum(i - N_PT, 0), 0))
    return pl.pallas_call(
        _post_kernel,
        out_shape=(jax.ShapeDtypeStruct((T, D), F32),
                   jax.ShapeDtypeStruct((T * ROW_F, LANES), F32),
                   jax.ShapeDtypeStruct((4 * TOP_K, T), I32),
                   jax.ShapeDtypeStruct((SUBLANES, T), F32),
                   jax.ShapeDtypeStruct((T // TM * SUBLANES, LANES), F32),
                   jax.ShapeDtypeStruct((SUBLANES, LANES), F32)),
        grid=(T // TM,),
        in_specs=[_PROMPT_ROWS, _SAMPLE_ROWS, full((SUBLANES, 6 * D)),
                  prow, prow, prow, prow, srow, srow, srow, srow,
                  full((4, GW)), full((D, D)), full((1, D)), full((D, 2 * LANES)), full((1, LANES)),
                  full((TM, TM)), full((LANES, LANES))],
        out_specs=(rows(D), pl.BlockSpec((TM * ROW_F, LANES), lambda i: (i, 0)),
                   pl.BlockSpec((4 * TOP_K, TM), lambda i: (0, i)),
                   pl.BlockSpec((SUBLANES, TM), lambda i: (0, i)),
                   pl.BlockSpec((SUBLANES, LANES), lambda i: (i, 0)), full((SUBLANES, LANES))),
        scratch_shapes=[pltpu.VMEM((SUBLANES, LANES), F32)],
        compiler_params=_cparams(("arbitrary",)),
        name="post",
    )(*x, mod_l, *ys_prompt, *ys_sample, lw["out_g"], lw["w_out"], lw["ffn_g"], lw["router_w"],
      lw["router_b"], tri, upper)


def _gather_kernel(dest_ref, nused_ref, zeros_hbm, src_hbm, o_ref, stok_ref, src_ref, sem, src_sem):
    i = pl.program_id(0)

    @pl.when(i == 0)
    def _():
        load = pltpu.make_async_copy(src_hbm, src_ref, src_sem)
        load.start()
        init = pltpu.make_async_copy(zeros_hbm, stok_ref, sem)
        init.start()
        init.wait()

        def scatter(t8, c):
            row0 = t8 * (SUBLANES * ROW_F)
            for k in range(TOP_K):
                for u in range(SUBLANES):
                    stok_ref[dest_ref[k * T + t8 * SUBLANES + u]] = row0 + u * ROW_F
            return c
        lax.fori_loop(0, T // SUBLANES, scatter, 0)
        load.wait()

    live = i * (TG // TM_E) < nused_ref[0]

    @pl.when(live)
    def _():
        def rows(r16, c):
            r0 = pl.multiple_of(r16 * GATHER_UNROLL, GATHER_UNROLL)
            for u in range(GATHER_UNROLL):
                src = pl.multiple_of(stok_ref[i * TG + r0 + u], ROW_F)
                dst = pl.multiple_of(r0 * ROW_F, GATHER_UNROLL * ROW_F) + u * ROW_F
                o_ref[pl.ds(dst, ROW_F), :] = src_ref[pl.ds(src, ROW_F), :]
            return c
        lax.fori_loop(0, TG // GATHER_UNROLL, rows, 0)

    @pl.when(jnp.logical_not(live))
    def _():
        o_ref[...] = jnp.zeros_like(o_ref)


def _gather_rows(dest, n_used, h2lin):
    return pl.pallas_call(
        _gather_kernel,
        out_shape=jax.ShapeDtypeStruct((N_SLOTS * ROW_F, LANES), F32),
        grid_spec=pltpu.PrefetchScalarGridSpec(
            num_scalar_prefetch=2,
            grid=(N_SLOTS // TG,),
            in_specs=[pl.BlockSpec(memory_space=pl.ANY), pl.BlockSpec(memory_space=pl.ANY)],
            out_specs=pl.BlockSpec((TG * ROW_F, LANES), lambda i, d, nu: (i, 0)),
            scratch_shapes=[pltpu.SMEM((N_SLOTS,), I32), pltpu.VMEM((T * ROW_F, LANES), F32),
                            pltpu.SemaphoreType.DMA(()), pltpu.SemaphoreType.DMA(())]),
        compiler_params=_cparams(("arbitrary",), GATHER_VMEM_LIMIT),
        name="moe_gather",
    )(dest, n_used, jnp.zeros((N_SLOTS,), I32), h2lin)


def _expert_rows(n, x_ref, w_ref, slot, bg_ref, bu_ref, bd_ref, o_ref):
    x = jnp.concatenate([x_ref[pl.ds(j, n, stride=ROW_F), :].astype(BF16) for j in range(ROW_F)],
                        axis=-1)
    g = jnp.dot(x, w_ref[slot, 0].astype(BF16), preferred_element_type=F32) + bg_ref[0, 0]
    u = jnp.dot(x, w_ref[slot, 1].astype(BF16), preferred_element_type=F32) + bu_ref[0, 0]
    g = jnp.minimum(g, LIMIT)
    u = jnp.clip(u, -LIMIT, LIMIT)
    act = (u + 1.0) * (g * jax.nn.sigmoid(ALPHA * g))
    y = jnp.dot(act.astype(BF16), w_ref[slot, 2].astype(BF16),
                preferred_element_type=F32) + bd_ref[0, 0]
    for j in range(ROW_F):
        o_ref[pl.ds(j, n, stride=ROW_F), :] = y[:, j * LANES:(j + 1) * LANES]
    if n < TM_E:
        o_ref[n * ROW_F:TM_E * ROW_F, :] = jnp.zeros(((TM_E - n) * ROW_F, LANES), F32)


def _expert_kernel(be_ref, nused_ref, valid_ref, first_ref, next_ref, par_ref,
                   x_ref, wg_hbm, bg_ref, wu_hbm, bu_ref, wd_hbm, bd_ref, o_ref, w_ref, sems, *, layer):
    i = pl.program_id(0)
    ib = jnp.minimum(i, NB_E - 1)
    live = i < nused_ref[0]
    valid = valid_ref[ib]
    expert = be_ref[ib]
    slot = par_ref[ib]

    def fetch(e, s):
        return [pltpu.make_async_copy(src.at[layer, e], w_ref.at[s, m], sems.at[s, m])
                for m, src in enumerate((wg_hbm, wu_hbm, wd_hbm))]

    @pl.when(i == 0)
    def _():
        for cp in fetch(expert, slot):
            cp.start()

    @pl.when(live & (first_ref[ib] == 1))
    def _():
        @pl.when(next_ref[ib] >= 0)
        def _():
            for cp in fetch(next_ref[ib], 1 - slot):
                cp.start()

        for cp in fetch(expert, slot):
            cp.wait()

    for n_idx, n in enumerate(EXPERT_ROWS):
        fits = valid <= n
        if n_idx + 1 < len(EXPERT_ROWS):
            fits = fits & (valid > EXPERT_ROWS[n_idx + 1])

        @pl.when(live & fits)
        def _(n=n):
            _expert_rows(n, x_ref, w_ref, slot, bg_ref, bu_ref, bd_ref, o_ref)

    @pl.when(jnp.logical_not(live))
    def _():
        o_ref[...] = jnp.zeros_like(o_ref)


def _experts(l, sched, xs, p):
    last = NB_E - 1
    bspec = pl.BlockSpec((1, 1, 1, D), lambda i, be, *_: (l, be[jnp.minimum(i, last)], 0, 0))
    hbm = pl.BlockSpec(memory_space=pl.ANY)
    bias = lambda b: b.reshape(DEPTH, N_EXP, 1, D)
    return pl.pallas_call(
        functools.partial(_expert_kernel, layer=l),
        out_shape=jax.ShapeDtypeStruct(((NB_E + 1) * TM_E * ROW_F, LANES), F32),
        grid_spec=pltpu.PrefetchScalarGridSpec(
            num_scalar_prefetch=len(sched),
            grid=(NB_E + 1,),
            in_specs=[pl.BlockSpec((TM_E * ROW_F, LANES), lambda i, *_: (jnp.minimum(i, last), 0)),
                      hbm, bspec, hbm, bspec, hbm, bspec],
            out_specs=pl.BlockSpec((TM_E * ROW_F, LANES), lambda i, *_: (i, 0)),
            scratch_shapes=[pltpu.VMEM((2, 3, D, D), F32), pltpu.SemaphoreType.DMA((2, 3))]),
        compiler_params=_cparams(("arbitrary",)),
        name="moe_experts",
    )(*sched, xs, p["w_gate"], bias(p["b_gate"]), p["w_up"], bias(p["b_up"]),
      p["w_down"], bias(p["b_down"]))


def _combine_kernel(loc_ref, p_ref, csrc_ref, nch_ref, ys_hbm, x_ref, mod_ref, fg_ref, *rest, final):
    *outs, buf_ref, acc_ref, sems = rest
    tb = pl.program_id(0)
    n_tb = pl.num_programs(0)
    slot = tb % 2
    row = _tile_row(tb)
    gate2 = mod_ref[pl.ds(row, 1), 5 * D:6 * D]
    chunk_rows = CH_C * ROW_F

    def chunk_copy(t, c, s):
        src = pl.multiple_of(csrc_ref[t * MAX_CH + c] * ROW_F, ROW_F)
        dst = pl.multiple_of(c * chunk_rows, chunk_rows)
        return pltpu.make_async_copy(ys_hbm.at[pl.ds(src, chunk_rows)],
                                     buf_ref.at[s, pl.ds(dst, chunk_rows)], sems.at[s])

    def issue(t, s):
        def body(c, carry):
            chunk_copy(t, c, s).start()
            return carry
        lax.fori_loop(0, nch_ref[t], body, 0)

    @pl.when(tb == 0)
    def _():
        issue(0, 0)

    @pl.when(tb + 1 < n_tb)
    def _():
        issue(tb + 1, 1 - slot)

    def drain(c, carry):
        chunk_copy(tb, c, slot).wait()
        return carry
    lax.fori_loop(0, nch_ref[tb], drain, 0)

    def tokens(r8, carry):
        r0 = pl.multiple_of(r8 * SUBLANES, SUBLANES)
        for u in range(SUBLANES):
            tok = tb * TM + r0 + u
            acc = None
            for k in range(TOP_K):
                off = pl.multiple_of(loc_ref[k * T + tok], ROW_F)
                term = buf_ref[slot, pl.ds(off, ROW_F), :] * p_ref[k * T + tok]
                acc = term if acc is None else acc + term
            dst = pl.multiple_of(r0 * ROW_F, SUBLANES * ROW_F) + u * ROW_F
            acc_ref[pl.ds(dst, ROW_F), :] = acc
        return carry
    lax.fori_loop(0, TM // SUBLANES, tokens, 0)

    moe = jnp.concatenate([acc_ref[pl.ds(j, TM, stride=ROW_F), :] for j in range(ROW_F)], axis=-1)
    x2 = x_ref[...] + gate2 * moe
    if final:
        x2 = _rms(x2) * fg_ref[...]
    prompt_ref, sample_ref = outs

    @pl.when(tb < N_PT)
    def _():
        prompt_ref[...] = x2

    @pl.when(tb >= N_PT)
    def _():
        sample_ref[...] = x2


def _combine(loc, top_p, chunk_src, n_chunks, ys, x1, mod_l, final_g, final):
    full = lambda shape: pl.BlockSpec(shape, lambda i, *_: (0,) * len(shape))
    rows = lambda w: pl.BlockSpec((TM, w), lambda i, *_: (i, 0))
    out_shape = (jax.ShapeDtypeStruct((T_P, D), F32), jax.ShapeDtypeStruct((T_S, D), F32))
    out_specs = (pl.BlockSpec((TM, D), lambda i, *_: (jnp.minimum(i, N_PT - 1), 0)),
                 pl.BlockSpec((TM, D), lambda i, *_: (jnp.maximum(i - N_PT, 0), 0)))
    return pl.pallas_call(
        functools.partial(_combine_kernel, final=final),
        out_shape=out_shape,
        grid_spec=pltpu.PrefetchScalarGridSpec(
            num_scalar_prefetch=4,
            grid=(T // TM,),
            in_specs=[pl.BlockSpec(memory_space=pl.ANY), rows(D), full((SUBLANES, 6 * D)),
                      full((1, D))],
            out_specs=out_specs,
            scratch_shapes=[pltpu.VMEM((2, MAX_CH * CH_C * ROW_F, LANES), F32),
                            pltpu.VMEM((TM * ROW_F, LANES), F32),
                            pltpu.SemaphoreType.DMA((2,))]),
        compiler_params=_cparams(("arbitrary",)),
        name="moe_combine",
    )(loc, top_p, chunk_src, n_chunks, ys, x1, mod_l, final_g)


def _rope_tables():
    rows = DEC_SEQ // GRID_W
    r = np.repeat(np.arange(rows, dtype=np.float64), GRID_W)
    c = np.tile(np.arange(GRID_W, dtype=np.float64), rows)
    n_freq = QK_ROPE // 4
    inv = (np.float32(ROPE_BASE) ** (-np.arange(n_freq, dtype=np.float32) / n_freq)).astype(np.float64)
    ang = np.concatenate([r[:, None] * inv, c[:, None] * inv], axis=-1).astype(np.float32)
    cos, sin = np.cos(ang.astype(np.float64)), np.sin(ang.astype(np.float64))
    half = QK_ROPE // 2

    def place(width, start):
        cf = np.ones((DEC_SEQ, width), np.float32)
        sa = np.zeros((DEC_SEQ, width), np.float32)
        sb = np.zeros((DEC_SEQ, width), np.float32)
        for s0 in start:
            cf[:, s0:s0 + half] = cos
            cf[:, s0 + half:s0 + 2 * half] = cos
            sa[:, s0:s0 + half] = -sin
            sb[:, s0 + half:s0 + 2 * half] = sin
        return jnp.asarray(cf), jnp.asarray(sa), jnp.asarray(sb)

    return (*place(HEADS * HEAD_PAD, [h * HEAD_PAD + QK_NOPE for h in range(HEADS)]),
            *place(LANES, [0]))


def _dft_tables(seq):
    kn = (np.arange(seq, dtype=np.int64)[:, None] * np.arange(seq, dtype=np.int64)[None, :]) % seq
    ang = 2.0 * np.pi * kn.astype(np.float64) / seq
    return jnp.asarray(np.cos(ang), F32), jnp.asarray(np.sin(ang), F32)


def _block_diag(blocks):
    g, n, _ = blocks.shape
    eye = jnp.eye(g, dtype=blocks.dtype)
    return jnp.einsum("gij,gh->gihj", blocks, eye).reshape(g * n, g * n)


def _layer_weights(l, p):
    w_in = p["w_in"][l]
    za, zq, zkv, zkr, zf, zr, zg = jnp.split(
        w_in, np.cumsum([GW, Q_LORA, KV_LORA, QK_ROPE, GW, GW])[:], axis=1)
    zpad = lambda n: jnp.zeros((D, n), F32)
    w_cols = jnp.concatenate([za, zf, zr, zg, zq, zpad(2 * LANES - Q_LORA), zkv, zkr,
                              zpad(LANES - QK_ROPE)], axis=1).astype(BF16)
    wq = HEADS * HEAD_PAD
    w_uq = p["w_uq"][l].reshape(Q_LORA, HEADS, QK_NOPE + QK_ROPE)
    w_uq = jnp.pad(w_uq, ((0, 2 * LANES - Q_LORA), (0, 0), (0, HEAD_PAD - QK_NOPE - QK_ROPE)))
    w_ukv = p["w_ukv"][l].reshape(KV_LORA, HEADS, QK_NOPE + V_DIM)
    w_kn = jnp.pad(w_ukv[:, :, :QK_NOPE], ((0, 0), (0, 0), (0, HEAD_PAD - QK_NOPE)))
    w_v = w_ukv[:, :, QK_NOPE:]
    place = np.zeros((LANES, HEADS, HEAD_PAD), np.float32)
    for h in range(HEADS):
        place[np.arange(QK_ROPE), h, QK_NOPE + np.arange(QK_ROPE)] = 1.0
    c64 = np.arange(GW // 4, dtype=np.int64)
    ang = 2.0 * np.pi * ((c64[:, None] * c64[None, :]) % (GW // 4)).astype(np.float64) / (GW // 4)
    four = lambda m: jnp.asarray(np.broadcast_to(m, (4,) + m.shape), F32)
    router_hi, router_lo = _split(jnp.pad(p["router_w"][l], ((0, 0), (0, LANES - N_EXP))))
    router_w = jnp.concatenate([router_hi, router_lo], axis=1)
    router_b = jnp.pad(p["router_b"][l], (0, LANES - N_EXP)).reshape(1, LANES)
    return {
        "attn_g": p["attn_norm_g"][l].reshape(1, D),
        "w_in": w_cols,
        "q_g": jnp.pad(p["q_norm_g"][l], (0, 2 * LANES - Q_LORA)).reshape(1, 2 * LANES),
        "w_uq": w_uq.reshape(2 * LANES, wq).astype(BF16),
        "kv_g": p["kv_norm_g"][l].reshape(1, KV_LORA),
        "w_kn": w_kn.reshape(KV_LORA, wq).astype(BF16),
        "w_kr": jnp.asarray(place.reshape(LANES, wq), BF16),
        "w_v": w_v.reshape(KV_LORA, HEADS * V_DIM).astype(BF16),
        "bd_c": _block_diag(four(np.cos(ang))),
        "bd_s": _block_diag(four(np.sin(ang))),
        "w_f": p["fourier_w"][l].astype(BF16),
        "bd_pool": _block_diag(p["pool_w"][l]).astype(BF16),
        "pool_scale": p["pool_scale"][l].reshape(1, GW),
        "conv_w": p["conv_w"][l],
        "conv_b": p["conv_b"][l].reshape(1, GW),
        "bd_wa": jnp.stack([_block_diag(p["lru_wa"][l, d]) for d in range(2)]).astype(BF16),
        "bd_wx": jnp.stack([_block_diag(p["lru_wx"][l, d]) for d in range(2)]).astype(BF16),
        "lru_ba": p["lru_ba"][l].reshape(2, 1, GW),
        "lru_bx": p["lru_bx"][l].reshape(2, 1, GW),
        "lru_lam": p["lru_lambda"][l].reshape(2, 1, GW),
        "out_g": p["out_norm_g"][l],
        "w_out": p["w_out"][l].astype(BF16),
        "ffn_g": p["ffn_norm_g"][l].reshape(1, D),
        "router_w": router_w,
        "router_b": router_b,
    }


def _routing_tables(route, prob, counts, tab):
    top_e, rank = route[0:TOP_K], route[TOP_K:2 * TOP_K]
    top_p = prob[0:TOP_K].reshape(-1)
    loc = route[2 * TOP_K:3 * TOP_K].reshape(-1)
    counts = counts.astype(I32)
    padded = (counts + TM_E - 1) // TM_E * TM_E
    pad_ends = jnp.cumsum(padded)
    pad_starts = pad_ends - padded
    experts = jnp.arange(N_EXP, dtype=I32)
    onehot = top_e[:, :, None] == experts
    dest = (jnp.sum(jnp.where(onehot, pad_starts, 0), axis=-1) + rank).reshape(T * TOP_K)
    n_used = (pad_ends[-1] // TM_E).astype(I32)
    blk = jnp.minimum(jnp.arange(NB_E, dtype=I32), n_used - 1) * TM_E
    block_e = jnp.minimum(jnp.sum(pad_ends[None, :] <= blk[:, None], axis=-1), N_EXP - 1).astype(I32)
    of_block = lambda a: jnp.sum(jnp.where(block_e[:, None] == experts, a, 0), axis=-1)
    valid = jnp.clip(of_block(pad_starts + counts) - blk, 0, TM_E).astype(I32)
    first = (blk == of_block(pad_starts)).astype(I32)
    later = (experts[None, :] > experts[:, None]) & (counts[None, :] > 0)
    next_of = jnp.min(jnp.where(later, experts[None, :], N_EXP), axis=-1)
    next_e = of_block(jnp.where(next_of < N_EXP, next_of, -1)).astype(I32)
    parity = (of_block(jnp.cumsum((counts > 0).astype(I32))) % 2).astype(I32)
    sched = (block_e, n_used.reshape(1), valid, first, next_e, parity)
    tab = tab.reshape(T // TM, SUBLANES, LANES)[:, :, :N_EXP].astype(I32)
    before, nch, base = tab[:, 0], tab[:, 1], tab[:, 2]
    run_start = pad_starts[None, :] + before
    ends = base + nch
    ci = jnp.arange(MAX_CH, dtype=I32)
    e_of = jnp.minimum(jnp.sum(ends[:, None, :] <= ci[None, :, None], axis=-1), N_EXP - 1)
    pick = lambda a: jnp.sum(jnp.where(e_of[:, :, None] == experts, a[:, None, :], 0), axis=-1)
    chunk_src = jnp.clip(pick(run_start) + (ci[None, :] - pick(base)) * CH_C, 0, N_SLOTS)
    return dest, sched, loc, top_p, chunk_src.reshape(-1), ends[:, N_EXP - 1]


def kernel(x_prompt, x_sample, cache_ckv, cache_krope, state_lru, c, c_ctx, w_mod, b_mod, attn_norm_g, w_in, pool_w, pool_scale, q_norm_g, w_uq, kv_norm_g, w_ukv, fourier_w, conv_w, conv_b, lru_wa, lru_ba, lru_wx, lru_bx, lru_lambda, out_norm_g, w_out, ffn_norm_g, router_w, router_b, w_gate, b_gate, w_up, b_up, w_down, b_down, final_norm_g):
    params = dict(attn_norm_g=attn_norm_g, w_in=w_in, pool_w=pool_w, pool_scale=pool_scale,
                  q_norm_g=q_norm_g, w_uq=w_uq, kv_norm_g=kv_norm_g, w_ukv=w_ukv,
                  fourier_w=fourier_w, conv_w=conv_w, conv_b=conv_b, lru_wa=lru_wa, lru_ba=lru_ba,
                  lru_wx=lru_wx, lru_bx=lru_bx, lru_lambda=lru_lambda, out_norm_g=out_norm_g,
                  w_out=w_out, ffn_norm_g=ffn_norm_g, router_w=router_w, router_b=router_b,
                  w_gate=w_gate, b_gate=b_gate, w_up=w_up, b_up=b_up, w_down=w_down, b_down=b_down)
    x = (x_prompt.reshape(T_P, D), x_sample.reshape(T_S, D))
    cvec = jnp.concatenate([c_ctx[None, :], c, jnp.zeros((SUBLANES - 1 - DEC_BATCH, D), F32)], axis=0)
    mod = _modulation(cvec, w_mod, b_mod)
    rope = _rope_tables()
    dft_p, dft_s = _dft_tables(SEQ), _dft_tables(DEC_SEQ)
    tri = jnp.asarray(np.tril(np.ones((TM, TM), np.float32)), BF16)
    upper = jnp.asarray(np.triu(np.ones((LANES, LANES), np.float32), 1), BF16)
    final_g = final_norm_g.reshape(1, D)
    h0_prompt = jnp.zeros((BATCH, 2, GW), F32)

    new_ckv, new_krope, new_lru = [], [], []
    for l in range(DEPTH):
        lw = _layer_weights(l, params)
        zmix, q, k, v, ckv, kr = _front(x, mod[l], lw, rope)
        new_ckv.append(ckv.reshape(BATCH, SEQ, KV_LORA))
        new_krope.append(kr.reshape(BATCH, SEQ, QK_ROPE))

        yb_p = _attention(q, [(k, v, SEQ, 0)], BATCH, SEQ, SEQ, 0, group=PROMPT_GROUP)
        yc_p = _fourier(zmix, dft_p, lw, BATCH, SEQ, 0, group=PROMPT_GROUP)
        ya_p, yd_p, st_p = _seq_mixers(zmix, h0_prompt, lw, BATCH, SEQ, 0)
        new_lru.append(st_p)

        kr_ctx = jnp.pad(cache_krope[:, l].reshape(DEC_BATCH * PAST, QK_ROPE),
                         ((0, 0), (0, LANES - QK_ROPE)))
        k_ctx, v_ctx = _kv_expand(cache_ckv[:, l].reshape(DEC_BATCH * PAST, KV_LORA), kr_ctx, lw)
        yb_s = _attention(q, [(k_ctx, v_ctx, PAST, 0), (k, v, DEC_SEQ, T_P)],
                          DEC_BATCH, DEC_SEQ, 512, T_P)
        yc_s = _fourier(zmix, dft_s, lw, DEC_BATCH, DEC_SEQ, T_P)
        ya_s, yd_s, _ = _seq_mixers(zmix, state_lru[:, l], lw, DEC_BATCH, DEC_SEQ, T_P)

        x1, h2lin, route, prob, tab, counts = _post(
            x, mod[l], (ya_p, yb_p, yc_p, yd_p), (ya_s, yb_s, yc_s, yd_s), lw, tri, upper)
        dest, sched, loc, top_p, chunk_src, n_chunks = _routing_tables(
            route, prob, counts[0, :N_EXP], tab)
        xs = _gather_rows(dest, sched[1], h2lin)
        ys = _experts(l, sched, xs, params)
        x = _combine(loc, top_p, chunk_src, n_chunks, ys, x1, mod[l], final_g,
                     final=(l == DEPTH - 1))

    y_prompt, y_sample = x
    return (y_prompt.reshape(BATCH, SEQ, D), y_sample.reshape(DEC_BATCH, DEC_SEQ, D),
            jnp.stack(new_ckv, axis=1), jnp.stack(new_krope, axis=1), jnp.stack(new_lru, axis=1))
```

```python
import functools

import numpy as np
import jax
import jax.numpy as jnp
from jax import lax
from jax.experimental import pallas as pl
from jax.experimental.pallas import tpu as pltpu

F32 = jnp.float32
BF16 = jnp.bfloat16
I32 = jnp.int32

D = 1024
BATCH, SEQ = 32, 256
DEC_BATCH, DEC_SEQ, PAST = 2, 2048, 512
T_P = BATCH * SEQ
T_S = DEC_BATCH * DEC_SEQ
T = T_P + T_S
DEPTH = 2
GRID_W = 64
GW = 256
Q_LORA, KV_LORA, QK_NOPE, QK_ROPE, V_DIM, HEADS = 192, 128, 64, 32, 64, 4
HEAD_PAD = 128
ROPE_BASE = 10000.0
Q_SCALE = float((QK_NOPE + QK_ROPE) ** -0.5 * np.log2(np.e))
POOL_WINDOWS = (2, 4, 8, 16)
LRU_C = 8.0
N_EXP, TOP_K = 32, 4
LIMIT, ALPHA = 7.0, 1.702
EPS = 1e-6

LANES = 128
SUBLANES = 8
MIB = 1024 * 1024
V7X_VMEM_BYTES = 64 * MIB
VMEM_LIMIT = V7X_VMEM_BYTES * 7 // 8

TM = 512
N_PT = T_P // TM
TILES_PER_DEC = DEC_SEQ // TM
TM_E = 512
EXPERT_ROWS = (TM_E, TM_E // 2, TM_E // 4)
N_SLOTS = T * TOP_K + N_EXP * TM_E
NB_E = N_SLOTS // TM_E
ROW_F = D // LANES
TG = 2 * TM_E
GATHER_VMEM_LIMIT = 4 * (T * D + 2 * TG * D) + 4 * MIB
CH_C = 32
MAX_CH = TM * TOP_K // CH_C + N_EXP
GATHER_UNROLL = 32
COMBINE_UNROLL = 16
PROMPT_GROUP = 4
SEQ_PAD = 32
FRONT = 8

W_COLS = 1536
COL_Q, COL_KV, COL_KR = 1024, 1280, 1408


_PROMPT_ROWS = pl.BlockSpec((TM, D), lambda i: (jnp.minimum(i, N_PT - 1), 0))
_SAMPLE_ROWS = pl.BlockSpec((TM, D), lambda i: (jnp.maximum(i - N_PT, 0), 0))


def _cparams(sem, vmem=VMEM_LIMIT):
    return pltpu.CompilerParams(dimension_semantics=sem, vmem_limit_bytes=vmem)


def _bdot(a, b):
    return jnp.dot(a.astype(BF16), b.astype(BF16), preferred_element_type=F32)


def _split(a):
    hi = a.astype(BF16)
    lo = (a - hi.astype(F32)).astype(BF16)
    return hi, lo


def _dot3(a, b):
    ah, al = _split(a)
    bh, bl = _split(b)
    d = functools.partial(jnp.dot, preferred_element_type=F32)
    return d(ah, bh) + (d(al, bh) + d(ah, bl))


def _rms(x, n=None):
    n = x.shape[-1] if n is None else n
    return x * lax.rsqrt(jnp.sum(x * x, axis=-1, keepdims=True) * (1.0 / n) + EPS)


def _neg_expm1_double(x):
    t = jnp.tanh(x)
    return -2.0 * t / (1.0 - t)


def _sigmoid(x):
    return 0.5 * jnp.tanh(0.5 * x) + 0.5


def _tile_row(i):
    return jnp.where(i >= N_PT, 1 + (i - N_PT) // TILES_PER_DEC, 0)


def _mod_kernel(c_ref, w_ref, b_ref, o_ref):
    s = jax.nn.silu(c_ref[...])
    o_ref[0] = _dot3(s, w_ref[0]) + b_ref[0]


def _modulation(cvec, w_mod, b_mod):
    tn = 1536
    n = 6 * D
    return pl.pallas_call(
        _mod_kernel,
        out_shape=jax.ShapeDtypeStruct((DEPTH, SUBLANES, n), F32),
        grid=(DEPTH, n // tn),
        in_specs=[pl.BlockSpec((SUBLANES, D), lambda l, j: (0, 0)),
                  pl.BlockSpec((1, D, tn), lambda l, j: (l, 0, j)),
                  pl.BlockSpec((1, 1, tn), lambda l, j: (l, 0, j))],
        out_specs=pl.BlockSpec((1, SUBLANES, tn), lambda l, j: (l, 0, j)),
        compiler_params=_cparams(("parallel", "parallel")),
        name="modulation",
    )(cvec, w_mod, b_mod.reshape(DEPTH, 1, n))


def _front_kernel(xp_ref, xs_ref, mod_ref, g_ref, w_ref, qg_ref, wuq_ref, kvg_ref, wkn_ref, wkr_ref,
                  wv_ref, cq_ref, saq_ref, sbq_ref, ck_ref, sak_ref, sbk_ref,
                  zmix_ref, q_ref, k_ref, v_ref, ckv_ref, kr_ref):
    i = pl.program_id(0)
    row = _tile_row(i)
    shift1 = mod_ref[pl.ds(row, 1), 0:D]
    scale1 = mod_ref[pl.ds(row, 1), D:2 * D]
    h = _rms(jnp.where(i >= N_PT, xs_ref[...], xp_ref[...])) * g_ref[...]
    h = h * (1.0 + scale1) + shift1
    z = jnp.dot(h.astype(BF16), w_ref[...], preferred_element_type=F32)
    zmix_ref[...] = z[:, 0:4 * GW]
    qn = _rms(z[:, COL_Q:COL_Q + 2 * LANES], Q_LORA) * qg_ref[...]
    q = _bdot(qn, wuq_ref[...])
    ckv = _rms(z[:, COL_KV:COL_KV + KV_LORA]) * kvg_ref[...]
    ckv_b = ckv.astype(BF16)
    v_ref[...] = jnp.dot(ckv_b, wv_ref[...], preferred_element_type=F32).astype(BF16)
    kn = jnp.dot(ckv_b, wkn_ref[...], preferred_element_type=F32)
    kr = z[:, COL_KR:COL_KR + LANES]

    @pl.when(i < N_PT)
    def _():
        q_ref[...] = (q * Q_SCALE).astype(BF16)
        ckv_ref[...] = ckv
        kr_ref[...] = kr[:, 0:QK_ROPE]
        k_ref[...] = (kn + _bdot(kr, wkr_ref[...])).astype(BF16)

    @pl.when(i >= N_PT)
    def _():
        wq = HEADS * HEAD_PAD
        half = QK_ROPE // 2
        qr = (q * cq_ref[...] + pltpu.roll(q, wq - half, 1) * saq_ref[...]
              + pltpu.roll(q, half, 1) * sbq_ref[...])
        krr = (kr * ck_ref[...] + pltpu.roll(kr, LANES - half, 1) * sak_ref[...]
               + pltpu.roll(kr, half, 1) * sbk_ref[...])
        q_ref[...] = (qr * Q_SCALE).astype(BF16)
        k_ref[...] = (kn + _bdot(krr, wkr_ref[...])).astype(BF16)


def _front(x, mod_l, lw, rope):
    full = lambda shape: pl.BlockSpec(shape, lambda i: (0,) * len(shape))
    rows = lambda w: pl.BlockSpec((TM, w), lambda i: (i, 0))
    prompt_rows = lambda w: pl.BlockSpec((TM, w), lambda i: (jnp.minimum(i, N_PT - 1), 0))
    rope_rows = lambda w: pl.BlockSpec(
        (TM, w), lambda i: (jnp.maximum(i - N_PT, 0) % TILES_PER_DEC, 0))
    wq = HEADS * HEAD_PAD
    return pl.pallas_call(
        _front_kernel,
        out_shape=(jax.ShapeDtypeStruct((T, 4 * GW), F32),
                   jax.ShapeDtypeStruct((T, wq), BF16),
                   jax.ShapeDtypeStruct((T, wq), BF16),
                   jax.ShapeDtypeStruct((T, HEADS * V_DIM), BF16),
                   jax.ShapeDtypeStruct((T_P, KV_LORA), F32),
                   jax.ShapeDtypeStruct((T_P, QK_ROPE), F32)),
        grid=(T // TM,),
        in_specs=[_PROMPT_ROWS, _SAMPLE_ROWS, full((SUBLANES, 6 * D)), full((1, D)), full((D, W_COLS)),
                  full((1, 2 * LANES)), full((2 * LANES, wq)), full((1, KV_LORA)),
                  full((KV_LORA, wq)), full((LANES, wq)), full((KV_LORA, HEADS * V_DIM)),
                  rope_rows(wq), rope_rows(wq), rope_rows(wq),
                  rope_rows(LANES), rope_rows(LANES), rope_rows(LANES)],
        out_specs=(rows(4 * GW), rows(wq), rows(wq), rows(HEADS * V_DIM), prompt_rows(KV_LORA),
                   prompt_rows(QK_ROPE)),
        compiler_params=_cparams(("arbitrary",)),
        name="front",
    )(*x, mod_l, lw["attn_g"], lw["w_in"], lw["q_g"], lw["w_uq"], lw["kv_g"], lw["w_kn"],
      lw["w_kr"], lw["w_v"], *rope)


def _kvexp_kernel(ckv_ref, kr_ref, wkn_ref, wkr_ref, wv_ref, k_ref, v_ref):
    ckv_b = ckv_ref[...].astype(BF16)
    v_ref[...] = jnp.dot(ckv_b, wv_ref[...], preferred_element_type=F32).astype(BF16)
    kn = jnp.dot(ckv_b, wkn_ref[...], preferred_element_type=F32)
    k_ref[...] = (kn + _bdot(kr_ref[...], wkr_ref[...])).astype(BF16)


def _kv_expand(ckv, kr_pad, lw):
    n = ckv.shape[0]
    wq = HEADS * HEAD_PAD
    full = lambda shape: pl.BlockSpec(shape, lambda i: (0,) * len(shape))
    return pl.pallas_call(
        _kvexp_kernel,
        out_shape=(jax.ShapeDtypeStruct((n, wq), BF16),
                   jax.ShapeDtypeStruct((n, HEADS * V_DIM), BF16)),
        grid=(1,),
        in_specs=[full((n, KV_LORA)), full((n, LANES)), full((KV_LORA, wq)), full((LANES, wq)),
                  full((KV_LORA, HEADS * V_DIM))],
        out_specs=(full((n, wq)), full((n, HEADS * V_DIM))),
        compiler_params=_cparams(("arbitrary",)),
        name="kv_expand",
    )(ckv, kr_pad, lw["w_kn"], lw["w_kr"], lw["w_v"])


def _attn_kernel(q_ref, *refs, group, tq, seg_keys):
    o_ref = refs[-1]
    dims = (((1,), (1,)), ((), ()))
    for g in range(group):
        outs = []
        for h in range(HEADS):
            qh = q_ref[g * tq:(g + 1) * tq, h * HEAD_PAD:(h + 1) * HEAD_PAD]
            scores = []
            for si, nk in enumerate(seg_keys):
                kh = refs[2 * si][g * nk:(g + 1) * nk, h * HEAD_PAD:(h + 1) * HEAD_PAD]
                scores.append(lax.dot_general(qh, kh, dims, preferred_element_type=F32))
            m = functools.reduce(jnp.maximum, [jnp.max(s, axis=-1, keepdims=True) for s in scores])
            l, acc = None, None
            for si, (nk, s) in enumerate(zip(seg_keys, scores)):
                p = jnp.exp2(s - m)
                vh = refs[2 * si + 1][g * nk:(g + 1) * nk, h * V_DIM:(h + 1) * V_DIM]
                part = jnp.dot(p.astype(BF16), vh, preferred_element_type=F32)
                psum = jnp.sum(p, axis=-1, keepdims=True)
                l, acc = (psum, part) if l is None else (l + psum, acc + part)
            outs.append(acc / l)
        o_ref[g * tq:(g + 1) * tq, :] = jnp.concatenate(outs, axis=-1)


def _attention(q, segments, n_batch, seq, tq, row0, group=1):
    nq = seq // tq
    wq = HEADS * HEAD_PAD
    in_specs = [pl.BlockSpec((group * tq, wq), lambda b, i: (row0 // (group * tq) + b * nq + i, 0))]
    operands = [q]
    for k_rows, v_rows, nk, first in segments:
        index = lambda b, i, base=first // (group * nk): (base + b, 0)
        in_specs += [pl.BlockSpec((group * nk, wq), index),
                     pl.BlockSpec((group * nk, HEADS * V_DIM), index)]
        operands += [k_rows, v_rows]
    return pl.pallas_call(
        functools.partial(_attn_kernel, group=group, tq=tq,
                          seg_keys=tuple(seg[2] for seg in segments)),
        out_shape=jax.ShapeDtypeStruct((n_batch * seq, GW), F32),
        grid=(n_batch // group, nq),
        in_specs=in_specs,
        out_specs=pl.BlockSpec((group * tq, GW), lambda b, i: (b * nq + i, 0)),
        compiler_params=_cparams(("parallel", "parallel")),
        name="attention",
    )(*operands)


def _fourier_kernel(z_ref, c_ref, s_ref, bdc_ref, bds_ref, w_ref, o_ref, xc_ref, xs_ref, *,
                    norm, group, seq, ts):
    b = pl.program_id(1)

    @pl.when(pl.program_id(0) == 0)
    def _():
        xb = z_ref[...].astype(BF16)
        xc_ref[b] = _bdot(xb, bdc_ref[...]).astype(BF16)
        xs_ref[b] = _bdot(xb, bds_ref[...]).astype(BF16)

    cb, sb = c_ref[...].astype(BF16), s_ref[...].astype(BF16)
    for g in range(group):
        f = (jnp.dot(cb, xc_ref[b, g * seq:(g + 1) * seq, :], preferred_element_type=F32)
             - jnp.dot(sb, xs_ref[b, g * seq:(g + 1) * seq, :], preferred_element_type=F32)) * norm
        o_ref[g * ts:(g + 1) * ts, :] = _bdot(f, w_ref[...])


def _fourier(zmix, consts, lw, n_batch, seq, row0, group=1):
    ts = min(seq, 512)
    nj = seq // ts
    assert group == 1 or nj == 1
    nb = n_batch // group
    cmat, smat = consts
    full = lambda shape: pl.BlockSpec(shape, lambda j, b: (0,) * len(shape))
    z_spec = pl.BlockSpec((group * seq, GW),
                          lambda j, b: (row0 // (group * seq) + jnp.where(j == 0, b, nb - 1), 1))
    return pl.pallas_call(
        functools.partial(_fourier_kernel, norm=float((seq * (GW // 4)) ** -0.5), group=group,
                          seq=seq, ts=ts),
        out_shape=jax.ShapeDtypeStruct((n_batch * seq, GW), F32),
        grid=(nj, nb),
        in_specs=[z_spec,
                  pl.BlockSpec((ts, seq), lambda j, b: (j, 0)),
                  pl.BlockSpec((ts, seq), lambda j, b: (j, 0)),
                  full((GW, GW)), full((GW, GW)), full((GW, GW))],
        out_specs=pl.BlockSpec((group * ts, GW), lambda j, b: (b * nj + j, 0)),
        scratch_shapes=[pltpu.VMEM((nb, group * seq, GW), BF16),
                        pltpu.VMEM((nb, group * seq, GW), BF16)],
        compiler_params=_cparams(("arbitrary", "arbitrary")),
        name="fourier",
    )(zmix, cmat, smat, lw["bd_c"], lw["bd_s"], lw["w_f"])


def _seq_kernel(za_ref, zr_ref, zg_ref, h0_ref, icnt_ref, wp_ref, ps_ref, cw_ref, cb_ref, wa_ref,
                ba_ref, wx_ref, bx_ref, lam_ref, ya_ref, yd_ref, st_ref,
                pa_ref, pb_ref, xp_ref, a_ref, b_ref, *, seq):
    n = seq + SEQ_PAD
    span = seq + 2 * FRONT
    zeros_pad = jnp.zeros((n, GW), F32)

    za = za_ref[...]
    pa_ref[...] = zeros_pad
    pb_ref[...] = zeros_pad
    pa_ref[FRONT:FRONT + seq, :] = za
    pb_ref[0:span, :] = pa_ref[0:span, :] + pa_ref[1:span + 1, :]
    win2 = pb_ref[FRONT - 1:FRONT - 1 + seq, :]
    pa_ref[0:span, :] = pb_ref[0:span, :] + pb_ref[2:span + 2, :]
    win4 = pa_ref[FRONT - 2:FRONT - 2 + seq, :]
    pb_ref[0:span, :] = pa_ref[0:span, :] + pa_ref[4:span + 4, :]
    win8 = pb_ref[FRONT - 4:FRONT - 4 + seq, :]
    pa_ref[0:span, :] = pb_ref[0:span, :] + pb_ref[8:span + 8, :]
    win16 = pa_ref[FRONT - 8:FRONT - 8 + seq, :]
    grp = lax.broadcasted_iota(I32, (1, GW), 1) // (GW // 4)
    win = jnp.where(grp == 0, win2, jnp.where(grp == 1, win4, jnp.where(grp == 2, win8, win16)))
    dlt = win * icnt_ref[...] - za
    ya_ref[...] = _bdot(dlt, wp_ref[...]) * ps_ref[...]

    xp_ref[...] = zeros_pad
    xp_ref[FRONT:FRONT + seq, :] = zr_ref[...]
    xc = cb_ref[...] + cw_ref[0:1, :] * xp_ref[FRONT - 2:FRONT - 2 + seq, :]
    for kk in range(1, 4):
        xc = xc + cw_ref[kk:kk + 1, :] * xp_ref[FRONT - 2 + kk:FRONT - 2 + kk + seq, :]
    xcb = xc.astype(BF16)
    n_grp = seq // SUBLANES
    sub = lax.broadcasted_iota(I32, (n_grp, SUBLANES, GW), 1)

    for d in range(2):
        r = _sigmoid(jnp.dot(xcb, wa_ref[d], preferred_element_type=F32) + ba_ref[d])
        ig = _sigmoid(jnp.dot(xcb, wx_ref[d], preferred_element_type=F32) + bx_ref[d])
        log_a = (-LRU_C) * r * jax.nn.softplus(-lam_ref[d])
        a = jnp.exp(log_a).reshape(n_grp, SUBLANES, GW)
        b = (jnp.sqrt(_neg_expm1_double(log_a)) * (ig * xc)).reshape(n_grp, SUBLANES, GW)
        for k in (1, 2, 4):
            shift = k if d == 0 else SUBLANES - k
            m = sub >= k if d == 0 else sub < SUBLANES - k
            ap, bp = pltpu.roll(a, shift, 1), pltpu.roll(b, shift, 1)
            b = jnp.where(m, a * bp + b, b)
            a = jnp.where(m, a * ap, a)
        a_ref[d] = a.reshape(seq, GW)
        b_ref[d] = b.reshape(seq, GW)

    def step(g, carry):
        fwd, bwd = carry
        off_f = pl.multiple_of(g * SUBLANES, SUBLANES)
        off_b = pl.multiple_of((n_grp - 1 - g) * SUBLANES, SUBLANES)
        hf = a_ref[0, pl.ds(off_f, SUBLANES), :] * fwd + b_ref[0, pl.ds(off_f, SUBLANES), :]
        hb = a_ref[1, pl.ds(off_b, SUBLANES), :] * bwd + b_ref[1, pl.ds(off_b, SUBLANES), :]
        b_ref[0, pl.ds(off_f, SUBLANES), :] = hf
        b_ref[1, pl.ds(off_b, SUBLANES), :] = hb
        return (jnp.broadcast_to(hf[SUBLANES - 1:SUBLANES, :], (SUBLANES, GW)),
                jnp.broadcast_to(hb[0:1, :], (SUBLANES, GW)))

    start = tuple(jnp.broadcast_to(h0_ref[0, d:d + 1, :], (SUBLANES, GW)) for d in range(2))
    last_f, last_b = lax.fori_loop(0, n_grp, step, start)
    st_ref[0, 0:1, :] = last_f[0:1, :]
    st_ref[0, 1:2, :] = last_b[0:1, :]
    yd_ref[...] = (b_ref[0] + b_ref[1]) * jax.nn.gelu(zg_ref[...])


def _pool_inverse_counts(seq):
    pos = np.arange(seq)[:, None]
    half = np.repeat(np.array(POOL_WINDOWS) // 2, GW // len(POOL_WINDOWS))[None, :]
    cnt = np.minimum(pos + half, seq) - np.maximum(pos - half, 0)
    return jnp.asarray(1.0 / cnt, F32)


def _seq_mixers(zmix, h0, lw, n_batch, seq, row0):
    full = lambda shape: pl.BlockSpec(shape, lambda b: (0,) * len(shape))
    col = lambda c: pl.BlockSpec((seq, GW), lambda b: (row0 // seq + b, c))
    out_rows = pl.BlockSpec((seq, GW), lambda b: (b, 0))
    pad = pltpu.VMEM((seq + SEQ_PAD, GW), F32)
    return pl.pallas_call(
        functools.partial(_seq_kernel, seq=seq),
        out_shape=(jax.ShapeDtypeStruct((n_batch * seq, GW), F32),
                   jax.ShapeDtypeStruct((n_batch * seq, GW), F32),
                   jax.ShapeDtypeStruct((n_batch, 2, GW), F32)),
        grid=(n_batch,),
        in_specs=[col(0), col(2), col(3), pl.BlockSpec((1, 2, GW), lambda b: (b, 0, 0)),
                  full((seq, GW)), full((GW, GW)), full((1, GW)), full((4, GW)), full((1, GW)),
                  full((2, GW, GW)), full((2, 1, GW)), full((2, GW, GW)), full((2, 1, GW)),
                  full((2, 1, GW))],
        out_specs=(out_rows, out_rows, pl.BlockSpec((1, 2, GW), lambda b: (b, 0, 0))),
        scratch_shapes=[pad, pad, pad, pltpu.VMEM((2, seq, GW), F32), pltpu.VMEM((2, seq, GW), F32)],
        compiler_params=_cparams(("parallel",)),
        name="seq_mixers",
    )(zmix, zmix, zmix, h0, _pool_inverse_counts(seq), lw["bd_pool"], lw["pool_scale"], lw["conv_w"],
      lw["conv_b"], lw["bd_wa"], lw["lru_ba"], lw["bd_wx"], lw["lru_bx"], lw["lru_lam"])


def _post_kernel(xp_ref, xs_ref, mod_ref, yap_ref, ybp_ref, ycp_ref, ydp_ref, yas_ref, ybs_ref, ycs_ref,
                 yds_ref, og_ref, wo_ref, fg_ref, rw_ref, rb_ref, tri_ref, upper_ref,
                 x1_ref, h2p_ref, route_ref, prob_ref, tab_ref, cnt_ref, carry_ref):
    i = pl.program_id(0)
    is_s = i >= N_PT
    row = _tile_row(i)
    gate1 = mod_ref[pl.ds(row, 1), 2 * D:3 * D]
    shift2 = mod_ref[pl.ds(row, 1), 3 * D:4 * D]
    scale2 = mod_ref[pl.ds(row, 1), 4 * D:5 * D]

    @pl.when(i == 0)
    def _():
        carry_ref[...] = jnp.zeros_like(carry_ref)

    groups = []
    for gi, (p_ref, s_ref) in enumerate(((yap_ref, yas_ref), (ybp_ref, ybs_ref),
                                         (ycp_ref, ycs_ref), (ydp_ref, yds_ref))):
        y = jnp.where(is_s, s_ref[...], p_ref[...])
        groups.append((_rms(y) * og_ref[gi:gi + 1, :]).astype(BF16))
    ycat = jnp.concatenate(groups, axis=-1)
    x = jnp.where(is_s, xs_ref[...], xp_ref[...])
    x1 = x + gate1 * jnp.dot(ycat, wo_ref[...], preferred_element_type=F32)
    x1_ref[...] = x1
    h2 = _rms(x1) * fg_ref[...]
    h2 = h2 * (1.0 + scale2) + shift2

    for j in range(ROW_F):
        h2p_ref[pl.ds(j, TM, stride=ROW_F), :] = h2[:, j * LANES:(j + 1) * LANES]

    h_hi, h_lo = _split(h2)
    both = jnp.dot(h_hi, rw_ref[...], preferred_element_type=F32)
    cross = both[:, LANES:2 * LANES] + jnp.dot(h_lo, rw_ref[:, 0:LANES], preferred_element_type=F32)
    logits = both[:, 0:LANES] + cross + rb_ref[...]
    lane = lax.broadcasted_iota(I32, (TM, LANES), 1)
    lane_f = lane.astype(F32)
    neg = jnp.float32(-jnp.inf)
    cur = jnp.where(lane < N_EXP, logits, neg)
    sel, vals, idxs = [], [], []
    for _ in range(TOP_K):
        m = jnp.max(cur, axis=-1, keepdims=True)
        idx = jnp.min(jnp.where(cur == m, lane_f, float(LANES)), axis=-1, keepdims=True)
        hit = lane_f == idx
        sel.append(hit)
        vals.append(m)
        idxs.append(idx)
        cur = jnp.where(hit, neg, cur)
    exps = [jnp.exp(v - vals[0]) for v in vals]
    denom = exps[0] + exps[1] + exps[2] + exps[3]
    onehot = jnp.where(sel[0] | sel[1] | sel[2] | sel[3], 1.0, 0.0)
    cum_l = jnp.dot(tri_ref[...], onehot.astype(BF16), preferred_element_type=F32)
    before = carry_ref[0:1, :]
    cum = cum_l + before
    n_tile = cum_l[TM - 1:TM, :]
    nch = jnp.floor((n_tile + (CH_C - 1.0)) * (1.0 / CH_C))
    nch8 = jnp.broadcast_to(nch, (SUBLANES, LANES))
    base = jnp.dot(nch8.astype(BF16), upper_ref[...], preferred_element_type=F32)[0:1, :]
    pk = jnp.zeros((TM, LANES), I32)
    pf = jnp.zeros((TM, LANES), F32)
    for k in range(TOP_K):
        rank = jnp.sum(jnp.where(sel[k], cum - 1.0, 0.0), axis=-1, keepdims=True).astype(I32)
        local = jnp.sum(jnp.where(sel[k], (base * CH_C + cum_l - 1.0) * ROW_F, 0.0), axis=-1,
                        keepdims=True).astype(I32)
        pk = jnp.where(lane == k, idxs[k].astype(I32), pk)
        pk = jnp.where(lane == TOP_K + k, rank, pk)
        pk = jnp.where(lane == 2 * TOP_K + k, local, pk)
        pf = jnp.where(lane == k, exps[k] / denom, pf)
    route_ref[...] = pk.T[0:4 * TOP_K, :]
    prob_ref[...] = pf.T[0:SUBLANES, :]
    row8 = lax.broadcasted_iota(I32, (SUBLANES, LANES), 0)
    tab_ref[...] = jnp.where(row8 == 0, before, jnp.where(row8 == 1, nch, base))
    new_carry = jnp.broadcast_to(cum[TM - 1:TM, :], (SUBLANES, LANES))
    carry_ref[...] = new_carry
    cnt_ref[...] = new_carry


def _post(x, mod_l, ys_prompt, ys_sample, lw, tri, upper):
    full = lambda shape: pl.BlockSpec(shape, lambda i: (0,) * len(shape))
    rows = lambda w: pl.BlockSpec((TM, w), lambda i: (i, 0))
    prow = pl.BlockSpec((TM, GW), lambda i: (jnp.minimum(i, N_PT - 1), 0))
    srow = pl.BlockSpec((TM, GW), lambda i: (jnp.maximum(i - N_PT, 0), 0))
    return pl.pallas_call(
        _post_kernel,
        out_shape=(jax.ShapeDtypeStruct((T, D), F32),
                   jax.ShapeDtypeStruct((T * ROW_F, LANES), F32),
                   jax.ShapeDtypeStruct((4 * TOP_K, T), I32),
                   jax.ShapeDtypeStruct((SUBLANES, T), F32),
                   jax.ShapeDtypeStruct((T // TM * SUBLANES, LANES), F32),
                   jax.ShapeDtypeStruct((SUBLANES, LANES), F32)),
        grid=(T // TM,),
        in_specs=[_PROMPT_ROWS, _SAMPLE_ROWS, full((SUBLANES, 6 * D)),
                  prow, prow, prow, prow, srow, srow, srow, srow,
                  full((4, GW)), full((D, D)), full((1, D)), full((D, 2 * LANES)), full((1, LANES)),
                  full((TM, TM)), full((LANES, LANES))],
        out_specs=(rows(D), pl.BlockSpec((TM * ROW_F, LANES), lambda i: (i, 0)),
                   pl.BlockSpec((4 * TOP_K, TM), lambda i: (0, i)),
                   pl.BlockSpec((SUBLANES, TM), lambda i: (0, i)),
                   pl.BlockSpec((SUBLANES, LANES), lambda i: (i, 0)), full((SUBLANES, LANES))),
        scratch_shapes=[pltpu.VMEM((SUBLANES, LANES), F32)],
        compiler_params=_cparams(("arbitrary",)),
        name="post",
    )(*x, mod_l, *ys_prompt, *ys_sample, lw["out_g"], lw["w_out"], lw["ffn_g"], lw["router_w"],
      lw["router_b"], tri, upper)


def _gather_kernel(dest_ref, nused_ref, zeros_hbm, src_hbm, o_ref, stok_ref, src_ref, sem, src_sem):
    i = pl.program_id(0)

    @pl.when(i == 0)
    def _():
        load = pltpu.make_async_copy(src_hbm, src_ref, src_sem)
        load.start()
        init = pltpu.make_async_copy(zeros_hbm, stok_ref, sem)
        init.start()
        init.wait()

        def scatter(t8, c):
            row0 = t8 * (SUBLANES * ROW_F)
            for k in range(TOP_K):
                for u in range(SUBLANES):
                    stok_ref[dest_ref[k * T + t8 * SUBLANES + u]] = row0 + u * ROW_F
            return c
        lax.fori_loop(0, T // SUBLANES, scatter, 0)
        load.wait()

    live = i * (TG // TM_E) < nused_ref[0]

    @pl.when(live)
    def _():
        def rows(r16, c):
            r0 = pl.multiple_of(r16 * GATHER_UNROLL, GATHER_UNROLL)
            for u in range(GATHER_UNROLL):
                src = pl.multiple_of(stok_ref[i * TG + r0 + u], ROW_F)
                dst = pl.multiple_of(r0 * ROW_F, GATHER_UNROLL * ROW_F) + u * ROW_F
                o_ref[pl.ds(dst, ROW_F), :] = src_ref[pl.ds(src, ROW_F), :]
            return c
        lax.fori_loop(0, TG // GATHER_UNROLL, rows, 0)

    @pl.when(jnp.logical_not(live))
    def _():
        o_ref[...] = jnp.zeros_like(o_ref)


def _gather_rows(dest, n_used, h2lin):
    return pl.pallas_call(
        _gather_kernel,
        out_shape=jax.ShapeDtypeStruct((N_SLOTS * ROW_F, LANES), F32),
        grid_spec=pltpu.PrefetchScalarGridSpec(
            num_scalar_prefetch=2,
            grid=(N_SLOTS // TG,),
            in_specs=[pl.BlockSpec(memory_space=pl.ANY), pl.BlockSpec(memory_space=pl.ANY)],
            out_specs=pl.BlockSpec((TG * ROW_F, LANES), lambda i, d, nu: (i, 0)),
            scratch_shapes=[pltpu.SMEM((N_SLOTS,), I32), pltpu.VMEM((T * ROW_F, LANES), F32),
                            pltpu.SemaphoreType.DMA(()), pltpu.SemaphoreType.DMA(())]),
        compiler_params=_cparams(("arbitrary",), GATHER_VMEM_LIMIT),
        name="moe_gather",
    )(dest, n_used, jnp.zeros((N_SLOTS,), I32), h2lin)


def _expert_rows(n, x_ref, w_ref, slot, bg_ref, bu_ref, bd_ref, o_ref):
    x = jnp.concatenate([x_ref[pl.ds(j, n, stride=ROW_F), :].astype(BF16) for j in range(ROW_F)],
                        axis=-1)
    g = jnp.dot(x, w_ref[slot, 0].astype(BF16), preferred_element_type=F32) + bg_ref[0, 0]
    u = jnp.dot(x, w_ref[slot, 1].astype(BF16), preferred_element_type=F32) + bu_ref[0, 0]
    g = jnp.minimum(g, LIMIT)
    u = jnp.clip(u, -LIMIT, LIMIT)
    act = (u + 1.0) * (g * jax.nn.sigmoid(ALPHA * g))
    y = jnp.dot(act.astype(BF16), w_ref[slot, 2].astype(BF16),
                preferred_element_type=F32) + bd_ref[0, 0]
    for j in range(ROW_F):
        o_ref[pl.ds(j, n, stride=ROW_F), :] = y[:, j * LANES:(j + 1) * LANES]
    if n < TM_E:
        o_ref[n * ROW_F:TM_E * ROW_F, :] = jnp.zeros(((TM_E - n) * ROW_F, LANES), F32)


def _expert_kernel(be_ref, nused_ref, valid_ref, first_ref, next_ref, par_ref,
                   x_ref, wg_hbm, bg_ref, wu_hbm, bu_ref, wd_hbm, bd_ref, o_ref, w_ref, sems, *, layer):
    i = pl.program_id(0)
    ib = jnp.minimum(i, NB_E - 1)
    live = i < nused_ref[0]
    valid = valid_ref[ib]
    expert = be_ref[ib]
    slot = par_ref[ib]

    def fetch(e, s):
        return [pltpu.make_async_copy(src.at[layer, e], w_ref.at[s, m], sems.at[s, m])
                for m, src in enumerate((wg_hbm, wu_hbm, wd_hbm))]

    @pl.when(i == 0)
    def _():
        for cp in fetch(expert, slot):
            cp.start()

    @pl.when(live & (first_ref[ib] == 1))
    def _():
        @pl.when(next_ref[ib] >= 0)
        def _():
            for cp in fetch(next_ref[ib], 1 - slot):
                cp.start()

        for cp in fetch(expert, slot):
            cp.wait()

    for n_idx, n in enumerate(EXPERT_ROWS):
        fits = valid <= n
        if n_idx + 1 < len(EXPERT_ROWS):
            fits = fits & (valid > EXPERT_ROWS[n_idx + 1])

        @pl.when(live & fits)
        def _(n=n):
            _expert_rows(n, x_ref, w_ref, slot, bg_ref, bu_ref, bd_ref, o_ref)

    @pl.when(jnp.logical_not(live))
    def _():
        o_ref[...] = jnp.zeros_like(o_ref)


def _experts(l, sched, xs, p):
    last = NB_E - 1
    bspec = pl.BlockSpec((1, 1, 1, D), lambda i, be, *_: (l, be[jnp.minimum(i, last)], 0, 0))
    hbm = pl.BlockSpec(memory_space=pl.ANY)
    bias = lambda b: b.reshape(DEPTH, N_EXP, 1, D)
    return pl.pallas_call(
        functools.partial(_expert_kernel, layer=l),
        out_shape=jax.ShapeDtypeStruct(((NB_E + 1) * TM_E * ROW_F, LANES), F32),
        grid_spec=pltpu.PrefetchScalarGridSpec(
            num_scalar_prefetch=len(sched),
            grid=(NB_E + 1,),
            in_specs=[pl.BlockSpec((TM_E * ROW_F, LANES), lambda i, *_: (jnp.minimum(i, last), 0)),
                      hbm, bspec, hbm, bspec, hbm, bspec],
            out_specs=pl.BlockSpec((TM_E * ROW_F, LANES), lambda i, *_: (i, 0)),
            scratch_shapes=[pltpu.VMEM((2, 3, D, D), F32), pltpu.SemaphoreType.DMA((2, 3))]),
        compiler_params=_cparams(("arbitrary",)),
        name="moe_experts",
    )(*sched, xs, p["w_gate"], bias(p["b_gate"]), p["w_up"], bias(p["b_up"]),
      p["w_down"], bias(p["b_down"]))


def _combine_kernel(loc_ref, p_ref, csrc_ref, nch_ref, ys_hbm, x_ref, mod_ref, fg_ref, *rest, final):
    *outs, buf_ref, acc_ref, sems = rest
    tb = pl.program_id(0)
    n_tb = pl.num_programs(0)
    slot = tb % 2
    row = _tile_row(tb)
    gate2 = mod_ref[pl.ds(row, 1), 5 * D:6 * D]
    chunk_rows = CH_C * ROW_F

    def chunk_copy(t, c, s):
        src = pl.multiple_of(csrc_ref[t * MAX_CH + c] * ROW_F, ROW_F)
        dst = pl.multiple_of(c * chunk_rows, chunk_rows)
        return pltpu.make_async_copy(ys_hbm.at[pl.ds(src, chunk_rows)],
                                     buf_ref.at[s, pl.ds(dst, chunk_rows)], sems.at[s])

    def issue(t, s):
        def body(c, carry):
            chunk_copy(t, c, s).start()
            return carry
        lax.fori_loop(0, nch_ref[t], body, 0)

    @pl.when(tb == 0)
    def _():
        issue(0, 0)

    @pl.when(tb + 1 < n_tb)
    def _():
        issue(tb + 1, 1 - slot)

    def drain(c, carry):
        chunk_copy(tb, c, slot).wait()
        return carry
    lax.fori_loop(0, nch_ref[tb], drain, 0)

    def tokens(rg, carry):
        r0 = pl.multiple_of(rg * COMBINE_UNROLL, COMBINE_UNROLL)
        for u in range(COMBINE_UNROLL):
            tok = tb * TM + r0 + u
            acc = None
            for k in range(TOP_K):
                off = pl.multiple_of(loc_ref[k * T + tok], ROW_F)
                term = buf_ref[slot, pl.ds(off, ROW_F), :] * p_ref[k * T + tok]
                acc = term if acc is None else acc + term
            dst = pl.multiple_of(r0 * ROW_F, COMBINE_UNROLL * ROW_F) + u * ROW_F
            acc_ref[pl.ds(dst, ROW_F), :] = acc
        return carry
    lax.fori_loop(0, TM // COMBINE_UNROLL, tokens, 0)

    moe = jnp.concatenate([acc_ref[pl.ds(j, TM, stride=ROW_F), :] for j in range(ROW_F)], axis=-1)
    x2 = x_ref[...] + gate2 * moe
    if final:
        x2 = _rms(x2) * fg_ref[...]
    prompt_ref, sample_ref = outs

    @pl.when(tb < N_PT)
    def _():
        prompt_ref[...] = x2

    @pl.when(tb >= N_PT)
    def _():
        sample_ref[...] = x2


def _combine(loc, top_p, chunk_src, n_chunks, ys, x1, mod_l, final_g, final):
    full = lambda shape: pl.BlockSpec(shape, lambda i, *_: (0,) * len(shape))
    rows = lambda w: pl.BlockSpec((TM, w), lambda i, *_: (i, 0))
    out_shape = (jax.ShapeDtypeStruct((T_P, D), F32), jax.ShapeDtypeStruct((T_S, D), F32))
    out_specs = (pl.BlockSpec((TM, D), lambda i, *_: (jnp.minimum(i, N_PT - 1), 0)),
                 pl.BlockSpec((TM, D), lambda i, *_: (jnp.maximum(i - N_PT, 0), 0)))
    return pl.pallas_call(
        functools.partial(_combine_kernel, final=final),
        out_shape=out_shape,
        grid_spec=pltpu.PrefetchScalarGridSpec(
            num_scalar_prefetch=4,
            grid=(T // TM,),
            in_specs=[pl.BlockSpec(memory_space=pl.ANY), rows(D), full((SUBLANES, 6 * D)),
                      full((1, D))],
            out_specs=out_specs,
            scratch_shapes=[pltpu.VMEM((2, MAX_CH * CH_C * ROW_F, LANES), F32),
                            pltpu.VMEM((TM * ROW_F, LANES), F32),
                            pltpu.SemaphoreType.DMA((2,))]),
        compiler_params=_cparams(("arbitrary",)),
        name="moe_combine",
    )(loc, top_p, chunk_src, n_chunks, ys, x1, mod_l, final_g)


def _rope_tables():
    rows = DEC_SEQ // GRID_W
    r = np.repeat(np.arange(rows, dtype=np.float64), GRID_W)
    c = np.tile(np.arange(GRID_W, dtype=np.float64), rows)
    n_freq = QK_ROPE // 4
    inv = (np.float32(ROPE_BASE) ** (-np.arange(n_freq, dtype=np.float32) / n_freq)).astype(np.float64)
    ang = np.concatenate([r[:, None] * inv, c[:, None] * inv], axis=-1).astype(np.float32)
    cos, sin = np.cos(ang.astype(np.float64)), np.sin(ang.astype(np.float64))
    half = QK_ROPE // 2

    def place(width, start):
        cf = np.ones((DEC_SEQ, width), np.float32)
        sa = np.zeros((DEC_SEQ, width), np.float32)
        sb = np.zeros((DEC_SEQ, width), np.float32)
        for s0 in start:
            cf[:, s0:s0 + half] = cos
            cf[:, s0 + half:s0 + 2 * half] = cos
            sa[:, s0:s0 + half] = -sin
            sb[:, s0 + half:s0 + 2 * half] = sin
        return jnp.asarray(cf), jnp.asarray(sa), jnp.asarray(sb)

    return (*place(HEADS * HEAD_PAD, [h * HEAD_PAD + QK_NOPE for h in range(HEADS)]),
            *place(LANES, [0]))


def _dft_tables(seq):
    kn = (np.arange(seq, dtype=np.int64)[:, None] * np.arange(seq, dtype=np.int64)[None, :]) % seq
    ang = 2.0 * np.pi * kn.astype(np.float64) / seq
    return jnp.asarray(np.cos(ang), F32), jnp.asarray(np.sin(ang), F32)


def _block_diag(blocks):
    g, n, _ = blocks.shape
    eye = jnp.eye(g, dtype=blocks.dtype)
    return jnp.einsum("gij,gh->gihj", blocks, eye).reshape(g * n, g * n)


def _layer_weights(l, p):
    w_in = p["w_in"][l]
    za, zq, zkv, zkr, zf, zr, zg = jnp.split(
        w_in, np.cumsum([GW, Q_LORA, KV_LORA, QK_ROPE, GW, GW])[:], axis=1)
    zpad = lambda n: jnp.zeros((D, n), F32)
    w_cols = jnp.concatenate([za, zf, zr, zg, zq, zpad(2 * LANES - Q_LORA), zkv, zkr,
                              zpad(LANES - QK_ROPE)], axis=1).astype(BF16)
    wq = HEADS * HEAD_PAD
    w_uq = p["w_uq"][l].reshape(Q_LORA, HEADS, QK_NOPE + QK_ROPE)
    w_uq = jnp.pad(w_uq, ((0, 2 * LANES - Q_LORA), (0, 0), (0, HEAD_PAD - QK_NOPE - QK_ROPE)))
    w_ukv = p["w_ukv"][l].reshape(KV_LORA, HEADS, QK_NOPE + V_DIM)
    w_kn = jnp.pad(w_ukv[:, :, :QK_NOPE], ((0, 0), (0, 0), (0, HEAD_PAD - QK_NOPE)))
    w_v = w_ukv[:, :, QK_NOPE:]
    place = np.zeros((LANES, HEADS, HEAD_PAD), np.float32)
    for h in range(HEADS):
        place[np.arange(QK_ROPE), h, QK_NOPE + np.arange(QK_ROPE)] = 1.0
    c64 = np.arange(GW // 4, dtype=np.int64)
    ang = 2.0 * np.pi * ((c64[:, None] * c64[None, :]) % (GW // 4)).astype(np.float64) / (GW // 4)
    four = lambda m: jnp.asarray(np.broadcast_to(m, (4,) + m.shape), F32)
    router_hi, router_lo = _split(jnp.pad(p["router_w"][l], ((0, 0), (0, LANES - N_EXP))))
    router_w = jnp.concatenate([router_hi, router_lo], axis=1)
    router_b = jnp.pad(p["router_b"][l], (0, LANES - N_EXP)).reshape(1, LANES)
    return {
        "attn_g": p["attn_norm_g"][l].reshape(1, D),
        "w_in": w_cols,
        "q_g": jnp.pad(p["q_norm_g"][l], (0, 2 * LANES - Q_LORA)).reshape(1, 2 * LANES),
        "w_uq": w_uq.reshape(2 * LANES, wq).astype(BF16),
        "kv_g": p["kv_norm_g"][l].reshape(1, KV_LORA),
        "w_kn": w_kn.reshape(KV_LORA, wq).astype(BF16),
        "w_kr": jnp.asarray(place.reshape(LANES, wq), BF16),
        "w_v": w_v.reshape(KV_LORA, HEADS * V_DIM).astype(BF16),
        "bd_c": _block_diag(four(np.cos(ang))),
        "bd_s": _block_diag(four(np.sin(ang))),
        "w_f": p["fourier_w"][l].astype(BF16),
        "bd_pool": _block_diag(p["pool_w"][l]).astype(BF16),
        "pool_scale": p["pool_scale"][l].reshape(1, GW),
        "conv_w": p["conv_w"][l],
        "conv_b": p["conv_b"][l].reshape(1, GW),
        "bd_wa": jnp.stack([_block_diag(p["lru_wa"][l, d]) for d in range(2)]).astype(BF16),
        "bd_wx": jnp.stack([_block_diag(p["lru_wx"][l, d]) for d in range(2)]).astype(BF16),
        "lru_ba": p["lru_ba"][l].reshape(2, 1, GW),
        "lru_bx": p["lru_bx"][l].reshape(2, 1, GW),
        "lru_lam": p["lru_lambda"][l].reshape(2, 1, GW),
        "out_g": p["out_norm_g"][l],
        "w_out": p["w_out"][l].astype(BF16),
        "ffn_g": p["ffn_norm_g"][l].reshape(1, D),
        "router_w": router_w,
        "router_b": router_b,
    }


def _routing_tables(route, prob, counts, tab):
    top_e, rank = route[0:TOP_K], route[TOP_K:2 * TOP_K]
    top_p = prob[0:TOP_K].reshape(-1)
    loc = route[2 * TOP_K:3 * TOP_K].reshape(-1)
    counts = counts.astype(I32)
    padded = (counts + TM_E - 1) // TM_E * TM_E
    pad_ends = jnp.cumsum(padded)
    pad_starts = pad_ends - padded
    experts = jnp.arange(N_EXP, dtype=I32)
    onehot = top_e[:, :, None] == experts
    dest = (jnp.sum(jnp.where(onehot, pad_starts, 0), axis=-1) + rank).reshape(T * TOP_K)
    n_used = (pad_ends[-1] // TM_E).astype(I32)
    blk = jnp.minimum(jnp.arange(NB_E, dtype=I32), n_used - 1) * TM_E
    block_e = jnp.minimum(jnp.sum(pad_ends[None, :] <= blk[:, None], axis=-1), N_EXP - 1).astype(I32)
    of_block = lambda a: jnp.sum(jnp.where(block_e[:, None] == experts, a, 0), axis=-1)
    valid = jnp.clip(of_block(pad_starts + counts) - blk, 0, TM_E).astype(I32)
    first = (blk == of_block(pad_starts)).astype(I32)
    later = (experts[None, :] > experts[:, None]) & (counts[None, :] > 0)
    next_of = jnp.min(jnp.where(later, experts[None, :], N_EXP), axis=-1)
    next_e = of_block(jnp.where(next_of < N_EXP, next_of, -1)).astype(I32)
    parity = (of_block(jnp.cumsum((counts > 0).astype(I32))) % 2).astype(I32)
    sched = (block_e, n_used.reshape(1), valid, first, next_e, parity)
    tab = tab.reshape(T // TM, SUBLANES, LANES)[:, :, :N_EXP].astype(I32)
    before, nch, base = tab[:, 0], tab[:, 1], tab[:, 2]
    run_start = pad_starts[None, :] + before
    ends = base + nch
    ci = jnp.arange(MAX_CH, dtype=I32)
    e_of = jnp.minimum(jnp.sum(ends[:, None, :] <= ci[None, :, None], axis=-1), N_EXP - 1)
    pick = lambda a: jnp.sum(jnp.where(e_of[:, :, None] == experts, a[:, None, :], 0), axis=-1)
    chunk_src = jnp.clip(pick(run_start) + (ci[None, :] - pick(base)) * CH_C, 0, N_SLOTS)
    return dest, sched, loc, top_p, chunk_src.reshape(-1), ends[:, N_EXP - 1]


def kernel(x_prompt, x_sample, cache_ckv, cache_krope, state_lru, c, c_ctx, w_mod, b_mod, attn_norm_g, w_in, pool_w, pool_scale, q_norm_g, w_uq, kv_norm_g, w_ukv, fourier_w, conv_w, conv_b, lru_wa, lru_ba, lru_wx, lru_bx, lru_lambda, out_norm_g, w_out, ffn_norm_g, router_w, router_b, w_gate, b_gate, w_up, b_up, w_down, b_down, final_norm_g):
    params = dict(attn_norm_g=attn_norm_g, w_in=w_in, pool_w=pool_w, pool_scale=pool_scale,
                  q_norm_g=q_norm_g, w_uq=w_uq, kv_norm_g=kv_norm_g, w_ukv=w_ukv,
                  fourier_w=fourier_w, conv_w=conv_w, conv_b=conv_b, lru_wa=lru_wa, lru_ba=lru_ba,
                  lru_wx=lru_wx, lru_bx=lru_bx, lru_lambda=lru_lambda, out_norm_g=out_norm_g,
                  w_out=w_out, ffn_norm_g=ffn_norm_g, router_w=router_w, router_b=router_b,
                  w_gate=w_gate, b_gate=b_gate, w_up=w_up, b_up=b_up, w_down=w_down, b_down=b_down)
    x = (x_prompt.reshape(T_P, D), x_sample.reshape(T_S, D))
    cvec = jnp.concatenate([c_ctx[None, :], c, jnp.zeros((SUBLANES - 1 - DEC_BATCH, D), F32)], axis=0)
    mod = _modulation(cvec, w_mod, b_mod)
    rope = _rope_tables()
    dft_p, dft_s = _dft_tables(SEQ), _dft_tables(DEC_SEQ)
    tri = jnp.asarray(np.tril(np.ones((TM, TM), np.float32)), BF16)
    upper = jnp.asarray(np.triu(np.ones((LANES, LANES), np.float32), 1), BF16)
    final_g = final_norm_g.reshape(1, D)
    h0_prompt = jnp.zeros((BATCH, 2, GW), F32)

    new_ckv, new_krope, new_lru = [], [], []
    for l in range(DEPTH):
        lw = _layer_weights(l, params)
        zmix, q, k, v, ckv, kr = _front(x, mod[l], lw, rope)
        new_ckv.append(ckv.reshape(BATCH, SEQ, KV_LORA))
        new_krope.append(kr.reshape(BATCH, SEQ, QK_ROPE))

        yb_p = _attention(q, [(k, v, SEQ, 0)], BATCH, SEQ, SEQ, 0, group=PROMPT_GROUP)
        yc_p = _fourier(zmix, dft_p, lw, BATCH, SEQ, 0, group=PROMPT_GROUP)
        ya_p, yd_p, st_p = _seq_mixers(zmix, h0_prompt, lw, BATCH, SEQ, 0)
        new_lru.append(st_p)

        kr_ctx = jnp.pad(cache_krope[:, l].reshape(DEC_BATCH * PAST, QK_ROPE),
                         ((0, 0), (0, LANES - QK_ROPE)))
        k_ctx, v_ctx = _kv_expand(cache_ckv[:, l].reshape(DEC_BATCH * PAST, KV_LORA), kr_ctx, lw)
        yb_s = _attention(q, [(k_ctx, v_ctx, PAST, 0), (k, v, DEC_SEQ, T_P)],
                          DEC_BATCH, DEC_SEQ, 512, T_P)
        yc_s = _fourier(zmix, dft_s, lw, DEC_BATCH, DEC_SEQ, T_P)
        ya_s, yd_s, _ = _seq_mixers(zmix, state_lru[:, l], lw, DEC_BATCH, DEC_SEQ, T_P)

        x1, h2lin, route, prob, tab, counts = _post(
            x, mod[l], (ya_p, yb_p, yc_p, yd_p), (ya_s, yb_s, yc_s, yd_s), lw, tri, upper)
        dest, sched, loc, top_p, chunk_src, n_chunks = _routing_tables(
            route, prob, counts[0, :N_EXP], tab)
        xs = _gather_rows(dest, sched[1], h2lin)
        ys = _experts(l, sched, xs, params)
        x = _combine(loc, top_p, chunk_src, n_chunks, ys, x1, mod[l], final_g,
                     final=(l == DEPTH - 1))

    y_prompt, y_sample = x
    return (y_prompt.reshape(BATCH, SEQ, D), y_sample.reshape(DEC_BATCH, DEC_SEQ, D),
            jnp.stack(new_ckv, axis=1), jnp.stack(new_krope, axis=1), jnp.stack(new_lru, axis=1))
```

```python
import functools

import numpy as np
import jax
import jax.numpy as jnp
from jax import lax
from jax.experimental import pallas as pl
from jax.experimental.pallas import tpu as pltpu

F32 = jnp.float32
BF16 = jnp.bfloat16
I32 = jnp.int32

D = 1024
BATCH, SEQ = 32, 256
DEC_BATCH, DEC_SEQ, PAST = 2, 2048, 512
T_P = BATCH * SEQ
T_S = DEC_BATCH * DEC_SEQ
T = T_P + T_S
DEPTH = 2
GRID_W = 64
GW = 256
Q_LORA, KV_LORA, QK_NOPE, QK_ROPE, V_DIM, HEADS = 192, 128, 64, 32, 64, 4
HEAD_PAD = 128
ROPE_BASE = 10000.0
Q_SCALE = float((QK_NOPE + QK_ROPE) ** -0.5 * np.log2(np.e))
POOL_WINDOWS = (2, 4, 8, 16)
LRU_C = 8.0
N_EXP, TOP_K = 32, 4
LIMIT, ALPHA = 7.0, 1.702
EPS = 1e-6

LANES = 128
SUBLANES = 8
MIB = 1024 * 1024
V7X_VMEM_BYTES = 64 * MIB
VMEM_LIMIT = V7X_VMEM_BYTES * 7 // 8

TM = 512
N_PT = T_P // TM
TILES_PER_DEC = DEC_SEQ // TM
TM_E = 512
EXPERT_ROWS = (TM_E, TM_E // 2, TM_E // 4)
N_SLOTS = T * TOP_K + N_EXP * TM_E
NB_E = N_SLOTS // TM_E
ROW_F = D // LANES
TG = 2 * TM_E
GATHER_VMEM_LIMIT = 4 * (T * D + 2 * TG * D) + 4 * MIB
CH_C = 32
MAX_CH = TM * TOP_K // CH_C + N_EXP
GATHER_UNROLL = 32
COMBINE_UNROLL = 16
PROMPT_GROUP = 4
SEQ_PAD = 32
FRONT = 8

W_COLS = 1536
COL_Q, COL_KV, COL_KR = 1024, 1280, 1408


_PROMPT_ROWS = pl.BlockSpec((TM, D), lambda i: (jnp.minimum(i, N_PT - 1), 0))
_SAMPLE_ROWS = pl.BlockSpec((TM, D), lambda i: (jnp.maximum(i - N_PT, 0), 0))


def _cparams(sem, vmem=VMEM_LIMIT):
    return pltpu.CompilerParams(dimension_semantics=sem, vmem_limit_bytes=vmem)


def _bdot(a, b):
    return jnp.dot(a.astype(BF16), b.astype(BF16), preferred_element_type=F32)


def _split(a):
    hi = a.astype(BF16)
    lo = (a - hi.astype(F32)).astype(BF16)
    return hi, lo


def _dot3(a, b):
    ah, al = _split(a)
    bh, bl = _split(b)
    d = functools.partial(jnp.dot, preferred_element_type=F32)
    return d(ah, bh) + (d(al, bh) + d(ah, bl))


def _rms(x, n=None):
    n = x.shape[-1] if n is None else n
    return x * lax.rsqrt(jnp.sum(x * x, axis=-1, keepdims=True) * (1.0 / n) + EPS)


def _neg_expm1_double(x):
    t = jnp.tanh(x)
    return -2.0 * t / (1.0 - t)


def _sigmoid(x):
    return 0.5 * jnp.tanh(0.5 * x) + 0.5


def _tile_row(i):
    return jnp.where(i >= N_PT, 1 + (i - N_PT) // TILES_PER_DEC, 0)


def _mod_kernel(c_ref, w_ref, b_ref, o_ref):
    s = jax.nn.silu(c_ref[...])
    o_ref[0] = _dot3(s, w_ref[0]) + b_ref[0]


def _modulation(cvec, w_mod, b_mod):
    tn = 1536
    n = 6 * D
    return pl.pallas_call(
        _mod_kernel,
        out_shape=jax.ShapeDtypeStruct((DEPTH, SUBLANES, n), F32),
        grid=(DEPTH, n // tn),
        in_specs=[pl.BlockSpec((SUBLANES, D), lambda l, j: (0, 0)),
                  pl.BlockSpec((1, D, tn), lambda l, j: (l, 0, j)),
                  pl.BlockSpec((1, 1, tn), lambda l, j: (l, 0, j))],
        out_specs=pl.BlockSpec((1, SUBLANES, tn), lambda l, j: (l, 0, j)),
        compiler_params=_cparams(("parallel", "parallel")),
        name="modulation",
    )(cvec, w_mod, b_mod.reshape(DEPTH, 1, n))


def _front_kernel(xp_ref, xs_ref, mod_ref, g_ref, w_ref, qg_ref, wuq_ref, kvg_ref, wkn_ref, wkr_ref,
                  wv_ref, cq_ref, saq_ref, sbq_ref, ck_ref, sak_ref, sbk_ref,
                  zmix_ref, q_ref, k_ref, v_ref, ckv_ref, kr_ref):
    i = pl.program_id(0)
    row = _tile_row(i)
    shift1 = mod_ref[pl.ds(row, 1), 0:D]
    scale1 = mod_ref[pl.ds(row, 1), D:2 * D]
    h = _rms(jnp.where(i >= N_PT, xs_ref[...], xp_ref[...])) * g_ref[...]
    h = h * (1.0 + scale1) + shift1
    z = jnp.dot(h.astype(BF16), w_ref[...], preferred_element_type=F32)
    zmix_ref[...] = z[:, 0:4 * GW]
    qn = _rms(z[:, COL_Q:COL_Q + 2 * LANES], Q_LORA) * qg_ref[...]
    q = _bdot(qn, wuq_ref[...])
    ckv = _rms(z[:, COL_KV:COL_KV + KV_LORA]) * kvg_ref[...]
    ckv_b = ckv.astype(BF16)
    v_ref[...] = jnp.dot(ckv_b, wv_ref[...], preferred_element_type=F32).astype(BF16)
    kn = jnp.dot(ckv_b, wkn_ref[...], preferred_element_type=F32)
    kr = z[:, COL_KR:COL_KR + LANES]

    @pl.when(i < N_PT)
    def _():
        q_ref[...] = (q * Q_SCALE).astype(BF16)
        ckv_ref[...] = ckv
        kr_ref[...] = kr[:, 0:QK_ROPE]
        k_ref[...] = (kn + _bdot(kr, wkr_ref[...])).astype(BF16)

    @pl.when(i >= N_PT)
    def _():
        wq = HEADS * HEAD_PAD
        half = QK_ROPE // 2
        qr = (q * cq_ref[...] + pltpu.roll(q, wq - half, 1) * saq_ref[...]
              + pltpu.roll(q, half, 1) * sbq_ref[...])
        krr = (kr * ck_ref[...] + pltpu.roll(kr, LANES - half, 1) * sak_ref[...]
               + pltpu.roll(kr, half, 1) * sbk_ref[...])
        q_ref[...] = (qr * Q_SCALE).astype(BF16)
        k_ref[...] = (kn + _bdot(krr, wkr_ref[...])).astype(BF16)


def _front(x, mod_l, lw, rope):
    full = lambda shape: pl.BlockSpec(shape, lambda i: (0,) * len(shape))
    rows = lambda w: pl.BlockSpec((TM, w), lambda i: (i, 0))
    prompt_rows = lambda w: pl.BlockSpec((TM, w), lambda i: (jnp.minimum(i, N_PT - 1), 0))
    rope_rows = lambda w: pl.BlockSpec(
        (TM, w), lambda i: (jnp.maximum(i - N_PT, 0) % TILES_PER_DEC, 0))
    wq = HEADS * HEAD_PAD
    return pl.pallas_call(
        _front_kernel,
        out_shape=(jax.ShapeDtypeStruct((T, 4 * GW), F32),
                   jax.ShapeDtypeStruct((T, wq), BF16),
                   jax.ShapeDtypeStruct((T, wq), BF16),
                   jax.ShapeDtypeStruct((T, HEADS * V_DIM), BF16),
                   jax.ShapeDtypeStruct((T_P, KV_LORA), F32),
                   jax.ShapeDtypeStruct((T_P, QK_ROPE), F32)),
        grid=(T // TM,),
        in_specs=[_PROMPT_ROWS, _SAMPLE_ROWS, full((SUBLANES, 6 * D)), full((1, D)), full((D, W_COLS)),
                  full((1, 2 * LANES)), full((2 * LANES, wq)), full((1, KV_LORA)),
                  full((KV_LORA, wq)), full((LANES, wq)), full((KV_LORA, HEADS * V_DIM)),
                  rope_rows(wq), rope_rows(wq), rope_rows(wq),
                  rope_rows(LANES), rope_rows(LANES), rope_rows(LANES)],
        out_specs=(rows(4 * GW), rows(wq), rows(wq), rows(HEADS * V_DIM), prompt_rows(KV_LORA),
                   prompt_rows(QK_ROPE)),
        compiler_params=_cparams(("arbitrary",)),
        name="front",
    )(*x, mod_l, lw["attn_g"], lw["w_in"], lw["q_g"], lw["w_uq"], lw["kv_g"], lw["w_kn"],
      lw["w_kr"], lw["w_v"], *rope)


def _kvexp_kernel(ckv_ref, kr_ref, wkn_ref, wkr_ref, wv_ref, k_ref, v_ref):
    ckv_b = ckv_ref[...].astype(BF16)
    v_ref[...] = jnp.dot(ckv_b, wv_ref[...], preferred_element_type=F32).astype(BF16)
    kn = jnp.dot(ckv_b, wkn_ref[...], preferred_element_type=F32)
    k_ref[...] = (kn + _bdot(kr_ref[...], wkr_ref[...])).astype(BF16)


def _kv_expand(ckv, kr_pad, lw):
    n = ckv.shape[0]
    wq = HEADS * HEAD_PAD
    full = lambda shape: pl.BlockSpec(shape, lambda i: (0,) * len(shape))
    return pl.pallas_call(
        _kvexp_kernel,
        out_shape=(jax.ShapeDtypeStruct((n, wq), BF16),
                   jax.ShapeDtypeStruct((n, HEADS * V_DIM), BF16)),
        grid=(1,),
        in_specs=[full((n, KV_LORA)), full((n, LANES)), full((KV_LORA, wq)), full((LANES, wq)),
                  full((KV_LORA, HEADS * V_DIM))],
        out_specs=(full((n, wq)), full((n, HEADS * V_DIM))),
        compiler_params=_cparams(("arbitrary",)),
        name="kv_expand",
    )(ckv, kr_pad, lw["w_kn"], lw["w_kr"], lw["w_v"])


def _attn_kernel(q_ref, *refs, group, tq, seg_keys):
    o_ref = refs[-1]
    dims = (((1,), (1,)), ((), ()))
    for g in range(group):
        outs = []
        for h in range(HEADS):
            qh = q_ref[g * tq:(g + 1) * tq, h * HEAD_PAD:(h + 1) * HEAD_PAD]
            scores = []
            for si, nk in enumerate(seg_keys):
                kh = refs[2 * si][g * nk:(g + 1) * nk, h * HEAD_PAD:(h + 1) * HEAD_PAD]
                scores.append(lax.dot_general(qh, kh, dims, preferred_element_type=F32))
            m = functools.reduce(jnp.maximum, [jnp.max(s, axis=-1, keepdims=True) for s in scores])
            l, acc = None, None
            for si, (nk, s) in enumerate(zip(seg_keys, scores)):
                p = jnp.exp2(s - m)
                vh = refs[2 * si + 1][g * nk:(g + 1) * nk, h * V_DIM:(h + 1) * V_DIM]
                part = jnp.dot(p.astype(BF16), vh, preferred_element_type=F32)
                psum = jnp.sum(p, axis=-1, keepdims=True)
                l, acc = (psum, part) if l is None else (l + psum, acc + part)
            outs.append(acc / l)
        o_ref[g * tq:(g + 1) * tq, :] = jnp.concatenate(outs, axis=-1)


def _attention(q, segments, n_batch, seq, tq, row0, group=1):
    nq = seq // tq
    wq = HEADS * HEAD_PAD
    in_specs = [pl.BlockSpec((group * tq, wq), lambda b, i: (row0 // (group * tq) + b * nq + i, 0))]
    operands = [q]
    for k_rows, v_rows, nk, first in segments:
        index = lambda b, i, base=first // (group * nk): (base + b, 0)
        in_specs += [pl.BlockSpec((group * nk, wq), index),
                     pl.BlockSpec((group * nk, HEADS * V_DIM), index)]
        operands += [k_rows, v_rows]
    return pl.pallas_call(
        functools.partial(_attn_kernel, group=group, tq=tq,
                          seg_keys=tuple(seg[2] for seg in segments)),
        out_shape=jax.ShapeDtypeStruct((n_batch * seq, GW), F32),
        grid=(n_batch // group, nq),
        in_specs=in_specs,
        out_specs=pl.BlockSpec((group * tq, GW), lambda b, i: (b * nq + i, 0)),
        compiler_params=_cparams(("parallel", "parallel")),
        name="attention",
    )(*operands)


def _fourier_kernel(z_ref, c_ref, s_ref, bdc_ref, bds_ref, w_ref, o_ref, xc_ref, xs_ref, *,
                    norm, group, seq, ts):
    b = pl.program_id(1)

    @pl.when(pl.program_id(0) == 0)
    def _():
        xb = z_ref[...].astype(BF16)
        xc_ref[b] = _bdot(xb, bdc_ref[...]).astype(BF16)
        xs_ref[b] = _bdot(xb, bds_ref[...]).astype(BF16)

    cb, sb = c_ref[...].astype(BF16), s_ref[...].astype(BF16)
    for g in range(group):
        f = (jnp.dot(cb, xc_ref[b, g * seq:(g + 1) * seq, :], preferred_element_type=F32)
             - jnp.dot(sb, xs_ref[b, g * seq:(g + 1) * seq, :], preferred_element_type=F32)) * norm
        o_ref[g * ts:(g + 1) * ts, :] = _bdot(f, w_ref[...])


def _fourier(zmix, consts, lw, n_batch, seq, row0, group=1):
    ts = min(seq, 512)
    nj = seq // ts
    assert group == 1 or nj == 1
    nb = n_batch // group
    cmat, smat = consts
    full = lambda shape: pl.BlockSpec(shape, lambda j, b: (0,) * len(shape))
    z_spec = pl.BlockSpec((group * seq, GW),
                          lambda j, b: (row0 // (group * seq) + jnp.where(j == 0, b, nb - 1), 1))
    return pl.pallas_call(
        functools.partial(_fourier_kernel, norm=float((seq * (GW // 4)) ** -0.5), group=group,
                          seq=seq, ts=ts),
        out_shape=jax.ShapeDtypeStruct((n_batch * seq, GW), F32),
        grid=(nj, nb),
        in_specs=[z_spec,
                  pl.BlockSpec((ts, seq), lambda j, b: (j, 0)),
                  pl.BlockSpec((ts, seq), lambda j, b: (j, 0)),
                  full((GW, GW)), full((GW, GW)), full((GW, GW))],
        out_specs=pl.BlockSpec((group * ts, GW), lambda j, b: (b * nj + j, 0)),
        scratch_shapes=[pltpu.VMEM((nb, group * seq, GW), BF16),
                        pltpu.VMEM((nb, group * seq, GW), BF16)],
        compiler_params=_cparams(("arbitrary", "arbitrary")),
        name="fourier",
    )(zmix, cmat, smat, lw["bd_c"], lw["bd_s"], lw["w_f"])


def _seq_kernel(za_ref, zr_ref, zg_ref, h0_ref, icnt_ref, wp_ref, ps_ref, cw_ref, cb_ref, wa_ref,
                ba_ref, wx_ref, bx_ref, lam_ref, ya_ref, yd_ref, st_ref,
                pa_ref, pb_ref, xp_ref, a_ref, b_ref, *, seq):
    n = seq + SEQ_PAD
    span = seq + 2 * FRONT
    zeros_pad = jnp.zeros((n, GW), F32)

    za = za_ref[...]
    pa_ref[...] = zeros_pad
    pb_ref[...] = zeros_pad
    pa_ref[FRONT:FRONT + seq, :] = za
    pb_ref[0:span, :] = pa_ref[0:span, :] + pa_ref[1:span + 1, :]
    win2 = pb_ref[FRONT - 1:FRONT - 1 + seq, :]
    pa_ref[0:span, :] = pb_ref[0:span, :] + pb_ref[2:span + 2, :]
    win4 = pa_ref[FRONT - 2:FRONT - 2 + seq, :]
    pb_ref[0:span, :] = pa_ref[0:span, :] + pa_ref[4:span + 4, :]
    win8 = pb_ref[FRONT - 4:FRONT - 4 + seq, :]
    pa_ref[0:span, :] = pb_ref[0:span, :] + pb_ref[8:span + 8, :]
    win16 = pa_ref[FRONT - 8:FRONT - 8 + seq, :]
    grp = lax.broadcasted_iota(I32, (1, GW), 1) // (GW // 4)
    win = jnp.where(grp == 0, win2, jnp.where(grp == 1, win4, jnp.where(grp == 2, win8, win16)))
    dlt = win * icnt_ref[...] - za
    ya_ref[...] = _bdot(dlt, wp_ref[...]) * ps_ref[...]

    xp_ref[...] = zeros_pad
    xp_ref[FRONT:FRONT + seq, :] = zr_ref[...]
    xc = cb_ref[...] + cw_ref[0:1, :] * xp_ref[FRONT - 2:FRONT - 2 + seq, :]
    for kk in range(1, 4):
        xc = xc + cw_ref[kk:kk + 1, :] * xp_ref[FRONT - 2 + kk:FRONT - 2 + kk + seq, :]
    xcb = xc.astype(BF16)
    n_grp = seq // SUBLANES
    sub = lax.broadcasted_iota(I32, (n_grp, SUBLANES, GW), 1)

    for d in range(2):
        r = _sigmoid(jnp.dot(xcb, wa_ref[d], preferred_element_type=F32) + ba_ref[d])
        ig = _sigmoid(jnp.dot(xcb, wx_ref[d], preferred_element_type=F32) + bx_ref[d])
        log_a = (-LRU_C) * r * jax.nn.softplus(-lam_ref[d])
        a = jnp.exp(log_a).reshape(n_grp, SUBLANES, GW)
        b = (jnp.sqrt(_neg_expm1_double(log_a)) * (ig * xc)).reshape(n_grp, SUBLANES, GW)
        for k in (1, 2, 4):
            shift = k if d == 0 else SUBLANES - k
            m = sub >= k if d == 0 else sub < SUBLANES - k
            ap, bp = pltpu.roll(a, shift, 1), pltpu.roll(b, shift, 1)
            b = jnp.where(m, a * bp + b, b)
            a = jnp.where(m, a * ap, a)
        a_ref[d] = a.reshape(seq, GW)
        b_ref[d] = b.reshape(seq, GW)

    def step(g, carry):
        fwd, bwd = carry
        off_f = pl.multiple_of(g * SUBLANES, SUBLANES)
        off_b = pl.multiple_of((n_grp - 1 - g) * SUBLANES, SUBLANES)
        hf = a_ref[0, pl.ds(off_f, SUBLANES), :] * fwd + b_ref[0, pl.ds(off_f, SUBLANES), :]
        hb = a_ref[1, pl.ds(off_b, SUBLANES), :] * bwd + b_ref[1, pl.ds(off_b, SUBLANES), :]
        b_ref[0, pl.ds(off_f, SUBLANES), :] = hf
        b_ref[1, pl.ds(off_b, SUBLANES), :] = hb
        return (jnp.broadcast_to(hf[SUBLANES - 1:SUBLANES, :], (SUBLANES, GW)),
                jnp.broadcast_to(hb[0:1, :], (SUBLANES, GW)))

    start = tuple(jnp.broadcast_to(h0_ref[0, d:d + 1, :], (SUBLANES, GW)) for d in range(2))
    last_f, last_b = lax.fori_loop(0, n_grp, step, start)
    st_ref[0, 0:1, :] = last_f[0:1, :]
    st_ref[0, 1:2, :] = last_b[0:1, :]
    yd_ref[...] = (b_ref[0] + b_ref[1]) * jax.nn.gelu(zg_ref[...])


def _pool_inverse_counts(seq):
    pos = np.arange(seq)[:, None]
    half = np.repeat(np.array(POOL_WINDOWS) // 2, GW // len(POOL_WINDOWS))[None, :]
    cnt = np.minimum(pos + half, seq) - np.maximum(pos - half, 0)
    return jnp.asarray(1.0 / cnt, F32)


def _seq_mixers(zmix, h0, lw, n_batch, seq, row0):
    full = lambda shape: pl.BlockSpec(shape, lambda b: (0,) * len(shape))
    col = lambda c: pl.BlockSpec((seq, GW), lambda b: (row0 // seq + b, c))
    out_rows = pl.BlockSpec((seq, GW), lambda b: (b, 0))
    pad = pltpu.VMEM((seq + SEQ_PAD, GW), F32)
    return pl.pallas_call(
        functools.partial(_seq_kernel, seq=seq),
        out_shape=(jax.ShapeDtypeStruct((n_batch * seq, GW), F32),
                   jax.ShapeDtypeStruct((n_batch * seq, GW), F32),
                   jax.ShapeDtypeStruct((n_batch, 2, GW), F32)),
        grid=(n_batch,),
        in_specs=[col(0), col(2), col(3), pl.BlockSpec((1, 2, GW), lambda b: (b, 0, 0)),
                  full((seq, GW)), full((GW, GW)), full((1, GW)), full((4, GW)), full((1, GW)),
                  full((2, GW, GW)), full((2, 1, GW)), full((2, GW, GW)), full((2, 1, GW)),
                  full((2, 1, GW))],
        out_specs=(out_rows, out_rows, pl.BlockSpec((1, 2, GW), lambda b: (b, 0, 0))),
        scratch_shapes=[pad, pad, pad, pltpu.VMEM((2, seq, GW), F32), pltpu.VMEM((2, seq, GW), F32)],
        compiler_params=_cparams(("parallel",)),
        name="seq_mixers",
    )(zmix, zmix, zmix, h0, _pool_inverse_counts(seq), lw["bd_pool"], lw["pool_scale"], lw["conv_w"],
      lw["conv_b"], lw["bd_wa"], lw["lru_ba"], lw["bd_wx"], lw["lru_bx"], lw["lru_lam"])


def _post_kernel(xp_ref, xs_ref, mod_ref, yap_ref, ybp_ref, ycp_ref, ydp_ref, yas_ref, ybs_ref, ycs_ref,
                 yds_ref, og_ref, wo_ref, fg_ref, rw_ref, rb_ref, tri_ref, lower_ref,
                 x1_ref, h2p_ref, route_ref, prob_ref, tab_ref, cnt_ref, carry_ref):
    i = pl.program_id(0)
    is_s = i >= N_PT
    row = _tile_row(i)
    gate1 = mod_ref[pl.ds(row, 1), 2 * D:3 * D]
    shift2 = mod_ref[pl.ds(row, 1), 3 * D:4 * D]
    scale2 = mod_ref[pl.ds(row, 1), 4 * D:5 * D]

    @pl.when(i == 0)
    def _():
        carry_ref[...] = jnp.zeros_like(carry_ref)

    groups = []
    for gi, (p_ref, s_ref) in enumerate(((yap_ref, yas_ref), (ybp_ref, ybs_ref),
                                         (ycp_ref, ycs_ref), (ydp_ref, yds_ref))):
        y = jnp.where(is_s, s_ref[...], p_ref[...])
        groups.append((_rms(y) * og_ref[gi:gi + 1, :]).astype(BF16))
    ycat = jnp.concatenate(groups, axis=-1)
    x = jnp.where(is_s, xs_ref[...], xp_ref[...])
    x1 = x + gate1 * jnp.dot(ycat, wo_ref[...], preferred_element_type=F32)
    x1_ref[...] = x1
    h2 = _rms(x1) * fg_ref[...]
    h2 = h2 * (1.0 + scale2) + shift2

    for j in range(ROW_F):
        h2p_ref[pl.ds(j, TM, stride=ROW_F), :] = h2[:, j * LANES:(j + 1) * LANES]

    h_hi, h_lo = _split(h2)
    both = jnp.dot(h_hi, rw_ref[...], preferred_element_type=F32)
    cross = both[:, LANES:2 * LANES] + jnp.dot(h_lo, rw_ref[:, 0:LANES], preferred_element_type=F32)
    logits = both[:, 0:LANES] + cross + rb_ref[...]
    lt = logits.T[0:N_EXP, :]
    expert = lax.broadcasted_iota(I32, (N_EXP, TM), 0).astype(F32)
    neg = jnp.float32(-jnp.inf)
    cur = lt
    sel, vals, idxs = [], [], []
    for _ in range(TOP_K):
        m = jnp.max(cur, axis=0, keepdims=True)
        idx = jnp.min(jnp.where(cur == m, expert, float(N_EXP)), axis=0, keepdims=True)
        hit = expert == idx
        sel.append(hit)
        vals.append(m)
        idxs.append(idx)
        cur = jnp.where(hit, neg, cur)
    exps = [jnp.exp(v - vals[0]) for v in vals]
    denom = exps[0] + exps[1] + exps[2] + exps[3]
    onehot = jnp.where(sel[0] | sel[1] | sel[2] | sel[3], 1.0, 0.0)
    cum_l = jnp.dot(onehot.astype(BF16), tri_ref[...], preferred_element_type=F32)
    before = carry_ref[:, 0:1]
    cum = cum_l + before
    n_tile = cum_l[:, TM - 1:TM]
    nch = jnp.floor((n_tile + (CH_C - 1.0)) * (1.0 / CH_C))
    nch_b = jnp.broadcast_to(nch, (N_EXP, LANES)).astype(BF16)
    base = jnp.dot(lower_ref[...], nch_b, preferred_element_type=F32)[:, 0:1]
    rows_e, rows_r, rows_l, rows_p = [], [], [], []
    for k in range(TOP_K):
        rows_e.append(idxs[k].astype(I32))
        rows_r.append(jnp.sum(jnp.where(sel[k], cum - 1.0, 0.0), axis=0, keepdims=True).astype(I32))
        rows_l.append(jnp.sum(jnp.where(sel[k], (base * CH_C + cum_l - 1.0) * ROW_F, 0.0), axis=0,
                              keepdims=True).astype(I32))
        rows_p.append(exps[k] / denom)
    route_ref[...] = jnp.concatenate(rows_e + rows_r + rows_l + [jnp.zeros((TOP_K, TM), I32)], axis=0)
    prob_ref[...] = jnp.concatenate(rows_p + [jnp.zeros((SUBLANES - TOP_K, TM), F32)], axis=0)
    lane = lax.broadcasted_iota(I32, (N_EXP, LANES), 1)
    tab_ref[...] = jnp.where(lane == 0, before, jnp.where(lane == 1, nch, base))
    new_carry = jnp.broadcast_to(cum[:, TM - 1:TM], (N_EXP, LANES))
    carry_ref[...] = new_carry
    cnt_ref[...] = new_carry


def _post(x, mod_l, ys_prompt, ys_sample, lw, tri, lower):
    full = lambda shape: pl.BlockSpec(shape, lambda i: (0,) * len(shape))
    rows = lambda w: pl.BlockSpec((TM, w), lambda i: (i, 0))
    prow = pl.BlockSpec((TM, GW), lambda i: (jnp.minimum(i, N_PT - 1), 0))
    srow = pl.BlockSpec((TM, GW), lambda i: (jnp.maximum(i - N_PT, 0), 0))
    return pl.pallas_call(
        _post_kernel,
        out_shape=(jax.ShapeDtypeStruct((T, D), F32),
                   jax.ShapeDtypeStruct((T * ROW_F, LANES), F32),
                   jax.ShapeDtypeStruct((4 * TOP_K, T), I32),
                   jax.ShapeDtypeStruct((SUBLANES, T), F32),
                   jax.ShapeDtypeStruct((T // TM * N_EXP, LANES), F32),
                   jax.ShapeDtypeStruct((N_EXP, LANES), F32)),
        grid=(T // TM,),
        in_specs=[_PROMPT_ROWS, _SAMPLE_ROWS, full((SUBLANES, 6 * D)),
                  prow, prow, prow, prow, srow, srow, srow, srow,
                  full((4, GW)), full((D, D)), full((1, D)), full((D, 2 * LANES)), full((1, LANES)),
                  full((TM, TM)), full((N_EXP, N_EXP))],
        out_specs=(rows(D), pl.BlockSpec((TM * ROW_F, LANES), lambda i: (i, 0)),
                   pl.BlockSpec((4 * TOP_K, TM), lambda i: (0, i)),
                   pl.BlockSpec((SUBLANES, TM), lambda i: (0, i)),
                   pl.BlockSpec((N_EXP, LANES), lambda i: (i, 0)), full((N_EXP, LANES))),
        scratch_shapes=[pltpu.VMEM((N_EXP, LANES), F32)],
        compiler_params=_cparams(("arbitrary",)),
        name="post",
    )(*x, mod_l, *ys_prompt, *ys_sample, lw["out_g"], lw["w_out"], lw["ffn_g"], lw["router_w"],
      lw["router_b"], tri, lower)


def _gather_kernel(dest_ref, nused_ref, zeros_hbm, src_hbm, o_ref, stok_ref, src_ref, sem, src_sem):
    i = pl.program_id(0)

    @pl.when(i == 0)
    def _():
        load = pltpu.make_async_copy(src_hbm, src_ref, src_sem)
        load.start()
        init = pltpu.make_async_copy(zeros_hbm, stok_ref, sem)
        init.start()
        init.wait()

        def scatter(t8, c):
            row0 = t8 * (SUBLANES * ROW_F)
            for k in range(TOP_K):
                for u in range(SUBLANES):
                    stok_ref[dest_ref[k * T + t8 * SUBLANES + u]] = row0 + u * ROW_F
            return c
        lax.fori_loop(0, T // SUBLANES, scatter, 0)
        load.wait()

    live = i * (TG // TM_E) < nused_ref[0]

    @pl.when(live)
    def _():
        def rows(r16, c):
            r0 = pl.multiple_of(r16 * GATHER_UNROLL, GATHER_UNROLL)
            for u in range(GATHER_UNROLL):
                src = pl.multiple_of(stok_ref[i * TG + r0 + u], ROW_F)
                dst = pl.multiple_of(r0 * ROW_F, GATHER_UNROLL * ROW_F) + u * ROW_F
                o_ref[pl.ds(dst, ROW_F), :] = src_ref[pl.ds(src, ROW_F), :]
            return c
        lax.fori_loop(0, TG // GATHER_UNROLL, rows, 0)

    @pl.when(jnp.logical_not(live))
    def _():
        o_ref[...] = jnp.zeros_like(o_ref)


def _gather_rows(dest, n_used, h2lin):
    return pl.pallas_call(
        _gather_kernel,
        out_shape=jax.ShapeDtypeStruct((N_SLOTS * ROW_F, LANES), F32),
        grid_spec=pltpu.PrefetchScalarGridSpec(
            num_scalar_prefetch=2,
            grid=(N_SLOTS // TG,),
            in_specs=[pl.BlockSpec(memory_space=pl.ANY), pl.BlockSpec(memory_space=pl.ANY)],
            out_specs=pl.BlockSpec((TG * ROW_F, LANES), lambda i, d, nu: (i, 0)),
            scratch_shapes=[pltpu.SMEM((N_SLOTS,), I32), pltpu.VMEM((T * ROW_F, LANES), F32),
                            pltpu.SemaphoreType.DMA(()), pltpu.SemaphoreType.DMA(())]),
        compiler_params=_cparams(("arbitrary",), GATHER_VMEM_LIMIT),
        name="moe_gather",
    )(dest, n_used, jnp.zeros((N_SLOTS,), I32), h2lin)


def _expert_rows(n, x_ref, w_ref, slot, bg_ref, bu_ref, bd_ref, o_ref):
    x = jnp.concatenate([x_ref[pl.ds(j, n, stride=ROW_F), :].astype(BF16) for j in range(ROW_F)],
                        axis=-1)
    g = jnp.dot(x, w_ref[slot, 0].astype(BF16), preferred_element_type=F32) + bg_ref[0, 0]
    u = jnp.dot(x, w_ref[slot, 1].astype(BF16), preferred_element_type=F32) + bu_ref[0, 0]
    g = jnp.minimum(g, LIMIT)
    u = jnp.clip(u, -LIMIT, LIMIT)
    act = (u + 1.0) * (g * jax.nn.sigmoid(ALPHA * g))
    y = jnp.dot(act.astype(BF16), w_ref[slot, 2].astype(BF16),
                preferred_element_type=F32) + bd_ref[0, 0]
    for j in range(ROW_F):
        o_ref[pl.ds(j, n, stride=ROW_F), :] = y[:, j * LANES:(j + 1) * LANES]
    if n < TM_E:
        o_ref[n * ROW_F:TM_E * ROW_F, :] = jnp.zeros(((TM_E - n) * ROW_F, LANES), F32)


def _expert_kernel(be_ref, nused_ref, valid_ref, first_ref, next_ref, par_ref,
                   x_ref, wg_hbm, bg_ref, wu_hbm, bu_ref, wd_hbm, bd_ref, o_ref, w_ref, sems, *, layer):
    i = pl.program_id(0)
    ib = jnp.minimum(i, NB_E - 1)
    live = i < nused_ref[0]
    valid = valid_ref[ib]
    expert = be_ref[ib]
    slot = par_ref[ib]

    def fetch(e, s):
        return [pltpu.make_async_copy(src.at[layer, e], w_ref.at[s, m], sems.at[s, m])
                for m, src in enumerate((wg_hbm, wu_hbm, wd_hbm))]

    @pl.when(i == 0)
    def _():
        for cp in fetch(expert, slot):
            cp.start()

    @pl.when(live & (first_ref[ib] == 1))
    def _():
        @pl.when(next_ref[ib] >= 0)
        def _():
            for cp in fetch(next_ref[ib], 1 - slot):
                cp.start()

        for cp in fetch(expert, slot):
            cp.wait()

    for n_idx, n in enumerate(EXPERT_ROWS):
        fits = valid <= n
        if n_idx + 1 < len(EXPERT_ROWS):
            fits = fits & (valid > EXPERT_ROWS[n_idx + 1])

        @pl.when(live & fits)
        def _(n=n):
            _expert_rows(n, x_ref, w_ref, slot, bg_ref, bu_ref, bd_ref, o_ref)

    @pl.when(jnp.logical_not(live))
    def _():
        o_ref[...] = jnp.zeros_like(o_ref)


def _experts(l, sched, xs, p):
    last = NB_E - 1
    bspec = pl.BlockSpec((1, 1, 1, D), lambda i, be, *_: (l, be[jnp.minimum(i, last)], 0, 0))
    hbm = pl.BlockSpec(memory_space=pl.ANY)
    bias = lambda b: b.reshape(DEPTH, N_EXP, 1, D)
    return pl.pallas_call(
        functools.partial(_expert_kernel, layer=l),
        out_shape=jax.ShapeDtypeStruct(((NB_E + 1) * TM_E * ROW_F, LANES), F32),
        grid_spec=pltpu.PrefetchScalarGridSpec(
            num_scalar_prefetch=len(sched),
            grid=(NB_E + 1,),
            in_specs=[pl.BlockSpec((TM_E * ROW_F, LANES), lambda i, *_: (jnp.minimum(i, last), 0)),
                      hbm, bspec, hbm, bspec, hbm, bspec],
            out_specs=pl.BlockSpec((TM_E * ROW_F, LANES), lambda i, *_: (i, 0)),
            scratch_shapes=[pltpu.VMEM((2, 3, D, D), F32), pltpu.SemaphoreType.DMA((2, 3))]),
        compiler_params=_cparams(("arbitrary",)),
        name="moe_experts",
    )(*sched, xs, p["w_gate"], bias(p["b_gate"]), p["w_up"], bias(p["b_up"]),
      p["w_down"], bias(p["b_down"]))


def _combine_kernel(loc_ref, p_ref, csrc_ref, nch_ref, ys_hbm, x_ref, mod_ref, fg_ref, *rest, final):
    *outs, buf_ref, acc_ref, sems = rest
    tb = pl.program_id(0)
    n_tb = pl.num_programs(0)
    slot = tb % 2
    row = _tile_row(tb)
    gate2 = mod_ref[pl.ds(row, 1), 5 * D:6 * D]
    chunk_rows = CH_C * ROW_F

    def chunk_copy(t, c, s):
        src = pl.multiple_of(csrc_ref[t * MAX_CH + c] * ROW_F, ROW_F)
        dst = pl.multiple_of(c * chunk_rows, chunk_rows)
        return pltpu.make_async_copy(ys_hbm.at[pl.ds(src, chunk_rows)],
                                     buf_ref.at[s, pl.ds(dst, chunk_rows)], sems.at[s])

    def issue(t, s):
        def body(c, carry):
            chunk_copy(t, c, s).start()
            return carry
        lax.fori_loop(0, nch_ref[t], body, 0)

    @pl.when(tb == 0)
    def _():
        issue(0, 0)

    @pl.when(tb + 1 < n_tb)
    def _():
        issue(tb + 1, 1 - slot)

    def drain(c, carry):
        chunk_copy(tb, c, slot).wait()
        return carry
    lax.fori_loop(0, nch_ref[tb], drain, 0)

    def tokens(rg, carry):
        r0 = pl.multiple_of(rg * COMBINE_UNROLL, COMBINE_UNROLL)
        for u in range(COMBINE_UNROLL):
            tok = tb * TM + r0 + u
            acc = None
            for k in range(TOP_K):
                off = pl.multiple_of(loc_ref[k * T + tok], ROW_F)
                term = buf_ref[slot, pl.ds(off, ROW_F), :] * p_ref[k * T + tok]
                acc = term if acc is None else acc + term
            dst = pl.multiple_of(r0 * ROW_F, COMBINE_UNROLL * ROW_F) + u * ROW_F
            acc_ref[pl.ds(dst, ROW_F), :] = acc
        return carry
    lax.fori_loop(0, TM // COMBINE_UNROLL, tokens, 0)

    moe = jnp.concatenate([acc_ref[pl.ds(j, TM, stride=ROW_F), :] for j in range(ROW_F)], axis=-1)
    x2 = x_ref[...] + gate2 * moe
    if final:
        x2 = _rms(x2) * fg_ref[...]
    prompt_ref, sample_ref = outs

    @pl.when(tb < N_PT)
    def _():
        prompt_ref[...] = x2

    @pl.when(tb >= N_PT)
    def _():
        sample_ref[...] = x2


def _combine(loc, top_p, chunk_src, n_chunks, ys, x1, mod_l, final_g, final):
    full = lambda shape: pl.BlockSpec(shape, lambda i, *_: (0,) * len(shape))
    rows = lambda w: pl.BlockSpec((TM, w), lambda i, *_: (i, 0))
    out_shape = (jax.ShapeDtypeStruct((T_P, D), F32), jax.ShapeDtypeStruct((T_S, D), F32))
    out_specs = (pl.BlockSpec((TM, D), lambda i, *_: (jnp.minimum(i, N_PT - 1), 0)),
                 pl.BlockSpec((TM, D), lambda i, *_: (jnp.maximum(i - N_PT, 0), 0)))
    return pl.pallas_call(
        functools.partial(_combine_kernel, final=final),
        out_shape=out_shape,
        grid_spec=pltpu.PrefetchScalarGridSpec(
            num_scalar_prefetch=4,
            grid=(T // TM,),
            in_specs=[pl.BlockSpec(memory_space=pl.ANY), rows(D), full((SUBLANES, 6 * D)),
                      full((1, D))],
            out_specs=out_specs,
            scratch_shapes=[pltpu.VMEM((2, MAX_CH * CH_C * ROW_F, LANES), F32),
                            pltpu.VMEM((TM * ROW_F, LANES), F32),
                            pltpu.SemaphoreType.DMA((2,))]),
        compiler_params=_cparams(("arbitrary",)),
        name="moe_combine",
    )(loc, top_p, chunk_src, n_chunks, ys, x1, mod_l, final_g)


def _rope_tables():
    rows = DEC_SEQ // GRID_W
    r = np.repeat(np.arange(rows, dtype=np.float64), GRID_W)
    c = np.tile(np.arange(GRID_W, dtype=np.float64), rows)
    n_freq = QK_ROPE // 4
    inv = (np.float32(ROPE_BASE) ** (-np.arange(n_freq, dtype=np.float32) / n_freq)).astype(np.float64)
    ang = np.concatenate([r[:, None] * inv, c[:, None] * inv], axis=-1).astype(np.float32)
    cos, sin = np.cos(ang.astype(np.float64)), np.sin(ang.astype(np.float64))
    half = QK_ROPE // 2

    def place(width, start):
        cf = np.ones((DEC_SEQ, width), np.float32)
        sa = np.zeros((DEC_SEQ, width), np.float32)
        sb = np.zeros((DEC_SEQ, width), np.float32)
        for s0 in start:
            cf[:, s0:s0 + half] = cos
            cf[:, s0 + half:s0 + 2 * half] = cos
            sa[:, s0:s0 + half] = -sin
            sb[:, s0 + half:s0 + 2 * half] = sin
        return jnp.asarray(cf), jnp.asarray(sa), jnp.asarray(sb)

    return (*place(HEADS * HEAD_PAD, [h * HEAD_PAD + QK_NOPE for h in range(HEADS)]),
            *place(LANES, [0]))


def _dft_tables(seq):
    kn = (np.arange(seq, dtype=np.int64)[:, None] * np.arange(seq, dtype=np.int64)[None, :]) % seq
    ang = 2.0 * np.pi * kn.astype(np.float64) / seq
    return jnp.asarray(np.cos(ang), F32), jnp.asarray(np.sin(ang), F32)


def _block_diag(blocks):
    g, n, _ = blocks.shape
    eye = jnp.eye(g, dtype=blocks.dtype)
    return jnp.einsum("gij,gh->gihj", blocks, eye).reshape(g * n, g * n)


def _layer_weights(l, p):
    w_in = p["w_in"][l]
    za, zq, zkv, zkr, zf, zr, zg = jnp.split(
        w_in, np.cumsum([GW, Q_LORA, KV_LORA, QK_ROPE, GW, GW])[:], axis=1)
    zpad = lambda n: jnp.zeros((D, n), F32)
    w_cols = jnp.concatenate([za, zf, zr, zg, zq, zpad(2 * LANES - Q_LORA), zkv, zkr,
                              zpad(LANES - QK_ROPE)], axis=1).astype(BF16)
    wq = HEADS * HEAD_PAD
    w_uq = p["w_uq"][l].reshape(Q_LORA, HEADS, QK_NOPE + QK_ROPE)
    w_uq = jnp.pad(w_uq, ((0, 2 * LANES - Q_LORA), (0, 0), (0, HEAD_PAD - QK_NOPE - QK_ROPE)))
    w_ukv = p["w_ukv"][l].reshape(KV_LORA, HEADS, QK_NOPE + V_DIM)
    w_kn = jnp.pad(w_ukv[:, :, :QK_NOPE], ((0, 0), (0, 0), (0, HEAD_PAD - QK_NOPE)))
    w_v = w_ukv[:, :, QK_NOPE:]
    place = np.zeros((LANES, HEADS, HEAD_PAD), np.float32)
    for h in range(HEADS):
        place[np.arange(QK_ROPE), h, QK_NOPE + np.arange(QK_ROPE)] = 1.0
    c64 = np.arange(GW // 4, dtype=np.int64)
    ang = 2.0 * np.pi * ((c64[:, None] * c64[None, :]) % (GW // 4)).astype(np.float64) / (GW // 4)
    four = lambda m: jnp.asarray(np.broadcast_to(m, (4,) + m.shape), F32)
    router_hi, router_lo = _split(jnp.pad(p["router_w"][l], ((0, 0), (0, LANES - N_EXP))))
    router_w = jnp.concatenate([router_hi, router_lo], axis=1)
    router_b = jnp.pad(p["router_b"][l], (0, LANES - N_EXP)).reshape(1, LANES)
    return {
        "attn_g": p["attn_norm_g"][l].reshape(1, D),
        "w_in": w_cols,
        "q_g": jnp.pad(p["q_norm_g"][l], (0, 2 * LANES - Q_LORA)).reshape(1, 2 * LANES),
        "w_uq": w_uq.reshape(2 * LANES, wq).astype(BF16),
        "kv_g": p["kv_norm_g"][l].reshape(1, KV_LORA),
        "w_kn": w_kn.reshape(KV_LORA, wq).astype(BF16),
        "w_kr": jnp.asarray(place.reshape(LANES, wq), BF16),
        "w_v": w_v.reshape(KV_LORA, HEADS * V_DIM).astype(BF16),
        "bd_c": _block_diag(four(np.cos(ang))),
        "bd_s": _block_diag(four(np.sin(ang))),
        "w_f": p["fourier_w"][l].astype(BF16),
        "bd_pool": _block_diag(p["pool_w"][l]).astype(BF16),
        "pool_scale": p["pool_scale"][l].reshape(1, GW),
        "conv_w": p["conv_w"][l],
        "conv_b": p["conv_b"][l].reshape(1, GW),
        "bd_wa": jnp.stack([_block_diag(p["lru_wa"][l, d]) for d in range(2)]).astype(BF16),
        "bd_wx": jnp.stack([_block_diag(p["lru_wx"][l, d]) for d in range(2)]).astype(BF16),
        "lru_ba": p["lru_ba"][l].reshape(2, 1, GW),
        "lru_bx": p["lru_bx"][l].reshape(2, 1, GW),
        "lru_lam": p["lru_lambda"][l].reshape(2, 1, GW),
        "out_g": p["out_norm_g"][l],
        "w_out": p["w_out"][l].astype(BF16),
        "ffn_g": p["ffn_norm_g"][l].reshape(1, D),
        "router_w": router_w,
        "router_b": router_b,
    }


def _routing_tables(route, prob, counts, tab):
    top_e, rank = route[0:TOP_K], route[TOP_K:2 * TOP_K]
    top_p = prob[0:TOP_K].reshape(-1)
    loc = route[2 * TOP_K:3 * TOP_K].reshape(-1)
    counts = counts.astype(I32)
    padded = (counts + TM_E - 1) // TM_E * TM_E
    pad_ends = jnp.cumsum(padded)
    pad_starts = pad_ends - padded
    experts = jnp.arange(N_EXP, dtype=I32)
    onehot = top_e[:, :, None] == experts
    dest = (jnp.sum(jnp.where(onehot, pad_starts, 0), axis=-1) + rank).reshape(T * TOP_K)
    n_used = (pad_ends[-1] // TM_E).astype(I32)
    blk = jnp.minimum(jnp.arange(NB_E, dtype=I32), n_used - 1) * TM_E
    block_e = jnp.minimum(jnp.sum(pad_ends[None, :] <= blk[:, None], axis=-1), N_EXP - 1).astype(I32)
    of_block = lambda a: jnp.sum(jnp.where(block_e[:, None] == experts, a, 0), axis=-1)
    valid = jnp.clip(of_block(pad_starts + counts) - blk, 0, TM_E).astype(I32)
    first = (blk == of_block(pad_starts)).astype(I32)
    later = (experts[None, :] > experts[:, None]) & (counts[None, :] > 0)
    next_of = jnp.min(jnp.where(later, experts[None, :], N_EXP), axis=-1)
    next_e = of_block(jnp.where(next_of < N_EXP, next_of, -1)).astype(I32)
    parity = (of_block(jnp.cumsum((counts > 0).astype(I32))) % 2).astype(I32)
    sched = (block_e, n_used.reshape(1), valid, first, next_e, parity)
    tab = tab.reshape(T // TM, N_EXP, LANES).astype(I32)
    before, nch, base = tab[:, :, 0], tab[:, :, 1], tab[:, :, 2]
    run_start = pad_starts[None, :] + before
    ends = base + nch
    ci = jnp.arange(MAX_CH, dtype=I32)
    e_of = jnp.minimum(jnp.sum(ends[:, None, :] <= ci[None, :, None], axis=-1), N_EXP - 1)
    pick = lambda a: jnp.sum(jnp.where(e_of[:, :, None] == experts, a[:, None, :], 0), axis=-1)
    chunk_src = jnp.clip(pick(run_start) + (ci[None, :] - pick(base)) * CH_C, 0, N_SLOTS)
    return dest, sched, loc, top_p, chunk_src.reshape(-1), ends[:, N_EXP - 1]


def kernel(x_prompt, x_sample, cache_ckv, cache_krope, state_lru, c, c_ctx, w_mod, b_mod, attn_norm_g, w_in, pool_w, pool_scale, q_norm_g, w_uq, kv_norm_g, w_ukv, fourier_w, conv_w, conv_b, lru_wa, lru_ba, lru_wx, lru_bx, lru_lambda, out_norm_g, w_out, ffn_norm_g, router_w, router_b, w_gate, b_gate, w_up, b_up, w_down, b_down, final_norm_g):
    params = dict(attn_norm_g=attn_norm_g, w_in=w_in, pool_w=pool_w, pool_scale=pool_scale,
                  q_norm_g=q_norm_g, w_uq=w_uq, kv_norm_g=kv_norm_g, w_ukv=w_ukv,
                  fourier_w=fourier_w, conv_w=conv_w, conv_b=conv_b, lru_wa=lru_wa, lru_ba=lru_ba,
                  lru_wx=lru_wx, lru_bx=lru_bx, lru_lambda=lru_lambda, out_norm_g=out_norm_g,
                  w_out=w_out, ffn_norm_g=ffn_norm_g, router_w=router_w, router_b=router_b,
                  w_gate=w_gate, b_gate=b_gate, w_up=w_up, b_up=b_up, w_down=w_down, b_down=b_down)
    x = (x_prompt.reshape(T_P, D), x_sample.reshape(T_S, D))
    cvec = jnp.concatenate([c_ctx[None, :], c, jnp.zeros((SUBLANES - 1 - DEC_BATCH, D), F32)], axis=0)
    mod = _modulation(cvec, w_mod, b_mod)
    rope = _rope_tables()
    dft_p, dft_s = _dft_tables(SEQ), _dft_tables(DEC_SEQ)
    tri = jnp.asarray(np.triu(np.ones((TM, TM), np.float32)), BF16)
    lower = jnp.asarray(np.tril(np.ones((N_EXP, N_EXP), np.float32), -1), BF16)
    final_g = final_norm_g.reshape(1, D)
    h0_prompt = jnp.zeros((BATCH, 2, GW), F32)

    new_ckv, new_krope, new_lru = [], [], []
    for l in range(DEPTH):
        lw = _layer_weights(l, params)
        zmix, q, k, v, ckv, kr = _front(x, mod[l], lw, rope)
        new_ckv.append(ckv.reshape(BATCH, SEQ, KV_LORA))
        new_krope.append(kr.reshape(BATCH, SEQ, QK_ROPE))

        yb_p = _attention(q, [(k, v, SEQ, 0)], BATCH, SEQ, SEQ, 0, group=PROMPT_GROUP)
        yc_p = _fourier(zmix, dft_p, lw, BATCH, SEQ, 0, group=PROMPT_GROUP)
        ya_p, yd_p, st_p = _seq_mixers(zmix, h0_prompt, lw, BATCH, SEQ, 0)
        new_lru.append(st_p)

        kr_ctx = jnp.pad(cache_krope[:, l].reshape(DEC_BATCH * PAST, QK_ROPE),
                         ((0, 0), (0, LANES - QK_ROPE)))
        k_ctx, v_ctx = _kv_expand(cache_ckv[:, l].reshape(DEC_BATCH * PAST, KV_LORA), kr_ctx, lw)
        yb_s = _attention(q, [(k_ctx, v_ctx, PAST, 0), (k, v, DEC_SEQ, T_P)],
                          DEC_BATCH, DEC_SEQ, 512, T_P)
        yc_s = _fourier(zmix, dft_s, lw, DEC_BATCH, DEC_SEQ, T_P)
        ya_s, yd_s, _ = _seq_mixers(zmix, state_lru[:, l], lw, DEC_BATCH, DEC_SEQ, T_P)

        x1, h2lin, route, prob, tab, counts = _post(
            x, mod[l], (ya_p, yb_p, yc_p, yd_p), (ya_s, yb_s, yc_s, yd_s), lw, tri, lower)
        dest, sched, loc, top_p, chunk_src, n_chunks = _routing_tables(
            route, prob, counts[:, 0], tab)
        xs = _gather_rows(dest, sched[1], h2lin)
        ys = _experts(l, sched, xs, params)
        x = _combine(loc, top_p, chunk_src, n_chunks, ys, x1, mod[l], final_g,
                     final=(l == DEPTH - 1))

    y_prompt, y_sample = x
    return (y_prompt.reshape(BATCH, SEQ, D), y_sample.reshape(DEC_BATCH, DEC_SEQ, D),
            jnp.stack(new_ckv, axis=1), jnp.stack(new_krope, axis=1), jnp.stack(new_lru, axis=1))
```

```python
import functools

import numpy as np
import jax
import jax.numpy as jnp
from jax import lax
from jax.experimental import pallas as pl
from jax.experimental.pallas import tpu as pltpu

F32 = jnp.float32
BF16 = jnp.bfloat16
I32 = jnp.int32

D = 1024
BATCH, SEQ = 32, 256
DEC_BATCH, DEC_SEQ, PAST = 2, 2048, 512
T_P = BATCH * SEQ
T_S = DEC_BATCH * DEC_SEQ
T = T_P + T_S
DEPTH = 2
GRID_W = 64
GW = 256
Q_LORA, KV_LORA, QK_NOPE, QK_ROPE, V_DIM, HEADS = 192, 128, 64, 32, 64, 4
HEAD_PAD = 128
ROPE_BASE = 10000.0
Q_SCALE = float((QK_NOPE + QK_ROPE) ** -0.5 * np.log2(np.e))
POOL_WINDOWS = (2, 4, 8, 16)
LRU_C = 8.0
N_EXP, TOP_K = 32, 4
LIMIT, ALPHA = 7.0, 1.702
EPS = 1e-6

LANES = 128
SUBLANES = 8
MIB = 1024 * 1024
V7X_VMEM_BYTES = 64 * MIB
VMEM_LIMIT = V7X_VMEM_BYTES * 7 // 8

TM = 512
N_PT = T_P // TM
TILES_PER_DEC = DEC_SEQ // TM
TM_E = 512
EXPERT_ROWS = (TM_E, TM_E // 2, TM_E // 4)
N_SLOTS = T * TOP_K + N_EXP * TM_E
NB_E = N_SLOTS // TM_E
ROW_F = D // LANES
TG = 2 * TM_E
GATHER_VMEM_LIMIT = 4 * (T * D + 2 * TG * D) + 4 * MIB
CH_C = 32
MAX_CH = TM * TOP_K // CH_C + N_EXP
GATHER_UNROLL = 32
COMBINE_UNROLL = 16
PROMPT_GROUP = 8
SEQ_PAD = 32
FRONT = 8

W_COLS = 1536
COL_Q, COL_KV, COL_KR = 1024, 1280, 1408


_PROMPT_ROWS = pl.BlockSpec((TM, D), lambda i: (jnp.minimum(i, N_PT - 1), 0))
_SAMPLE_ROWS = pl.BlockSpec((TM, D), lambda i: (jnp.maximum(i - N_PT, 0), 0))


def _cparams(sem, vmem=VMEM_LIMIT):
    return pltpu.CompilerParams(dimension_semantics=sem, vmem_limit_bytes=vmem)


def _bdot(a, b):
    return jnp.dot(a.astype(BF16), b.astype(BF16), preferred_element_type=F32)


def _split(a):
    hi = a.astype(BF16)
    lo = (a - hi.astype(F32)).astype(BF16)
    return hi, lo


def _dot3(a, b):
    ah, al = _split(a)
    bh, bl = _split(b)
    d = functools.partial(jnp.dot, preferred_element_type=F32)
    return d(ah, bh) + (d(al, bh) + d(ah, bl))


def _rms(x, n=None):
    n = x.shape[-1] if n is None else n
    return x * lax.rsqrt(jnp.sum(x * x, axis=-1, keepdims=True) * (1.0 / n) + EPS)


def _neg_expm1_double(x):
    t = jnp.tanh(x)
    return -2.0 * t / (1.0 - t)


def _sigmoid(x):
    return 0.5 * jnp.tanh(0.5 * x) + 0.5


def _tile_row(i):
    return jnp.where(i >= N_PT, 1 + (i - N_PT) // TILES_PER_DEC, 0)


def _mod_kernel(c_ref, w_ref, b_ref, o_ref):
    s = jax.nn.silu(c_ref[...])
    o_ref[0] = _dot3(s, w_ref[0]) + b_ref[0]


def _modulation(cvec, w_mod, b_mod):
    tn = 1536
    n = 6 * D
    return pl.pallas_call(
        _mod_kernel,
        out_shape=jax.ShapeDtypeStruct((DEPTH, SUBLANES, n), F32),
        grid=(DEPTH, n // tn),
        in_specs=[pl.BlockSpec((SUBLANES, D), lambda l, j: (0, 0)),
                  pl.BlockSpec((1, D, tn), lambda l, j: (l, 0, j)),
                  pl.BlockSpec((1, 1, tn), lambda l, j: (l, 0, j))],
        out_specs=pl.BlockSpec((1, SUBLANES, tn), lambda l, j: (l, 0, j)),
        compiler_params=_cparams(("parallel", "parallel")),
        name="modulation",
    )(cvec, w_mod, b_mod.reshape(DEPTH, 1, n))


def _front_kernel(xp_ref, xs_ref, mod_ref, g_ref, w_ref, qg_ref, wuq_ref, kvg_ref, wkn_ref, wkr_ref,
                  wv_ref, cq_ref, saq_ref, sbq_ref, ck_ref, sak_ref, sbk_ref,
                  zmix_ref, q_ref, k_ref, v_ref, ckv_ref, kr_ref):
    i = pl.program_id(0)
    row = _tile_row(i)
    shift1 = mod_ref[pl.ds(row, 1), 0:D]
    scale1 = mod_ref[pl.ds(row, 1), D:2 * D]
    h = _rms(jnp.where(i >= N_PT, xs_ref[...], xp_ref[...])) * g_ref[...]
    h = h * (1.0 + scale1) + shift1
    z = jnp.dot(h.astype(BF16), w_ref[...], preferred_element_type=F32)
    zmix_ref[...] = z[:, 0:4 * GW]
    qn = _rms(z[:, COL_Q:COL_Q + 2 * LANES], Q_LORA) * qg_ref[...]
    q = _bdot(qn, wuq_ref[...])
    ckv = _rms(z[:, COL_KV:COL_KV + KV_LORA]) * kvg_ref[...]
    ckv_b = ckv.astype(BF16)
    v_ref[...] = jnp.dot(ckv_b, wv_ref[...], preferred_element_type=F32).astype(BF16)
    kn = jnp.dot(ckv_b, wkn_ref[...], preferred_element_type=F32)
    kr = z[:, COL_KR:COL_KR + LANES]

    @pl.when(i < N_PT)
    def _():
        q_ref[...] = (q * Q_SCALE).astype(BF16)
        ckv_ref[...] = ckv
        kr_ref[...] = kr[:, 0:QK_ROPE]
        k_ref[...] = (kn + _bdot(kr, wkr_ref[...])).astype(BF16)

    @pl.when(i >= N_PT)
    def _():
        wq = HEADS * HEAD_PAD
        half = QK_ROPE // 2
        qr = (q * cq_ref[...] + pltpu.roll(q, wq - half, 1) * saq_ref[...]
              + pltpu.roll(q, half, 1) * sbq_ref[...])
        krr = (kr * ck_ref[...] + pltpu.roll(kr, LANES - half, 1) * sak_ref[...]
               + pltpu.roll(kr, half, 1) * sbk_ref[...])
        q_ref[...] = (qr * Q_SCALE).astype(BF16)
        k_ref[...] = (kn + _bdot(krr, wkr_ref[...])).astype(BF16)


def _front(x, mod_l, lw, rope):
    full = lambda shape: pl.BlockSpec(shape, lambda i: (0,) * len(shape))
    rows = lambda w: pl.BlockSpec((TM, w), lambda i: (i, 0))
    prompt_rows = lambda w: pl.BlockSpec((TM, w), lambda i: (jnp.minimum(i, N_PT - 1), 0))
    rope_rows = lambda w: pl.BlockSpec(
        (TM, w), lambda i: (jnp.maximum(i - N_PT, 0) % TILES_PER_DEC, 0))
    wq = HEADS * HEAD_PAD
    return pl.pallas_call(
        _front_kernel,
        out_shape=(jax.ShapeDtypeStruct((T, 4 * GW), F32),
                   jax.ShapeDtypeStruct((T, wq), BF16),
                   jax.ShapeDtypeStruct((T, wq), BF16),
                   jax.ShapeDtypeStruct((T, HEADS * V_DIM), BF16),
                   jax.ShapeDtypeStruct((T_P, KV_LORA), F32),
                   jax.ShapeDtypeStruct((T_P, QK_ROPE), F32)),
        grid=(T // TM,),
        in_specs=[_PROMPT_ROWS, _SAMPLE_ROWS, full((SUBLANES, 6 * D)), full((1, D)), full((D, W_COLS)),
                  full((1, 2 * LANES)), full((2 * LANES, wq)), full((1, KV_LORA)),
                  full((KV_LORA, wq)), full((LANES, wq)), full((KV_LORA, HEADS * V_DIM)),
                  rope_rows(wq), rope_rows(wq), rope_rows(wq),
                  rope_rows(LANES), rope_rows(LANES), rope_rows(LANES)],
        out_specs=(rows(4 * GW), rows(wq), rows(wq), rows(HEADS * V_DIM), prompt_rows(KV_LORA),
                   prompt_rows(QK_ROPE)),
        compiler_params=_cparams(("arbitrary",)),
        name="front",
    )(*x, mod_l, lw["attn_g"], lw["w_in"], lw["q_g"], lw["w_uq"], lw["kv_g"], lw["w_kn"],
      lw["w_kr"], lw["w_v"], *rope)


def _kvexp_kernel(ckv_ref, kr_ref, wkn_ref, wkr_ref, wv_ref, k_ref, v_ref):
    ckv_b = ckv_ref[...].astype(BF16)
    v_ref[...] = jnp.dot(ckv_b, wv_ref[...], preferred_element_type=F32).astype(BF16)
    kn = jnp.dot(ckv_b, wkn_ref[...], preferred_element_type=F32)
    k_ref[...] = (kn + _bdot(kr_ref[...], wkr_ref[...])).astype(BF16)


def _kv_expand(ckv, kr_pad, lw):
    n = ckv.shape[0]
    wq = HEADS * HEAD_PAD
    full = lambda shape: pl.BlockSpec(shape, lambda i: (0,) * len(shape))
    return pl.pallas_call(
        _kvexp_kernel,
        out_shape=(jax.ShapeDtypeStruct((n, wq), BF16),
                   jax.ShapeDtypeStruct((n, HEADS * V_DIM), BF16)),
        grid=(1,),
        in_specs=[full((n, KV_LORA)), full((n, LANES)), full((KV_LORA, wq)), full((LANES, wq)),
                  full((KV_LORA, HEADS * V_DIM))],
        out_specs=(full((n, wq)), full((n, HEADS * V_DIM))),
        compiler_params=_cparams(("arbitrary",)),
        name="kv_expand",
    )(ckv, kr_pad, lw["w_kn"], lw["w_kr"], lw["w_v"])


def _attn_kernel(q_ref, *refs, group, tq, seg_keys):
    o_ref = refs[-1]
    dims = (((1,), (1,)), ((), ()))
    for g in range(group):
        outs = []
        for h in range(HEADS):
            qh = q_ref[g * tq:(g + 1) * tq, h * HEAD_PAD:(h + 1) * HEAD_PAD]
            scores = []
            for si, nk in enumerate(seg_keys):
                kh = refs[2 * si][g * nk:(g + 1) * nk, h * HEAD_PAD:(h + 1) * HEAD_PAD]
                scores.append(lax.dot_general(qh, kh, dims, preferred_element_type=F32))
            m = functools.reduce(jnp.maximum, [jnp.max(s, axis=-1, keepdims=True) for s in scores])
            l, acc = None, None
            for si, (nk, s) in enumerate(zip(seg_keys, scores)):
                p = jnp.exp2(s - m)
                vh = refs[2 * si + 1][g * nk:(g + 1) * nk, h * V_DIM:(h + 1) * V_DIM]
                part = jnp.dot(p.astype(BF16), vh, preferred_element_type=F32)
                psum = jnp.sum(p, axis=-1, keepdims=True)
                l, acc = (psum, part) if l is None else (l + psum, acc + part)
            outs.append(acc / l)
        o_ref[g * tq:(g + 1) * tq, :] = jnp.concatenate(outs, axis=-1)


def _attention(q, segments, n_batch, seq, tq, row0, group=1):
    nq = seq // tq
    wq = HEADS * HEAD_PAD
    in_specs = [pl.BlockSpec((group * tq, wq), lambda b, i: (row0 // (group * tq) + b * nq + i, 0))]
    operands = [q]
    for k_rows, v_rows, nk, first in segments:
        index = lambda b, i, base=first // (group * nk): (base + b, 0)
        in_specs += [pl.BlockSpec((group * nk, wq), index),
                     pl.BlockSpec((group * nk, HEADS * V_DIM), index)]
        operands += [k_rows, v_rows]
    return pl.pallas_call(
        functools.partial(_attn_kernel, group=group, tq=tq,
                          seg_keys=tuple(seg[2] for seg in segments)),
        out_shape=jax.ShapeDtypeStruct((n_batch * seq, GW), F32),
        grid=(n_batch // group, nq),
        in_specs=in_specs,
        out_specs=pl.BlockSpec((group * tq, GW), lambda b, i: (b * nq + i, 0)),
        compiler_params=_cparams(("parallel", "parallel")),
        name="attention",
    )(*operands)


def _fourier_kernel(z_ref, c_ref, s_ref, bdc_ref, bds_ref, w_ref, o_ref, xc_ref, xs_ref, *,
                    norm, group, seq, ts):
    b = pl.program_id(1)

    @pl.when(pl.program_id(0) == 0)
    def _():
        xb = z_ref[...].astype(BF16)
        xc_ref[b] = _bdot(xb, bdc_ref[...]).astype(BF16)
        xs_ref[b] = _bdot(xb, bds_ref[...]).astype(BF16)

    cb, sb = c_ref[...].astype(BF16), s_ref[...].astype(BF16)
    for g in range(group):
        f = (jnp.dot(cb, xc_ref[b, g * seq:(g + 1) * seq, :], preferred_element_type=F32)
             - jnp.dot(sb, xs_ref[b, g * seq:(g + 1) * seq, :], preferred_element_type=F32)) * norm
        o_ref[g * ts:(g + 1) * ts, :] = _bdot(f, w_ref[...])


def _fourier(zmix, consts, lw, n_batch, seq, row0, group=1):
    ts = min(seq, 512)
    nj = seq // ts
    assert group == 1 or nj == 1
    nb = n_batch // group
    cmat, smat = consts
    full = lambda shape: pl.BlockSpec(shape, lambda j, b: (0,) * len(shape))
    z_spec = pl.BlockSpec((group * seq, GW),
                          lambda j, b: (row0 // (group * seq) + jnp.where(j == 0, b, nb - 1), 1))
    return pl.pallas_call(
        functools.partial(_fourier_kernel, norm=float((seq * (GW // 4)) ** -0.5), group=group,
                          seq=seq, ts=ts),
        out_shape=jax.ShapeDtypeStruct((n_batch * seq, GW), F32),
        grid=(nj, nb),
        in_specs=[z_spec,
                  pl.BlockSpec((ts, seq), lambda j, b: (j, 0)),
                  pl.BlockSpec((ts, seq), lambda j, b: (j, 0)),
                  full((GW, GW)), full((GW, GW)), full((GW, GW))],
        out_specs=pl.BlockSpec((group * ts, GW), lambda j, b: (b * nj + j, 0)),
        scratch_shapes=[pltpu.VMEM((nb, group * seq, GW), BF16),
                        pltpu.VMEM((nb, group * seq, GW), BF16)],
        compiler_params=_cparams(("arbitrary", "arbitrary")),
        name="fourier",
    )(zmix, cmat, smat, lw["bd_c"], lw["bd_s"], lw["w_f"])


def _seq_kernel(za_ref, zr_ref, zg_ref, h0_ref, icnt_ref, wp_ref, ps_ref, cw_ref, cb_ref, wa_ref,
                ba_ref, wx_ref, bx_ref, lam_ref, ya_ref, yd_ref, st_ref,
                pa_ref, pb_ref, xp_ref, a_ref, b_ref, *, seq):
    n = seq + SEQ_PAD
    span = seq + 2 * FRONT
    zeros_pad = jnp.zeros((n, GW), F32)

    za = za_ref[...]
    pa_ref[...] = zeros_pad
    pb_ref[...] = zeros_pad
    pa_ref[FRONT:FRONT + seq, :] = za
    pb_ref[0:span, :] = pa_ref[0:span, :] + pa_ref[1:span + 1, :]
    win2 = pb_ref[FRONT - 1:FRONT - 1 + seq, :]
    pa_ref[0:span, :] = pb_ref[0:span, :] + pb_ref[2:span + 2, :]
    win4 = pa_ref[FRONT - 2:FRONT - 2 + seq, :]
    pb_ref[0:span, :] = pa_ref[0:span, :] + pa_ref[4:span + 4, :]
    win8 = pb_ref[FRONT - 4:FRONT - 4 + seq, :]
    pa_ref[0:span, :] = pb_ref[0:span, :] + pb_ref[8:span + 8, :]
    win16 = pa_ref[FRONT - 8:FRONT - 8 + seq, :]
    grp = lax.broadcasted_iota(I32, (1, GW), 1) // (GW // 4)
    win = jnp.where(grp == 0, win2, jnp.where(grp == 1, win4, jnp.where(grp == 2, win8, win16)))
    dlt = win * icnt_ref[...] - za
    ya_ref[...] = _bdot(dlt, wp_ref[...]) * ps_ref[...]

    xp_ref[...] = zeros_pad
    xp_ref[FRONT:FRONT + seq, :] = zr_ref[...]
    xc = cb_ref[...] + cw_ref[0:1, :] * xp_ref[FRONT - 2:FRONT - 2 + seq, :]
    for kk in range(1, 4):
        xc = xc + cw_ref[kk:kk + 1, :] * xp_ref[FRONT - 2 + kk:FRONT - 2 + kk + seq, :]
    xcb = xc.astype(BF16)
    n_grp = seq // SUBLANES
    sub = lax.broadcasted_iota(I32, (n_grp, SUBLANES, GW), 1)

    for d in range(2):
        r = _sigmoid(jnp.dot(xcb, wa_ref[d], preferred_element_type=F32) + ba_ref[d])
        ig = _sigmoid(jnp.dot(xcb, wx_ref[d], preferred_element_type=F32) + bx_ref[d])
        log_a = (-LRU_C) * r * jax.nn.softplus(-lam_ref[d])
        a = jnp.exp(log_a).reshape(n_grp, SUBLANES, GW)
        b = (jnp.sqrt(_neg_expm1_double(log_a)) * (ig * xc)).reshape(n_grp, SUBLANES, GW)
        for k in (1, 2, 4):
            shift = k if d == 0 else SUBLANES - k
            m = sub >= k if d == 0 else sub < SUBLANES - k
            ap, bp = pltpu.roll(a, shift, 1), pltpu.roll(b, shift, 1)
            b = jnp.where(m, a * bp + b, b)
            a = jnp.where(m, a * ap, a)
        a_ref[d] = a.reshape(seq, GW)
        b_ref[d] = b.reshape(seq, GW)

    def step(g, carry):
        fwd, bwd = carry
        off_f = pl.multiple_of(g * SUBLANES, SUBLANES)
        off_b = pl.multiple_of((n_grp - 1 - g) * SUBLANES, SUBLANES)
        hf = a_ref[0, pl.ds(off_f, SUBLANES), :] * fwd + b_ref[0, pl.ds(off_f, SUBLANES), :]
        hb = a_ref[1, pl.ds(off_b, SUBLANES), :] * bwd + b_ref[1, pl.ds(off_b, SUBLANES), :]
        b_ref[0, pl.ds(off_f, SUBLANES), :] = hf
        b_ref[1, pl.ds(off_b, SUBLANES), :] = hb
        return (jnp.broadcast_to(hf[SUBLANES - 1:SUBLANES, :], (SUBLANES, GW)),
                jnp.broadcast_to(hb[0:1, :], (SUBLANES, GW)))

    start = tuple(jnp.broadcast_to(h0_ref[0, d:d + 1, :], (SUBLANES, GW)) for d in range(2))
    last_f, last_b = lax.fori_loop(0, n_grp, step, start)
    st_ref[0, 0:1, :] = last_f[0:1, :]
    st_ref[0, 1:2, :] = last_b[0:1, :]
    yd_ref[...] = (b_ref[0] + b_ref[1]) * jax.nn.gelu(zg_ref[...])


def _pool_inverse_counts(seq):
    pos = np.arange(seq)[:, None]
    half = np.repeat(np.array(POOL_WINDOWS) // 2, GW // len(POOL_WINDOWS))[None, :]
    cnt = np.minimum(pos + half, seq) - np.maximum(pos - half, 0)
    return jnp.asarray(1.0 / cnt, F32)


def _seq_mixers(zmix, h0, lw, n_batch, seq, row0):
    full = lambda shape: pl.BlockSpec(shape, lambda b: (0,) * len(shape))
    col = lambda c: pl.BlockSpec((seq, GW), lambda b: (row0 // seq + b, c))
    out_rows = pl.BlockSpec((seq, GW), lambda b: (b, 0))
    pad = pltpu.VMEM((seq + SEQ_PAD, GW), F32)
    return pl.pallas_call(
        functools.partial(_seq_kernel, seq=seq),
        out_shape=(jax.ShapeDtypeStruct((n_batch * seq, GW), F32),
                   jax.ShapeDtypeStruct((n_batch * seq, GW), F32),
                   jax.ShapeDtypeStruct((n_batch, 2, GW), F32)),
        grid=(n_batch,),
        in_specs=[col(0), col(2), col(3), pl.BlockSpec((1, 2, GW), lambda b: (b, 0, 0)),
                  full((seq, GW)), full((GW, GW)), full((1, GW)), full((4, GW)), full((1, GW)),
                  full((2, GW, GW)), full((2, 1, GW)), full((2, GW, GW)), full((2, 1, GW)),
                  full((2, 1, GW))],
        out_specs=(out_rows, out_rows, pl.BlockSpec((1, 2, GW), lambda b: (b, 0, 0))),
        scratch_shapes=[pad, pad, pad, pltpu.VMEM((2, seq, GW), F32), pltpu.VMEM((2, seq, GW), F32)],
        compiler_params=_cparams(("parallel",)),
        name="seq_mixers",
    )(zmix, zmix, zmix, h0, _pool_inverse_counts(seq), lw["bd_pool"], lw["pool_scale"], lw["conv_w"],
      lw["conv_b"], lw["bd_wa"], lw["lru_ba"], lw["bd_wx"], lw["lru_bx"], lw["lru_lam"])


def _post_kernel(xp_ref, xs_ref, mod_ref, yap_ref, ybp_ref, ycp_ref, ydp_ref, yas_ref, ybs_ref, ycs_ref,
                 yds_ref, og_ref, wo_ref, fg_ref, rw_ref, rb_ref, tri_ref, lower_ref,
                 x1_ref, h2p_ref, route_ref, prob_ref, tab_ref, cnt_ref, carry_ref):
    i = pl.program_id(0)
    is_s = i >= N_PT
    row = _tile_row(i)
    gate1 = mod_ref[pl.ds(row, 1), 2 * D:3 * D]
    shift2 = mod_ref[pl.ds(row, 1), 3 * D:4 * D]
    scale2 = mod_ref[pl.ds(row, 1), 4 * D:5 * D]

    @pl.when(i == 0)
    def _():
        carry_ref[...] = jnp.zeros_like(carry_ref)

    groups = []
    for gi, (p_ref, s_ref) in enumerate(((yap_ref, yas_ref), (ybp_ref, ybs_ref),
                                         (ycp_ref, ycs_ref), (ydp_ref, yds_ref))):
        y = jnp.where(is_s, s_ref[...], p_ref[...])
        groups.append((_rms(y) * og_ref[gi:gi + 1, :]).astype(BF16))
    ycat = jnp.concatenate(groups, axis=-1)
    x = jnp.where(is_s, xs_ref[...], xp_ref[...])
    x1 = x + gate1 * jnp.dot(ycat, wo_ref[...], preferred_element_type=F32)
    x1_ref[...] = x1
    h2 = _rms(x1) * fg_ref[...]
    h2 = h2 * (1.0 + scale2) + shift2

    for j in range(ROW_F):
        h2p_ref[pl.ds(j, TM, stride=ROW_F), :] = h2[:, j * LANES:(j + 1) * LANES]

    h_hi, h_lo = _split(h2)
    both = jnp.dot(h_hi, rw_ref[...], preferred_element_type=F32)
    cross = both[:, LANES:2 * LANES] + jnp.dot(h_lo, rw_ref[:, 0:LANES], preferred_element_type=F32)
    logits = both[:, 0:LANES] + cross + rb_ref[...]
    lt = logits.T[0:N_EXP, :]
    expert = lax.broadcasted_iota(I32, (N_EXP, TM), 0).astype(F32)
    neg = jnp.float32(-jnp.inf)
    cur = lt
    sel, vals, idxs = [], [], []
    for _ in range(TOP_K):
        m = jnp.max(cur, axis=0, keepdims=True)
        idx = jnp.min(jnp.where(cur == m, expert, float(N_EXP)), axis=0, keepdims=True)
        hit = expert == idx
        sel.append(hit)
        vals.append(m)
        idxs.append(idx)
        cur = jnp.where(hit, neg, cur)
    exps = [jnp.exp(v - vals[0]) for v in vals]
    denom = exps[0] + exps[1] + exps[2] + exps[3]
    onehot = jnp.where(sel[0] | sel[1] | sel[2] | sel[3], 1.0, 0.0)
    cum_l = jnp.dot(onehot.astype(BF16), tri_ref[...], preferred_element_type=F32)
    before = carry_ref[:, 0:1]
    cum = cum_l + before
    n_tile = cum_l[:, TM - 1:TM]
    nch = jnp.floor((n_tile + (CH_C - 1.0)) * (1.0 / CH_C))
    nch_b = jnp.broadcast_to(nch, (N_EXP, LANES)).astype(BF16)
    base = jnp.dot(lower_ref[...], nch_b, preferred_element_type=F32)[:, 0:1]
    rows_e, rows_r, rows_l, rows_p = [], [], [], []
    for k in range(TOP_K):
        rows_e.append(idxs[k].astype(I32))
        rows_r.append(jnp.sum(jnp.where(sel[k], cum - 1.0, 0.0), axis=0, keepdims=True).astype(I32))
        rows_l.append(jnp.sum(jnp.where(sel[k], (base * CH_C + cum_l - 1.0) * ROW_F, 0.0), axis=0,
                              keepdims=True).astype(I32))
        rows_p.append(exps[k] / denom)
    route_ref[...] = jnp.concatenate(rows_e + rows_r + rows_l + [jnp.zeros((TOP_K, TM), I32)], axis=0)
    prob_ref[...] = jnp.concatenate(rows_p + [jnp.zeros((SUBLANES - TOP_K, TM), F32)], axis=0)
    lane = lax.broadcasted_iota(I32, (N_EXP, LANES), 1)
    tab_ref[...] = jnp.where(lane == 0, before, jnp.where(lane == 1, nch, base))
    new_carry = jnp.broadcast_to(cum[:, TM - 1:TM], (N_EXP, LANES))
    carry_ref[...] = new_carry
    cnt_ref[...] = new_carry


def _post(x, mod_l, ys_prompt, ys_sample, lw, tri, lower):
    full = lambda shape: pl.BlockSpec(shape, lambda i: (0,) * len(shape))
    rows = lambda w: pl.BlockSpec((TM, w), lambda i: (i, 0))
    prow = pl.BlockSpec((TM, GW), lambda i: (jnp.minimum(i, N_PT - 1), 0))
    srow = pl.BlockSpec((TM, GW), lambda i: (jnp.maximum(i - N_PT, 0), 0))
    return pl.pallas_call(
        _post_kernel,
        out_shape=(jax.ShapeDtypeStruct((T, D), F32),
                   jax.ShapeDtypeStruct((T * ROW_F, LANES), F32),
                   jax.ShapeDtypeStruct((4 * TOP_K, T), I32),
                   jax.ShapeDtypeStruct((SUBLANES, T), F32),
                   jax.ShapeDtypeStruct((T // TM * N_EXP, LANES), F32),
                   jax.ShapeDtypeStruct((N_EXP, LANES), F32)),
        grid=(T // TM,),
        in_specs=[_PROMPT_ROWS, _SAMPLE_ROWS, full((SUBLANES, 6 * D)),
                  prow, prow, prow, prow, srow, srow, srow, srow,
                  full((4, GW)), full((D, D)), full((1, D)), full((D, 2 * LANES)), full((1, LANES)),
                  full((TM, TM)), full((N_EXP, N_EXP))],
        out_specs=(rows(D), pl.BlockSpec((TM * ROW_F, LANES), lambda i: (i, 0)),
                   pl.BlockSpec((4 * TOP_K, TM), lambda i: (0, i)),
                   pl.BlockSpec((SUBLANES, TM), lambda i: (0, i)),
                   pl.BlockSpec((N_EXP, LANES), lambda i: (i, 0)), full((N_EXP, LANES))),
        scratch_shapes=[pltpu.VMEM((N_EXP, LANES), F32)],
        compiler_params=_cparams(("arbitrary",)),
        name="post",
    )(*x, mod_l, *ys_prompt, *ys_sample, lw["out_g"], lw["w_out"], lw["ffn_g"], lw["router_w"],
      lw["router_b"], tri, lower)


def _gather_kernel(dest_ref, nused_ref, zeros_hbm, src_hbm, o_ref, stok_ref, src_ref, sem, src_sem):
    i = pl.program_id(0)

    @pl.when(i == 0)
    def _():
        load = pltpu.make_async_copy(src_hbm, src_ref, src_sem)
        load.start()
        init = pltpu.make_async_copy(zeros_hbm, stok_ref, sem)
        init.start()
        init.wait()

        def scatter(t8, c):
            row0 = t8 * (SUBLANES * ROW_F)
            for k in range(TOP_K):
                for u in range(SUBLANES):
                    stok_ref[dest_ref[k * T + t8 * SUBLANES + u]] = row0 + u * ROW_F
            return c
        lax.fori_loop(0, T // SUBLANES, scatter, 0)
        load.wait()

    live = i * (TG // TM_E) < nused_ref[0]

    @pl.when(live)
    def _():
        def rows(r16, c):
            r0 = pl.multiple_of(r16 * GATHER_UNROLL, GATHER_UNROLL)
            for u in range(GATHER_UNROLL):
                src = pl.multiple_of(stok_ref[i * TG + r0 + u], ROW_F)
                dst = pl.multiple_of(r0 * ROW_F, GATHER_UNROLL * ROW_F) + u * ROW_F
                o_ref[pl.ds(dst, ROW_F), :] = src_ref[pl.ds(src, ROW_F), :]
            return c
        lax.fori_loop(0, TG // GATHER_UNROLL, rows, 0)

    @pl.when(jnp.logical_not(live))
    def _():
        o_ref[...] = jnp.zeros_like(o_ref)


def _gather_rows(dest, n_used, h2lin):
    return pl.pallas_call(
        _gather_kernel,
        out_shape=jax.ShapeDtypeStruct((N_SLOTS * ROW_F, LANES), F32),
        grid_spec=pltpu.PrefetchScalarGridSpec(
            num_scalar_prefetch=2,
            grid=(N_SLOTS // TG,),
            in_specs=[pl.BlockSpec(memory_space=pl.ANY), pl.BlockSpec(memory_space=pl.ANY)],
            out_specs=pl.BlockSpec((TG * ROW_F, LANES), lambda i, d, nu: (i, 0)),
            scratch_shapes=[pltpu.SMEM((N_SLOTS,), I32), pltpu.VMEM((T * ROW_F, LANES), F32),
                            pltpu.SemaphoreType.DMA(()), pltpu.SemaphoreType.DMA(())]),
        compiler_params=_cparams(("arbitrary",), GATHER_VMEM_LIMIT),
        name="moe_gather",
    )(dest, n_used, jnp.zeros((N_SLOTS,), I32), h2lin)


def _expert_rows(n, x_ref, w_ref, slot, bg_ref, bu_ref, bd_ref, o_ref):
    x = jnp.concatenate([x_ref[pl.ds(j, n, stride=ROW_F), :].astype(BF16) for j in range(ROW_F)],
                        axis=-1)
    g = jnp.dot(x, w_ref[slot, 0].astype(BF16), preferred_element_type=F32) + bg_ref[0, 0]
    u = jnp.dot(x, w_ref[slot, 1].astype(BF16), preferred_element_type=F32) + bu_ref[0, 0]
    g = jnp.minimum(g, LIMIT)
    u = jnp.clip(u, -LIMIT, LIMIT)
    act = (u + 1.0) * (g * jax.nn.sigmoid(ALPHA * g))
    y = jnp.dot(act.astype(BF16), w_ref[slot, 2].astype(BF16),
                preferred_element_type=F32) + bd_ref[0, 0]
    for j in range(ROW_F):
        o_ref[pl.ds(j, n, stride=ROW_F), :] = y[:, j * LANES:(j + 1) * LANES]
    if n < TM_E:
        o_ref[n * ROW_F:TM_E * ROW_F, :] = jnp.zeros(((TM_E - n) * ROW_F, LANES), F32)


def _expert_kernel(be_ref, nused_ref, valid_ref, first_ref, next_ref, par_ref,
                   x_ref, wg_hbm, bg_ref, wu_hbm, bu_ref, wd_hbm, bd_ref, o_ref, w_ref, sems, *, layer):
    i = pl.program_id(0)
    ib = jnp.minimum(i, NB_E - 1)
    live = i < nused_ref[0]
    valid = valid_ref[ib]
    expert = be_ref[ib]
    slot = par_ref[ib]

    def fetch(e, s):
        return [pltpu.make_async_copy(src.at[layer, e], w_ref.at[s, m], sems.at[s, m])
                for m, src in enumerate((wg_hbm, wu_hbm, wd_hbm))]

    @pl.when(i == 0)
    def _():
        for cp in fetch(expert, slot):
            cp.start()

    @pl.when(live & (first_ref[ib] == 1))
    def _():
        @pl.when(next_ref[ib] >= 0)
        def _():
            for cp in fetch(next_ref[ib], 1 - slot):
                cp.start()

        for cp in fetch(expert, slot):
            cp.wait()

    for n_idx, n in enumerate(EXPERT_ROWS):
        fits = valid <= n
        if n_idx + 1 < len(EXPERT_ROWS):
            fits = fits & (valid > EXPERT_ROWS[n_idx + 1])

        @pl.when(live & fits)
        def _(n=n):
            _expert_rows(n, x_ref, w_ref, slot, bg_ref, bu_ref, bd_ref, o_ref)

    @pl.when(jnp.logical_not(live))
    def _():
        o_ref[...] = jnp.zeros_like(o_ref)


def _experts(l, sched, xs, p):
    last = NB_E - 1
    bspec = pl.BlockSpec((1, 1, 1, D), lambda i, be, *_: (l, be[jnp.minimum(i, last)], 0, 0))
    hbm = pl.BlockSpec(memory_space=pl.ANY)
    bias = lambda b: b.reshape(DEPTH, N_EXP, 1, D)
    return pl.pallas_call(
        functools.partial(_expert_kernel, layer=l),
        out_shape=jax.ShapeDtypeStruct(((NB_E + 1) * TM_E * ROW_F, LANES), F32),
        grid_spec=pltpu.PrefetchScalarGridSpec(
            num_scalar_prefetch=len(sched),
            grid=(NB_E + 1,),
            in_specs=[pl.BlockSpec((TM_E * ROW_F, LANES),
                                   lambda i, be, nu, *_: (jnp.minimum(i, nu[0] - 1), 0)),
                      hbm, bspec, hbm, bspec, hbm, bspec],
            out_specs=pl.BlockSpec((TM_E * ROW_F, LANES), lambda i, *_: (i, 0)),
            scratch_shapes=[pltpu.VMEM((2, 3, D, D), F32), pltpu.SemaphoreType.DMA((2, 3))]),
        compiler_params=_cparams(("arbitrary",)),
        name="moe_experts",
    )(*sched, xs, p["w_gate"], bias(p["b_gate"]), p["w_up"], bias(p["b_up"]),
      p["w_down"], bias(p["b_down"]))


def _combine_kernel(loc_ref, p_ref, csrc_ref, nch_ref, ys_hbm, x_ref, mod_ref, fg_ref, *rest, final):
    *outs, buf_ref, acc_ref, sems = rest
    tb = pl.program_id(0)
    n_tb = pl.num_programs(0)
    slot = tb % 2
    row = _tile_row(tb)
    gate2 = mod_ref[pl.ds(row, 1), 5 * D:6 * D]
    chunk_rows = CH_C * ROW_F

    def chunk_copy(t, c, s):
        src = pl.multiple_of(csrc_ref[t * MAX_CH + c] * ROW_F, ROW_F)
        dst = pl.multiple_of(c * chunk_rows, chunk_rows)
        return pltpu.make_async_copy(ys_hbm.at[pl.ds(src, chunk_rows)],
                                     buf_ref.at[s, pl.ds(dst, chunk_rows)], sems.at[s])

    def issue(t, s):
        def body(c, carry):
            chunk_copy(t, c, s).start()
            return carry
        lax.fori_loop(0, nch_ref[t], body, 0)

    @pl.when(tb == 0)
    def _():
        issue(0, 0)

    @pl.when(tb + 1 < n_tb)
    def _():
        issue(tb + 1, 1 - slot)

    def drain(c, carry):
        chunk_copy(tb, c, slot).wait()
        return carry
    lax.fori_loop(0, nch_ref[tb], drain, 0)

    def tokens(rg, carry):
        r0 = pl.multiple_of(rg * COMBINE_UNROLL, COMBINE_UNROLL)
        for u in range(COMBINE_UNROLL):
            tok = tb * TM + r0 + u
            acc = None
            for k in range(TOP_K):
                off = pl.multiple_of(loc_ref[k * T + tok], ROW_F)
                term = buf_ref[slot, pl.ds(off, ROW_F), :] * p_ref[k * T + tok]
                acc = term if acc is None else acc + term
            dst = pl.multiple_of(r0 * ROW_F, COMBINE_UNROLL * ROW_F) + u * ROW_F
            acc_ref[pl.ds(dst, ROW_F), :] = acc
        return carry
    lax.fori_loop(0, TM // COMBINE_UNROLL, tokens, 0)

    moe = jnp.concatenate([acc_ref[pl.ds(j, TM, stride=ROW_F), :] for j in range(ROW_F)], axis=-1)
    x2 = x_ref[...] + gate2 * moe
    if final:
        x2 = _rms(x2) * fg_ref[...]
    prompt_ref, sample_ref = outs

    @pl.when(tb < N_PT)
    def _():
        prompt_ref[...] = x2

    @pl.when(tb >= N_PT)
    def _():
        sample_ref[...] = x2


def _combine(loc, top_p, chunk_src, n_chunks, ys, x1, mod_l, final_g, final):
    full = lambda shape: pl.BlockSpec(shape, lambda i, *_: (0,) * len(shape))
    rows = lambda w: pl.BlockSpec((TM, w), lambda i, *_: (i, 0))
    out_shape = (jax.ShapeDtypeStruct((T_P, D), F32), jax.ShapeDtypeStruct((T_S, D), F32))
    out_specs = (pl.BlockSpec((TM, D), lambda i, *_: (jnp.minimum(i, N_PT - 1), 0)),
                 pl.BlockSpec((TM, D), lambda i, *_: (jnp.maximum(i - N_PT, 0), 0)))
    return pl.pallas_call(
        functools.partial(_combine_kernel, final=final),
        out_shape=out_shape,
        grid_spec=pltpu.PrefetchScalarGridSpec(
            num_scalar_prefetch=4,
            grid=(T // TM,),
            in_specs=[pl.BlockSpec(memory_space=pl.ANY), rows(D), full((SUBLANES, 6 * D)),
                      full((1, D))],
            out_specs=out_specs,
            scratch_shapes=[pltpu.VMEM((2, MAX_CH * CH_C * ROW_F, LANES), F32),
                            pltpu.VMEM((TM * ROW_F, LANES), F32),
                            pltpu.SemaphoreType.DMA((2,))]),
        compiler_params=_cparams(("arbitrary",)),
        name="moe_combine",
    )(loc, top_p, chunk_src, n_chunks, ys, x1, mod_l, final_g)


def _rope_tables():
    rows = DEC_SEQ // GRID_W
    r = np.repeat(np.arange(rows, dtype=np.float64), GRID_W)
    c = np.tile(np.arange(GRID_W, dtype=np.float64), rows)
    n_freq = QK_ROPE // 4
    inv = (np.float32(ROPE_BASE) ** (-np.arange(n_freq, dtype=np.float32) / n_freq)).astype(np.float64)
    ang = np.concatenate([r[:, None] * inv, c[:, None] * inv], axis=-1).astype(np.float32)
    cos, sin = np.cos(ang.astype(np.float64)), np.sin(ang.astype(np.float64))
    half = QK_ROPE // 2

    def place(width, start):
        cf = np.ones((DEC_SEQ, width), np.float32)
        sa = np.zeros((DEC_SEQ, width), np.float32)
        sb = np.zeros((DEC_SEQ, width), np.float32)
        for s0 in start:
            cf[:, s0:s0 + half] = cos
            cf[:, s0 + half:s0 + 2 * half] = cos
            sa[:, s0:s0 + half] = -sin
            sb[:, s0 + half:s0 + 2 * half] = sin
        return jnp.asarray(cf), jnp.asarray(sa), jnp.asarray(sb)

    return (*place(HEADS * HEAD_PAD, [h * HEAD_PAD + QK_NOPE for h in range(HEADS)]),
            *place(LANES, [0]))


def _dft_tables(seq):
    kn = (np.arange(seq, dtype=np.int64)[:, None] * np.arange(seq, dtype=np.int64)[None, :]) % seq
    ang = 2.0 * np.pi * kn.astype(np.float64) / seq
    return jnp.asarray(np.cos(ang), F32), jnp.asarray(np.sin(ang), F32)


def _block_diag(blocks):
    g, n, _ = blocks.shape
    eye = jnp.eye(g, dtype=blocks.dtype)
    return jnp.einsum("gij,gh->gihj", blocks, eye).reshape(g * n, g * n)


def _layer_weights(l, p):
    w_in = p["w_in"][l]
    za, zq, zkv, zkr, zf, zr, zg = jnp.split(
        w_in, np.cumsum([GW, Q_LORA, KV_LORA, QK_ROPE, GW, GW])[:], axis=1)
    zpad = lambda n: jnp.zeros((D, n), F32)
    w_cols = jnp.concatenate([za, zf, zr, zg, zq, zpad(2 * LANES - Q_LORA), zkv, zkr,
                              zpad(LANES - QK_ROPE)], axis=1).astype(BF16)
    wq = HEADS * HEAD_PAD
    w_uq = p["w_uq"][l].reshape(Q_LORA, HEADS, QK_NOPE + QK_ROPE)
    w_uq = jnp.pad(w_uq, ((0, 2 * LANES - Q_LORA), (0, 0), (0, HEAD_PAD - QK_NOPE - QK_ROPE)))
    w_ukv = p["w_ukv"][l].reshape(KV_LORA, HEADS, QK_NOPE + V_DIM)
    w_kn = jnp.pad(w_ukv[:, :, :QK_NOPE], ((0, 0), (0, 0), (0, HEAD_PAD - QK_NOPE)))
    w_v = w_ukv[:, :, QK_NOPE:]
    place = np.zeros((LANES, HEADS, HEAD_PAD), np.float32)
    for h in range(HEADS):
        place[np.arange(QK_ROPE), h, QK_NOPE + np.arange(QK_ROPE)] = 1.0
    c64 = np.arange(GW // 4, dtype=np.int64)
    ang = 2.0 * np.pi * ((c64[:, None] * c64[None, :]) % (GW // 4)).astype(np.float64) / (GW // 4)
    four = lambda m: jnp.asarray(np.broadcast_to(m, (4,) + m.shape), F32)
    router_hi, router_lo = _split(jnp.pad(p["router_w"][l], ((0, 0), (0, LANES - N_EXP))))
    router_w = jnp.concatenate([router_hi, router_lo], axis=1)
    router_b = jnp.pad(p["router_b"][l], (0, LANES - N_EXP)).reshape(1, LANES)
    return {
        "attn_g": p["attn_norm_g"][l].reshape(1, D),
        "w_in": w_cols,
        "q_g": jnp.pad(p["q_norm_g"][l], (0, 2 * LANES - Q_LORA)).reshape(1, 2 * LANES),
        "w_uq": w_uq.reshape(2 * LANES, wq).astype(BF16),
        "kv_g": p["kv_norm_g"][l].reshape(1, KV_LORA),
        "w_kn": w_kn.reshape(KV_LORA, wq).astype(BF16),
        "w_kr": jnp.asarray(place.reshape(LANES, wq), BF16),
        "w_v": w_v.reshape(KV_LORA, HEADS * V_DIM).astype(BF16),
        "bd_c": _block_diag(four(np.cos(ang))),
        "bd_s": _block_diag(four(np.sin(ang))),
        "w_f": p["fourier_w"][l].astype(BF16),
        "bd_pool": _block_diag(p["pool_w"][l]).astype(BF16),
        "pool_scale": p["pool_scale"][l].reshape(1, GW),
        "conv_w": p["conv_w"][l],
        "conv_b": p["conv_b"][l].reshape(1, GW),
        "bd_wa": jnp.stack([_block_diag(p["lru_wa"][l, d]) for d in range(2)]).astype(BF16),
        "bd_wx": jnp.stack([_block_diag(p["lru_wx"][l, d]) for d in range(2)]).astype(BF16),
        "lru_ba": p["lru_ba"][l].reshape(2, 1, GW),
        "lru_bx": p["lru_bx"][l].reshape(2, 1, GW),
        "lru_lam": p["lru_lambda"][l].reshape(2, 1, GW),
        "out_g": p["out_norm_g"][l],
        "w_out": p["w_out"][l].astype(BF16),
        "ffn_g": p["ffn_norm_g"][l].reshape(1, D),
        "router_w": router_w,
        "router_b": router_b,
    }


def _routing_tables(route, prob, counts, tab):
    top_e, rank = route[0:TOP_K], route[TOP_K:2 * TOP_K]
    top_p = prob[0:TOP_K].reshape(-1)
    loc = route[2 * TOP_K:3 * TOP_K].reshape(-1)
    counts = counts.astype(I32)
    padded = (counts + TM_E - 1) // TM_E * TM_E
    pad_ends = jnp.cumsum(padded)
    pad_starts = pad_ends - padded
    experts = jnp.arange(N_EXP, dtype=I32)
    onehot = top_e[:, :, None] == experts
    dest = (jnp.sum(jnp.where(onehot, pad_starts, 0), axis=-1) + rank).reshape(T * TOP_K)
    n_used = (pad_ends[-1] // TM_E).astype(I32)
    blk = jnp.minimum(jnp.arange(NB_E, dtype=I32), n_used - 1) * TM_E
    block_e = jnp.minimum(jnp.sum(pad_ends[None, :] <= blk[:, None], axis=-1), N_EXP - 1).astype(I32)
    of_block = lambda a: jnp.sum(jnp.where(block_e[:, None] == experts, a, 0), axis=-1)
    valid = jnp.clip(of_block(pad_starts + counts) - blk, 0, TM_E).astype(I32)
    first = (blk == of_block(pad_starts)).astype(I32)
    later = (experts[None, :] > experts[:, None]) & (counts[None, :] > 0)
    next_of = jnp.min(jnp.where(later, experts[None, :], N_EXP), axis=-1)
    next_e = of_block(jnp.where(next_of < N_EXP, next_of, -1)).astype(I32)
    parity = (of_block(jnp.cumsum((counts > 0).astype(I32))) % 2).astype(I32)
    sched = (block_e, n_used.reshape(1), valid, first, next_e, parity)
    tab = tab.reshape(T // TM, N_EXP, LANES).astype(I32)
    before, nch, base = tab[:, :, 0], tab[:, :, 1], tab[:, :, 2]
    run_start = pad_starts[None, :] + before
    ends = base + nch
    ci = jnp.arange(MAX_CH, dtype=I32)
    e_of = jnp.minimum(jnp.sum(ends[:, None, :] <= ci[None, :, None], axis=-1), N_EXP - 1)
    pick = lambda a: jnp.sum(jnp.where(e_of[:, :, None] == experts, a[:, None, :], 0), axis=-1)
    chunk_src = jnp.clip(pick(run_start) + (ci[None, :] - pick(base)) * CH_C, 0, N_SLOTS)
    return dest, sched, loc, top_p, chunk_src.reshape(-1), ends[:, N_EXP - 1]


def kernel(x_prompt, x_sample, cache_ckv, cache_krope, state_lru, c, c_ctx, w_mod, b_mod, attn_norm_g, w_in, pool_w, pool_scale, q_norm_g, w_uq, kv_norm_g, w_ukv, fourier_w, conv_w, conv_b, lru_wa, lru_ba, lru_wx, lru_bx, lru_lambda, out_norm_g, w_out, ffn_norm_g, router_w, router_b, w_gate, b_gate, w_up, b_up, w_down, b_down, final_norm_g):
    params = dict(attn_norm_g=attn_norm_g, w_in=w_in, pool_w=pool_w, pool_scale=pool_scale,
                  q_norm_g=q_norm_g, w_uq=w_uq, kv_norm_g=kv_norm_g, w_ukv=w_ukv,
                  fourier_w=fourier_w, conv_w=conv_w, conv_b=conv_b, lru_wa=lru_wa, lru_ba=lru_ba,
                  lru_wx=lru_wx, lru_bx=lru_bx, lru_lambda=lru_lambda, out_norm_g=out_norm_g,
                  w_out=w_out, ffn_norm_g=ffn_norm_g, router_w=router_w, router_b=router_b,
                  w_gate=w_gate, b_gate=b_gate, w_up=w_up, b_up=b_up, w_down=w_down, b_down=b_down)
    x = (x_prompt.reshape(T_P, D), x_sample.reshape(T_S, D))
    cvec = jnp.concatenate([c_ctx[None, :], c, jnp.zeros((SUBLANES - 1 - DEC_BATCH, D), F32)], axis=0)
    mod = _modulation(cvec, w_mod, b_mod)
    rope = _rope_tables()
    dft_p, dft_s = _dft_tables(SEQ), _dft_tables(DEC_SEQ)
    tri = jnp.asarray(np.triu(np.ones((TM, TM), np.float32)), BF16)
    lower = jnp.asarray(np.tril(np.ones((N_EXP, N_EXP), np.float32), -1), BF16)
    final_g = final_norm_g.reshape(1, D)
    h0_prompt = jnp.zeros((BATCH, 2, GW), F32)

    new_ckv, new_krope, new_lru = [], [], []
    for l in range(DEPTH):
        lw = _layer_weights(l, params)
        zmix, q, k, v, ckv, kr = _front(x, mod[l], lw, rope)
        new_ckv.append(ckv.reshape(BATCH, SEQ, KV_LORA))
        new_krope.append(kr.reshape(BATCH, SEQ, QK_ROPE))

        yb_p = _attention(q, [(k, v, SEQ, 0)], BATCH, SEQ, SEQ, 0, group=PROMPT_GROUP)
        yc_p = _fourier(zmix, dft_p, lw, BATCH, SEQ, 0, group=PROMPT_GROUP)
        ya_p, yd_p, st_p = _seq_mixers(zmix, h0_prompt, lw, BATCH, SEQ, 0)
        new_lru.append(st_p)

        kr_ctx = jnp.pad(cache_krope[:, l].reshape(DEC_BATCH * PAST, QK_ROPE),
                         ((0, 0), (0, LANES - QK_ROPE)))
        k_ctx, v_ctx = _kv_expand(cache_ckv[:, l].reshape(DEC_BATCH * PAST, KV_LORA), kr_ctx, lw)
        yb_s = _attention(q, [(k_ctx, v_ctx, PAST, 0), (k, v, DEC_SEQ, T_P)],
                          DEC_BATCH, DEC_SEQ, 512, T_P)
        yc_s = _fourier(zmix, dft_s, lw, DEC_BATCH, DEC_SEQ, T_P)
        ya_s, yd_s, _ = _seq_mixers(zmix, state_lru[:, l], lw, DEC_BATCH, DEC_SEQ, T_P)

        x1, h2lin, route, prob, tab, counts = _post(
            x, mod[l], (ya_p, yb_p, yc_p, yd_p), (ya_s, yb_s, yc_s, yd_s), lw, tri, lower)
        dest, sched, loc, top_p, chunk_src, n_chunks = _routing_tables(
            route, prob, counts[:, 0], tab)
        xs = _gather_rows(dest, sched[1], h2lin)
        ys = _experts(l, sched, xs, params)
        x = _combine(loc, top_p, chunk_src, n_chunks, ys, x1, mod[l], final_g,
                     final=(l == DEPTH - 1))

    y_prompt, y_sample = x
    return (y_prompt.reshape(BATCH, SEQ, D), y_sample.reshape(DEC_BATCH, DEC_SEQ, D),
            jnp.stack(new_ckv, axis=1), jnp.stack(new_krope, axis=1), jnp.stack(new_lru, axis=1))
```

```python
import functools

import numpy as np
import jax
import jax.numpy as jnp
from jax import lax
from jax.experimental import pallas as pl
from jax.experimental.pallas import tpu as pltpu

F32 = jnp.float32
BF16 = jnp.bfloat16
I32 = jnp.int32

D = 1024
BATCH, SEQ = 32, 256
DEC_BATCH, DEC_SEQ, PAST = 2, 2048, 512
T_P = BATCH * SEQ
T_S = DEC_BATCH * DEC_SEQ
T = T_P + T_S
DEPTH = 2
GRID_W = 64
GW = 256
Q_LORA, KV_LORA, QK_NOPE, QK_ROPE, V_DIM, HEADS = 192, 128, 64, 32, 64, 4
HEAD_PAD = 128
ROPE_BASE = 10000.0
Q_SCALE = float((QK_NOPE + QK_ROPE) ** -0.5 * np.log2(np.e))
POOL_WINDOWS = (2, 4, 8, 16)
LRU_C = 8.0
N_EXP, TOP_K = 32, 4
LIMIT, ALPHA = 7.0, 1.702
EPS = 1e-6

LANES = 128
SUBLANES = 8
MIB = 1024 * 1024
V7X_VMEM_BYTES = 64 * MIB
VMEM_LIMIT = V7X_VMEM_BYTES * 7 // 8

TM = 512
N_PT = T_P // TM
TILES_PER_DEC = DEC_SEQ // TM
TM_E = 512
EXPERT_ROWS = (TM_E, TM_E // 2, TM_E // 4)
N_SLOTS = T * TOP_K + N_EXP * TM_E
NB_E = N_SLOTS // TM_E
ROW_F = D // LANES
TG = 2 * TM_E
GATHER_VMEM_LIMIT = 4 * (T * D + 2 * TG * D) + 4 * MIB
CH_C = 32
MAX_CH = TM * TOP_K // CH_C + N_EXP
GATHER_UNROLL = 32
COMBINE_UNROLL = 16
PROMPT_GROUP = 8
SEQ_PAD = 32
FRONT = 8

W_COLS = 1536
COL_Q, COL_KV, COL_KR = 1024, 1280, 1408


_PROMPT_ROWS = pl.BlockSpec((TM, D), lambda i: (jnp.minimum(i, N_PT - 1), 0))
_SAMPLE_ROWS = pl.BlockSpec((TM, D), lambda i: (jnp.maximum(i - N_PT, 0), 0))


def _cparams(sem, vmem=VMEM_LIMIT):
    return pltpu.CompilerParams(dimension_semantics=sem, vmem_limit_bytes=vmem)


def _bdot(a, b):
    return jnp.dot(a.astype(BF16), b.astype(BF16), preferred_element_type=F32)


def _split(a):
    hi = a.astype(BF16)
    lo = (a - hi.astype(F32)).astype(BF16)
    return hi, lo


def _dot3(a, b):
    ah, al = _split(a)
    bh, bl = _split(b)
    d = functools.partial(jnp.dot, preferred_element_type=F32)
    return d(ah, bh) + (d(al, bh) + d(ah, bl))


def _rms(x, n=None):
    n = x.shape[-1] if n is None else n
    return x * lax.rsqrt(jnp.sum(x * x, axis=-1, keepdims=True) * (1.0 / n) + EPS)


def _neg_expm1_double(x):
    t = jnp.tanh(x)
    return -2.0 * t / (1.0 - t)


def _sigmoid(x):
    return 0.5 * jnp.tanh(0.5 * x) + 0.5


def _tile_row(i):
    return jnp.where(i >= N_PT, 1 + (i - N_PT) // TILES_PER_DEC, 0)


def _mod_kernel(c_ref, w_ref, b_ref, o_ref):
    s = jax.nn.silu(c_ref[...])
    o_ref[0] = _dot3(s, w_ref[0]) + b_ref[0]


def _modulation(cvec, w_mod, b_mod):
    tn = 1536
    n = 6 * D
    return pl.pallas_call(
        _mod_kernel,
        out_shape=jax.ShapeDtypeStruct((DEPTH, SUBLANES, n), F32),
        grid=(DEPTH, n // tn),
        in_specs=[pl.BlockSpec((SUBLANES, D), lambda l, j: (0, 0)),
                  pl.BlockSpec((1, D, tn), lambda l, j: (l, 0, j)),
                  pl.BlockSpec((1, 1, tn), lambda l, j: (l, 0, j))],
        out_specs=pl.BlockSpec((1, SUBLANES, tn), lambda l, j: (l, 0, j)),
        compiler_params=_cparams(("parallel", "parallel")),
        name="modulation",
    )(cvec, w_mod, b_mod.reshape(DEPTH, 1, n))


def _front_kernel(xp_ref, xs_ref, mod_ref, g_ref, w_ref, qg_ref, wuq_ref, kvg_ref, wkn_ref, wkr_ref,
                  wv_ref, cq_ref, saq_ref, sbq_ref, ck_ref, sak_ref, sbk_ref,
                  zmix_ref, q_ref, k_ref, v_ref, ckv_ref, kr_ref):
    i = pl.program_id(0)
    row = _tile_row(i)
    shift1 = mod_ref[pl.ds(row, 1), 0:D]
    scale1 = mod_ref[pl.ds(row, 1), D:2 * D]
    h = _rms(jnp.where(i >= N_PT, xs_ref[...], xp_ref[...])) * g_ref[...]
    h = h * (1.0 + scale1) + shift1
    z = jnp.dot(h.astype(BF16), w_ref[...], preferred_element_type=F32)
    zmix_ref[...] = z[:, 0:4 * GW]
    qn = _rms(z[:, COL_Q:COL_Q + 2 * LANES], Q_LORA) * qg_ref[...]
    q = _bdot(qn, wuq_ref[...])
    ckv = _rms(z[:, COL_KV:COL_KV + KV_LORA]) * kvg_ref[...]
    ckv_b = ckv.astype(BF16)
    v_ref[...] = jnp.dot(ckv_b, wv_ref[...], preferred_element_type=F32).astype(BF16)
    kn = jnp.dot(ckv_b, wkn_ref[...], preferred_element_type=F32)
    kr = z[:, COL_KR:COL_KR + LANES]

    @pl.when(i < N_PT)
    def _():
        q_ref[...] = (q * Q_SCALE).astype(BF16)
        ckv_ref[...] = ckv
        kr_ref[...] = kr[:, 0:QK_ROPE]
        k_ref[...] = (kn + _bdot(kr, wkr_ref[...])).astype(BF16)

    @pl.when(i >= N_PT)
    def _():
        wq = HEADS * HEAD_PAD
        half = QK_ROPE // 2
        qr = (q * cq_ref[...] + pltpu.roll(q, wq - half, 1) * saq_ref[...]
              + pltpu.roll(q, half, 1) * sbq_ref[...])
        krr = (kr * ck_ref[...] + pltpu.roll(kr, LANES - half, 1) * sak_ref[...]
               + pltpu.roll(kr, half, 1) * sbk_ref[...])
        q_ref[...] = (qr * Q_SCALE).astype(BF16)
        k_ref[...] = (kn + _bdot(krr, wkr_ref[...])).astype(BF16)


def _front(x, mod_l, lw, rope):
    full = lambda shape: pl.BlockSpec(shape, lambda i: (0,) * len(shape))
    rows = lambda w: pl.BlockSpec((TM, w), lambda i: (i, 0))
    prompt_rows = lambda w: pl.BlockSpec((TM, w), lambda i: (jnp.minimum(i, N_PT - 1), 0))
    rope_rows = lambda w: pl.BlockSpec(
        (TM, w), lambda i: (jnp.maximum(i - N_PT, 0) % TILES_PER_DEC, 0))
    wq = HEADS * HEAD_PAD
    return pl.pallas_call(
        _front_kernel,
        out_shape=(jax.ShapeDtypeStruct((T, 4 * GW), F32),
                   jax.ShapeDtypeStruct((T, wq), BF16),
                   jax.ShapeDtypeStruct((T, wq), BF16),
                   jax.ShapeDtypeStruct((T, HEADS * V_DIM), BF16),
                   jax.ShapeDtypeStruct((T_P, KV_LORA), F32),
                   jax.ShapeDtypeStruct((T_P, QK_ROPE), F32)),
        grid=(T // TM,),
        in_specs=[_PROMPT_ROWS, _SAMPLE_ROWS, full((SUBLANES, 6 * D)), full((1, D)), full((D, W_COLS)),
                  full((1, 2 * LANES)), full((2 * LANES, wq)), full((1, KV_LORA)),
                  full((KV_LORA, wq)), full((LANES, wq)), full((KV_LORA, HEADS * V_DIM)),
                  rope_rows(wq), rope_rows(wq), rope_rows(wq),
                  rope_rows(LANES), rope_rows(LANES), rope_rows(LANES)],
        out_specs=(rows(4 * GW), rows(wq), rows(wq), rows(HEADS * V_DIM), prompt_rows(KV_LORA),
                   prompt_rows(QK_ROPE)),
        compiler_params=_cparams(("arbitrary",)),
        name="front",
    )(*x, mod_l, lw["attn_g"], lw["w_in"], lw["q_g"], lw["w_uq"], lw["kv_g"], lw["w_kn"],
      lw["w_kr"], lw["w_v"], *rope)


def _kvexp_kernel(ckv_ref, kr_ref, wkn_ref, wkr_ref, wv_ref, k_ref, v_ref):
    ckv_b = ckv_ref[...].astype(BF16)
    v_ref[...] = jnp.dot(ckv_b, wv_ref[...], preferred_element_type=F32).astype(BF16)
    kn = jnp.dot(ckv_b, wkn_ref[...], preferred_element_type=F32)
    k_ref[...] = (kn + _bdot(kr_ref[...], wkr_ref[...])).astype(BF16)


def _kv_expand(ckv, kr_pad, lw):
    n = ckv.shape[0]
    wq = HEADS * HEAD_PAD
    full = lambda shape: pl.BlockSpec(shape, lambda i: (0,) * len(shape))
    return pl.pallas_call(
        _kvexp_kernel,
        out_shape=(jax.ShapeDtypeStruct((n, wq), BF16),
                   jax.ShapeDtypeStruct((n, HEADS * V_DIM), BF16)),
        grid=(1,),
        in_specs=[full((n, KV_LORA)), full((n, LANES)), full((KV_LORA, wq)), full((LANES, wq)),
                  full((KV_LORA, HEADS * V_DIM))],
        out_specs=(full((n, wq)), full((n, HEADS * V_DIM))),
        compiler_params=_cparams(("arbitrary",)),
        name="kv_expand",
    )(ckv, kr_pad, lw["w_kn"], lw["w_kr"], lw["w_v"])


def _attn_kernel(q_ref, *refs, group, tq, seg_keys):
    o_ref = refs[-1]
    dims = (((1,), (1,)), ((), ()))
    for g in range(group):
        outs = []
        for h in range(HEADS):
            qh = q_ref[g * tq:(g + 1) * tq, h * HEAD_PAD:(h + 1) * HEAD_PAD]
            scores = []
            for si, nk in enumerate(seg_keys):
                kh = refs[2 * si][g * nk:(g + 1) * nk, h * HEAD_PAD:(h + 1) * HEAD_PAD]
                scores.append(lax.dot_general(qh, kh, dims, preferred_element_type=F32))
            m = functools.reduce(jnp.maximum, [jnp.max(s, axis=-1, keepdims=True) for s in scores])
            l, acc = None, None
            for si, (nk, s) in enumerate(zip(seg_keys, scores)):
                p = jnp.exp2(s - m)
                vh = refs[2 * si + 1][g * nk:(g + 1) * nk, h * V_DIM:(h + 1) * V_DIM]
                part = jnp.dot(p.astype(BF16), vh, preferred_element_type=F32)
                psum = jnp.sum(p, axis=-1, keepdims=True)
                l, acc = (psum, part) if l is None else (l + psum, acc + part)
            outs.append(acc / l)
        o_ref[g * tq:(g + 1) * tq, :] = jnp.concatenate(outs, axis=-1)


def _attention(q, segments, n_batch, seq, tq, row0, group=1):
    nq = seq // tq
    wq = HEADS * HEAD_PAD
    in_specs = [pl.BlockSpec((group * tq, wq), lambda b, i: (row0 // (group * tq) + b * nq + i, 0))]
    operands = [q]
    for k_rows, v_rows, nk, first in segments:
        index = lambda b, i, base=first // (group * nk): (base + b, 0)
        in_specs += [pl.BlockSpec((group * nk, wq), index),
                     pl.BlockSpec((group * nk, HEADS * V_DIM), index)]
        operands += [k_rows, v_rows]
    return pl.pallas_call(
        functools.partial(_attn_kernel, group=group, tq=tq,
                          seg_keys=tuple(seg[2] for seg in segments)),
        out_shape=jax.ShapeDtypeStruct((n_batch * seq, GW), F32),
        grid=(n_batch // group, nq),
        in_specs=in_specs,
        out_specs=pl.BlockSpec((group * tq, GW), lambda b, i: (b * nq + i, 0)),
        compiler_params=_cparams(("parallel", "parallel")),
        name="attention",
    )(*operands)


def _fourier_kernel(z_ref, c_ref, s_ref, bdc_ref, bds_ref, w_ref, o_ref, xc_ref, xs_ref, *,
                    norm, group, seq, ts):
    b = pl.program_id(1)

    @pl.when(pl.program_id(0) == 0)
    def _():
        xb = z_ref[...].astype(BF16)
        xc_ref[b] = _bdot(xb, bdc_ref[...]).astype(BF16)
        xs_ref[b] = _bdot(xb, bds_ref[...]).astype(BF16)

    cb, sb = c_ref[...].astype(BF16), s_ref[...].astype(BF16)
    for g in range(group):
        f = (jnp.dot(cb, xc_ref[b, g * seq:(g + 1) * seq, :], preferred_element_type=F32)
             - jnp.dot(sb, xs_ref[b, g * seq:(g + 1) * seq, :], preferred_element_type=F32)) * norm
        o_ref[g * ts:(g + 1) * ts, :] = _bdot(f, w_ref[...])


def _fourier(zmix, consts, lw, n_batch, seq, row0, group=1):
    ts = min(seq, 512)
    nj = seq // ts
    assert group == 1 or nj == 1
    nb = n_batch // group
    cmat, smat = consts
    full = lambda shape: pl.BlockSpec(shape, lambda j, b: (0,) * len(shape))
    z_spec = pl.BlockSpec((group * seq, GW),
                          lambda j, b: (row0 // (group * seq) + jnp.where(j == 0, b, nb - 1), 1))
    return pl.pallas_call(
        functools.partial(_fourier_kernel, norm=float((seq * (GW // 4)) ** -0.5), group=group,
                          seq=seq, ts=ts),
        out_shape=jax.ShapeDtypeStruct((n_batch * seq, GW), F32),
        grid=(nj, nb),
        in_specs=[z_spec,
                  pl.BlockSpec((ts, seq), lambda j, b: (j, 0)),
                  pl.BlockSpec((ts, seq), lambda j, b: (j, 0)),
                  full((GW, GW)), full((GW, GW)), full((GW, GW))],
        out_specs=pl.BlockSpec((group * ts, GW), lambda j, b: (b * nj + j, 0)),
        scratch_shapes=[pltpu.VMEM((nb, group * seq, GW), BF16),
                        pltpu.VMEM((nb, group * seq, GW), BF16)],
        compiler_params=_cparams(("arbitrary", "arbitrary")),
        name="fourier",
    )(zmix, cmat, smat, lw["bd_c"], lw["bd_s"], lw["w_f"])


def _seq_kernel(za_ref, zr_ref, zg_ref, h0_ref, icnt_ref, wp_ref, ps_ref, cw_ref, cb_ref, wa_ref,
                ba_ref, wx_ref, bx_ref, lam_ref, ya_ref, yd_ref, st_ref,
                pa_ref, pb_ref, xp_ref, a_ref, b_ref, *, seq):
    n = seq + SEQ_PAD
    span = seq + 2 * FRONT
    zeros_pad = jnp.zeros((n, GW), F32)

    za = za_ref[...]
    pa_ref[...] = zeros_pad
    pb_ref[...] = zeros_pad
    pa_ref[FRONT:FRONT + seq, :] = za
    pb_ref[0:span, :] = pa_ref[0:span, :] + pa_ref[1:span + 1, :]
    win2 = pb_ref[FRONT - 1:FRONT - 1 + seq, :]
    pa_ref[0:span, :] = pb_ref[0:span, :] + pb_ref[2:span + 2, :]
    win4 = pa_ref[FRONT - 2:FRONT - 2 + seq, :]
    pb_ref[0:span, :] = pa_ref[0:span, :] + pa_ref[4:span + 4, :]
    win8 = pb_ref[FRONT - 4:FRONT - 4 + seq, :]
    pa_ref[0:span, :] = pb_ref[0:span, :] + pb_ref[8:span + 8, :]
    win16 = pa_ref[FRONT - 8:FRONT - 8 + seq, :]
    grp = lax.broadcasted_iota(I32, (1, GW), 1) // (GW // 4)
    win = jnp.where(grp == 0, win2, jnp.where(grp == 1, win4, jnp.where(grp == 2, win8, win16)))
    dlt = win * icnt_ref[...] - za
    ya_ref[...] = _bdot(dlt, wp_ref[...]) * ps_ref[...]

    xp_ref[...] = zeros_pad
    xp_ref[FRONT:FRONT + seq, :] = zr_ref[...]
    xc = cb_ref[...] + cw_ref[0:1, :] * xp_ref[FRONT - 2:FRONT - 2 + seq, :]
    for kk in range(1, 4):
        xc = xc + cw_ref[kk:kk + 1, :] * xp_ref[FRONT - 2 + kk:FRONT - 2 + kk + seq, :]
    xcb = xc.astype(BF16)
    n_grp = seq // SUBLANES
    sub = lax.broadcasted_iota(I32, (n_grp, SUBLANES, GW), 1)

    for d in range(2):
        r = _sigmoid(jnp.dot(xcb, wa_ref[d], preferred_element_type=F32) + ba_ref[d])
        ig = _sigmoid(jnp.dot(xcb, wx_ref[d], preferred_element_type=F32) + bx_ref[d])
        log_a = (-LRU_C) * r * jax.nn.softplus(-lam_ref[d])
        a = jnp.exp(log_a).reshape(n_grp, SUBLANES, GW)
        b = (jnp.sqrt(_neg_expm1_double(log_a)) * (ig * xc)).reshape(n_grp, SUBLANES, GW)
        for k in (1, 2, 4):
            shift = k if d == 0 else SUBLANES - k
            m = sub >= k if d == 0 else sub < SUBLANES - k
            ap, bp = pltpu.roll(a, shift, 1), pltpu.roll(b, shift, 1)
            b = jnp.where(m, a * bp + b, b)
            a = jnp.where(m, a * ap, a)
        a_ref[d] = a.reshape(seq, GW)
        b_ref[d] = b.reshape(seq, GW)

    def step(g, carry):
        fwd, bwd = carry
        off_f = pl.multiple_of(g * SUBLANES, SUBLANES)
        off_b = pl.multiple_of((n_grp - 1 - g) * SUBLANES, SUBLANES)
        hf = a_ref[0, pl.ds(off_f, SUBLANES), :] * fwd + b_ref[0, pl.ds(off_f, SUBLANES), :]
        hb = a_ref[1, pl.ds(off_b, SUBLANES), :] * bwd + b_ref[1, pl.ds(off_b, SUBLANES), :]
        b_ref[0, pl.ds(off_f, SUBLANES), :] = hf
        b_ref[1, pl.ds(off_b, SUBLANES), :] = hb
        return (jnp.broadcast_to(hf[SUBLANES - 1:SUBLANES, :], (SUBLANES, GW)),
                jnp.broadcast_to(hb[0:1, :], (SUBLANES, GW)))

    start = tuple(jnp.broadcast_to(h0_ref[0, d:d + 1, :], (SUBLANES, GW)) for d in range(2))
    last_f, last_b = lax.fori_loop(0, n_grp, step, start)
    st_ref[0, 0:1, :] = last_f[0:1, :]
    st_ref[0, 1:2, :] = last_b[0:1, :]
    yd_ref[...] = (b_ref[0] + b_ref[1]) * jax.nn.gelu(zg_ref[...])


def _pool_inverse_counts(seq):
    pos = np.arange(seq)[:, None]
    half = np.repeat(np.array(POOL_WINDOWS) // 2, GW // len(POOL_WINDOWS))[None, :]
    cnt = np.minimum(pos + half, seq) - np.maximum(pos - half, 0)
    return jnp.asarray(1.0 / cnt, F32)


def _seq_mixers(zmix, h0, lw, n_batch, seq, row0):
    full = lambda shape: pl.BlockSpec(shape, lambda b: (0,) * len(shape))
    col = lambda c: pl.BlockSpec((seq, GW), lambda b: (row0 // seq + b, c))
    out_rows = pl.BlockSpec((seq, GW), lambda b: (b, 0))
    pad = pltpu.VMEM((seq + SEQ_PAD, GW), F32)
    return pl.pallas_call(
        functools.partial(_seq_kernel, seq=seq),
        out_shape=(jax.ShapeDtypeStruct((n_batch * seq, GW), F32),
                   jax.ShapeDtypeStruct((n_batch * seq, GW), F32),
                   jax.ShapeDtypeStruct((n_batch, 2, GW), F32)),
        grid=(n_batch,),
        in_specs=[col(0), col(2), col(3), pl.BlockSpec((1, 2, GW), lambda b: (b, 0, 0)),
                  full((seq, GW)), full((GW, GW)), full((1, GW)), full((4, GW)), full((1, GW)),
                  full((2, GW, GW)), full((2, 1, GW)), full((2, GW, GW)), full((2, 1, GW)),
                  full((2, 1, GW))],
        out_specs=(out_rows, out_rows, pl.BlockSpec((1, 2, GW), lambda b: (b, 0, 0))),
        scratch_shapes=[pad, pad, pad, pltpu.VMEM((2, seq, GW), F32), pltpu.VMEM((2, seq, GW), F32)],
        compiler_params=_cparams(("parallel",)),
        name="seq_mixers",
    )(zmix, zmix, zmix, h0, _pool_inverse_counts(seq), lw["bd_pool"], lw["pool_scale"], lw["conv_w"],
      lw["conv_b"], lw["bd_wa"], lw["lru_ba"], lw["bd_wx"], lw["lru_bx"], lw["lru_lam"])


def _post_kernel(xp_ref, xs_ref, mod_ref, yap_ref, ybp_ref, ycp_ref, ydp_ref, yas_ref, ybs_ref, ycs_ref,
                 yds_ref, og_ref, wo_ref, fg_ref, rw_ref, rb_ref, tri_ref, lower_ref,
                 x1_ref, h2p_ref, route_ref, prob_ref, tab_ref, cnt_ref, carry_ref):
    i = pl.program_id(0)
    is_s = i >= N_PT
    row = _tile_row(i)
    gate1 = mod_ref[pl.ds(row, 1), 2 * D:3 * D]
    shift2 = mod_ref[pl.ds(row, 1), 3 * D:4 * D]
    scale2 = mod_ref[pl.ds(row, 1), 4 * D:5 * D]

    @pl.when(i == 0)
    def _():
        carry_ref[...] = jnp.zeros_like(carry_ref)

    groups = []
    for gi, (p_ref, s_ref) in enumerate(((yap_ref, yas_ref), (ybp_ref, ybs_ref),
                                         (ycp_ref, ycs_ref), (ydp_ref, yds_ref))):
        y = jnp.where(is_s, s_ref[...], p_ref[...])
        groups.append((_rms(y) * og_ref[gi:gi + 1, :]).astype(BF16))
    ycat = jnp.concatenate(groups, axis=-1)
    x = jnp.where(is_s, xs_ref[...], xp_ref[...])
    x1 = x + gate1 * jnp.dot(ycat, wo_ref[...], preferred_element_type=F32)
    x1_ref[...] = x1
    h2 = _rms(x1) * fg_ref[...]
    h2 = h2 * (1.0 + scale2) + shift2

    for j in range(ROW_F):
        h2p_ref[pl.ds(j, TM, stride=ROW_F), :] = h2[:, j * LANES:(j + 1) * LANES]

    h_hi, h_lo = _split(h2)
    both = jnp.dot(h_hi, rw_ref[...], preferred_element_type=F32)
    cross = both[:, LANES:2 * LANES] + jnp.dot(h_lo, rw_ref[:, 0:LANES], preferred_element_type=F32)
    logits = both[:, 0:LANES] + cross + rb_ref[...]
    lt = logits.T[0:N_EXP, :]
    expert = lax.broadcasted_iota(I32, (N_EXP, TM), 0).astype(F32)
    neg = jnp.float32(-jnp.inf)
    cur = lt
    sel, vals, idxs = [], [], []
    for _ in range(TOP_K):
        m = jnp.max(cur, axis=0, keepdims=True)
        idx = jnp.min(jnp.where(cur == m, expert, float(N_EXP)), axis=0, keepdims=True)
        hit = expert == idx
        sel.append(hit)
        vals.append(m)
        idxs.append(idx)
        cur = jnp.where(hit, neg, cur)
    exps = [jnp.exp(v - vals[0]) for v in vals]
    denom = exps[0] + exps[1] + exps[2] + exps[3]
    onehot = jnp.where(sel[0] | sel[1] | sel[2] | sel[3], 1.0, 0.0)
    cum_l = jnp.dot(onehot.astype(BF16), tri_ref[...], preferred_element_type=F32)
    before = carry_ref[:, 0:1]
    cum = cum_l + before
    n_tile = cum_l[:, TM - 1:TM]
    nch = jnp.floor((n_tile + (CH_C - 1.0)) * (1.0 / CH_C))
    nch_b = jnp.broadcast_to(nch, (N_EXP, LANES)).astype(BF16)
    base = jnp.dot(lower_ref[...], nch_b, preferred_element_type=F32)[:, 0:1]
    rows_e, rows_r, rows_l, rows_p = [], [], [], []
    for k in range(TOP_K):
        rows_e.append(idxs[k].astype(I32))
        rows_r.append(jnp.sum(jnp.where(sel[k], cum - 1.0, 0.0), axis=0, keepdims=True).astype(I32))
        rows_l.append(jnp.sum(jnp.where(sel[k], (base * CH_C + cum_l - 1.0) * ROW_F, 0.0), axis=0,
                              keepdims=True).astype(I32))
        rows_p.append(exps[k] / denom)
    route_ref[...] = jnp.concatenate(rows_e + rows_r + rows_l + [jnp.zeros((TOP_K, TM), I32)], axis=0)
    prob_ref[...] = jnp.concatenate(rows_p + [jnp.zeros((SUBLANES - TOP_K, TM), F32)], axis=0)
    lane = lax.broadcasted_iota(I32, (N_EXP, LANES), 1)
    tab_ref[...] = jnp.where(lane == 0, before, jnp.where(lane == 1, nch, base))
    new_carry = jnp.broadcast_to(cum[:, TM - 1:TM], (N_EXP, LANES))
    carry_ref[...] = new_carry
    cnt_ref[...] = new_carry


def _post(x, mod_l, ys_prompt, ys_sample, lw, tri, lower):
    full = lambda shape: pl.BlockSpec(shape, lambda i: (0,) * len(shape))
    rows = lambda w: pl.BlockSpec((TM, w), lambda i: (i, 0))
    prow = pl.BlockSpec((TM, GW), lambda i: (jnp.minimum(i, N_PT - 1), 0))
    srow = pl.BlockSpec((TM, GW), lambda i: (jnp.maximum(i - N_PT, 0), 0))
    return pl.pallas_call(
        _post_kernel,
        out_shape=(jax.ShapeDtypeStruct((T, D), F32),
                   jax.ShapeDtypeStruct((T * ROW_F, LANES), F32),
                   jax.ShapeDtypeStruct((4 * TOP_K, T), I32),
                   jax.ShapeDtypeStruct((SUBLANES, T), F32),
                   jax.ShapeDtypeStruct((T // TM * N_EXP, LANES), F32),
                   jax.ShapeDtypeStruct((N_EXP, LANES), F32)),
        grid=(T // TM,),
        in_specs=[_PROMPT_ROWS, _SAMPLE_ROWS, full((SUBLANES, 6 * D)),
                  prow, prow, prow, prow, srow, srow, srow, srow,
                  full((4, GW)), full((D, D)), full((1, D)), full((D, 2 * LANES)), full((1, LANES)),
                  full((TM, TM)), full((N_EXP, N_EXP))],
        out_specs=(rows(D), pl.BlockSpec((TM * ROW_F, LANES), lambda i: (i, 0)),
                   pl.BlockSpec((4 * TOP_K, TM), lambda i: (0, i)),
                   pl.BlockSpec((SUBLANES, TM), lambda i: (0, i)),
                   pl.BlockSpec((N_EXP, LANES), lambda i: (i, 0)), full((N_EXP, LANES))),
        scratch_shapes=[pltpu.VMEM((N_EXP, LANES), F32)],
        compiler_params=_cparams(("arbitrary",)),
        name="post",
    )(*x, mod_l, *ys_prompt, *ys_sample, lw["out_g"], lw["w_out"], lw["ffn_g"], lw["router_w"],
      lw["router_b"], tri, lower)


def _gather_kernel(dest_ref, nused_ref, zeros_hbm, src_hbm, o_ref, stok_ref, src_ref, sem, src_sem):
    i = pl.program_id(0)

    @pl.when(i == 0)
    def _():
        load = pltpu.make_async_copy(src_hbm, src_ref, src_sem)
        load.start()
        init = pltpu.make_async_copy(zeros_hbm, stok_ref, sem)
        init.start()
        init.wait()

        def scatter(t8, c):
            row0 = t8 * (SUBLANES * ROW_F)
            for k in range(TOP_K):
                for u in range(SUBLANES):
                    stok_ref[dest_ref[k * T + t8 * SUBLANES + u]] = row0 + u * ROW_F
            return c
        lax.fori_loop(0, T // SUBLANES, scatter, 0)
        load.wait()

    live = i * (TG // TM_E) < nused_ref[0]

    @pl.when(live)
    def _():
        def rows(r16, c):
            r0 = pl.multiple_of(r16 * GATHER_UNROLL, GATHER_UNROLL)
            for u in range(GATHER_UNROLL):
                src = pl.multiple_of(stok_ref[i * TG + r0 + u], ROW_F)
                dst = pl.multiple_of(r0 * ROW_F, GATHER_UNROLL * ROW_F) + u * ROW_F
                o_ref[pl.ds(dst, ROW_F), :] = src_ref[pl.ds(src, ROW_F), :]
            return c
        lax.fori_loop(0, TG // GATHER_UNROLL, rows, 0)

    @pl.when(jnp.logical_not(live))
    def _():
        o_ref[...] = jnp.zeros_like(o_ref)


def _gather_rows(dest, n_used, h2lin):
    return pl.pallas_call(
        _gather_kernel,
        out_shape=jax.ShapeDtypeStruct((N_SLOTS * ROW_F, LANES), F32),
        grid_spec=pltpu.PrefetchScalarGridSpec(
            num_scalar_prefetch=2,
            grid=(N_SLOTS // TG,),
            in_specs=[pl.BlockSpec(memory_space=pl.ANY), pl.BlockSpec(memory_space=pl.ANY)],
            out_specs=pl.BlockSpec((TG * ROW_F, LANES), lambda i, d, nu: (i, 0)),
            scratch_shapes=[pltpu.SMEM((N_SLOTS,), I32), pltpu.VMEM((T * ROW_F, LANES), F32),
                            pltpu.SemaphoreType.DMA(()), pltpu.SemaphoreType.DMA(())]),
        compiler_params=_cparams(("arbitrary",), GATHER_VMEM_LIMIT),
        name="moe_gather",
    )(dest, n_used, jnp.zeros((N_SLOTS,), I32), h2lin)


def _expert_rows(n, x_ref, w_ref, slot, bg_ref, bu_ref, bd_ref, o_ref):
    x = jnp.concatenate([x_ref[pl.ds(j, n, stride=ROW_F), :].astype(BF16) for j in range(ROW_F)],
                        axis=-1)
    g = jnp.dot(x, w_ref[slot, 0].astype(BF16), preferred_element_type=F32) + bg_ref[0, 0]
    u = jnp.dot(x, w_ref[slot, 1].astype(BF16), preferred_element_type=F32) + bu_ref[0, 0]
    g = jnp.minimum(g, LIMIT)
    u = jnp.clip(u, -LIMIT, LIMIT)
    act = (u + 1.0) * (g * jax.nn.sigmoid(ALPHA * g))
    y = jnp.dot(act.astype(BF16), w_ref[slot, 2].astype(BF16),
                preferred_element_type=F32) + bd_ref[0, 0]
    for j in range(ROW_F):
        o_ref[pl.ds(j, n, stride=ROW_F), :] = y[:, j * LANES:(j + 1) * LANES]
    if n < TM_E:
        o_ref[n * ROW_F:TM_E * ROW_F, :] = jnp.zeros(((TM_E - n) * ROW_F, LANES), F32)


def _expert_kernel(be_ref, nused_ref, valid_ref, first_ref, next_ref, par_ref,
                   x_ref, wg_hbm, bg_ref, wu_hbm, bu_ref, wd_hbm, bd_ref, o_ref, w_ref, sems, *, layer):
    i = pl.program_id(0)
    ib = jnp.minimum(i, NB_E - 1)
    live = i < nused_ref[0]
    valid = valid_ref[ib]
    expert = be_ref[ib]
    slot = par_ref[ib]

    def fetch(e, s):
        return [pltpu.make_async_copy(src.at[layer, e], w_ref.at[s, m], sems.at[s, m])
                for m, src in enumerate((wg_hbm, wu_hbm, wd_hbm))]

    @pl.when(i == 0)
    def _():
        for cp in fetch(expert, slot):
            cp.start()

    @pl.when(live & (first_ref[ib] == 1))
    def _():
        @pl.when(next_ref[ib] >= 0)
        def _():
            for cp in fetch(next_ref[ib], 1 - slot):
                cp.start()

        for cp in fetch(expert, slot):
            cp.wait()

    for n_idx, n in enumerate(EXPERT_ROWS):
        fits = valid <= n
        if n_idx + 1 < len(EXPERT_ROWS):
            fits = fits & (valid > EXPERT_ROWS[n_idx + 1])

        @pl.when(live & fits)
        def _(n=n):
            _expert_rows(n, x_ref, w_ref, slot, bg_ref, bu_ref, bd_ref, o_ref)

    @pl.when(jnp.logical_not(live))
    def _():
        o_ref[...] = jnp.zeros_like(o_ref)


def _experts(l, sched, xs, p):
    last = NB_E - 1
    bspec = pl.BlockSpec((1, 1, 1, D), lambda i, be, *_: (l, be[jnp.minimum(i, last)], 0, 0))
    hbm = pl.BlockSpec(memory_space=pl.ANY)
    bias = lambda b: b.reshape(DEPTH, N_EXP, 1, D)
    return pl.pallas_call(
        functools.partial(_expert_kernel, layer=l),
        out_shape=jax.ShapeDtypeStruct(((NB_E + 1) * TM_E * ROW_F, LANES), F32),
        grid_spec=pltpu.PrefetchScalarGridSpec(
            num_scalar_prefetch=len(sched),
            grid=(NB_E + 1,),
            in_specs=[pl.BlockSpec((TM_E * ROW_F, LANES),
                                   lambda i, be, nu, *_: (jnp.minimum(i, nu[0] - 1), 0)),
                      hbm, bspec, hbm, bspec, hbm, bspec],
            out_specs=pl.BlockSpec((TM_E * ROW_F, LANES), lambda i, *_: (i, 0)),
            scratch_shapes=[pltpu.VMEM((2, 3, D, D), F32), pltpu.SemaphoreType.DMA((2, 3))]),
        compiler_params=_cparams(("arbitrary",)),
        name="moe_experts",
    )(*sched, xs, p["w_gate"], bias(p["b_gate"]), p["w_up"], bias(p["b_up"]),
      p["w_down"], bias(p["b_down"]))


def _combine_kernel(loc_ref, p_ref, csrc_ref, nch_ref, ys_hbm, x_ref, mod_ref, fg_ref, *rest, final):
    *outs, buf_ref, acc_ref, sems = rest
    tb = pl.program_id(0)
    n_tb = pl.num_programs(0)
    slot = tb % 2
    row = _tile_row(tb)
    gate2 = mod_ref[pl.ds(row, 1), 5 * D:6 * D]
    chunk_rows = CH_C * ROW_F

    def chunk_copy(t, c, s):
        src = pl.multiple_of(csrc_ref[t * MAX_CH + c] * ROW_F, ROW_F)
        dst = pl.multiple_of(c * chunk_rows, chunk_rows)
        return pltpu.make_async_copy(ys_hbm.at[pl.ds(src, chunk_rows)],
                                     buf_ref.at[s, pl.ds(dst, chunk_rows)], sems.at[s])

    def issue(t, s):
        n = nch_ref[t]

        def body(c2, carry):
            chunk_copy(t, 2 * c2, s).start(priority=0)

            @pl.when(2 * c2 + 1 < n)
            def _():
                chunk_copy(t, 2 * c2 + 1, s).start(priority=1)
            return carry
        lax.fori_loop(0, (n + 1) // 2, body, 0)

    @pl.when(tb == 0)
    def _():
        issue(0, 0)

    @pl.when(tb + 1 < n_tb)
    def _():
        issue(tb + 1, 1 - slot)

    def drain(c, carry):
        chunk_copy(tb, c, slot).wait()
        return carry
    lax.fori_loop(0, nch_ref[tb], drain, 0)

    def tokens(rg, carry):
        r0 = pl.multiple_of(rg * COMBINE_UNROLL, COMBINE_UNROLL)
        for u in range(COMBINE_UNROLL):
            tok = tb * TM + r0 + u
            acc = None
            for k in range(TOP_K):
                off = pl.multiple_of(loc_ref[k * T + tok], ROW_F)
                term = buf_ref[slot, pl.ds(off, ROW_F), :] * p_ref[k * T + tok]
                acc = term if acc is None else acc + term
            dst = pl.multiple_of(r0 * ROW_F, COMBINE_UNROLL * ROW_F) + u * ROW_F
            acc_ref[pl.ds(dst, ROW_F), :] = acc
        return carry
    lax.fori_loop(0, TM // COMBINE_UNROLL, tokens, 0)

    moe = jnp.concatenate([acc_ref[pl.ds(j, TM, stride=ROW_F), :] for j in range(ROW_F)], axis=-1)
    x2 = x_ref[...] + gate2 * moe
    if final:
        x2 = _rms(x2) * fg_ref[...]
    prompt_ref, sample_ref = outs

    @pl.when(tb < N_PT)
    def _():
        prompt_ref[...] = x2

    @pl.when(tb >= N_PT)
    def _():
        sample_ref[...] = x2


def _combine(loc, top_p, chunk_src, n_chunks, ys, x1, mod_l, final_g, final):
    full = lambda shape: pl.BlockSpec(shape, lambda i, *_: (0,) * len(shape))
    rows = lambda w: pl.BlockSpec((TM, w), lambda i, *_: (i, 0))
    out_shape = (jax.ShapeDtypeStruct((T_P, D), F32), jax.ShapeDtypeStruct((T_S, D), F32))
    out_specs = (pl.BlockSpec((TM, D), lambda i, *_: (jnp.minimum(i, N_PT - 1), 0)),
                 pl.BlockSpec((TM, D), lambda i, *_: (jnp.maximum(i - N_PT, 0), 0)))
    return pl.pallas_call(
        functools.partial(_combine_kernel, final=final),
        out_shape=out_shape,
        grid_spec=pltpu.PrefetchScalarGridSpec(
            num_scalar_prefetch=4,
            grid=(T // TM,),
            in_specs=[pl.BlockSpec(memory_space=pl.ANY), rows(D), full((SUBLANES, 6 * D)),
                      full((1, D))],
            out_specs=out_specs,
            scratch_shapes=[pltpu.VMEM((2, MAX_CH * CH_C * ROW_F, LANES), F32),
                            pltpu.VMEM((TM * ROW_F, LANES), F32),
                            pltpu.SemaphoreType.DMA((2,))]),
        compiler_params=_cparams(("arbitrary",)),
        name="moe_combine",
    )(loc, top_p, chunk_src, n_chunks, ys, x1, mod_l, final_g)


def _rope_tables():
    rows = DEC_SEQ // GRID_W
    r = np.repeat(np.arange(rows, dtype=np.float64), GRID_W)
    c = np.tile(np.arange(GRID_W, dtype=np.float64), rows)
    n_freq = QK_ROPE // 4
    inv = (np.float32(ROPE_BASE) ** (-np.arange(n_freq, dtype=np.float32) / n_freq)).astype(np.float64)
    ang = np.concatenate([r[:, None] * inv, c[:, None] * inv], axis=-1).astype(np.float32)
    cos, sin = np.cos(ang.astype(np.float64)), np.sin(ang.astype(np.float64))
    half = QK_ROPE // 2

    def place(width, start):
        cf = np.ones((DEC_SEQ, width), np.float32)
        sa = np.zeros((DEC_SEQ, width), np.float32)
        sb = np.zeros((DEC_SEQ, width), np.float32)
        for s0 in start:
            cf[:, s0:s0 + half] = cos
            cf[:, s0 + half:s0 + 2 * half] = cos
            sa[:, s0:s0 + half] = -sin
            sb[:, s0 + half:s0 + 2 * half] = sin
        return jnp.asarray(cf), jnp.asarray(sa), jnp.asarray(sb)

    return (*place(HEADS * HEAD_PAD, [h * HEAD_PAD + QK_NOPE for h in range(HEADS)]),
            *place(LANES, [0]))


def _dft_tables(seq):
    kn = (np.arange(seq, dtype=np.int64)[:, None] * np.arange(seq, dtype=np.int64)[None, :]) % seq
    ang = 2.0 * np.pi * kn.astype(np.float64) / seq
    return jnp.asarray(np.cos(ang), F32), jnp.asarray(np.sin(ang), F32)


def _block_diag(blocks):
    g, n, _ = blocks.shape
    eye = jnp.eye(g, dtype=blocks.dtype)
    return jnp.einsum("gij,gh->gihj", blocks, eye).reshape(g * n, g * n)


def _layer_weights(l, p):
    w_in = p["w_in"][l]
    za, zq, zkv, zkr, zf, zr, zg = jnp.split(
        w_in, np.cumsum([GW, Q_LORA, KV_LORA, QK_ROPE, GW, GW])[:], axis=1)
    zpad = lambda n: jnp.zeros((D, n), F32)
    w_cols = jnp.concatenate([za, zf, zr, zg, zq, zpad(2 * LANES - Q_LORA), zkv, zkr,
                              zpad(LANES - QK_ROPE)], axis=1).astype(BF16)
    wq = HEADS * HEAD_PAD
    w_uq = p["w_uq"][l].reshape(Q_LORA, HEADS, QK_NOPE + QK_ROPE)
    w_uq = jnp.pad(w_uq, ((0, 2 * LANES - Q_LORA), (0, 0), (0, HEAD_PAD - QK_NOPE - QK_ROPE)))
    w_ukv = p["w_ukv"][l].reshape(KV_LORA, HEADS, QK_NOPE + V_DIM)
    w_kn = jnp.pad(w_ukv[:, :, :QK_NOPE], ((0, 0), (0, 0), (0, HEAD_PAD - QK_NOPE)))
    w_v = w_ukv[:, :, QK_NOPE:]
    place = np.zeros((LANES, HEADS, HEAD_PAD), np.float32)
    for h in range(HEADS):
        place[np.arange(QK_ROPE), h, QK_NOPE + np.arange(QK_ROPE)] = 1.0
    c64 = np.arange(GW // 4, dtype=np.int64)
    ang = 2.0 * np.pi * ((c64[:, None] * c64[None, :]) % (GW // 4)).astype(np.float64) / (GW // 4)
    four = lambda m: jnp.asarray(np.broadcast_to(m, (4,) + m.shape), F32)
    router_hi, router_lo = _split(jnp.pad(p["router_w"][l], ((0, 0), (0, LANES - N_EXP))))
    router_w = jnp.concatenate([router_hi, router_lo], axis=1)
    router_b = jnp.pad(p["router_b"][l], (0, LANES - N_EXP)).reshape(1, LANES)
    return {
        "attn_g": p["attn_norm_g"][l].reshape(1, D),
        "w_in": w_cols,
        "q_g": jnp.pad(p["q_norm_g"][l], (0, 2 * LANES - Q_LORA)).reshape(1, 2 * LANES),
        "w_uq": w_uq.reshape(2 * LANES, wq).astype(BF16),
        "kv_g": p["kv_norm_g"][l].reshape(1, KV_LORA),
        "w_kn": w_kn.reshape(KV_LORA, wq).astype(BF16),
        "w_kr": jnp.asarray(place.reshape(LANES, wq), BF16),
        "w_v": w_v.reshape(KV_LORA, HEADS * V_DIM).astype(BF16),
        "bd_c": _block_diag(four(np.cos(ang))),
        "bd_s": _block_diag(four(np.sin(ang))),
        "w_f": p["fourier_w"][l].astype(BF16),
        "bd_pool": _block_diag(p["pool_w"][l]).astype(BF16),
        "pool_scale": p["pool_scale"][l].reshape(1, GW),
        "conv_w": p["conv_w"][l],
        "conv_b": p["conv_b"][l].reshape(1, GW),
        "bd_wa": jnp.stack([_block_diag(p["lru_wa"][l, d]) for d in range(2)]).astype(BF16),
        "bd_wx": jnp.stack([_block_diag(p["lru_wx"][l, d]) for d in range(2)]).astype(BF16),
        "lru_ba": p["lru_ba"][l].reshape(2, 1, GW),
        "lru_bx": p["lru_bx"][l].reshape(2, 1, GW),
        "lru_lam": p["lru_lambda"][l].reshape(2, 1, GW),
        "out_g": p["out_norm_g"][l],
        "w_out": p["w_out"][l].astype(BF16),
        "ffn_g": p["ffn_norm_g"][l].reshape(1, D),
        "router_w": router_w,
        "router_b": router_b,
    }


def _routing_tables(route, prob, counts, tab):
    top_e, rank = route[0:TOP_K], route[TOP_K:2 * TOP_K]
    top_p = prob[0:TOP_K].reshape(-1)
    loc = route[2 * TOP_K:3 * TOP_K].reshape(-1)
    counts = counts.astype(I32)
    padded = (counts + TM_E - 1) // TM_E * TM_E
    pad_ends = jnp.cumsum(padded)
    pad_starts = pad_ends - padded
    experts = jnp.arange(N_EXP, dtype=I32)
    onehot = top_e[:, :, None] == experts
    dest = (jnp.sum(jnp.where(onehot, pad_starts, 0), axis=-1) + rank).reshape(T * TOP_K)
    n_used = (pad_ends[-1] // TM_E).astype(I32)
    blk = jnp.minimum(jnp.arange(NB_E, dtype=I32), n_used - 1) * TM_E
    block_e = jnp.minimum(jnp.sum(pad_ends[None, :] <= blk[:, None], axis=-1), N_EXP - 1).astype(I32)
    of_block = lambda a: jnp.sum(jnp.where(block_e[:, None] == experts, a, 0), axis=-1)
    valid = jnp.clip(of_block(pad_starts + counts) - blk, 0, TM_E).astype(I32)
    first = (blk == of_block(pad_starts)).astype(I32)
    later = (experts[None, :] > experts[:, None]) & (counts[None, :] > 0)
    next_of = jnp.min(jnp.where(later, experts[None, :], N_EXP), axis=-1)
    next_e = of_block(jnp.where(next_of < N_EXP, next_of, -1)).astype(I32)
    parity = (of_block(jnp.cumsum((counts > 0).astype(I32))) % 2).astype(I32)
    sched = (block_e, n_used.reshape(1), valid, first, next_e, parity)
    tab = tab.reshape(T // TM, N_EXP, LANES).astype(I32)
    before, nch, base = tab[:, :, 0], tab[:, :, 1], tab[:, :, 2]
    run_start = pad_starts[None, :] + before
    ends = base + nch
    ci = jnp.arange(MAX_CH, dtype=I32)
    e_of = jnp.minimum(jnp.sum(ends[:, None, :] <= ci[None, :, None], axis=-1), N_EXP - 1)
    pick = lambda a: jnp.sum(jnp.where(e_of[:, :, None] == experts, a[:, None, :], 0), axis=-1)
    chunk_src = jnp.clip(pick(run_start) + (ci[None, :] - pick(base)) * CH_C, 0, N_SLOTS)
    return dest, sched, loc, top_p, chunk_src.reshape(-1), ends[:, N_EXP - 1]


def kernel(x_prompt, x_sample, cache_ckv, cache_krope, state_lru, c, c_ctx, w_mod, b_mod, attn_norm_g, w_in, pool_w, pool_scale, q_norm_g, w_uq, kv_norm_g, w_ukv, fourier_w, conv_w, conv_b, lru_wa, lru_ba, lru_wx, lru_bx, lru_lambda, out_norm_g, w_out, ffn_norm_g, router_w, router_b, w_gate, b_gate, w_up, b_up, w_down, b_down, final_norm_g):
    params = dict(attn_norm_g=attn_norm_g, w_in=w_in, pool_w=pool_w, pool_scale=pool_scale,
                  q_norm_g=q_norm_g, w_uq=w_uq, kv_norm_g=kv_norm_g, w_ukv=w_ukv,
                  fourier_w=fourier_w, conv_w=conv_w, conv_b=conv_b, lru_wa=lru_wa, lru_ba=lru_ba,
                  lru_wx=lru_wx, lru_bx=lru_bx, lru_lambda=lru_lambda, out_norm_g=out_norm_g,
                  w_out=w_out, ffn_norm_g=ffn_norm_g, router_w=router_w, router_b=router_b,
                  w_gate=w_gate, b_gate=b_gate, w_up=w_up, b_up=b_up, w_down=w_down, b_down=b_down)
    x = (x_prompt.reshape(T_P, D), x_sample.reshape(T_S, D))
    cvec = jnp.concatenate([c_ctx[None, :], c, jnp.zeros((SUBLANES - 1 - DEC_BATCH, D), F32)], axis=0)
    mod = _modulation(cvec, w_mod, b_mod)
    rope = _rope_tables()
    dft_p, dft_s = _dft_tables(SEQ), _dft_tables(DEC_SEQ)
    tri = jnp.asarray(np.triu(np.ones((TM, TM), np.float32)), BF16)
    lower = jnp.asarray(np.tril(np.ones((N_EXP, N_EXP), np.float32), -1), BF16)
    final_g = final_norm_g.reshape(1, D)
    h0_prompt = jnp.zeros((BATCH, 2, GW), F32)

    new_ckv, new_krope, new_lru = [], [], []
    for l in range(DEPTH):
        lw = _layer_weights(l, params)
        zmix, q, k, v, ckv, kr = _front(x, mod[l], lw, rope)
        new_ckv.append(ckv.reshape(BATCH, SEQ, KV_LORA))
        new_krope.append(kr.reshape(BATCH, SEQ, QK_ROPE))

        yb_p = _attention(q, [(k, v, SEQ, 0)], BATCH, SEQ, SEQ, 0, group=PROMPT_GROUP)
        yc_p = _fourier(zmix, dft_p, lw, BATCH, SEQ, 0, group=PROMPT_GROUP)
        ya_p, yd_p, st_p = _seq_mixers(zmix, h0_prompt, lw, BATCH, SEQ, 0)
        new_lru.append(st_p)

        kr_ctx = jnp.pad(cache_krope[:, l].reshape(DEC_BATCH * PAST, QK_ROPE),
                         ((0, 0), (0, LANES - QK_ROPE)))
        k_ctx, v_ctx = _kv_expand(cache_ckv[:, l].reshape(DEC_BATCH * PAST, KV_LORA), kr_ctx, lw)
        yb_s = _attention(q, [(k_ctx, v_ctx, PAST, 0), (k, v, DEC_SEQ, T_P)],
                          DEC_BATCH, DEC_SEQ, 512, T_P)
        yc_s = _fourier(zmix, dft_s, lw, DEC_BATCH, DEC_SEQ, T_P)
        ya_s, yd_s, _ = _seq_mixers(zmix, state_lru[:, l], lw, DEC_BATCH, DEC_SEQ, T_P)

        x1, h2lin, route, prob, tab, counts = _post(
            x, mod[l], (ya_p, yb_p, yc_p, yd_p), (ya_s, yb_s, yc_s, yd_s), lw, tri, lower)
        dest, sched, loc, top_p, chunk_src, n_chunks = _routing_tables(
            route, prob, counts[:, 0], tab)
        xs = _gather_rows(dest, sched[1], h2lin)
        ys = _experts(l, sched, xs, params)
        x = _combine(loc, top_p, chunk_src, n_chunks, ys, x1, mod[l], final_g,
                     final=(l == DEPTH - 1))

    y_prompt, y_sample = x
    return (y_prompt.reshape(BATCH, SEQ, D), y_sample.reshape(DEC_BATCH, DEC_SEQ, D),
            jnp.stack(new_ckv, axis=1), jnp.stack(new_krope, axis=1), jnp.stack(new_lru, axis=1))
```
